```python
import jax, jax.numpy as jnp
from jax import lax
import numpy as np

D_MODEL = 2048
BATCH = 4
SEQ = 2048
DEPTH = 2
DEC_BATCH = 8
DEC_SEQ = 4
PAST_LEN = 16384
PAGE_SIZE = 128

N_EVEN = (DEPTH + 1) // 2
N_ODD = DEPTH // 2
A_HEADS = 4
A_DH = D_MODEL // 8
A_WIDTH = A_HEADS * A_DH
A_CHUNK = 128
B_HEADS = 8
B_DH = D_MODEL // 16
B_WIDTH = B_HEADS * B_DH
Q_BLOCK = 128
C_WIDTH = D_MODEL
C_GROUPS = 8
C_GDIM = C_WIDTH // C_GROUPS
C_CHUNK = 128
EVEN_MIX = A_WIDTH + B_WIDTH
EVEN_IN = 5 * A_WIDTH + 2 * A_HEADS + 4 * B_WIDTH
ODD_IN = 3 * C_WIDTH
RMS_EPS = 1e-6

kernel_name = "hybrid_mlstm_stickbreak_chunkgmlp_step"


def rmsnorm(x, g):
    xf = x.astype(jnp.float32)
    y = xf * lax.rsqrt(jnp.mean(xf * xf, axis=-1, keepdims=True) + RMS_EPS) * g.astype(jnp.float32)
    return y.astype(x.dtype)


def _split_points(sizes):
    pts, acc = [], 0
    for s in sizes[:-1]:
        acc += s
        pts.append(acc)
    return pts


def mlstm_chunk(state, chunk):
    c0, n0, m0 = state
    q, k, v, ig, lf = chunk
    L = q.shape[2]
    b = jnp.cumsum(lf, axis=-1)
    causal = jnp.tril(jnp.ones((L, L), dtype=bool))
    d = jnp.where(causal, b[..., :, None] - b[..., None, :] + ig[..., None, :], -jnp.inf)
    m_carry = b + m0[..., None]
    m = jnp.maximum(m_carry, jnp.max(d, axis=-1))
    w_intra = jnp.exp(d - m[..., None])
    w_carry = jnp.exp(m_carry - m)
    s = jnp.einsum('bhtd,bhsd->bhts', q, k) * w_intra
    num = jnp.einsum('bhts,bhsd->bhtd', s, v) + w_carry[..., None] * jnp.einsum('bhvk,bhtk->bhtv', c0, q)
    den = jnp.sum(s, axis=-1) + w_carry * jnp.einsum('bhk,bhtk->bht', n0, q)
    h = num / jnp.maximum(jnp.abs(den), jnp.exp(-m))[..., None]
    m_new = m[..., -1]
    w_last = jnp.exp(d[..., -1, :] - m_new[..., None])
    wc_last = w_carry[..., -1]
    c_new = wc_last[..., None, None] * c0 + jnp.einsum('bhs,bhsv,bhsk->bhvk', w_last, v, k)
    n_new = wc_last[..., None] * n0 + jnp.einsum('bhs,bhsk->bhk', w_last, k)
    return (c_new, n_new, m_new), h


def mlstm_seq(q, k, v, ig, lf, state):
    t_len = q.shape[2]
    L = min(A_CHUNK, t_len)
    if t_len % L:
        L = t_len
    nc = t_len // L

    def to_chunks(t):
        return jnp.moveaxis(t.reshape(t.shape[:2] + (nc, L) + t.shape[3:]), 2, 0)

    chunks = (to_chunks(q), to_chunks(k), to_chunks(v), to_chunks(ig), to_chunks(lf))
    state, h = lax.scan(mlstm_chunk, state, chunks)
    h = jnp.moveaxis(h, 0, 2).reshape(q.shape)
    return h, state


def stick_breaking(q, k, v, q_pos, k_pos, b_sb):
    z = (jnp.einsum('bqhd,bkhd->bhqk', q, k).astype(jnp.float32) * (B_DH ** -0.5)
         + b_sb.astype(jnp.float32)[None, :, None, None])
    mask = k_pos[None, :] < q_pos[:, None]
    log_1m = jnp.where(mask, jax.nn.log_sigmoid(-z), 0.0)
    rest = lax.cumsum(log_1m, axis=3, reverse=True) - log_1m
    a = jnp.where(mask, jnp.exp(jax.nn.log_sigmoid(z) + rest), 0.0)
    return jnp.einsum('bhqk,bkhd->bqhd', a.astype(v.dtype), v)


def stick_breaking_blocks(q, k, v, q_start, k_pos, b_sb):
    bsz, tq, nh, dh = q.shape
    blk = min(Q_BLOCK, tq)
    nb = -(-tq // blk)
    pad = nb * blk - tq
    qp = jnp.pad(q, ((0, 0), (0, pad), (0, 0), (0, 0)))
    qb = qp.reshape(bsz, nb, blk, nh, dh).transpose(1, 0, 2, 3, 4)
    pos = (q_start + jnp.arange(nb * blk)).reshape(nb, blk)
    out = lax.map(lambda a: stick_breaking(a[0], k, v, a[1], k_pos, b_sb), (qb, pos))
    return out.transpose(1, 0, 2, 3, 4).reshape(bsz, nb * blk, nh, dh)[:, :tq]


def even_mixer(x, g_norm, w_in, b_i, b_f, b_sb, w_out, a_state, past_kv, q_start):
    bsz, t_len, _ = x.shape
    h = rmsnorm(x, g_norm)
    sizes = [A_WIDTH] * 5 + [A_HEADS] * 2 + [B_WIDTH] * 4
    qa, ka, va, oa, ga, ia, fa, qb, kb, vb, gb = jnp.split(h @ w_in, _split_points(sizes), axis=-1)

    def a_heads(t):
        return t.reshape(bsz, t_len, A_HEADS, A_DH).transpose(0, 2, 1, 3).astype(jnp.float32)

    ig = (ia + b_i).astype(jnp.float32).transpose(0, 2, 1)
    lf = jax.nn.log_sigmoid((fa + b_f).astype(jnp.float32)).transpose(0, 2, 1)
    if a_state is None:
        a_state = (jnp.zeros((bsz, A_HEADS, A_DH, A_DH), jnp.float32),
                   jnp.zeros((bsz, A_HEADS, A_DH), jnp.float32),
                   jnp.zeros((bsz, A_HEADS), jnp.float32))
    else:
        a_state = (a_state[0].astype(jnp.float32), a_state[1].astype(jnp.float32),
                   a_state[2].astype(jnp.float32))
    h_a, new_state = mlstm_seq(a_heads(qa), a_heads(ka) * (A_DH ** -0.5), a_heads(va), ig, lf, a_state)
    h_a = h_a.transpose(0, 2, 1, 3).reshape(bsz, t_len, A_WIDTH).astype(x.dtype) * jax.nn.sigmoid(oa)

    qb4 = qb.reshape(bsz, t_len, B_HEADS, B_DH)
    kb4 = kb.reshape(bsz, t_len, B_HEADS, B_DH)
    vb4 = vb.reshape(bsz, t_len, B_HEADS, B_DH)
    if past_kv is None:
        k_all, v_all = kb4, vb4
    else:
        k_all = jnp.concatenate([past_kv[0].astype(kb4.dtype), kb4], axis=1)
        v_all = jnp.concatenate([past_kv[1].astype(vb4.dtype), vb4], axis=1)
    k_pos = jnp.arange(k_all.shape[1])
    h_b = stick_breaking_blocks(qb4, k_all, v_all, q_start, k_pos, b_sb).reshape(bsz, t_len, B_WIDTH)

    y = jnp.concatenate([h_a * jax.nn.silu(ga), h_b * jax.nn.silu(gb)], axis=-1) @ w_out
    return x + y, new_state, kb4, vb4


def chunk_spatial_gate(u, v, w_s, b_s):
    bsz, t_len, _ = v.shape
    nc = -(-t_len // C_CHUNK)
    pad = nc * C_CHUNK - t_len
    vp = jnp.pad(v, ((0, 0), (0, pad), (0, 0))).reshape(bsz, nc, C_CHUNK, C_GROUPS, C_GDIM)
    causal = jnp.tril(jnp.ones((C_CHUNK, C_CHUNK), dtype=bool))
    wm = jnp.where(causal, w_s, 0.0).astype(v.dtype)
    sv = jnp.einsum('gts,bnsgc->bntgc', wm, vp) + b_s.T.astype(v.dtype)[None, None, :, :, None]
    sv = sv.reshape(bsz, nc * C_CHUNK, C_WIDTH)[:, :t_len]
    return u * sv


def odd_mixer(x, g_norm, w_in, v_gain, w_s, b_s, w_out):
    h = rmsnorm(x, g_norm)
    u, v, g = jnp.split(h @ w_in, 3, axis=-1)
    u = jax.nn.gelu(u)
    v = rmsnorm(jax.nn.gelu(v), v_gain)
    y = chunk_spatial_gate(u, v, w_s, b_s) * jax.nn.silu(g)
    return x + y @ w_out, v


def setup_inputs(seed: int = 0) -> dict:
    key = jax.random.key(seed)
    ks = jax.random.split(key, 24)
    n_pages = PAST_LEN // PAGE_SIZE
    n_used = DEC_BATCH * n_pages
    n_pool = (5 * n_used + 3) // 4
    f32 = jnp.float32
    x_prompt = jax.random.normal(ks[0], (BATCH, SEQ, D_MODEL), f32)
    x_sample = jax.random.normal(ks[1], (DEC_BATCH, DEC_SEQ, D_MODEL), f32)
    state_a_C = 0.05 * jax.random.normal(ks[2], (N_EVEN, DEC_BATCH, A_HEADS, A_DH, A_DH), f32)
    state_a_n = 0.1 * jax.random.normal(ks[3], (N_EVEN, DEC_BATCH, A_HEADS, A_DH), f32)
    state_a_m = jax.random.normal(ks[4], (N_EVEN, DEC_BATCH, A_HEADS), f32)
    cache_b_k = jax.random.normal(ks[5], (N_EVEN, n_pool, PAGE_SIZE, B_HEADS, B_DH), f32)
    cache_b_v = jax.random.normal(ks[6], (N_EVEN, n_pool, PAGE_SIZE, B_HEADS, B_DH), f32)
    page_table = jax.random.permutation(ks[7], n_pool)[:n_used].reshape(DEC_BATCH, n_pages).astype(jnp.int32)
    even_norm = 1.0 + 0.01 * jax.random.normal(ks[8], (N_EVEN, D_MODEL), f32)
    even_w_in = jax.random.normal(ks[9], (N_EVEN, D_MODEL, EVEN_IN), f32) * (D_MODEL ** -0.5)
    even_b_i = 0.1 * jax.random.normal(ks[10], (N_EVEN, A_HEADS), f32)
    even_b_f = jnp.linspace(3.0, 6.0, A_HEADS, dtype=f32)[None, :] + 0.1 * jax.random.normal(ks[11], (N_EVEN, A_HEADS), f32)
    even_b_sb = jnp.linspace(-4.0, -10.0, B_HEADS, dtype=f32)[None, :] + 0.1 * jax.random.normal(ks[20], (N_EVEN, B_HEADS), f32)
    even_w_out = jax.random.normal(ks[12], (N_EVEN, EVEN_MIX, D_MODEL), f32) * (EVEN_MIX ** -0.5)
    odd_norm = 1.0 + 0.01 * jax.random.normal(ks[13], (N_ODD, D_MODEL), f32)
    odd_w_in = jax.random.normal(ks[14], (N_ODD, D_MODEL, ODD_IN), f32) * (D_MODEL ** -0.5)
    odd_v_gain = 1.0 + 0.01 * jax.random.normal(ks[15], (N_ODD, C_WIDTH), f32)
    odd_w_s = jax.random.normal(ks[16], (N_ODD, C_GROUPS, C_CHUNK, C_CHUNK), f32) * (C_CHUNK ** -0.5)
    odd_b_s = 1.0 + 0.1 * jax.random.normal(ks[17], (N_ODD, C_GROUPS, C_CHUNK), f32)
    odd_w_out = jax.random.normal(ks[18], (N_ODD, C_WIDTH, D_MODEL), f32) * (C_WIDTH ** -0.5)
    final_norm = 1.0 + 0.01 * jax.random.normal(ks[19], (D_MODEL,), f32)
    return {"x_prompt": x_prompt, "x_sample": x_sample,
            "state_a_C": state_a_C, "state_a_n": state_a_n, "state_a_m": state_a_m,
            "cache_b_k": cache_b_k, "cache_b_v": cache_b_v, "page_table": page_table,
            "even_norm": even_norm, "even_w_in": even_w_in, "even_b_i": even_b_i,
            "even_b_f": even_b_f, "even_b_sb": even_b_sb, "even_w_out": even_w_out,
            "odd_norm": odd_norm, "odd_w_in": odd_w_in, "odd_v_gain": odd_v_gain,
            "odd_w_s": odd_w_s, "odd_b_s": odd_b_s, "odd_w_out": odd_w_out,
            "final_norm": final_norm}


def reference(x_prompt, x_sample, state_a_C, state_a_n, state_a_m, cache_b_k, cache_b_v, page_table,
              even_norm, even_w_in, even_b_i, even_b_f, even_b_sb, even_w_out,
              odd_norm, odd_w_in, odd_v_gain, odd_w_s, odd_b_s, odd_w_out, final_norm):
    n_seq, n_pages = page_table.shape
    past_len = n_pages * PAGE_SIZE
    xp, xs = x_prompt, x_sample
    aCp, anp, amp, aCs, ans, ams = [], [], [], [], [], []
    bkp, bvp, bks, bvs, cvs = [], [], [], [], []
    for layer in range(DEPTH):
        j = layer // 2
        if layer % 2 == 0:
            w = (even_norm[j], even_w_in[j], even_b_i[j], even_b_f[j], even_b_sb[j], even_w_out[j])
            xp, st_p, k_p, v_p = even_mixer(xp, *w, None, None, 0)
            past_k = jnp.take(cache_b_k[j], page_table, axis=0).reshape(n_seq, past_len, B_HEADS, B_DH)
            past_v = jnp.take(cache_b_v[j], page_table, axis=0).reshape(n_seq, past_len, B_HEADS, B_DH)
            st_in = (state_a_C[j], state_a_n[j], state_a_m[j])
            xs, st_s, k_s, v_s = even_mixer(xs, *w, st_in, (past_k, past_v), past_len)
            aCp.append(st_p[0]); anp.append(st_p[1]); amp.append(st_p[2])
            aCs.append(st_s[0]); ans.append(st_s[1]); ams.append(st_s[2])
            bkp.append(k_p); bvp.append(v_p); bks.append(k_s); bvs.append(v_s)
        else:
            w = (odd_norm[j], odd_w_in[j], odd_v_gain[j], odd_w_s[j], odd_b_s[j], odd_w_out[j])
            xp, _ = odd_mixer(xp, *w)
            xs, v_rows = odd_mixer(xs, *w)
            cvs.append(v_rows)
    y_prompt = rmsnorm(xp, final_norm)
    y_sample = rmsnorm(xs, final_norm)
    return (y_prompt, y_sample,
            jnp.stack(aCp), jnp.stack(anp), jnp.stack(amp),
            jnp.stack(aCs), jnp.stack(ans), jnp.stack(ams),
            jnp.stack(bkp), jnp.stack(bvp), jnp.stack(bks), jnp.stack(bvs),
            jnp.stack(cvs))
```

```python
import functools

import jax
import jax.numpy as jnp
from jax import lax
from jax.experimental import pallas as pl
from jax.experimental.pallas import tpu as pltpu

F32 = jnp.float32
BF16 = jnp.bfloat16

D_MODEL = 2048
PAGE_SIZE = 128
A_HEADS = 4
A_DH = 256
A_WIDTH = A_HEADS * A_DH
A_CHUNK = 128
B_HEADS = 8
B_DH = 128
B_WIDTH = B_HEADS * B_DH
C_WIDTH = D_MODEL
C_GROUPS = 8
C_GDIM = C_WIDTH // C_GROUPS
C_CHUNK = 128
RMS_EPS = 1e-6
GATE_LANES = 128
NEG_BIG = -1e30
SAMPLE_PAD = 8

VMEM_LIMIT_BYTES = 56 * 1024 * 1024


def _params(*sem):
    return pltpu.CompilerParams(dimension_semantics=sem, vmem_limit_bytes=VMEM_LIMIT_BYTES)


def _dot(a, b):
    return jnp.dot(a, b, preferred_element_type=F32)


def _dot_nt(a, b):
    return lax.dot_general(a, b, (((1,), (1,)), ((), ())), preferred_element_type=F32)


def _dot_tn(a, b):
    return lax.dot_general(a, b, (((0,), (0,)), ((), ())), preferred_element_type=F32)


def _softplus(z):
    return jnp.maximum(z, 0.0) + jnp.log1p(jnp.exp(-jnp.abs(z)))


def _sigmoid(z):
    return 1.0 / (1.0 + jnp.exp(-z))


def _silu(z):
    return z * _sigmoid(z)


def _gelu_tanh(x):
    c = 0.7978845608028654
    return x * (0.5 * (1.0 + jnp.tanh(c * (x + 0.044715 * (x * x * x)))))


def _rms(x, g):
    return x * lax.rsqrt(jnp.mean(x * x, axis=-1, keepdims=True) + RMS_EPS) * g


def _split_hi_lo(x):
    hi = x.astype(BF16)
    lo = (x - hi.astype(F32)).astype(BF16)
    return hi, lo


def _norm_gates_kernel(x_ref, g_ref, whi_ref, wlo_ref, bias_ref, h_ref, gate_ref):
    h = _rms(x_ref[...], g_ref[...])
    h_hi, h_lo = _split_hi_lo(h)
    h_ref[...] = h_hi
    pre = (_dot(h_hi, whi_ref[...]) + _dot(h_hi, wlo_ref[...]) + _dot(h_lo, whi_ref[...])
           + bias_ref[...])
    lane = lax.broadcasted_iota(jnp.int32, pre.shape, 1)
    is_forget = (lane >= A_HEADS) & (lane < 2 * A_HEADS)
    gate_ref[...] = jnp.where(is_forget, -_softplus(-pre), pre)


def norm_gates(x, gain, w_gate, bias, tm):
    m = x.shape[0]
    whi, wlo = _split_hi_lo(w_gate)
    return pl.pallas_call(
        _norm_gates_kernel,
        grid=(m // tm,),
        in_specs=[pl.BlockSpec((tm, D_MODEL), lambda i: (i, 0)),
                  pl.BlockSpec((1, D_MODEL), lambda i: (0, 0)),
                  pl.BlockSpec((D_MODEL, GATE_LANES), lambda i: (0, 0)),
                  pl.BlockSpec((D_MODEL, GATE_LANES), lambda i: (0, 0)),
                  pl.BlockSpec((1, GATE_LANES), lambda i: (0, 0))],
        out_specs=[pl.BlockSpec((tm, D_MODEL), lambda i: (i, 0)),
                   pl.BlockSpec((tm, GATE_LANES), lambda i: (i, 0))],
        out_shape=[jax.ShapeDtypeStruct((m, D_MODEL), BF16),
                   jax.ShapeDtypeStruct((m, GATE_LANES), F32)],
        compiler_params=_params("arbitrary"),
        name="norm_gates",
    )(x, gain.reshape(1, D_MODEL), whi, wlo, bias)


def _mm_kernel(a_ref, w_ref, o_ref):
    o_ref[...] = _dot(a_ref[...], w_ref[...]).astype(o_ref.dtype)


def matmul(a, w, out_dtype, tm, tn):
    m, k = a.shape
    n = w.shape[1]
    return pl.pallas_call(
        _mm_kernel,
        grid=(m // tm, n // tn),
        in_specs=[pl.BlockSpec((tm, k), lambda i, j: (i, 0)),
                  pl.BlockSpec((k, tn), lambda i, j: (0, j))],
        out_specs=pl.BlockSpec((tm, tn), lambda i, j: (i, j)),
        out_shape=jax.ShapeDtypeStruct((m, n), out_dtype),
        compiler_params=_params("arbitrary", "arbitrary"),
        name="proj",
    )(a, w)


def _out_proj_kernel(*refs, n_lhs, emit_x):
    a_refs = refs[:n_lhs]
    w_refs = refs[n_lhs:2 * n_lhs]
    x_ref, g_ref = refs[2 * n_lhs], refs[2 * n_lhs + 1]
    out_refs = refs[2 * n_lhs + 2:]
    y = x_ref[...]
    for a_ref, w_ref in zip(a_refs, w_refs):
        y = y + _dot(a_ref[...].astype(BF16), w_ref[...])
    if emit_x:
        out_refs[0][...] = y
    out_refs[-1][...] = _rms(y, g_ref[...]).astype(out_refs[-1].dtype)


def out_proj_norm(lhs, ws, x, gain, tm, emit_x, norm_dtype):
    m = x.shape[0]
    n_lhs = len(lhs)
    in_specs = ([pl.BlockSpec((tm, a.shape[1]), lambda i: (i, 0)) for a in lhs]
                + [pl.BlockSpec(w.shape, lambda i: (0, 0)) for w in ws]
                + [pl.BlockSpec((tm, D_MODEL), lambda i: (i, 0)),
                   pl.BlockSpec((1, D_MODEL), lambda i: (0, 0))])
    out_specs = [pl.BlockSpec((tm, D_MODEL), lambda i: (i, 0))]
    out_shape = [jax.ShapeDtypeStruct((m, D_MODEL), norm_dtype)]
    if emit_x:
        out_specs = [pl.BlockSpec((tm, D_MODEL), lambda i: (i, 0))] + out_specs
        out_shape = [jax.ShapeDtypeStruct((m, D_MODEL), F32)] + out_shape
    return pl.pallas_call(
        functools.partial(_out_proj_kernel, n_lhs=n_lhs, emit_x=emit_x),
        grid=(m // tm,),
        in_specs=in_specs,
        out_specs=out_specs,
        out_shape=out_shape,
        compiler_params=_params("arbitrary"),
        name="out_proj_norm",
    )(*lhs, *ws, x, gain.reshape(1, D_MODEL))


def _mlstm_kernel(q_ref, k_ref, v_ref, og_ref, gg_ref, gcol_ref, grow_ref, c0_ref, n0_ref, m0_ref,
                  h_ref, c_out_ref, n_out_ref, m_out_ref, c_sc, n_sc, m_sc, *, chunk):
    head = pl.program_id(1)
    ci = pl.program_id(2)
    L = chunk

    @pl.when(ci == 0)
    def _():
        c_sc[...] = c0_ref[0, 0]
        n_sc[...] = n0_ref[0, 0]
        m_sc[...] = m0_ref[0, 0]

    q = q_ref[0]
    ks = k_ref[0] * jnp.asarray(A_DH ** -0.5, BF16)
    v = v_ref[0]
    gcol = gcol_ref[0]
    grow = grow_ref[0]
    lane = lax.broadcasted_iota(jnp.int32, gcol.shape, 1)
    sub = lax.broadcasted_iota(jnp.int32, grow.shape, 0)
    ig_col = jnp.sum(jnp.where(lane == head, gcol, 0.0), axis=1, keepdims=True)
    lf_col = jnp.sum(jnp.where(lane == head + A_HEADS, gcol, 0.0), axis=1, keepdims=True)
    ig_row = jnp.sum(jnp.where(sub == head, grow, 0.0), axis=0, keepdims=True)
    lf_row = jnp.sum(jnp.where(sub == head + A_HEADS, grow, 0.0), axis=0, keepdims=True)

    tt = lax.broadcasted_iota(jnp.int32, (L, L), 0)
    ss = lax.broadcasted_iota(jnp.int32, (L, L), 1)
    causal = ss <= tt
    b_col = jnp.sum(jnp.where(causal, lf_row, 0.0), axis=1, keepdims=True)
    b_row = jnp.sum(jnp.where(tt <= ss, lf_col, 0.0), axis=0, keepdims=True)
    b_last = jnp.sum(lf_row, axis=1, keepdims=True)

    m0 = m_sc[...][:, :1]
    n0 = n_sc[...]
    c0 = c_sc[...]

    d = jnp.where(causal, b_col - b_row + ig_row, NEG_BIG)
    m_carry = b_col + m0
    m = jnp.maximum(m_carry, jnp.max(d, axis=1, keepdims=True))
    w_intra = jnp.exp(d - m)
    w_carry = jnp.exp(m_carry - m)
    s = _dot_nt(q, ks) * w_intra
    qf = q.astype(F32)
    num = _dot(s.astype(BF16), v) + w_carry * _dot_nt(q, c0.astype(BF16))
    den = jnp.sum(s, axis=1, keepdims=True) + w_carry * jnp.sum(qf * n0, axis=1, keepdims=True)
    h = num / jnp.maximum(jnp.abs(den), jnp.exp(-m))
    gated = h * _sigmoid(og_ref[0].astype(F32)) * _silu(gg_ref[0].astype(F32))
    h_ref[0] = gated.astype(h_ref.dtype)

    m_carry_last = b_last + m0
    d_last_row = b_last - b_row + ig_row
    m_new = jnp.maximum(m_carry_last, jnp.max(d_last_row, axis=1, keepdims=True))
    wc_last = jnp.exp(m_carry_last - m_new)
    w_last_col = jnp.exp(b_last - b_col + ig_col - m_new)
    vw = (v.astype(F32) * w_last_col).astype(BF16)
    c_new = wc_last * c0 + _dot_tn(vw, ks)
    n_new = wc_last * n0 + jnp.sum(ks.astype(F32) * w_last_col, axis=0, keepdims=True)
    c_sc[...] = c_new
    n_sc[...] = n_new
    m_sc[...] = jnp.broadcast_to(m_new, m_sc.shape)

    @pl.when(ci == pl.num_programs(2) - 1)
    def _():
        c_out_ref[0, 0] = c_new
        n_out_ref[0, 0] = n_new
        m_out_ref[0, 0] = jnp.broadcast_to(m_new, m_sc.shape)


def mlstm(qkv, og, gcol, grow, c0, n0, m0, chunk):
    bsz, t_len, _ = qkv.shape
    nc = t_len // chunk
    hd = A_HEADS
    blk = lambda off: pl.BlockSpec((1, chunk, A_DH), lambda b, h, c, off=off: (b, c, off + h))
    st4 = lambda r, w: pl.BlockSpec((1, 1, r, w), lambda b, h, c: (b, h, 0, 0))
    return pl.pallas_call(
        functools.partial(_mlstm_kernel, chunk=chunk),
        grid=(bsz, hd, nc),
        in_specs=[blk(0), blk(hd), blk(2 * hd), blk(0), blk(hd),
                  pl.BlockSpec((1, chunk, GATE_LANES), lambda b, h, c: (b, c, 0)),
                  pl.BlockSpec((1, 8, chunk), lambda b, h, c: (b, 0, c)),
                  st4(A_DH, A_DH), st4(1, A_DH), st4(1, GATE_LANES)],
        out_specs=[pl.BlockSpec((1, chunk, A_DH), lambda b, h, c: (b, c, h)),
                   st4(A_DH, A_DH), st4(1, A_DH), st4(1, GATE_LANES)],
        out_shape=[jax.ShapeDtypeStruct((bsz, t_len, A_WIDTH), BF16),
                   jax.ShapeDtypeStruct((bsz, hd, A_DH, A_DH), F32),
                   jax.ShapeDtypeStruct((bsz, hd, 1, A_DH), F32),
                   jax.ShapeDtypeStruct((bsz, hd, 1, GATE_LANES), F32)],
        scratch_shapes=[pltpu.VMEM((A_DH, A_DH), F32),
                        pltpu.VMEM((1, A_DH), F32),
                        pltpu.VMEM((1, GATE_LANES), F32)],
        compiler_params=_params("arbitrary", "arbitrary", "arbitrary"),
        name="mlstm",
    )(qkv, qkv, qkv, og, og, gcol, grow, c0, n0, m0)


def _stick_block(q, kb, vb, bias, run, mask, upper):
    z = _dot_nt(q, kb) * (B_DH ** -0.5) + bias
    sp = _softplus(z)
    log_1m = -sp if mask is None else jnp.where(mask, -sp, 0.0)
    hi, lo = _split_hi_lo(log_1m)
    rest = _dot(hi, upper) + _dot(lo, upper) + run
    a = jnp.exp(z - sp + rest)
    if mask is not None:
        a = jnp.where(mask, a, 0.0)
    return _dot(a.astype(BF16), vb), jnp.sum(log_1m, axis=1, keepdims=True)


def _strict_upper(n):
    j = lax.broadcasted_iota(jnp.int32, (n, n), 0)
    s = lax.broadcasted_iota(jnp.int32, (n, n), 1)
    return jnp.where(j > s, 1.0, 0.0).astype(BF16)


def _attn_prompt_kernel(bias_ref, q_ref, k_ref, v_ref, g_ref, o_ref, kbf, vbf, *, blk):
    head = pl.program_id(1)
    qi = pl.program_id(2)

    @pl.when(qi == 0)
    def _():
        kbf[...] = k_ref[0].astype(BF16)
        vbf[...] = v_ref[0].astype(BF16)

    q = q_ref[0]
    bias = bias_ref[head]
    upper = _strict_upper(blk)
    row = lax.broadcasted_iota(jnp.int32, (blk, blk), 0)
    col = lax.broadcasted_iota(jnp.int32, (blk, blk), 1)

    start = pl.multiple_of(qi * blk, blk)
    acc, run = _stick_block(q, kbf[pl.ds(start, blk), :], vbf[pl.ds(start, blk), :], bias,
                            jnp.zeros((blk, 1), F32), col < row, upper)

    def body(it, carry):
        acc, run = carry
        st = pl.multiple_of((qi - 1 - it) * blk, blk)
        contrib, tot = _stick_block(q, kbf[pl.ds(st, blk), :], vbf[pl.ds(st, blk), :], bias,
                                    run, None, upper)
        return acc + contrib, run + tot

    acc, run = lax.fori_loop(0, qi, body, (acc, run))
    o_ref[0] = (acc * _silu(g_ref[0].astype(F32))).astype(o_ref.dtype)


def attn_prompt(qg, k, v, b_sb, blk):
    bsz, t_len, _ = k.shape
    return pl.pallas_call(
        functools.partial(_attn_prompt_kernel, blk=blk),
        grid=(bsz, B_HEADS, t_len // blk),
        in_specs=[pl.BlockSpec(memory_space=pltpu.SMEM),
                  pl.BlockSpec((1, blk, B_DH), lambda b, h, i: (b, i, h)),
                  pl.BlockSpec((1, t_len, B_DH), lambda b, h, i: (b, 0, h)),
                  pl.BlockSpec((1, t_len, B_DH), lambda b, h, i: (b, 0, h)),
                  pl.BlockSpec((1, blk, B_DH), lambda b, h, i: (b, i, B_HEADS + h))],
        out_specs=pl.BlockSpec((1, blk, B_DH), lambda b, h, i: (b, i, h)),
        out_shape=jax.ShapeDtypeStruct((bsz, t_len, B_WIDTH), BF16),
        scratch_shapes=[pltpu.VMEM((t_len, B_DH), BF16), pltpu.VMEM((t_len, B_DH), BF16)],
        compiler_params=_params("arbitrary", "arbitrary", "arbitrary"),
        name="attn_prompt",
    )(b_sb, qg, k, v, qg)


def _attn_sample_kernel(pt_ref, bias_ref, q_ref, g_ref, knew_ref, vnew_ref, *refs, n_group):
    del pt_ref
    k_refs = refs[:n_group]
    v_refs = refs[n_group:2 * n_group]
    o_ref, qbd, acc, run = refs[2 * n_group:]
    p = pl.program_id(1)
    rows = B_HEADS * SAMPLE_PAD
    upper = _strict_upper(PAGE_SIZE)
    bias = bias_ref[...][:, :1]

    def step(kp, vp, mask):
        contrib, tot = _stick_block(qbd[...], kp.astype(BF16), vp.astype(BF16), bias,
                                    run[...][:, :1], mask, upper)
        acc[...] += contrib
        run[...] += jnp.broadcast_to(tot, run.shape)

    @pl.when(p == 0)
    def _():
        r = lax.broadcasted_iota(jnp.int32, (rows, B_WIDTH), 0)
        c = lax.broadcasted_iota(jnp.int32, (rows, B_WIDTH), 1)
        q_rep = jnp.concatenate([q_ref[0].astype(F32)] * B_HEADS, axis=0)
        qbd[...] = jnp.where((r // SAMPLE_PAD) == (c // B_DH), q_rep, 0.0).astype(BF16)
        acc[...] = jnp.zeros_like(acc)
        run[...] = jnp.zeros_like(run)
        t = lax.broadcasted_iota(jnp.int32, (rows, PAGE_SIZE), 0) % SAMPLE_PAD
        s = lax.broadcasted_iota(jnp.int32, (rows, PAGE_SIZE), 1)
        step(knew_ref[0], vnew_ref[0], s < t)

    for kr, vr in zip(k_refs, v_refs):
        step(kr[0], vr[0], None)

    @pl.when(p == pl.num_programs(1) - 1)
    def _():
        a = acc[...]
        c = lax.broadcasted_iota(jnp.int32, (SAMPLE_PAD, B_WIDTH), 1) // B_DH
        out = jnp.zeros((SAMPLE_PAD, B_WIDTH), F32)
        for h in range(B_HEADS):
            out = out + jnp.where(c == h, a[h * SAMPLE_PAD:(h + 1) * SAMPLE_PAD, :], 0.0)
        o_ref[0] = (out * _silu(g_ref[0].astype(F32))).astype(o_ref.dtype)


def attn_sample(qg, k_new, v_new, cache_k, cache_v, page_table, bias_rows, n_group):
    n_seq, n_pages = page_table.shape
    steps = n_pages // n_group
    rows = B_HEADS * SAMPLE_PAD

    def page_spec(i):
        return pl.BlockSpec((1, PAGE_SIZE, B_WIDTH),
                            lambda b, p, pt, i=i: (pt[b, n_pages - 1 - (p * n_group + i)], 0, 0))

    seq_spec = lambda r, col=0: pl.BlockSpec((1, r, B_WIDTH), lambda b, p, pt, col=col: (b, 0, col))
    grid_spec = pltpu.PrefetchScalarGridSpec(
        num_scalar_prefetch=1,
        grid=(n_seq, steps),
        in_specs=[pl.BlockSpec((rows, GATE_LANES), lambda b, p, pt: (0, 0)),
                  seq_spec(SAMPLE_PAD, 0), seq_spec(SAMPLE_PAD, 1), seq_spec(PAGE_SIZE), seq_spec(PAGE_SIZE)]
                 + [page_spec(i) for i in range(n_group)] * 2,
        out_specs=seq_spec(SAMPLE_PAD),
        scratch_shapes=[pltpu.VMEM((rows, B_WIDTH), BF16),
                        pltpu.VMEM((rows, B_WIDTH), F32),
                        pltpu.VMEM((rows, GATE_LANES), F32)],
    )
    return pl.pallas_call(
        functools.partial(_attn_sample_kernel, n_group=n_group),
        grid_spec=grid_spec,
        out_shape=jax.ShapeDtypeStruct((n_seq, SAMPLE_PAD, B_WIDTH), F32),
        compiler_params=_params("arbitrary", "arbitrary"),
        name="attn_sample",
    )(page_table, bias_rows, qg, qg, k_new, v_new, *([cache_k] * n_group), *([cache_v] * n_group))


def _odd_in_kernel(a_ref, w_ref, vg_ref, u_ref, v_ref, g_ref, *maybe_v32):
    j = pl.program_id(1)
    y = _dot(a_ref[...], w_ref[...])

    @pl.when(j == 0)
    def _():
        u_ref[...] = _gelu_tanh(y).astype(u_ref.dtype)

    @pl.when(j == 1)
    def _():
        vn = _rms(_gelu_tanh(y), vg_ref[...])
        v_ref[...] = vn.astype(v_ref.dtype)
        if maybe_v32:
            maybe_v32[0][...] = vn

    @pl.when(j == 2)
    def _():
        g_ref[...] = _silu(y).astype(g_ref.dtype)


def odd_in(h, w, v_gain, tm, act_dtype, emit_v32):
    m = h.shape[0]
    row_spec = pl.BlockSpec((tm, C_WIDTH), lambda i, j: (i, 0))
    n_out = 4 if emit_v32 else 3
    return pl.pallas_call(
        _odd_in_kernel,
        grid=(m // tm, 3),
        in_specs=[pl.BlockSpec((tm, D_MODEL), lambda i, j: (i, 0)),
                  pl.BlockSpec((D_MODEL, C_WIDTH), lambda i, j: (0, j)),
                  pl.BlockSpec((1, C_WIDTH), lambda i, j: (0, 0))],
        out_specs=[row_spec] * n_out,
        out_shape=[jax.ShapeDtypeStruct((m, C_WIDTH), act_dtype)] * 3
                  + ([jax.ShapeDtypeStruct((m, C_WIDTH), F32)] if emit_v32 else []),
        compiler_params=_params("arbitrary", "arbitrary"),
        name="odd_in",
    )(h, w, v_gain.reshape(1, C_WIDTH))


def _spatial_kernel(u_ref, v_ref, g_ref, ws_ref, bs_ref, y_ref, *, chunk, n_chunks):
    tt = lax.broadcasted_iota(jnp.int32, (chunk, chunk), 0)
    ss = lax.broadcasted_iota(jnp.int32, (chunk, chunk), 1)
    causal = ss <= tt
    for grp in range(C_GROUPS):
        wm = jnp.where(causal, ws_ref[grp], 0.0)
        bcol = bs_ref[:, grp:grp + 1]
        cols = slice(grp * C_GDIM, (grp + 1) * C_GDIM)
        for c in range(n_chunks):
            rows = slice(c * chunk, (c + 1) * chunk)
            vv = v_ref[rows, cols]
            if chunk >= 128:
                sv = _dot(wm.astype(BF16), vv)
            else:
                vf = vv.astype(F32)
                sv = jnp.zeros((chunk, C_GDIM), F32)
                for s in range(chunk):
                    sv = sv + wm[:, s:s + 1] * vf[s:s + 1, :]
            sv = sv + bcol
            y = u_ref[rows, cols].astype(F32) * sv * g_ref[rows, cols].astype(F32)
            y_ref[rows, cols] = y.astype(y_ref.dtype)


def spatial_gate(u, v, g, w_s, b_s_t, chunk, n_chunks):
    m = u.shape[0]
    tm = chunk * n_chunks
    row_spec = pl.BlockSpec((tm, C_WIDTH), lambda i: (i, 0))
    return pl.pallas_call(
        functools.partial(_spatial_kernel, chunk=chunk, n_chunks=n_chunks),
        grid=(m // tm,),
        in_specs=[row_spec, row_spec, row_spec,
                  pl.BlockSpec((C_GROUPS, chunk, chunk), lambda i: (0, 0, 0)),
                  pl.BlockSpec((chunk, C_GROUPS), lambda i: (0, 0))],
        out_specs=row_spec,
        out_shape=jax.ShapeDtypeStruct((m, C_WIDTH), u.dtype),
        compiler_params=_params("arbitrary"),
        name="spatial_gate",
    )(u, v, g, w_s, b_s_t)


def _even_weights(w_in, b_i, b_f):
    aw, bw = A_WIDTH, B_WIDTH
    gate0 = 5 * aw
    b0 = gate0 + 2 * A_HEADS
    w_qkv = w_in[:, :3 * aw].astype(BF16)
    w_og = w_in[:, 3 * aw:5 * aw].astype(BF16)
    w_gate = jnp.pad(w_in[:, gate0:b0], ((0, 0), (0, GATE_LANES - 2 * A_HEADS)))
    bias = jnp.pad(jnp.concatenate([b_i, b_f]), (0, GATE_LANES - 2 * A_HEADS)).reshape(1, GATE_LANES)
    w_qg = jnp.concatenate([w_in[:, b0:b0 + bw], w_in[:, b0 + 3 * bw:]], axis=1).astype(BF16)
    w_k = w_in[:, b0 + bw:b0 + 2 * bw].astype(BF16)
    w_v = w_in[:, b0 + 2 * bw:b0 + 3 * bw].astype(BF16)
    return w_qkv, w_og, w_gate, bias, w_qg, w_k, w_v


def _even_layer(x, bsz, t_len, valid_len, ew, g_norm, w_out, b_sb, state, paged, next_gain, tm):
    w_qkv, w_og, w_gate, bias, w_qg, w_k, w_v = ew
    m_rows = bsz * t_len
    h, gates = norm_gates(x, g_norm, w_gate, bias, min(tm, 512))
    tn = 1024
    qkv = matmul(h, w_qkv, BF16, tm, tn).reshape(bsz, t_len, 3 * A_WIDTH)
    og = matmul(h, w_og, BF16, tm, tn).reshape(bsz, t_len, 2 * A_WIDTH)
    qg_dtype = BF16 if paged is None else F32
    qg = matmul(h, w_qg, qg_dtype, tm, tn).reshape(bsz, t_len, 2 * B_WIDTH)
    k_new = matmul(h, w_k, F32, tm, tn).reshape(bsz, t_len, B_WIDTH)
    v_new = matmul(h, w_v, F32, tm, tn).reshape(bsz, t_len, B_WIDTH)

    gates = gates.reshape(bsz, t_len, GATE_LANES)
    t_pad = -(-t_len // A_CHUNK) * A_CHUNK
    pad = ((0, 0), (0, t_pad - t_len), (0, 0))
    if valid_len < t_pad:
        qkv, og, gates = jnp.pad(qkv, pad), jnp.pad(og, pad), jnp.pad(gates, pad)
        pos = jnp.arange(t_pad)[None, :, None]
        lane = jnp.arange(GATE_LANES)[None, None, :]
        gates = jnp.where((pos >= valid_len) & (lane < A_HEADS), NEG_BIG, gates)
        gates = jnp.where((pos >= valid_len) & (lane >= A_HEADS), 0.0, gates)
    grow = gates[:, :, :2 * A_HEADS].transpose(0, 2, 1)
    c0, n0, m0 = state
    h_a, c_new, n_new, m_new = mlstm(qkv, og, gates, grow, c0, n0, m0, A_CHUNK)
    h_a = h_a[:, :t_len]

    if paged is None:
        h_b = attn_prompt(qg, k_new, v_new, b_sb, 256)
    else:
        cache_k, cache_v, page_table = paged
        kv_pad = ((0, 0), (0, PAGE_SIZE - t_len), (0, 0))
        bias_rows = jnp.broadcast_to(jnp.repeat(b_sb, SAMPLE_PAD)[:, None], (B_HEADS * SAMPLE_PAD, GATE_LANES))
        h_b = attn_sample(qg, jnp.pad(k_new, kv_pad), jnp.pad(v_new, kv_pad),
                          cache_k, cache_v, page_table, bias_rows, 4)

    w_out_b = w_out.astype(BF16)
    x_new, h_next = out_proj_norm(
        [h_a.reshape(m_rows, A_WIDTH), h_b.reshape(m_rows, B_WIDTH)],
        [w_out_b[:A_WIDTH], w_out_b[A_WIDTH:]], x, next_gain, min(tm, 512), True, BF16)
    return x_new, h_next, (c_new, n_new[:, :, 0, :], m_new[:, :, 0, 0]), k_new, v_new


def _odd_layer(x, h, w_in_b, v_gain, w_s, b_s, w_out_b, final_gain, tm, chunk, n_chunks, act_dtype, emit_v32):
    outs = odd_in(h, w_in_b, v_gain, tm, act_dtype, emit_v32)
    u, v, g = outs[:3]
    y = spatial_gate(u, v, g, w_s[:, :chunk, :chunk], b_s[:, :chunk].T, chunk, n_chunks)
    (y_out,) = out_proj_norm([y], [w_out_b], x, final_gain, tm, False, F32)
    return y_out, (outs[3] if emit_v32 else None)


def kernel(x_prompt, x_sample, state_a_C, state_a_n, state_a_m, cache_b_k, cache_b_v, page_table,
           even_norm, even_w_in, even_b_i, even_b_f, even_b_sb, even_w_out,
           odd_norm, odd_w_in, odd_v_gain, odd_w_s, odd_b_s, odd_w_out, final_norm):
    bsz, seq, _ = x_prompt.shape
    n_seq, dec_seq, _ = x_sample.shape
    n_pool = cache_b_k.shape[1]

    ew = _even_weights(even_w_in[0], even_b_i[0], even_b_f[0])
    odd_w_in_b = odd_w_in[0].astype(BF16)
    odd_w_out_b = odd_w_out[0].astype(BF16)

    xp = x_prompt.reshape(bsz * seq, D_MODEL)
    zero_state = (jnp.zeros((bsz, A_HEADS, A_DH, A_DH), F32),
                  jnp.zeros((bsz, A_HEADS, 1, A_DH), F32),
                  jnp.zeros((bsz, A_HEADS, 1, GATE_LANES), F32))
    xp1, hp1, st_p, k_p, v_p = _even_layer(
        xp, bsz, seq, seq, ew, even_norm[0], even_w_out[0], even_b_sb[0], zero_state, None,
        odd_norm[0], 1024)
    y_p, _ = _odd_layer(xp1, hp1, odd_w_in_b, odd_v_gain[0], odd_w_s[0], odd_b_s[0], odd_w_out_b,
                        final_norm, 512, C_CHUNK, 4, BF16, False)

    xs = jnp.pad(x_sample, ((0, 0), (0, SAMPLE_PAD - dec_seq), (0, 0))).reshape(n_seq * SAMPLE_PAD, D_MODEL)
    st_in = (state_a_C[0], state_a_n[0][:, :, None, :],
             jnp.broadcast_to(state_a_m[0][:, :, None, None], (n_seq, A_HEADS, 1, GATE_LANES)))
    paged = (cache_b_k[0].reshape(n_pool, PAGE_SIZE, B_WIDTH),
             cache_b_v[0].reshape(n_pool, PAGE_SIZE, B_WIDTH), page_table)
    m_s = n_seq * SAMPLE_PAD
    xs1, hs1, st_s, k_s, v_s = _even_layer(
        xs, n_seq, SAMPLE_PAD, dec_seq, ew, even_norm[0], even_w_out[0], even_b_sb[0], st_in, paged,
        odd_norm[0], m_s)
    y_s, v_rows = _odd_layer(xs1, hs1, odd_w_in_b, odd_v_gain[0], odd_w_s[0], odd_b_s[0], odd_w_out_b,
                             final_norm, m_s, SAMPLE_PAD, n_seq, F32, True)

    def sample_rows(a, width):
        return a.reshape(n_seq, SAMPLE_PAD, width)[:, :dec_seq]

    return (y_p.reshape(bsz, seq, D_MODEL),
            sample_rows(y_s, D_MODEL),
            st_p[0][None], st_p[1][None], st_p[2][None],
            st_s[0][None], st_s[1][None], st_s[2][None],
            k_p.reshape(1, bsz, seq, B_HEADS, B_DH), v_p.reshape(1, bsz, seq, B_HEADS, B_DH),
            sample_rows(k_s, B_WIDTH).reshape(1, n_seq, dec_seq, B_HEADS, B_DH),
            sample_rows(v_s, B_WIDTH).reshape(1, n_seq, dec_seq, B_HEADS, B_DH),
            sample_rows(v_rows, C_WIDTH)[None])
```

```python
import functools

import jax
import jax.numpy as jnp
from jax import lax
from jax.experimental import pallas as pl
from jax.experimental.pallas import tpu as pltpu

F32 = jnp.float32
BF16 = jnp.bfloat16

D_MODEL = 2048
PAGE_SIZE = 128
A_HEADS = 4
A_DH = 256
A_WIDTH = A_HEADS * A_DH
A_CHUNK = 128
B_HEADS = 8
B_DH = 128
B_WIDTH = B_HEADS * B_DH
C_WIDTH = D_MODEL
C_GROUPS = 8
C_GDIM = C_WIDTH // C_GROUPS
C_CHUNK = 128
RMS_EPS = 1e-6
GATE_LANES = 128
NEG_BIG = -1e30
SAMPLE_PAD = 8
ATTN_BQ = 512
ATTN_BK = 256
ATTN_HEADS_PER_STEP = 2
PAGES_PER_STEP = 8

VMEM_LIMIT_BYTES = 56 * 1024 * 1024


def _params(*sem, flags=None):
    return pltpu.CompilerParams(dimension_semantics=sem, vmem_limit_bytes=VMEM_LIMIT_BYTES, flags=flags)


def _dot(a, b):
    return jnp.dot(a, b, preferred_element_type=F32)


def _dot_nt(a, b):
    return lax.dot_general(a, b, (((1,), (1,)), ((), ())), preferred_element_type=F32)


def _dot_tn(a, b):
    return lax.dot_general(a, b, (((0,), (0,)), ((), ())), preferred_element_type=F32)


def _softplus(z):
    return jnp.maximum(z, 0.0) + jnp.log(1.0 + jnp.exp(-jnp.abs(z)))


def _sigmoid(z):
    return 1.0 / (1.0 + jnp.exp(-z))


def _silu(z):
    return z * _sigmoid(z)


def _gelu_tanh(x):
    c = 0.7978845608028654
    return x * (0.5 * (1.0 + jnp.tanh(c * (x + 0.044715 * (x * x * x)))))


def _rms(x, g):
    return x * lax.rsqrt(jnp.mean(x * x, axis=-1, keepdims=True) + RMS_EPS) * g


def _split_hi_lo(x):
    hi = x.astype(BF16)
    lo = (x - hi.astype(F32)).astype(BF16)
    return hi, lo


def _norm_gates_kernel(x_ref, g_ref, whi_ref, wlo_ref, bias_ref, h_ref, gate_ref):
    h = _rms(x_ref[...], g_ref[...])
    h_hi, h_lo = _split_hi_lo(h)
    h_ref[...] = h_hi
    pre = (_dot(h_hi, whi_ref[...]) + _dot(h_hi, wlo_ref[...]) + _dot(h_lo, whi_ref[...])
           + bias_ref[...])
    lane = lax.broadcasted_iota(jnp.int32, pre.shape, 1)
    is_forget = (lane >= A_HEADS) & (lane < 2 * A_HEADS)
    gate_ref[...] = jnp.where(is_forget, -_softplus(-pre), pre)


def norm_gates(x, gain, w_gate, bias, tm):
    m = x.shape[0]
    whi, wlo = _split_hi_lo(w_gate)
    return pl.pallas_call(
        _norm_gates_kernel,
        grid=(m // tm,),
        in_specs=[pl.BlockSpec((tm, D_MODEL), lambda i: (i, 0)),
                  pl.BlockSpec((1, D_MODEL), lambda i: (0, 0)),
                  pl.BlockSpec((D_MODEL, GATE_LANES), lambda i: (0, 0)),
                  pl.BlockSpec((D_MODEL, GATE_LANES), lambda i: (0, 0)),
                  pl.BlockSpec((1, GATE_LANES), lambda i: (0, 0))],
        out_specs=[pl.BlockSpec((tm, D_MODEL), lambda i: (i, 0)),
                   pl.BlockSpec((tm, GATE_LANES), lambda i: (i, 0))],
        out_shape=[jax.ShapeDtypeStruct((m, D_MODEL), BF16),
                   jax.ShapeDtypeStruct((m, GATE_LANES), F32)],
        compiler_params=_params("arbitrary"),
        name="norm_gates",
    )(x, gain.reshape(1, D_MODEL), whi, wlo, bias)


def _proj_kernel(a_ref, wt_ref, o_ref, wbf):
    @pl.when(pl.program_id(1) == 0)
    def _():
        wbf[...] = wt_ref[...].astype(BF16)

    o_ref[...] = _dot_nt(a_ref[...], wbf[...]).astype(o_ref.dtype)


def proj(a, wt, row0, n_out, out_dtype, tm, tn):
    m, k = a.shape
    j0 = row0 // tn
    return pl.pallas_call(
        _proj_kernel,
        grid=(n_out // tn, m // tm),
        in_specs=[pl.BlockSpec((tm, k), lambda j, i: (i, 0)),
                  pl.BlockSpec((tn, k), lambda j, i: (j0 + j, 0))],
        out_specs=pl.BlockSpec((tm, tn), lambda j, i: (i, j)),
        out_shape=jax.ShapeDtypeStruct((m, n_out), out_dtype),
        scratch_shapes=[pltpu.VMEM((tn, k), BF16)],
        compiler_params=_params("arbitrary", "arbitrary"),
        name="proj",
    )(a, wt)


def _kv_proj_kernel(a_ref, wt_ref, o_ref, obf_ref, wbf):
    @pl.when(pl.program_id(0) == 0)
    def _():
        wbf[...] = wt_ref[...].astype(BF16)

    y = _dot_nt(a_ref[...], wbf[...])
    obf_ref[...] = y.astype(BF16)
    tm = y.shape[0]
    for h in range(B_HEADS):
        o_ref[pl.ds(h, tm, stride=B_HEADS), :] = y[:, h * B_DH:(h + 1) * B_DH]


def kv_proj(a, wt, row0, tm):
    m, k = a.shape
    j0 = row0 // B_WIDTH
    return pl.pallas_call(
        _kv_proj_kernel,
        grid=(m // tm,),
        in_specs=[pl.BlockSpec((tm, k), lambda i: (i, 0)),
                  pl.BlockSpec((B_WIDTH, k), lambda i: (j0, 0))],
        out_specs=[pl.BlockSpec((tm * B_HEADS, B_DH), lambda i: (i, 0)),
                   pl.BlockSpec((tm, B_WIDTH), lambda i: (i, 0))],
        out_shape=[jax.ShapeDtypeStruct((m * B_HEADS, B_DH), F32),
                   jax.ShapeDtypeStruct((m, B_WIDTH), BF16)],
        scratch_shapes=[pltpu.VMEM((B_WIDTH, k), BF16)],
        compiler_params=_params("arbitrary"),
        name="kv_proj",
    )(a, wt)


def _out_proj_kernel(*refs, n_lhs, emit_x):
    a_refs = refs[:n_lhs]
    w_refs = refs[n_lhs:2 * n_lhs]
    x_ref, g_ref = refs[2 * n_lhs], refs[2 * n_lhs + 1]
    out_refs = refs[2 * n_lhs + 2:]
    tm = x_ref.shape[0]
    row_chunk = min(tm, 256)
    for r in range(tm // row_chunk):
        rows = slice(r * row_chunk, (r + 1) * row_chunk)
        y = x_ref[rows, :]
        for a_ref, w_ref in zip(a_refs, w_refs):
            y = y + _dot(a_ref[rows, :].astype(BF16), w_ref[...])
        if emit_x:
            out_refs[0][rows, :] = y
        out_refs[-1][rows, :] = _rms(y, g_ref[...]).astype(out_refs[-1].dtype)


def out_proj_norm(lhs, ws, x, gain, tm, emit_x, norm_dtype):
    m = x.shape[0]
    n_lhs = len(lhs)
    in_specs = ([pl.BlockSpec((tm, a.shape[1]), lambda i: (i, 0)) for a in lhs]
                + [pl.BlockSpec(w.shape, lambda i: (0, 0)) for w in ws]
                + [pl.BlockSpec((tm, D_MODEL), lambda i: (i, 0)),
                   pl.BlockSpec((1, D_MODEL), lambda i: (0, 0))])
    out_specs = [pl.BlockSpec((tm, D_MODEL), lambda i: (i, 0))]
    out_shape = [jax.ShapeDtypeStruct((m, D_MODEL), norm_dtype)]
    if emit_x:
        out_specs = [pl.BlockSpec((tm, D_MODEL), lambda i: (i, 0))] + out_specs
        out_shape = [jax.ShapeDtypeStruct((m, D_MODEL), F32)] + out_shape
    return pl.pallas_call(
        functools.partial(_out_proj_kernel, n_lhs=n_lhs, emit_x=emit_x),
        grid=(m // tm,),
        in_specs=in_specs,
        out_specs=out_specs,
        out_shape=out_shape,
        compiler_params=_params("arbitrary"),
        name="out_proj_norm",
    )(*lhs, *ws, x, gain.reshape(1, D_MODEL))


def _mlstm_kernel(q_ref, k_ref, v_ref, og_ref, gg_ref, gcol_ref, grow_ref, c0_ref, n0_ref, m0_ref,
                  h_ref, c_out_ref, n_out_ref, m_out_ref, c_sc, n_sc, m_sc, *, chunk):
    head = pl.program_id(1)
    ci = pl.program_id(2)
    L = chunk

    @pl.when(ci == 0)
    def _():
        c_sc[...] = c0_ref[0, 0]
        n_sc[...] = n0_ref[0, 0]
        m_sc[...] = m0_ref[0, 0]

    q = q_ref[0]
    ks = k_ref[0] * jnp.asarray(A_DH ** -0.5, BF16)
    v = v_ref[0]
    gcol = gcol_ref[0]
    grow = grow_ref[0]
    lane = lax.broadcasted_iota(jnp.int32, gcol.shape, 1)
    sub = lax.broadcasted_iota(jnp.int32, grow.shape, 0)
    ig_col = jnp.sum(jnp.where(lane == head, gcol, 0.0), axis=1, keepdims=True)
    lf_col = jnp.sum(jnp.where(lane == head + A_HEADS, gcol, 0.0), axis=1, keepdims=True)
    ig_row = jnp.sum(jnp.where(sub == head, grow, 0.0), axis=0, keepdims=True)
    lf_row = jnp.sum(jnp.where(sub == head + A_HEADS, grow, 0.0), axis=0, keepdims=True)

    tt = lax.broadcasted_iota(jnp.int32, (L, L), 0)
    ss = lax.broadcasted_iota(jnp.int32, (L, L), 1)
    causal = ss <= tt
    b_col = jnp.sum(jnp.where(causal, lf_row, 0.0), axis=1, keepdims=True)
    b_row = jnp.sum(jnp.where(tt <= ss, lf_col, 0.0), axis=0, keepdims=True)
    b_last = jnp.sum(lf_row, axis=1, keepdims=True)

    m0 = m_sc[...][:, :1]
    n0 = n_sc[...]
    c0 = c_sc[...]

    d = jnp.where(causal, b_col - b_row + ig_row, NEG_BIG)
    m_carry = b_col + m0
    m = jnp.maximum(m_carry, jnp.max(d, axis=1, keepdims=True))
    w_intra = jnp.exp(d - m)
    w_carry = jnp.exp(m_carry - m)
    s = _dot_nt(q, ks) * w_intra
    qf = q.astype(F32)
    num = _dot(s.astype(BF16), v) + w_carry * _dot_nt(q, c0.astype(BF16))
    den = jnp.sum(s, axis=1, keepdims=True) + w_carry * jnp.sum(qf * n0, axis=1, keepdims=True)
    h = num / jnp.maximum(jnp.abs(den), jnp.exp(-m))
    gated = h * _sigmoid(og_ref[0].astype(F32)) * _silu(gg_ref[0].astype(F32))
    h_ref[0] = gated.astype(h_ref.dtype)

    m_carry_last = b_last + m0
    d_last_row = b_last - b_row + ig_row
    m_new = jnp.maximum(m_carry_last, jnp.max(d_last_row, axis=1, keepdims=True))
    wc_last = jnp.exp(m_carry_last - m_new)
    w_last_col = jnp.exp(b_last - b_col + ig_col - m_new)
    vw = (v.astype(F32) * w_last_col).astype(BF16)
    c_new = wc_last * c0 + _dot_tn(vw, ks)
    n_new = wc_last * n0 + jnp.sum(ks.astype(F32) * w_last_col, axis=0, keepdims=True)
    c_sc[...] = c_new
    n_sc[...] = n_new
    m_sc[...] = jnp.broadcast_to(m_new, m_sc.shape)

    @pl.when(ci == pl.num_programs(2) - 1)
    def _():
        c_out_ref[0, 0] = c_new
        n_out_ref[0, 0] = n_new
        m_out_ref[0, 0] = jnp.broadcast_to(m_new, m_sc.shape)


def mlstm(qkvog, gcol, grow, c0, n0, m0, chunk):
    bsz, t_len, _ = qkvog.shape
    nc = t_len // chunk
    hd = A_HEADS
    blk = lambda off: pl.BlockSpec((1, chunk, A_DH), lambda b, h, c, off=off: (b, c, off + h))
    st4 = lambda r, w: pl.BlockSpec((1, 1, r, w), lambda b, h, c: (b, h, 0, 0))
    return pl.pallas_call(
        functools.partial(_mlstm_kernel, chunk=chunk),
        grid=(bsz, hd, nc),
        in_specs=[blk(0), blk(hd), blk(2 * hd), blk(3 * hd), blk(4 * hd),
                  pl.BlockSpec((1, chunk, GATE_LANES), lambda b, h, c: (b, c, 0)),
                  pl.BlockSpec((1, 8, chunk), lambda b, h, c: (b, 0, c)),
                  st4(A_DH, A_DH), st4(1, A_DH), st4(1, GATE_LANES)],
        out_specs=[pl.BlockSpec((1, chunk, A_DH), lambda b, h, c: (b, c, h)),
                   st4(A_DH, A_DH), st4(1, A_DH), st4(1, GATE_LANES)],
        out_shape=[jax.ShapeDtypeStruct((bsz, t_len, A_WIDTH), BF16),
                   jax.ShapeDtypeStruct((bsz, hd, A_DH, A_DH), F32),
                   jax.ShapeDtypeStruct((bsz, hd, 1, A_DH), F32),
                   jax.ShapeDtypeStruct((bsz, hd, 1, GATE_LANES), F32)],
        scratch_shapes=[pltpu.VMEM((A_DH, A_DH), F32),
                        pltpu.VMEM((1, A_DH), F32),
                        pltpu.VMEM((1, GATE_LANES), F32)],
        compiler_params=_params("arbitrary", "arbitrary", "arbitrary"),
        name="mlstm",
    )(qkvog, qkvog, qkvog, qkvog, qkvog, gcol, grow, c0, n0, m0)


def _stick_block(q, kb, vb, bias, run, mask, upper):
    rows = q.shape[0]
    z = _dot_nt(q, kb) * (B_DH ** -0.5) + bias
    sp = _softplus(z)
    spm = sp if mask is None else jnp.where(mask, sp, 0.0)
    hi, lo = _split_hi_lo(spm)
    both = _dot(jnp.concatenate([hi, lo], axis=0), upper)
    later = both[:rows] + both[rows:] + run
    a = jnp.exp(z - sp - later)
    if mask is not None:
        a = jnp.where(mask, a, 0.0)
    return _dot(a.astype(BF16), vb), jnp.sum(spm, axis=1, keepdims=True)


def _strict_upper(n):
    j = lax.broadcasted_iota(jnp.int32, (n, n), 0)
    s = lax.broadcasted_iota(jnp.int32, (n, n), 1)
    return jnp.where(j > s, 1.0, 0.0).astype(BF16)


def _attn_prompt_kernel(bias_ref, q_ref, k_ref, v_ref, g_ref, o_ref, *, bq, bk, n_heads):
    head0 = pl.program_id(1) * n_heads
    qi = pl.program_id(2)
    kbf = k_ref.at[0]
    vbf = v_ref.at[0]
    upper = _strict_upper(bk)
    row = lax.broadcasted_iota(jnp.int32, (bq, bk), 0)
    col = lax.broadcasted_iota(jnp.int32, (bq, bk), 1)
    lanes = [slice(h * B_DH, (h + 1) * B_DH) for h in range(n_heads)]
    qs = [q_ref[0, :, ln] for ln in lanes]
    biases = [bias_ref[head0 + h] for h in range(n_heads)]

    def blocks(kj, runs, mask):
        start = pl.multiple_of(kj * bk, bk)
        return [_stick_block(qs[h], kbf[pl.ds(start, bk), ln], vbf[pl.ds(start, bk), ln],
                             biases[h], runs[h], mask, upper) for h, ln in enumerate(lanes)]

    q0 = qi * bq
    n_full = q0 // bk
    accs = [jnp.zeros((bq, B_DH), F32)] * n_heads
    runs = [jnp.zeros((bq, 1), F32)] * n_heads
    for m in reversed(range(max(1, bq // bk))):
        kj = n_full + m
        res = blocks(kj, runs, col + (kj * bk - q0) < row)
        accs = [a + c for a, (c, _) in zip(accs, res)]
        runs = [r + t for r, (_, t) in zip(runs, res)]

    def body(it, carry):
        accs, runs = carry
        res = blocks(n_full - 1 - it, runs, None)
        return (tuple(a + c for a, (c, _) in zip(accs, res)),
                tuple(r + t for r, (_, t) in zip(runs, res)))

    accs, runs = lax.fori_loop(0, n_full, body, (tuple(accs), tuple(runs)))
    for h, ln in enumerate(lanes):
        o_ref[0, :, ln] = (accs[h] * _silu(g_ref[0, :, ln].astype(F32))).astype(o_ref.dtype)


def attn_prompt(q, g, k, v, b_sb, bq, bk, n_heads):
    bsz, t_len, _ = k.shape
    width = n_heads * B_DH
    q_spec = pl.BlockSpec((1, bq, width), lambda b, h, i: (b, i, h))
    kv_spec = pl.BlockSpec((1, t_len, width), lambda b, h, i: (b, 0, h))
    return pl.pallas_call(
        functools.partial(_attn_prompt_kernel, bq=bq, bk=bk, n_heads=n_heads),
        grid=(bsz, B_HEADS // n_heads, t_len // bq),
        in_specs=[pl.BlockSpec(memory_space=pltpu.SMEM), q_spec, kv_spec, kv_spec, q_spec],
        out_specs=q_spec,
        out_shape=jax.ShapeDtypeStruct((bsz, t_len, B_WIDTH), BF16),
        compiler_params=_params("arbitrary", "arbitrary", "arbitrary"),
        name="attn_prompt",
    )(b_sb, q, k, v, g)


def _attn_sample_kernel(pt_ref, bias_ref, q_ref, g_ref, knew_ref, vnew_ref, *refs, n_group):
    del pt_ref
    k_refs = refs[:n_group]
    v_refs = refs[n_group:2 * n_group]
    o_ref, qbd, acc, run, kcat, vcat = refs[2 * n_group:]
    p = pl.program_id(1)
    rows = B_HEADS * SAMPLE_PAD
    upper = _strict_upper(PAGE_SIZE)
    bias = bias_ref[...][:, :1]

    def repack(page, dst, i):
        for h in range(B_HEADS):
            dst[i * PAGE_SIZE:(i + 1) * PAGE_SIZE, h * B_DH:(h + 1) * B_DH] = page(h).astype(BF16)

    def step(n_blk, mask):
        n_keys = n_blk * PAGE_SIZE
        z = _dot_nt(qbd[...], kcat[:n_keys, :]) * (B_DH ** -0.5) + bias
        sp = _softplus(z)
        spm = sp if mask is None else jnp.where(mask, sp, 0.0)
        hi, lo = _split_hi_lo(spm)
        carry = run[...][:, :1]
        laters = []
        for i in range(n_blk):
            ln = slice(i * PAGE_SIZE, (i + 1) * PAGE_SIZE)
            both = _dot(jnp.concatenate([hi[:, ln], lo[:, ln]], axis=0), upper)
            laters.append(both[:rows] + both[rows:] + carry)
            carry = carry + jnp.sum(spm[:, ln], axis=1, keepdims=True)
        later = laters[0] if n_blk == 1 else jnp.concatenate(laters, axis=1)
        a = jnp.exp(z - sp - later)
        if mask is not None:
            a = jnp.where(mask, a, 0.0)
        acc[...] += _dot(a.astype(BF16), vcat[:n_keys, :])
        run[...] = jnp.broadcast_to(carry, run.shape)

    @pl.when(p == 0)
    def _():
        r = lax.broadcasted_iota(jnp.int32, (rows, B_WIDTH), 0)
        c = lax.broadcasted_iota(jnp.int32, (rows, B_WIDTH), 1)
        q_rep = jnp.concatenate([q_ref[0].astype(F32)] * B_HEADS, axis=0)
        qbd[...] = jnp.where((r // SAMPLE_PAD) == (c // B_DH), q_rep, 0.0).astype(BF16)
        acc[...] = jnp.zeros_like(acc)
        run[...] = jnp.zeros_like(run)
        repack(lambda h: knew_ref[0, pl.ds(h, PAGE_SIZE, stride=B_HEADS), :], kcat, 0)
        repack(lambda h: vnew_ref[0, pl.ds(h, PAGE_SIZE, stride=B_HEADS), :], vcat, 0)
        t = lax.broadcasted_iota(jnp.int32, (rows, PAGE_SIZE), 0) % SAMPLE_PAD
        s = lax.broadcasted_iota(jnp.int32, (rows, PAGE_SIZE), 1)
        step(1, s < t)

    for i in range(n_group):
        repack(lambda h, r=k_refs[i]: r[0, 0, pl.ds(h, PAGE_SIZE, stride=B_HEADS), :], kcat, i)
        repack(lambda h, r=v_refs[i]: r[0, 0, pl.ds(h, PAGE_SIZE, stride=B_HEADS), :], vcat, i)
    step(n_group, None)

    @pl.when(p == pl.num_programs(1) - 1)
    def _():
        a = acc[...]
        c = lax.broadcasted_iota(jnp.int32, (SAMPLE_PAD, B_WIDTH), 1) // B_DH
        out = jnp.zeros((SAMPLE_PAD, B_WIDTH), F32)
        for h in range(B_HEADS):
            out = out + jnp.where(c == h, a[h * SAMPLE_PAD:(h + 1) * SAMPLE_PAD, :], 0.0)
        o_ref[0] = (out * _silu(g_ref[0].astype(F32))).astype(o_ref.dtype)


def attn_sample(q, g, k_new, v_new, cache_k, cache_v, page_table, bias_rows, n_group):
    n_seq, n_pages = page_table.shape
    steps = n_pages // n_group
    rows = B_HEADS * SAMPLE_PAD
    page_rows = PAGE_SIZE * B_HEADS

    def page_spec(i):
        return pl.BlockSpec((1, 1, page_rows, B_DH),
                            lambda b, p, pt, i=i: (0, pt[b, n_pages - 1 - (p * n_group + i)], 0, 0))

    seq_spec = pl.BlockSpec((1, SAMPLE_PAD, B_WIDTH), lambda b, p, pt: (b, 0, 0))
    new_spec = pl.BlockSpec((1, page_rows, B_DH), lambda b, p, pt: (b, 0, 0))
    grid_spec = pltpu.PrefetchScalarGridSpec(
        num_scalar_prefetch=1,
        grid=(n_seq, steps),
        in_specs=[pl.BlockSpec((rows, GATE_LANES), lambda b, p, pt: (0, 0)),
                  seq_spec, seq_spec, new_spec, new_spec]
                 + [page_spec(i) for i in range(n_group)] * 2,
        out_specs=seq_spec,
        scratch_shapes=[pltpu.VMEM((rows, B_WIDTH), BF16),
                        pltpu.VMEM((rows, B_WIDTH), F32),
                        pltpu.VMEM((rows, GATE_LANES), F32),
                        pltpu.VMEM((n_group * PAGE_SIZE, B_WIDTH), BF16),
                        pltpu.VMEM((n_group * PAGE_SIZE, B_WIDTH), BF16)],
    )
    return pl.pallas_call(
        functools.partial(_attn_sample_kernel, n_group=n_group),
        grid_spec=grid_spec,
        out_shape=jax.ShapeDtypeStruct((n_seq, SAMPLE_PAD, B_WIDTH), F32),
        compiler_params=_params("arbitrary", "arbitrary"),
        name="attn_sample",
    )(page_table, bias_rows, q, g, k_new, v_new, *([cache_k] * n_group), *([cache_v] * n_group))


def _proj_act_kernel(a_ref, w_ref, vg_ref, o_ref, *maybe_f32, act, row_chunk):
    for r in range(a_ref.shape[0] // row_chunk):
        rows = slice(r * row_chunk, (r + 1) * row_chunk)
        y = _dot(a_ref[rows, :], w_ref[...])
        if act == "gelu":
            y = _gelu_tanh(y)
        elif act == "silu":
            y = _silu(y)
        else:
            y = _rms(_gelu_tanh(y), vg_ref[...])
        o_ref[rows, :] = y.astype(o_ref.dtype)
        if maybe_f32:
            maybe_f32[0][rows, :] = y


def proj_act(h, w, col0, v_gain, act, tm, tn, out_dtype, emit_f32=False):
    m = h.shape[0]
    j0 = col0 // tn
    out_spec = pl.BlockSpec((tm, tn), lambda j, i: (i, j))
    n_out = 2 if emit_f32 else 1
    return pl.pallas_call(
        functools.partial(_proj_act_kernel, act=act, row_chunk=min(tm, 256)),
        grid=(C_WIDTH // tn, m // tm),
        in_specs=[pl.BlockSpec((tm, D_MODEL), lambda j, i: (i, 0)),
                  pl.BlockSpec((D_MODEL, tn), lambda j, i: (0, j0 + j)),
                  pl.BlockSpec((1, tn), lambda j, i: (0, j))],
        out_specs=[out_spec] * n_out,
        out_shape=[jax.ShapeDtypeStruct((m, C_WIDTH), out_dtype)]
                  + ([jax.ShapeDtypeStruct((m, C_WIDTH), F32)] if emit_f32 else []),
        compiler_params=_params("arbitrary", "arbitrary"),
        name="proj_" + act,
    )(h, w, v_gain.reshape(1, C_WIDTH))


def odd_in(h, w, v_gain, tm, act_dtype, emit_v32):
    u = proj_act(h, w, 0, v_gain, "gelu", tm, 1024, act_dtype)[0]
    v = proj_act(h, w, C_WIDTH, v_gain, "gelu_rms", min(tm, 512), C_WIDTH, act_dtype, emit_v32)
    g = proj_act(h, w, 2 * C_WIDTH, v_gain, "silu", tm, 1024, act_dtype)[0]
    return [u, v[0], g] + ([v[1]] if emit_v32 else [])


def _spatial_kernel(u_ref, v_ref, g_ref, ws_ref, bs_ref, y_ref, *, chunk, n_chunks):
    tt = lax.broadcasted_iota(jnp.int32, (chunk, chunk), 0)
    ss = lax.broadcasted_iota(jnp.int32, (chunk, chunk), 1)
    causal = ss <= tt
    for grp in range(C_GROUPS):
        wm = jnp.where(causal, ws_ref[grp], 0.0)
        bcol = bs_ref[:, grp:grp + 1]
        cols = slice(grp * C_GDIM, (grp + 1) * C_GDIM)
        for c in range(n_chunks):
            rows = slice(c * chunk, (c + 1) * chunk)
            vv = v_ref[rows, cols]
            if chunk >= 128:
                sv = _dot(wm.astype(BF16), vv)
            else:
                vf = vv.astype(F32)
                sv = jnp.zeros((chunk, C_GDIM), F32)
                for s in range(chunk):
                    sv = sv + wm[:, s:s + 1] * vf[s:s + 1, :]
            sv = sv + bcol
            y = u_ref[rows, cols].astype(F32) * sv * g_ref[rows, cols].astype(F32)
            y_ref[rows, cols] = y.astype(y_ref.dtype)


def spatial_gate(u, v, g, w_s, b_s_t, chunk, n_chunks):
    m = u.shape[0]
    tm = chunk * n_chunks
    row_spec = pl.BlockSpec((tm, C_WIDTH), lambda i: (i, 0))
    return pl.pallas_call(
        functools.partial(_spatial_kernel, chunk=chunk, n_chunks=n_chunks),
        grid=(m // tm,),
        in_specs=[row_spec, row_spec, row_spec,
                  pl.BlockSpec((C_GROUPS, chunk, chunk), lambda i: (0, 0, 0)),
                  pl.BlockSpec((chunk, C_GROUPS), lambda i: (0, 0))],
        out_specs=row_spec,
        out_shape=jax.ShapeDtypeStruct((m, C_WIDTH), u.dtype),
        compiler_params=_params("arbitrary"),
        name="spatial_gate",
    )(u, v, g, w_s, b_s_t)


def _even_weights(w_in, b_i, b_f):
    gate0 = 5 * A_WIDTH
    b0 = gate0 + 2 * A_HEADS
    wt = jnp.swapaxes(w_in, 0, 1)
    w_gate = jnp.pad(w_in[:, gate0:b0], ((0, 0), (0, GATE_LANES - 2 * A_HEADS)))
    bias = jnp.pad(jnp.concatenate([b_i, b_f]), (0, GATE_LANES - 2 * A_HEADS)).reshape(1, GATE_LANES)
    return wt, wt[b0:], w_gate, bias


def _even_layer(x, bsz, t_len, valid_len, ew, g_norm, w_out, b_sb, state, paged, next_gain, tm):
    wt, wt_b, w_gate, bias = ew
    m_rows = bsz * t_len
    h, gates = norm_gates(x, g_norm, w_gate, bias, min(tm, 512))
    tn = 1024
    qkvog = proj(h, wt, 0, 5 * A_WIDTH, BF16, tm, tn).reshape(bsz, t_len, 5 * A_WIDTH)
    act_dtype = BF16 if paged is None else F32
    q_b = proj(h, wt_b, 0, B_WIDTH, act_dtype, tm, tn).reshape(bsz, t_len, B_WIDTH)
    g_b = proj(h, wt_b, 3 * B_WIDTH, B_WIDTH, act_dtype, tm, tn).reshape(bsz, t_len, B_WIDTH)
    k_new, k_bf = kv_proj(h, wt_b, B_WIDTH, min(tm, 512))
    v_new, v_bf = kv_proj(h, wt_b, 2 * B_WIDTH, min(tm, 512))

    gates = gates.reshape(bsz, t_len, GATE_LANES)
    t_pad = -(-t_len // A_CHUNK) * A_CHUNK
    pad = ((0, 0), (0, t_pad - t_len), (0, 0))
    if valid_len < t_pad:
        qkvog, gates = jnp.pad(qkvog, pad), jnp.pad(gates, pad)
        pos = jnp.arange(t_pad)[None, :, None]
        lane = jnp.arange(GATE_LANES)[None, None, :]
        gates = jnp.where((pos >= valid_len) & (lane < A_HEADS), NEG_BIG, gates)
        gates = jnp.where((pos >= valid_len) & (lane >= A_HEADS), 0.0, gates)
    grow = gates[:, :, :2 * A_HEADS].transpose(0, 2, 1)
    c0, n0, m0 = state
    h_a, c_new, n_new, m_new = mlstm(qkvog, gates, grow, c0, n0, m0, A_CHUNK)
    h_a = h_a[:, :t_len]

    if paged is None:
        h_b = attn_prompt(q_b, g_b, k_bf.reshape(bsz, t_len, B_WIDTH), v_bf.reshape(bsz, t_len, B_WIDTH),
                          b_sb, ATTN_BQ, ATTN_BK, ATTN_HEADS_PER_STEP)
    else:
        cache_k, cache_v, page_table = paged
        kv_pad = ((0, 0), (0, (PAGE_SIZE - t_len) * B_HEADS), (0, 0))
        bias_rows = jnp.broadcast_to(jnp.repeat(b_sb, SAMPLE_PAD)[:, None], (B_HEADS * SAMPLE_PAD, GATE_LANES))
        h_b = attn_sample(q_b, g_b,
                          jnp.pad(k_new.reshape(bsz, t_len * B_HEADS, B_DH), kv_pad),
                          jnp.pad(v_new.reshape(bsz, t_len * B_HEADS, B_DH), kv_pad),
                          cache_k, cache_v, page_table, bias_rows, PAGES_PER_STEP)

    w_out_b = w_out.astype(BF16)
    x_new, h_next = out_proj_norm(
        [h_a.reshape(m_rows, A_WIDTH), h_b.reshape(m_rows, B_WIDTH)],
        [w_out_b[:A_WIDTH], w_out_b[A_WIDTH:]], x, next_gain, min(tm, 512), True, BF16)
    return x_new, h_next, (c_new, n_new[:, :, 0, :], m_new[:, :, 0, 0]), k_new, v_new


def _odd_layer(x, h, w_in_b, v_gain, w_s, b_s, w_out_b, final_gain, tm, chunk, n_chunks, act_dtype, emit_v32):
    outs = odd_in(h, w_in_b, v_gain, min(2 * tm, h.shape[0]), act_dtype, emit_v32)
    u, v, g = outs[:3]
    y = spatial_gate(u, v, g, w_s[:, :chunk, :chunk], b_s[:, :chunk].T, chunk, n_chunks)
    (y_out,) = out_proj_norm([y], [w_out_b], x, final_gain, tm, False, F32)
    return y_out, (outs[3] if emit_v32 else None)


def kernel(x_prompt, x_sample, state_a_C, state_a_n, state_a_m, cache_b_k, cache_b_v, page_table,
           even_norm, even_w_in, even_b_i, even_b_f, even_b_sb, even_w_out,
           odd_norm, odd_w_in, odd_v_gain, odd_w_s, odd_b_s, odd_w_out, final_norm):
    bsz, seq, _ = x_prompt.shape
    n_seq, dec_seq, _ = x_sample.shape
    n_pool = cache_b_k.shape[1]

    ew = _even_weights(even_w_in[0], even_b_i[0], even_b_f[0])
    odd_w_in_b = odd_w_in[0].astype(BF16)
    odd_w_out_b = odd_w_out[0].astype(BF16)

    xp = x_prompt.reshape(bsz * seq, D_MODEL)
    zero_state = (jnp.zeros((bsz, A_HEADS, A_DH, A_DH), F32),
                  jnp.zeros((bsz, A_HEADS, 1, A_DH), F32),
                  jnp.zeros((bsz, A_HEADS, 1, GATE_LANES), F32))
    xp1, hp1, st_p, k_p, v_p = _even_layer(
        xp, bsz, seq, seq, ew, even_norm[0], even_w_out[0], even_b_sb[0], zero_state, None,
        odd_norm[0], 1024)
    y_p, _ = _odd_layer(xp1, hp1, odd_w_in_b, odd_v_gain[0], odd_w_s[0], odd_b_s[0], odd_w_out_b,
                        final_norm, 512, C_CHUNK, 4, BF16, False)

    xs = jnp.pad(x_sample, ((0, 0), (0, SAMPLE_PAD - dec_seq), (0, 0))).reshape(n_seq * SAMPLE_PAD, D_MODEL)
    st_in = (state_a_C[0], state_a_n[0][:, :, None, :],
             jnp.broadcast_to(state_a_m[0][:, :, None, None], (n_seq, A_HEADS, 1, GATE_LANES)))
    assert cache_b_k.shape[0] == 1 and cache_b_v.shape[0] == 1
    page_view = (1, n_pool, PAGE_SIZE * B_HEADS, B_DH)
    paged = (cache_b_k.reshape(page_view), cache_b_v.reshape(page_view), page_table)
    m_s = n_seq * SAMPLE_PAD
    xs1, hs1, st_s, k_s, v_s = _even_layer(
        xs, n_seq, SAMPLE_PAD, dec_seq, ew, even_norm[0], even_w_out[0], even_b_sb[0], st_in, paged,
        odd_norm[0], m_s)
    y_s, v_rows = _odd_layer(xs1, hs1, odd_w_in_b, odd_v_gain[0], odd_w_s[0], odd_b_s[0], odd_w_out_b,
                             final_norm, m_s, SAMPLE_PAD, n_seq, F32, True)

    def sample_rows(a, *dims):
        return a.reshape((n_seq, SAMPLE_PAD) + dims)[:, :dec_seq]

    return (y_p.reshape(bsz, seq, D_MODEL),
            sample_rows(y_s, D_MODEL),
            st_p[0][None], st_p[1][None], st_p[2][None],
            st_s[0][None], st_s[1][None], st_s[2][None],
            k_p.reshape(1, bsz, seq, B_HEADS, B_DH), v_p.reshape(1, bsz, seq, B_HEADS, B_DH),
            sample_rows(k_s, B_HEADS, B_DH)[None], sample_rows(v_s, B_HEADS, B_DH)[None],
            sample_rows(v_rows, C_WIDTH)[None])
```

```python
import functools

import jax
import jax.numpy as jnp
from jax import lax
from jax.experimental import pallas as pl
from jax.experimental.pallas import tpu as pltpu

F32 = jnp.float32
BF16 = jnp.bfloat16

D_MODEL = 2048
PAGE_SIZE = 128
A_HEADS = 4
A_DH = 256
A_WIDTH = A_HEADS * A_DH
A_CHUNK = 128
B_HEADS = 8
B_DH = 128
B_WIDTH = B_HEADS * B_DH
C_WIDTH = D_MODEL
C_GROUPS = 8
C_GDIM = C_WIDTH // C_GROUPS
C_CHUNK = 128
RMS_EPS = 1e-6
GATE_LANES = 128
NEG_BIG = -1e30
SAMPLE_PAD = 8
ATTN_BQ = 512
ATTN_BK = 512
CUMSUM_BLOCK = 256
ATTN_HEADS_PER_STEP = 2
PAGES_PER_STEP = 16

VMEM_LIMIT_BYTES = 56 * 1024 * 1024


def _params(*sem, flags=None):
    return pltpu.CompilerParams(dimension_semantics=sem, vmem_limit_bytes=VMEM_LIMIT_BYTES, flags=flags)


def _dot(a, b):
    return jnp.dot(a, b, preferred_element_type=F32)


def _dot_nt(a, b):
    return lax.dot_general(a, b, (((1,), (1,)), ((), ())), preferred_element_type=F32)


def _dot_tn(a, b):
    return lax.dot_general(a, b, (((0,), (0,)), ((), ())), preferred_element_type=F32)


def _softplus(z):
    return jnp.maximum(z, 0.0) + jnp.log(1.0 + jnp.exp(-jnp.abs(z)))


def _sigmoid(z):
    return 1.0 / (1.0 + jnp.exp(-z))


def _silu(z):
    return z * _sigmoid(z)


def _gelu_tanh(x):
    c = 0.7978845608028654
    return x * (0.5 * (1.0 + jnp.tanh(c * (x + 0.044715 * (x * x * x)))))


def _rms(x, g):
    return x * lax.rsqrt(jnp.mean(x * x, axis=-1, keepdims=True) + RMS_EPS) * g


def _split_hi_lo(x):
    hi = x.astype(BF16)
    lo = (x - hi.astype(F32)).astype(BF16)
    return hi, lo


def _norm_gates_kernel(x_ref, g_ref, whi_ref, wlo_ref, bias_ref, h_ref, gate_ref):
    h = _rms(x_ref[...], g_ref[...])
    h_hi, h_lo = _split_hi_lo(h)
    h_ref[...] = h_hi
    pre = (_dot(h_hi, whi_ref[...]) + _dot(h_hi, wlo_ref[...]) + _dot(h_lo, whi_ref[...])
           + bias_ref[...])
    lane = lax.broadcasted_iota(jnp.int32, pre.shape, 1)
    is_forget = (lane >= A_HEADS) & (lane < 2 * A_HEADS)
    gate_ref[...] = jnp.where(is_forget, -_softplus(-pre), pre)


def norm_gates(x, gain, w_gate, bias, tm):
    m = x.shape[0]
    whi, wlo = _split_hi_lo(w_gate)
    return pl.pallas_call(
        _norm_gates_kernel,
        grid=(m // tm,),
        in_specs=[pl.BlockSpec((tm, D_MODEL), lambda i: (i, 0)),
                  pl.BlockSpec((1, D_MODEL), lambda i: (0, 0)),
                  pl.BlockSpec((D_MODEL, GATE_LANES), lambda i: (0, 0)),
                  pl.BlockSpec((D_MODEL, GATE_LANES), lambda i: (0, 0)),
                  pl.BlockSpec((1, GATE_LANES), lambda i: (0, 0))],
        out_specs=[pl.BlockSpec((tm, D_MODEL), lambda i: (i, 0)),
                   pl.BlockSpec((tm, GATE_LANES), lambda i: (i, 0))],
        out_shape=[jax.ShapeDtypeStruct((m, D_MODEL), BF16),
                   jax.ShapeDtypeStruct((m, GATE_LANES), F32)],
        compiler_params=_params("arbitrary"),
        name="norm_gates",
    )(x, gain.reshape(1, D_MODEL), whi, wlo, bias)


def _proj_kernel(a_ref, wt_ref, o_ref, wbf):
    @pl.when(pl.program_id(1) == 0)
    def _():
        wbf[...] = wt_ref[...].astype(BF16)

    o_ref[...] = _dot_nt(a_ref[...], wbf[...]).astype(o_ref.dtype)


def proj(a, wt, row0, n_out, out_dtype, tm, tn):
    m, k = a.shape
    return pl.pallas_call(
        _proj_kernel,
        grid=(n_out // tn, m // tm),
        in_specs=[pl.BlockSpec((tm, k), lambda j, i: (i, 0)),
                  pl.BlockSpec((pl.Element(tn), pl.Element(k)),
                               lambda j, i: (pl.multiple_of(row0 + j * tn, 8), 0))],
        out_specs=pl.BlockSpec((tm, tn), lambda j, i: (i, j)),
        out_shape=jax.ShapeDtypeStruct((m, n_out), out_dtype),
        scratch_shapes=[pltpu.VMEM((tn, k), BF16)],
        compiler_params=_params("arbitrary", "arbitrary"),
        name="proj",
    )(a, wt)


def _kv_proj_kernel(a_ref, wt_ref, o_ref, obf_ref, wbf):
    @pl.when(pl.program_id(0) == 0)
    def _():
        wbf[...] = wt_ref[...].astype(BF16)

    y = _dot_nt(a_ref[...], wbf[...])
    obf_ref[...] = y.astype(BF16)
    tm = y.shape[0]
    for h in range(B_HEADS):
        o_ref[pl.ds(h, tm, stride=B_HEADS), :] = y[:, h * B_DH:(h + 1) * B_DH]


def kv_proj(a, wt, row0, tm):
    m, k = a.shape
    return pl.pallas_call(
        _kv_proj_kernel,
        grid=(m // tm,),
        in_specs=[pl.BlockSpec((tm, k), lambda i: (i, 0)),
                  pl.BlockSpec((pl.Element(B_WIDTH), pl.Element(k)), lambda i: (row0, 0))],
        out_specs=[pl.BlockSpec((tm * B_HEADS, B_DH), lambda i: (i, 0)),
                   pl.BlockSpec((tm, B_WIDTH), lambda i: (i, 0))],
        out_shape=[jax.ShapeDtypeStruct((m * B_HEADS, B_DH), F32),
                   jax.ShapeDtypeStruct((m, B_WIDTH), BF16)],
        scratch_shapes=[pltpu.VMEM((B_WIDTH, k), BF16)],
        compiler_params=_params("arbitrary"),
        name="kv_proj",
    )(a, wt)


def _out_proj_kernel(*refs, n_lhs, emit_x):
    a_refs = refs[:n_lhs]
    w_refs = refs[n_lhs:2 * n_lhs]
    x_ref, g_ref = refs[2 * n_lhs], refs[2 * n_lhs + 1]
    out_refs = refs[2 * n_lhs + 2:]
    tm = x_ref.shape[0]
    row_chunk = min(tm, 256)
    for r in range(tm // row_chunk):
        rows = slice(r * row_chunk, (r + 1) * row_chunk)
        y = x_ref[rows, :]
        for a_ref, w_ref in zip(a_refs, w_refs):
            y = y + _dot(a_ref[rows, :].astype(BF16), w_ref[...])
        if emit_x:
            out_refs[0][rows, :] = y
        out_refs[-1][rows, :] = _rms(y, g_ref[...]).astype(out_refs[-1].dtype)


def out_proj_norm(lhs, ws, x, gain, tm, emit_x, norm_dtype):
    m = x.shape[0]
    n_lhs = len(lhs)
    in_specs = ([pl.BlockSpec((tm, a.shape[1]), lambda i: (i, 0)) for a in lhs]
                + [pl.BlockSpec(w.shape, lambda i: (0, 0)) for w in ws]
                + [pl.BlockSpec((tm, D_MODEL), lambda i: (i, 0)),
                   pl.BlockSpec((1, D_MODEL), lambda i: (0, 0))])
    out_specs = [pl.BlockSpec((tm, D_MODEL), lambda i: (i, 0))]
    out_shape = [jax.ShapeDtypeStruct((m, D_MODEL), norm_dtype)]
    if emit_x:
        out_specs = [pl.BlockSpec((tm, D_MODEL), lambda i: (i, 0))] + out_specs
        out_shape = [jax.ShapeDtypeStruct((m, D_MODEL), F32)] + out_shape
    return pl.pallas_call(
        functools.partial(_out_proj_kernel, n_lhs=n_lhs, emit_x=emit_x),
        grid=(m // tm,),
        in_specs=in_specs,
        out_specs=out_specs,
        out_shape=out_shape,
        compiler_params=_params("arbitrary"),
        name="out_proj_norm",
    )(*lhs, *ws, x, gain.reshape(1, D_MODEL))


def _mlstm_kernel(q_ref, k_ref, v_ref, og_ref, gg_ref, gcol_ref, grow_ref, c0_ref, n0_ref, m0_ref,
                  h_ref, c_out_ref, n_out_ref, m_out_ref, c_sc, n_sc, m_sc, *, chunk):
    ci = pl.program_id(1)
    L = chunk

    @pl.when(ci == 0)
    def _():
        c_sc[...] = c0_ref[0]
        n_sc[...] = n0_ref[0]
        m_sc[...] = m0_ref[0]

    gcol = gcol_ref[0]
    grow = grow_ref[0]
    tt = lax.broadcasted_iota(jnp.int32, (L, L), 0)
    ss = lax.broadcasted_iota(jnp.int32, (L, L), 1)
    causal = ss <= tt

    for head in range(A_HEADS):
        cols = slice(head * A_DH, (head + 1) * A_DH)
        q = q_ref[0, :, cols]
        ks = k_ref[0, :, cols] * jnp.asarray(A_DH ** -0.5, BF16)
        v = v_ref[0, :, cols]
        ig_col = gcol[:, head:head + 1]
        lf_col = gcol[:, head + A_HEADS:head + A_HEADS + 1]
        ig_row = grow[head:head + 1, :]
        lf_row = grow[head + A_HEADS:head + A_HEADS + 1, :]
        b_col = jnp.sum(jnp.where(causal, lf_row, 0.0), axis=1, keepdims=True)
        b_row = jnp.sum(jnp.where(tt <= ss, lf_col, 0.0), axis=0, keepdims=True)
        b_last = jnp.sum(lf_row, axis=1, keepdims=True)

        m0 = m_sc[head][:, :1]
        n0 = n_sc[head]
        c0 = c_sc[head]

        d = jnp.where(causal, b_col - b_row + ig_row, NEG_BIG)
        m_carry = b_col + m0
        m = jnp.maximum(m_carry, jnp.max(d, axis=1, keepdims=True))
        w_intra = jnp.exp(d - m)
        w_carry = jnp.exp(m_carry - m)
        s = _dot_nt(q, ks) * w_intra
        qf = q.astype(F32)
        num = _dot(s.astype(BF16), v) + w_carry * _dot_nt(q, c0.astype(BF16))
        den = jnp.sum(s, axis=1, keepdims=True) + w_carry * jnp.sum(qf * n0, axis=1, keepdims=True)
        h = num / jnp.maximum(jnp.abs(den), jnp.exp(-m))
        gated = h * _sigmoid(og_ref[0, :, cols].astype(F32)) * _silu(gg_ref[0, :, cols].astype(F32))
        h_ref[0, :, cols] = gated.astype(h_ref.dtype)

        m_carry_last = b_last + m0
        d_last_row = b_last - b_row + ig_row
        m_new = jnp.maximum(m_carry_last, jnp.max(d_last_row, axis=1, keepdims=True))
        wc_last = jnp.exp(m_carry_last - m_new)
        w_last_col = jnp.exp(b_last - b_col + ig_col - m_new)
        vw = (v.astype(F32) * w_last_col).astype(BF16)
        c_new = wc_last * c0 + _dot_tn(vw, ks)
        n_new = wc_last * n0 + jnp.sum(ks.astype(F32) * w_last_col, axis=0, keepdims=True)
        c_sc[head] = c_new
        n_sc[head] = n_new
        m_sc[head] = jnp.broadcast_to(m_new, (1, GATE_LANES))

    @pl.when(ci == pl.num_programs(1) - 1)
    def _():
        c_out_ref[0] = c_sc[...]
        n_out_ref[0] = n_sc[...]
        m_out_ref[0] = m_sc[...]


def mlstm(qkvog, gcol, grow, c0, n0, m0, chunk):
    bsz, t_len, _ = qkvog.shape
    nc = t_len // chunk
    hd = A_HEADS
    blk = lambda seg: pl.BlockSpec((1, chunk, A_WIDTH), lambda b, c, seg=seg: (b, c, seg))
    st4 = lambda r, w: pl.BlockSpec((1, hd, r, w), lambda b, c: (b, 0, 0, 0))
    return pl.pallas_call(
        functools.partial(_mlstm_kernel, chunk=chunk),
        grid=(bsz, nc),
        in_specs=[blk(0), blk(1), blk(2), blk(3), blk(4),
                  pl.BlockSpec((1, chunk, GATE_LANES), lambda b, c: (b, c, 0)),
                  pl.BlockSpec((1, 8, chunk), lambda b, c: (b, 0, c)),
                  st4(A_DH, A_DH), st4(1, A_DH), st4(1, GATE_LANES)],
        out_specs=[pl.BlockSpec((1, chunk, A_WIDTH), lambda b, c: (b, c, 0)),
                   st4(A_DH, A_DH), st4(1, A_DH), st4(1, GATE_LANES)],
        out_shape=[jax.ShapeDtypeStruct((bsz, t_len, A_WIDTH), BF16),
                   jax.ShapeDtypeStruct((bsz, hd, A_DH, A_DH), F32),
                   jax.ShapeDtypeStruct((bsz, hd, 1, A_DH), F32),
                   jax.ShapeDtypeStruct((bsz, hd, 1, GATE_LANES), F32)],
        scratch_shapes=[pltpu.VMEM((hd, A_DH, A_DH), F32),
                        pltpu.VMEM((hd, 1, A_DH), F32),
                        pltpu.VMEM((hd, 1, GATE_LANES), F32)],
        compiler_params=_params("arbitrary", "arbitrary"),
        name="mlstm",
    )(qkvog, qkvog, qkvog, qkvog, qkvog, gcol, grow, c0, n0, m0)


def _stick_block(q, kb, vb, bias, run, mask, upper):
    rows = q.shape[0]
    sub = upper.shape[0]
    n_sub = kb.shape[0] // sub
    z = _dot_nt(q, kb) * (B_DH ** -0.5) + bias
    sp = _softplus(z)
    spm = sp if mask is None else jnp.where(mask, sp, 0.0)
    hi, lo = _split_hi_lo(spm)
    laters = [None] * n_sub
    total = None
    for i in reversed(range(n_sub)):
        ln = slice(i * sub, (i + 1) * sub)
        both = _dot(jnp.concatenate([hi[:, ln], lo[:, ln]], axis=0), upper)
        carry = run if total is None else run + total
        laters[i] = both[:rows] + both[rows:] + carry
        part = jnp.sum(spm[:, ln], axis=1, keepdims=True)
        total = part if total is None else total + part
    later = laters[0] if n_sub == 1 else jnp.concatenate(laters, axis=1)
    a = jnp.exp(z - sp - later)
    if mask is not None:
        a = jnp.where(mask, a, 0.0)
    return _dot(a.astype(BF16), vb), total


def _strict_upper(n):
    j = lax.broadcasted_iota(jnp.int32, (n, n), 0)
    s = lax.broadcasted_iota(jnp.int32, (n, n), 1)
    return jnp.where(j > s, 1.0, 0.0).astype(BF16)


def _attn_prompt_kernel(bias_ref, q_ref, k_ref, v_ref, g_ref, o_ref, *, bq, bk, n_heads):
    head0 = pl.program_id(1) * n_heads
    qi = pl.program_id(2)
    kbf = k_ref.at[0]
    vbf = v_ref.at[0]
    upper = _strict_upper(min(bk, CUMSUM_BLOCK))
    row = lax.broadcasted_iota(jnp.int32, (bq, bk), 0)
    col = lax.broadcasted_iota(jnp.int32, (bq, bk), 1)
    lanes = [slice(h * B_DH, (h + 1) * B_DH) for h in range(n_heads)]
    qs = [q_ref[0, :, ln] for ln in lanes]
    biases = [bias_ref[head0 + h] for h in range(n_heads)]

    def blocks(kj, runs, mask):
        start = pl.multiple_of(kj * bk, bk)
        return [_stick_block(qs[h], kbf[pl.ds(start, bk), ln], vbf[pl.ds(start, bk), ln],
                             biases[h], runs[h], mask, upper) for h, ln in enumerate(lanes)]

    q0 = qi * bq
    n_full = q0 // bk
    accs = [jnp.zeros((bq, B_DH), F32)] * n_heads
    runs = [jnp.zeros((bq, 1), F32)] * n_heads
    for m in reversed(range(max(1, bq // bk))):
        kj = n_full + m
        res = blocks(kj, runs, col + (kj * bk - q0) < row)
        accs = [a + c for a, (c, _) in zip(accs, res)]
        runs = [r + t for r, (_, t) in zip(runs, res)]

    def body(it, carry):
        accs, runs = carry
        res = blocks(n_full - 1 - it, runs, None)
        return (tuple(a + c for a, (c, _) in zip(accs, res)),
                tuple(r + t for r, (_, t) in zip(runs, res)))

    accs, runs = lax.fori_loop(0, n_full, body, (tuple(accs), tuple(runs)))
    for h, ln in enumerate(lanes):
        o_ref[0, :, ln] = (accs[h] * _silu(g_ref[0, :, ln].astype(F32))).astype(o_ref.dtype)


def attn_prompt(q, g, k, v, b_sb, bq, bk, n_heads):
    bsz, t_len, _ = k.shape
    width = n_heads * B_DH
    q_spec = pl.BlockSpec((1, bq, width), lambda b, h, i: (b, i, h))
    kv_spec = pl.BlockSpec((1, t_len, width), lambda b, h, i: (b, 0, h))
    return pl.pallas_call(
        functools.partial(_attn_prompt_kernel, bq=bq, bk=bk, n_heads=n_heads),
        grid=(bsz, B_HEADS // n_heads, t_len // bq),
        in_specs=[pl.BlockSpec(memory_space=pltpu.SMEM), q_spec, kv_spec, kv_spec, q_spec],
        out_specs=q_spec,
        out_shape=jax.ShapeDtypeStruct((bsz, t_len, B_WIDTH), BF16),
        compiler_params=_params("arbitrary", "arbitrary", "arbitrary"),
        name="attn_prompt",
    )(b_sb, q, k, v, g)


def _attn_sample_kernel(pt_ref, bias_ref, q_ref, g_ref, knew_ref, vnew_ref, *refs, n_group):
    del pt_ref
    k_refs = refs[:n_group]
    v_refs = refs[n_group:2 * n_group]
    o_ref, qbd, acc, run, kcat, vcat = refs[2 * n_group:]
    p = pl.program_id(1)
    rows = B_HEADS * SAMPLE_PAD
    upper = _strict_upper(PAGE_SIZE)
    bias = bias_ref[...][:, :1]

    def repack(page, dst, i):
        for h in range(B_HEADS):
            dst[i * PAGE_SIZE:(i + 1) * PAGE_SIZE, h * B_DH:(h + 1) * B_DH] = page(h).astype(BF16)

    def step(n_blk, mask):
        n_keys = n_blk * PAGE_SIZE
        z = _dot_nt(qbd[...], kcat[:n_keys, :]) * (B_DH ** -0.5) + bias
        sp = _softplus(z)
        spm = sp if mask is None else jnp.where(mask, sp, 0.0)
        hi, lo = _split_hi_lo(spm)
        carry = run[...][:, :1]
        laters = []
        for i in range(n_blk):
            ln = slice(i * PAGE_SIZE, (i + 1) * PAGE_SIZE)
            both = _dot(jnp.concatenate([hi[:, ln], lo[:, ln]], axis=0), upper)
            laters.append(both[:rows] + both[rows:] + carry)
            carry = carry + jnp.sum(spm[:, ln], axis=1, keepdims=True)
        later = laters[0] if n_blk == 1 else jnp.concatenate(laters, axis=1)
        a = jnp.exp(z - sp - later)
        if mask is not None:
            a = jnp.where(mask, a, 0.0)
        acc[...] += _dot(a.astype(BF16), vcat[:n_keys, :])
        run[...] = jnp.broadcast_to(carry, run.shape)

    @pl.when(p == 0)
    def _():
        r = lax.broadcasted_iota(jnp.int32, (rows, B_WIDTH), 0)
        c = lax.broadcasted_iota(jnp.int32, (rows, B_WIDTH), 1)
        q_rep = jnp.concatenate([q_ref[0].astype(F32)] * B_HEADS, axis=0)
        qbd[...] = jnp.where((r // SAMPLE_PAD) == (c // B_DH), q_rep, 0.0).astype(BF16)
        acc[...] = jnp.zeros_like(acc)
        run[...] = jnp.zeros_like(run)
        repack(lambda h: knew_ref[0, pl.ds(h, PAGE_SIZE, stride=B_HEADS), :], kcat, 0)
        repack(lambda h: vnew_ref[0, pl.ds(h, PAGE_SIZE, stride=B_HEADS), :], vcat, 0)
        t = lax.broadcasted_iota(jnp.int32, (rows, PAGE_SIZE), 0) % SAMPLE_PAD
        s = lax.broadcasted_iota(jnp.int32, (rows, PAGE_SIZE), 1)
        step(1, s < t)

    for i in range(n_group):
        repack(lambda h, r=k_refs[i]: r[0, 0, pl.ds(h, PAGE_SIZE, stride=B_HEADS), :], kcat, i)
        repack(lambda h, r=v_refs[i]: r[0, 0, pl.ds(h, PAGE_SIZE, stride=B_HEADS), :], vcat, i)
    step(n_group, None)

    @pl.when(p == pl.num_programs(1) - 1)
    def _():
        a = acc[...]
        c = lax.broadcasted_iota(jnp.int32, (SAMPLE_PAD, B_WIDTH), 1) // B_DH
        out = jnp.zeros((SAMPLE_PAD, B_WIDTH), F32)
        for h in range(B_HEADS):
            out = out + jnp.where(c == h, a[h * SAMPLE_PAD:(h + 1) * SAMPLE_PAD, :], 0.0)
        o_ref[0] = (out * _silu(g_ref[0].astype(F32))).astype(o_ref.dtype)


def attn_sample(q, g, k_new, v_new, cache_k, cache_v, page_table, bias_rows, n_group):
    n_seq, n_pages = page_table.shape
    steps = n_pages // n_group
    rows = B_HEADS * SAMPLE_PAD
    page_rows = PAGE_SIZE * B_HEADS

    def page_spec(i):
        return pl.BlockSpec((1, 1, page_rows, B_DH),
                            lambda b, p, pt, i=i: (0, pt[b, n_pages - 1 - (p * n_group + i)], 0, 0))

    seq_spec = pl.BlockSpec((1, SAMPLE_PAD, B_WIDTH), lambda b, p, pt: (b, 0, 0))
    new_spec = pl.BlockSpec((1, page_rows, B_DH), lambda b, p, pt: (b, 0, 0))
    grid_spec = pltpu.PrefetchScalarGridSpec(
        num_scalar_prefetch=1,
        grid=(n_seq, steps),
        in_specs=[pl.BlockSpec((rows, GATE_LANES), lambda b, p, pt: (0, 0)),
                  seq_spec, seq_spec, new_spec, new_spec]
                 + [page_spec(i) for i in range(n_group)] * 2,
        out_specs=seq_spec,
        scratch_shapes=[pltpu.VMEM((rows, B_WIDTH), BF16),
                        pltpu.VMEM((rows, B_WIDTH), F32),
                        pltpu.VMEM((rows, GATE_LANES), F32),
                        pltpu.VMEM((n_group * PAGE_SIZE, B_WIDTH), BF16),
                        pltpu.VMEM((n_group * PAGE_SIZE, B_WIDTH), BF16)],
    )
    return pl.pallas_call(
        functools.partial(_attn_sample_kernel, n_group=n_group),
        grid_spec=grid_spec,
        out_shape=jax.ShapeDtypeStruct((n_seq, SAMPLE_PAD, B_WIDTH), F32),
        compiler_params=_params("arbitrary", "arbitrary"),
        name="attn_sample",
    )(page_table, bias_rows, q, g, k_new, v_new, *([cache_k] * n_group), *([cache_v] * n_group))


def _proj_act_kernel(a_ref, w_ref, vg_ref, o_ref, *rest, act, row_chunk):
    maybe_f32, wbf = rest[:-1], rest[-1]

    @pl.when(pl.program_id(1) == 0)
    def _():
        wbf[...] = w_ref[...].astype(BF16)

    for r in range(a_ref.shape[0] // row_chunk):
        rows = slice(r * row_chunk, (r + 1) * row_chunk)
        y = _dot(a_ref[rows, :], wbf[...])
        if act == "gelu":
            y = _gelu_tanh(y)
        elif act == "silu":
            y = _silu(y)
        else:
            y = _rms(_gelu_tanh(y), vg_ref[...])
        o_ref[rows, :] = y.astype(o_ref.dtype)
        if maybe_f32:
            maybe_f32[0][rows, :] = y


def proj_act(h, w, col0, v_gain, act, tm, tn, out_dtype, emit_f32=False):
    m = h.shape[0]
    j0 = col0 // tn
    n_col = C_WIDTH // tn
    out_spec = pl.BlockSpec((tm, tn), lambda j, i: (i, j))
    n_out = 2 if emit_f32 else 1
    w_mode = dict(pipeline_mode=pl.Buffered(1)) if n_col == 1 else {}
    return pl.pallas_call(
        functools.partial(_proj_act_kernel, act=act, row_chunk=min(tm, 256)),
        grid=(n_col, m // tm),
        in_specs=[pl.BlockSpec((tm, D_MODEL), lambda j, i: (i, 0)),
                  pl.BlockSpec((D_MODEL, tn), lambda j, i: (0, j0 + j), **w_mode),
                  pl.BlockSpec((1, tn), lambda j, i: (0, j))],
        out_specs=[out_spec] * n_out,
        out_shape=[jax.ShapeDtypeStruct((m, C_WIDTH), out_dtype)]
                  + ([jax.ShapeDtypeStruct((m, C_WIDTH), F32)] if emit_f32 else []),
        scratch_shapes=[pltpu.VMEM((D_MODEL, tn), BF16)],
        compiler_params=_params("arbitrary", "arbitrary"),
        name="proj_" + act,
    )(h, w, v_gain.reshape(1, C_WIDTH))


def odd_in(h, w, v_gain, tm, act_dtype, emit_v32):
    u = proj_act(h, w, 0, v_gain, "gelu", tm, 1024, act_dtype)[0]
    v = proj_act(h, w, C_WIDTH, v_gain, "gelu_rms", min(tm, 512), C_WIDTH, act_dtype, emit_v32)
    g = proj_act(h, w, 2 * C_WIDTH, v_gain, "silu", tm, 1024, act_dtype)[0]
    return [u, v[0], g] + ([v[1]] if emit_v32 else [])


def _spatial_kernel(u_ref, v_ref, g_ref, ws_ref, bs_ref, y_ref, *, chunk, n_chunks):
    tt = lax.broadcasted_iota(jnp.int32, (chunk, chunk), 0)
    ss = lax.broadcasted_iota(jnp.int32, (chunk, chunk), 1)
    causal = ss <= tt
    for grp in range(C_GROUPS):
        wm = jnp.where(causal, ws_ref[grp], 0.0)
        bcol = bs_ref[:, grp:grp + 1]
        cols = slice(grp * C_GDIM, (grp + 1) * C_GDIM)
        for c in range(n_chunks):
            rows = slice(c * chunk, (c + 1) * chunk)
            vv = v_ref[rows, cols]
            if chunk >= 128:
                sv = _dot(wm.astype(BF16), vv)
            else:
                vf = vv.astype(F32)
                sv = jnp.zeros((chunk, C_GDIM), F32)
                for s in range(chunk):
                    sv = sv + wm[:, s:s + 1] * vf[s:s + 1, :]
            sv = sv + bcol
            y = u_ref[rows, cols].astype(F32) * sv * g_ref[rows, cols].astype(F32)
            y_ref[rows, cols] = y.astype(y_ref.dtype)


def spatial_gate(u, v, g, w_s, b_s_t, chunk, n_chunks):
    m = u.shape[0]
    tm = chunk * n_chunks
    row_spec = pl.BlockSpec((tm, C_WIDTH), lambda i: (i, 0))
    return pl.pallas_call(
        functools.partial(_spatial_kernel, chunk=chunk, n_chunks=n_chunks),
        grid=(m // tm,),
        in_specs=[row_spec, row_spec, row_spec,
                  pl.BlockSpec((C_GROUPS, chunk, chunk), lambda i: (0, 0, 0)),
                  pl.BlockSpec((chunk, C_GROUPS), lambda i: (0, 0))],
        out_specs=row_spec,
        out_shape=jax.ShapeDtypeStruct((m, C_WIDTH), u.dtype),
        compiler_params=_params("arbitrary"),
        name="spatial_gate",
    )(u, v, g, w_s, b_s_t)


def _even_weights(w_in, b_i, b_f):
    gate0 = 5 * A_WIDTH
    b0 = gate0 + 2 * A_HEADS
    wt = jnp.swapaxes(w_in, 0, 1)
    w_gate = jnp.pad(w_in[:, gate0:b0], ((0, 0), (0, GATE_LANES - 2 * A_HEADS)))
    bias = jnp.pad(jnp.concatenate([b_i, b_f]), (0, GATE_LANES - 2 * A_HEADS)).reshape(1, GATE_LANES)
    return wt, b0, w_gate, bias


def _even_layer(x, bsz, t_len, valid_len, ew, g_norm, w_out, b_sb, state, paged, next_gain, tm):
    wt, b0, w_gate, bias = ew
    m_rows = bsz * t_len
    h, gates = norm_gates(x, g_norm, w_gate, bias, min(tm, 512))
    tn = 1024
    qkvog = proj(h, wt, 0, 5 * A_WIDTH, BF16, tm, tn).reshape(bsz, t_len, 5 * A_WIDTH)
    act_dtype = BF16 if paged is None else F32
    q_b = proj(h, wt, b0, B_WIDTH, act_dtype, tm, tn).reshape(bsz, t_len, B_WIDTH)
    g_b = proj(h, wt, b0 + 3 * B_WIDTH, B_WIDTH, act_dtype, tm, tn).reshape(bsz, t_len, B_WIDTH)
    k_new, k_bf = kv_proj(h, wt, b0 + B_WIDTH, min(tm, 512))
    v_new, v_bf = kv_proj(h, wt, b0 + 2 * B_WIDTH, min(tm, 512))

    gates = gates.reshape(bsz, t_len, GATE_LANES)
    t_pad = -(-t_len // A_CHUNK) * A_CHUNK
    pad = ((0, 0), (0, t_pad - t_len), (0, 0))
    if valid_len < t_pad:
        qkvog, gates = jnp.pad(qkvog, pad), jnp.pad(gates, pad)
        pos = jnp.arange(t_pad)[None, :, None]
        lane = jnp.arange(GATE_LANES)[None, None, :]
        gates = jnp.where((pos >= valid_len) & (lane < A_HEADS), NEG_BIG, gates)
        gates = jnp.where((pos >= valid_len) & (lane >= A_HEADS), 0.0, gates)
    grow = gates[:, :, :2 * A_HEADS].transpose(0, 2, 1)
    c0, n0, m0 = state
    h_a, c_new, n_new, m_new = mlstm(qkvog, gates, grow, c0, n0, m0, A_CHUNK)
    h_a = h_a[:, :t_len]

    if paged is None:
        h_b = attn_prompt(q_b, g_b, k_bf.reshape(bsz, t_len, B_WIDTH), v_bf.reshape(bsz, t_len, B_WIDTH),
                          b_sb, ATTN_BQ, ATTN_BK, ATTN_HEADS_PER_STEP)
    else:
        cache_k, cache_v, page_table = paged
        kv_pad = ((0, 0), (0, (PAGE_SIZE - t_len) * B_HEADS), (0, 0))
        bias_rows = jnp.broadcast_to(jnp.repeat(b_sb, SAMPLE_PAD)[:, None], (B_HEADS * SAMPLE_PAD, GATE_LANES))
        h_b = attn_sample(q_b, g_b,
                          jnp.pad(k_new.reshape(bsz, t_len * B_HEADS, B_DH), kv_pad),
                          jnp.pad(v_new.reshape(bsz, t_len * B_HEADS, B_DH), kv_pad),
                          cache_k, cache_v, page_table, bias_rows, PAGES_PER_STEP)

    w_out_b = w_out.astype(BF16)
    x_new, h_next = out_proj_norm(
        [h_a.reshape(m_rows, A_WIDTH), h_b.reshape(m_rows, B_WIDTH)],
        [w_out_b[:A_WIDTH], w_out_b[A_WIDTH:]], x, next_gain, min(tm, 512), True, BF16)
    return x_new, h_next, (c_new, n_new[:, :, 0, :], m_new[:, :, 0, 0]), k_new, v_new


def _odd_layer(x, h, w_in_b, v_gain, w_s, b_s, w_out_b, final_gain, tm, chunk, n_chunks, act_dtype, emit_v32):
    outs = odd_in(h, w_in_b, v_gain, min(2 * tm, h.shape[0]), act_dtype, emit_v32)
    u, v, g = outs[:3]
    y = spatial_gate(u, v, g, w_s[:, :chunk, :chunk], b_s[:, :chunk].T, chunk, n_chunks)
    (y_out,) = out_proj_norm([y], [w_out_b], x, final_gain, tm, False, F32)
    return y_out, (outs[3] if emit_v32 else None)


def kernel(x_prompt, x_sample, state_a_C, state_a_n, state_a_m, cache_b_k, cache_b_v, page_table,
           even_norm, even_w_in, even_b_i, even_b_f, even_b_sb, even_w_out,
           odd_norm, odd_w_in, odd_v_gain, odd_w_s, odd_b_s, odd_w_out, final_norm):
    bsz, seq, _ = x_prompt.shape
    n_seq, dec_seq, _ = x_sample.shape
    n_pool = cache_b_k.shape[1]

    ew = _even_weights(even_w_in[0], even_b_i[0], even_b_f[0])
    odd_w_in_b = odd_w_in[0]
    odd_w_out_b = odd_w_out[0].astype(BF16)

    xp = x_prompt.reshape(bsz * seq, D_MODEL)
    zero_state = (jnp.zeros((bsz, A_HEADS, A_DH, A_DH), F32),
                  jnp.zeros((bsz, A_HEADS, 1, A_DH), F32),
                  jnp.zeros((bsz, A_HEADS, 1, GATE_LANES), F32))
    xp1, hp1, st_p, k_p, v_p = _even_layer(
        xp, bsz, seq, seq, ew, even_norm[0], even_w_out[0], even_b_sb[0], zero_state, None,
        odd_norm[0], 1024)
    y_p, _ = _odd_layer(xp1, hp1, odd_w_in_b, odd_v_gain[0], odd_w_s[0], odd_b_s[0], odd_w_out_b,
                        final_norm, 512, C_CHUNK, 4, BF16, False)

    xs = jnp.pad(x_sample, ((0, 0), (0, SAMPLE_PAD - dec_seq), (0, 0))).reshape(n_seq * SAMPLE_PAD, D_MODEL)
    st_in = (state_a_C[0], state_a_n[0][:, :, None, :],
             jnp.broadcast_to(state_a_m[0][:, :, None, None], (n_seq, A_HEADS, 1, GATE_LANES)))
    assert cache_b_k.shape[0] == 1 and cache_b_v.shape[0] == 1
    page_view = (1, n_pool, PAGE_SIZE * B_HEADS, B_DH)
    paged = (cache_b_k.reshape(page_view), cache_b_v.reshape(page_view), page_table)
    m_s = n_seq * SAMPLE_PAD
    xs1, hs1, st_s, k_s, v_s = _even_layer(
        xs, n_seq, SAMPLE_PAD, dec_seq, ew, even_norm[0], even_w_out[0], even_b_sb[0], st_in, paged,
        odd_norm[0], m_s)
    y_s, v_rows = _odd_layer(xs1, hs1, odd_w_in_b, odd_v_gain[0], odd_w_s[0], odd_b_s[0], odd_w_out_b,
                             final_norm, m_s, SAMPLE_PAD, n_seq, F32, True)

    def sample_rows(a, *dims):
        return a.reshape((n_seq, SAMPLE_PAD) + dims)[:, :dec_seq]

    return (y_p.reshape(bsz, seq, D_MODEL),
            sample_rows(y_s, D_MODEL),
            st_p[0][None], st_p[1][None], st_p[2][None],
            st_s[0][None], st_s[1][None], st_s[2][None],
            k_p.reshape(1, bsz, seq, B_HEADS, B_DH), v_p.reshape(1, bsz, seq, B_HEADS, B_DH),
            sample_rows(k_s, B_HEADS, B_DH)[None], sample_rows(v_s, B_HEADS, B_DH)[None],
            sample_rows(v_rows, C_WIDTH)[None])
```

```python
import functools

import jax
import jax.numpy as jnp
from jax import lax
from jax.experimental import pallas as pl
from jax.experimental.pallas import tpu as pltpu

F32 = jnp.float32
BF16 = jnp.bfloat16

D_MODEL = 2048
PAGE_SIZE = 128
A_HEADS = 4
A_DH = 256
A_WIDTH = A_HEADS * A_DH
A_CHUNK = 128
B_HEADS = 8
B_DH = 128
B_WIDTH = B_HEADS * B_DH
C_WIDTH = D_MODEL
C_GROUPS = 8
C_GDIM = C_WIDTH // C_GROUPS
C_CHUNK = 128
RMS_EPS = 1e-6
GATE_LANES = 128
NEG_BIG = -1e30
SAMPLE_PAD = 8
ATTN_BQ = 512
ATTN_BK = 512
CUMSUM_BLOCK = 256
ATTN_HEADS_PER_STEP = 2
PAGES_PER_STEP = 16

VMEM_LIMIT_BYTES = 56 * 1024 * 1024


def _params(*sem, flags=None):
    return pltpu.CompilerParams(dimension_semantics=sem, vmem_limit_bytes=VMEM_LIMIT_BYTES, flags=flags)


def _dot(a, b):
    return jnp.dot(a, b, preferred_element_type=F32)


def _dot_nt(a, b):
    return lax.dot_general(a, b, (((1,), (1,)), ((), ())), preferred_element_type=F32)


def _dot_tn(a, b):
    return lax.dot_general(a, b, (((0,), (0,)), ((), ())), preferred_element_type=F32)


def _softplus(z):
    return jnp.maximum(z, 0.0) + jnp.log(1.0 + jnp.exp(-jnp.abs(z)))


def _sigmoid(z):
    return 1.0 / (1.0 + jnp.exp(-z))


def _silu(z):
    return z * _sigmoid(z)


def _gelu_tanh(x):
    c = 0.7978845608028654
    return x * (0.5 * (1.0 + jnp.tanh(c * (x + 0.044715 * (x * x * x)))))


def _rms(x, g):
    return x * lax.rsqrt(jnp.mean(x * x, axis=-1, keepdims=True) + RMS_EPS) * g


def _split_hi_lo(x):
    hi = x.astype(BF16)
    lo = (x - hi.astype(F32)).astype(BF16)
    return hi, lo


def _norm_gates_kernel(x_ref, g_ref, whi_ref, wlo_ref, bias_ref, h_ref, gate_ref):
    h = _rms(x_ref[...], g_ref[...])
    h_hi, h_lo = _split_hi_lo(h)
    h_ref[...] = h_hi
    pre = (_dot(h_hi, whi_ref[...]) + _dot(h_hi, wlo_ref[...]) + _dot(h_lo, whi_ref[...])
           + bias_ref[...])
    lane = lax.broadcasted_iota(jnp.int32, pre.shape, 1)
    is_forget = (lane >= A_HEADS) & (lane < 2 * A_HEADS)
    gate_ref[...] = jnp.where(is_forget, -_softplus(-pre), pre)


def norm_gates(x, gain, w_gate, bias, tm):
    m = x.shape[0]
    whi, wlo = _split_hi_lo(w_gate)
    return pl.pallas_call(
        _norm_gates_kernel,
        grid=(m // tm,),
        in_specs=[pl.BlockSpec((tm, D_MODEL), lambda i: (i, 0)),
                  pl.BlockSpec((1, D_MODEL), lambda i: (0, 0)),
                  pl.BlockSpec((D_MODEL, GATE_LANES), lambda i: (0, 0)),
                  pl.BlockSpec((D_MODEL, GATE_LANES), lambda i: (0, 0)),
                  pl.BlockSpec((1, GATE_LANES), lambda i: (0, 0))],
        out_specs=[pl.BlockSpec((tm, D_MODEL), lambda i: (i, 0)),
                   pl.BlockSpec((tm, GATE_LANES), lambda i: (i, 0))],
        out_shape=[jax.ShapeDtypeStruct((m, D_MODEL), BF16),
                   jax.ShapeDtypeStruct((m, GATE_LANES), F32)],
        compiler_params=_params("arbitrary"),
        name="norm_gates",
    )(x, gain.reshape(1, D_MODEL), whi, wlo, bias)


def _proj_kernel(a_ref, wt_ref, o_ref, wbf):
    @pl.when(pl.program_id(1) == 0)
    def _():
        wbf[...] = wt_ref[...].astype(BF16)

    o_ref[...] = _dot_nt(a_ref[...], wbf[...]).astype(o_ref.dtype)


def proj(a, wt, row0, n_out, out_dtype, tm, tn):
    m, k = a.shape
    return pl.pallas_call(
        _proj_kernel,
        grid=(n_out // tn, m // tm),
        in_specs=[pl.BlockSpec((tm, k), lambda j, i: (i, 0)),
                  pl.BlockSpec((pl.Element(tn), pl.Element(k)),
                               lambda j, i: (pl.multiple_of(row0 + j * tn, 8), 0))],
        out_specs=pl.BlockSpec((tm, tn), lambda j, i: (i, j)),
        out_shape=jax.ShapeDtypeStruct((m, n_out), out_dtype),
        scratch_shapes=[pltpu.VMEM((tn, k), BF16)],
        compiler_params=_params("arbitrary", "arbitrary"),
        name="proj",
    )(a, wt)


def _kv_proj_kernel(a_ref, wt_ref, o_ref, obf_ref, wbf):
    @pl.when(pl.program_id(0) == 0)
    def _():
        wbf[...] = wt_ref[...].astype(BF16)

    y = _dot_nt(a_ref[...], wbf[...])
    obf_ref[...] = y.astype(BF16)
    tm = y.shape[0]
    for h in range(B_HEADS):
        o_ref[pl.ds(h, tm, stride=B_HEADS), :] = y[:, h * B_DH:(h + 1) * B_DH]


def kv_proj(a, wt, row0, tm):
    m, k = a.shape
    return pl.pallas_call(
        _kv_proj_kernel,
        grid=(m // tm,),
        in_specs=[pl.BlockSpec((tm, k), lambda i: (i, 0)),
                  pl.BlockSpec((pl.Element(B_WIDTH), pl.Element(k)), lambda i: (row0, 0))],
        out_specs=[pl.BlockSpec((tm * B_HEADS, B_DH), lambda i: (i, 0)),
                   pl.BlockSpec((tm, B_WIDTH), lambda i: (i, 0))],
        out_shape=[jax.ShapeDtypeStruct((m * B_HEADS, B_DH), F32),
                   jax.ShapeDtypeStruct((m, B_WIDTH), BF16)],
        scratch_shapes=[pltpu.VMEM((B_WIDTH, k), BF16)],
        compiler_params=_params("arbitrary"),
        name="kv_proj",
    )(a, wt)


def _out_proj_kernel(*refs, n_lhs, emit_x):
    a_refs = refs[:n_lhs]
    w_refs = refs[n_lhs:2 * n_lhs]
    x_ref, g_ref = refs[2 * n_lhs], refs[2 * n_lhs + 1]
    out_refs = refs[2 * n_lhs + 2:]
    tm = x_ref.shape[0]
    row_chunk = min(tm, 256)
    for r in range(tm // row_chunk):
        rows = slice(r * row_chunk, (r + 1) * row_chunk)
        y = x_ref[rows, :]
        for a_ref, w_ref in zip(a_refs, w_refs):
            y = y + _dot(a_ref[rows, :].astype(BF16), w_ref[...])
        if emit_x:
            out_refs[0][rows, :] = y
        out_refs[-1][rows, :] = _rms(y, g_ref[...]).astype(out_refs[-1].dtype)


def out_proj_norm(lhs, ws, x, gain, tm, emit_x, norm_dtype):
    m = x.shape[0]
    n_lhs = len(lhs)
    in_specs = ([pl.BlockSpec((tm, a.shape[1]), lambda i: (i, 0)) for a in lhs]
                + [pl.BlockSpec(w.shape, lambda i: (0, 0)) for w in ws]
                + [pl.BlockSpec((tm, D_MODEL), lambda i: (i, 0)),
                   pl.BlockSpec((1, D_MODEL), lambda i: (0, 0))])
    out_specs = [pl.BlockSpec((tm, D_MODEL), lambda i: (i, 0))]
    out_shape = [jax.ShapeDtypeStruct((m, D_MODEL), norm_dtype)]
    if emit_x:
        out_specs = [pl.BlockSpec((tm, D_MODEL), lambda i: (i, 0))] + out_specs
        out_shape = [jax.ShapeDtypeStruct((m, D_MODEL), F32)] + out_shape
    return pl.pallas_call(
        functools.partial(_out_proj_kernel, n_lhs=n_lhs, emit_x=emit_x),
        grid=(m // tm,),
        in_specs=in_specs,
        out_specs=out_specs,
        out_shape=out_shape,
        compiler_params=_params("arbitrary"),
        name="out_proj_norm",
    )(*lhs, *ws, x, gain.reshape(1, D_MODEL))


def _mlstm_chunk(q_ref, k_ref, v_ref, og_ref, gg_ref, gcol_ref, grow_ref, h_ref, c_sc, n_sc, m_sc, chunk):
    L = chunk
    gcol = gcol_ref[0]
    grow = grow_ref[0]
    tt = lax.broadcasted_iota(jnp.int32, (L, L), 0)
    ss = lax.broadcasted_iota(jnp.int32, (L, L), 1)
    causal = ss <= tt

    for head in range(A_HEADS):
        cols = slice(head * A_DH, (head + 1) * A_DH)
        q = q_ref[0, :, cols]
        ks = k_ref[0, :, cols] * jnp.asarray(A_DH ** -0.5, BF16)
        v = v_ref[0, :, cols]
        ig_col = gcol[:, head:head + 1]
        lf_col = gcol[:, head + A_HEADS:head + A_HEADS + 1]
        ig_row = grow[head:head + 1, :]
        lf_row = grow[head + A_HEADS:head + A_HEADS + 1, :]
        b_col = jnp.sum(jnp.where(causal, lf_row, 0.0), axis=1, keepdims=True)
        b_row = jnp.sum(jnp.where(tt <= ss, lf_col, 0.0), axis=0, keepdims=True)
        b_last = jnp.sum(lf_row, axis=1, keepdims=True)

        m0 = m_sc[head][:, :1]
        n0 = n_sc[head]
        c0 = c_sc[head]

        d = jnp.where(causal, b_col - b_row + ig_row, NEG_BIG)
        m_carry = b_col + m0
        m = jnp.maximum(m_carry, jnp.max(d, axis=1, keepdims=True))
        w_intra = jnp.exp(d - m)
        w_carry = jnp.exp(m_carry - m)
        s = _dot_nt(q, ks) * w_intra
        qf = q.astype(F32)
        num = _dot(s.astype(BF16), v) + w_carry * _dot_nt(q, c0.astype(BF16))
        den = jnp.sum(s, axis=1, keepdims=True) + w_carry * jnp.sum(qf * n0, axis=1, keepdims=True)
        h = num / jnp.maximum(jnp.abs(den), jnp.exp(-m))
        gated = h * _sigmoid(og_ref[0, :, cols].astype(F32)) * _silu(gg_ref[0, :, cols].astype(F32))
        h_ref[0, :, cols] = gated.astype(h_ref.dtype)

        m_carry_last = b_last + m0
        d_last_row = b_last - b_row + ig_row
        m_new = jnp.maximum(m_carry_last, jnp.max(d_last_row, axis=1, keepdims=True))
        wc_last = jnp.exp(m_carry_last - m_new)
        w_last_col = jnp.exp(b_last - b_col + ig_col - m_new)
        vw = (v.astype(F32) * w_last_col).astype(BF16)
        c_new = wc_last * c0 + _dot_tn(vw, ks)
        n_new = wc_last * n0 + jnp.sum(ks.astype(F32) * w_last_col, axis=0, keepdims=True)
        c_sc[head] = c_new
        n_sc[head] = n_new
        m_sc[head] = jnp.broadcast_to(m_new, (1, GATE_LANES))


def _mlstm_phases(ins, outs, state, chunk):
    c0_ref, n0_ref, m0_ref = ins[7:]
    h_ref, c_out_ref, n_out_ref, m_out_ref = outs
    c_sc, n_sc, m_sc = state

    def init():
        c_sc[...] = c0_ref[0]
        n_sc[...] = n0_ref[0]
        m_sc[...] = m0_ref[0]

    def body():
        _mlstm_chunk(*ins[:7], h_ref, c_sc, n_sc, m_sc, chunk)

    def final():
        c_out_ref[0] = c_sc[...]
        n_out_ref[0] = n_sc[...]
        m_out_ref[0] = m_sc[...]

    return init, body, final


def _run_phases(phase_sets, firsts, lasts):
    for (init, _, _), first in zip(phase_sets, firsts):
        pl.when(first)(init)
    for _, body, _ in phase_sets:
        body()
    for (_, _, final), last in zip(phase_sets, lasts):
        pl.when(last)(final)


def _mlstm_kernel(*refs, chunk):
    ci = pl.program_id(1)
    _run_phases([_mlstm_phases(refs[:10], refs[10:14], refs[14:], chunk)],
                [ci == 0], [ci == pl.num_programs(1) - 1])


def mlstm(qkvog, gcol, grow, c0, n0, m0, chunk):
    bsz, t_len, _ = qkvog.shape
    nc = t_len // chunk
    hd = A_HEADS
    blk = lambda seg: pl.BlockSpec((1, chunk, A_WIDTH), lambda b, c, seg=seg: (b, c, seg))
    st4 = lambda r, w: pl.BlockSpec((1, hd, r, w), lambda b, c: (b, 0, 0, 0))
    return pl.pallas_call(
        functools.partial(_mlstm_kernel, chunk=chunk),
        grid=(bsz, nc),
        in_specs=[blk(0), blk(1), blk(2), blk(3), blk(4),
                  pl.BlockSpec((1, chunk, GATE_LANES), lambda b, c: (b, c, 0)),
                  pl.BlockSpec((1, 8, chunk), lambda b, c: (b, 0, c)),
                  st4(A_DH, A_DH), st4(1, A_DH), st4(1, GATE_LANES)],
        out_specs=[pl.BlockSpec((1, chunk, A_WIDTH), lambda b, c: (b, c, 0)),
                   st4(A_DH, A_DH), st4(1, A_DH), st4(1, GATE_LANES)],
        out_shape=[jax.ShapeDtypeStruct((bsz, t_len, A_WIDTH), BF16),
                   jax.ShapeDtypeStruct((bsz, hd, A_DH, A_DH), F32),
                   jax.ShapeDtypeStruct((bsz, hd, 1, A_DH), F32),
                   jax.ShapeDtypeStruct((bsz, hd, 1, GATE_LANES), F32)],
        scratch_shapes=[pltpu.VMEM((hd, A_DH, A_DH), F32),
                        pltpu.VMEM((hd, 1, A_DH), F32),
                        pltpu.VMEM((hd, 1, GATE_LANES), F32)],
        compiler_params=_params("arbitrary", "arbitrary"),
        name="mlstm",
    )(qkvog, qkvog, qkvog, qkvog, qkvog, gcol, grow, c0, n0, m0)


def _stick_block(q, kb, vb, bias, run, mask, upper):
    rows = q.shape[0]
    sub = upper.shape[0]
    n_sub = kb.shape[0] // sub
    z = _dot_nt(q, kb) * (B_DH ** -0.5) + bias
    sp = _softplus(z)
    spm = sp if mask is None else jnp.where(mask, sp, 0.0)
    hi, lo = _split_hi_lo(spm)
    laters = [None] * n_sub
    total = None
    for i in reversed(range(n_sub)):
        ln = slice(i * sub, (i + 1) * sub)
        both = _dot(jnp.concatenate([hi[:, ln], lo[:, ln]], axis=0), upper)
        carry = run if total is None else run + total
        laters[i] = both[:rows] + both[rows:] + carry
        part = jnp.sum(spm[:, ln], axis=1, keepdims=True)
        total = part if total is None else total + part
    later = laters[0] if n_sub == 1 else jnp.concatenate(laters, axis=1)
    a = jnp.exp(z - sp - later)
    if mask is not None:
        a = jnp.where(mask, a, 0.0)
    return _dot(a.astype(BF16), vb), total


def _strict_upper(n):
    j = lax.broadcasted_iota(jnp.int32, (n, n), 0)
    s = lax.broadcasted_iota(jnp.int32, (n, n), 1)
    return jnp.where(j > s, 1.0, 0.0).astype(BF16)


def _attn_prompt_kernel(bias_ref, q_ref, k_ref, v_ref, g_ref, o_ref, *, bq, bk, n_heads):
    head0 = pl.program_id(1) * n_heads
    qi = pl.program_id(2)
    kbf = k_ref.at[0]
    vbf = v_ref.at[0]
    upper = _strict_upper(min(bk, CUMSUM_BLOCK))
    row = lax.broadcasted_iota(jnp.int32, (bq, bk), 0)
    col = lax.broadcasted_iota(jnp.int32, (bq, bk), 1)
    lanes = [slice(h * B_DH, (h + 1) * B_DH) for h in range(n_heads)]
    qs = [q_ref[0, :, ln] for ln in lanes]
    biases = [bias_ref[head0 + h] for h in range(n_heads)]

    def blocks(kj, runs, mask):
        start = pl.multiple_of(kj * bk, bk)
        return [_stick_block(qs[h], kbf[pl.ds(start, bk), ln], vbf[pl.ds(start, bk), ln],
                             biases[h], runs[h], mask, upper) for h, ln in enumerate(lanes)]

    q0 = qi * bq
    n_full = q0 // bk
    accs = [jnp.zeros((bq, B_DH), F32)] * n_heads
    runs = [jnp.zeros((bq, 1), F32)] * n_heads
    for m in reversed(range(max(1, bq // bk))):
        kj = n_full + m
        res = blocks(kj, runs, col + (kj * bk - q0) < row)
        accs = [a + c for a, (c, _) in zip(accs, res)]
        runs = [r + t for r, (_, t) in zip(runs, res)]

    def body(it, carry):
        accs, runs = carry
        res = blocks(n_full - 1 - it, runs, None)
        return (tuple(a + c for a, (c, _) in zip(accs, res)),
                tuple(r + t for r, (_, t) in zip(runs, res)))

    accs, runs = lax.fori_loop(0, n_full, body, (tuple(accs), tuple(runs)))
    for h, ln in enumerate(lanes):
        o_ref[0, :, ln] = (accs[h] * _silu(g_ref[0, :, ln].astype(F32))).astype(o_ref.dtype)


def attn_prompt(q, g, k, v, b_sb, bq, bk, n_heads):
    bsz, t_len, _ = k.shape
    width = n_heads * B_DH
    q_spec = pl.BlockSpec((1, bq, width), lambda b, h, i: (b, i, h))
    kv_spec = pl.BlockSpec((1, t_len, width), lambda b, h, i: (b, 0, h))
    return pl.pallas_call(
        functools.partial(_attn_prompt_kernel, bq=bq, bk=bk, n_heads=n_heads),
        grid=(bsz, B_HEADS // n_heads, t_len // bq),
        in_specs=[pl.BlockSpec(memory_space=pltpu.SMEM), q_spec, kv_spec, kv_spec, q_spec],
        out_specs=q_spec,
        out_shape=jax.ShapeDtypeStruct((bsz, t_len, B_WIDTH), BF16),
        compiler_params=_params("arbitrary", "arbitrary", "arbitrary"),
        name="attn_prompt",
    )(b_sb, q, k, v, g)


def _attn_sample_phases(ins, o_ref, scratch, n_group):
    bias_ref, q_ref, g_ref, knew_ref, vnew_ref = ins[:5]
    k_refs = ins[5:5 + n_group]
    v_refs = ins[5 + n_group:]
    qbd, acc, run, kcat, vcat = scratch
    rows = B_HEADS * SAMPLE_PAD
    upper = _strict_upper(PAGE_SIZE)
    bias = bias_ref[...][:, :1]

    def repack(page, dst, i):
        for h in range(B_HEADS):
            dst[i * PAGE_SIZE:(i + 1) * PAGE_SIZE, h * B_DH:(h + 1) * B_DH] = page(h).astype(BF16)

    def step(n_blk, mask):
        n_keys = n_blk * PAGE_SIZE
        z = _dot_nt(qbd[...], kcat[:n_keys, :]) * (B_DH ** -0.5) + bias
        sp = _softplus(z)
        spm = sp if mask is None else jnp.where(mask, sp, 0.0)
        hi, lo = _split_hi_lo(spm)
        carry = run[...][:, :1]
        laters = []
        for i in range(n_blk):
            ln = slice(i * PAGE_SIZE, (i + 1) * PAGE_SIZE)
            both = _dot(jnp.concatenate([hi[:, ln], lo[:, ln]], axis=0), upper)
            laters.append(both[:rows] + both[rows:] + carry)
            carry = carry + jnp.sum(spm[:, ln], axis=1, keepdims=True)
        later = laters[0] if n_blk == 1 else jnp.concatenate(laters, axis=1)
        a = jnp.exp(z - sp - later)
        if mask is not None:
            a = jnp.where(mask, a, 0.0)
        acc[...] += _dot(a.astype(BF16), vcat[:n_keys, :])
        run[...] = jnp.broadcast_to(carry, run.shape)

    def init():
        r = lax.broadcasted_iota(jnp.int32, (rows, B_WIDTH), 0)
        c = lax.broadcasted_iota(jnp.int32, (rows, B_WIDTH), 1)
        q_rep = jnp.concatenate([q_ref[0].astype(F32)] * B_HEADS, axis=0)
        qbd[...] = jnp.where((r // SAMPLE_PAD) == (c // B_DH), q_rep, 0.0).astype(BF16)
        acc[...] = jnp.zeros_like(acc)
        run[...] = jnp.zeros_like(run)
        repack(lambda h: knew_ref[0, pl.ds(h, PAGE_SIZE, stride=B_HEADS), :], kcat, 0)
        repack(lambda h: vnew_ref[0, pl.ds(h, PAGE_SIZE, stride=B_HEADS), :], vcat, 0)
        t = lax.broadcasted_iota(jnp.int32, (rows, PAGE_SIZE), 0) % SAMPLE_PAD
        s = lax.broadcasted_iota(jnp.int32, (rows, PAGE_SIZE), 1)
        step(1, s < t)

    def body():
        for i in range(n_group):
            repack(lambda h, r=k_refs[i]: r[0, 0, pl.ds(h, PAGE_SIZE, stride=B_HEADS), :], kcat, i)
            repack(lambda h, r=v_refs[i]: r[0, 0, pl.ds(h, PAGE_SIZE, stride=B_HEADS), :], vcat, i)
        step(n_group, None)

    def final():
        a = acc[...]
        c = lax.broadcasted_iota(jnp.int32, (SAMPLE_PAD, B_WIDTH), 1) // B_DH
        out = jnp.zeros((SAMPLE_PAD, B_WIDTH), F32)
        for h in range(B_HEADS):
            out = out + jnp.where(c == h, a[h * SAMPLE_PAD:(h + 1) * SAMPLE_PAD, :], 0.0)
        o_ref[0] = (out * _silu(g_ref[0].astype(F32))).astype(o_ref.dtype)

    return init, body, final


def _mlstm_attn_kernel(pt_ref, *refs, chunk, n_group, n_chunks, n_page_steps):
    del pt_ref
    n_at = 5 + 2 * n_group
    ml_in, at_in = refs[:10], refs[10:10 + n_at]
    ml_out, at_out = refs[10 + n_at:14 + n_at], refs[14 + n_at]
    ml_sc, at_sc = refs[15 + n_at:18 + n_at], refs[18 + n_at:]
    s = pl.program_id(0)
    ci = s % n_chunks
    p = s % n_page_steps
    _run_phases([_mlstm_phases(ml_in, ml_out, ml_sc, chunk),
                 _attn_sample_phases(at_in, at_out, at_sc, n_group)],
                [ci == 0, p == 0], [ci == n_chunks - 1, p == n_page_steps - 1])


def mlstm_with_attn_sample(qkvog, gcol, grow, c0, n0, m0, chunk,
                           q, g, k_new, v_new, cache_k, cache_v, page_table, bias_rows, n_group):
    bsz, t_len, _ = qkvog.shape
    nc = t_len // chunk
    hd = A_HEADS
    n_seq, n_pages = page_table.shape
    steps = n_pages // n_group
    assert bsz * nc == n_seq * steps
    rows = B_HEADS * SAMPLE_PAD
    page_rows = PAGE_SIZE * B_HEADS

    blk = lambda seg: pl.BlockSpec((1, chunk, A_WIDTH), lambda s, pt, seg=seg: (s // nc, s % nc, seg))
    st4 = lambda r, w: pl.BlockSpec((1, hd, r, w), lambda s, pt: (s // nc, 0, 0, 0))

    def page_spec(i):
        return pl.BlockSpec(
            (1, 1, page_rows, B_DH),
            lambda s, pt, i=i: (0, pt[s // steps, n_pages - 1 - ((s % steps) * n_group + i)], 0, 0))

    seq_spec = pl.BlockSpec((1, SAMPLE_PAD, B_WIDTH), lambda s, pt: (s // steps, 0, 0))
    new_spec = pl.BlockSpec((1, page_rows, B_DH), lambda s, pt: (s // steps, 0, 0))
    grid_spec = pltpu.PrefetchScalarGridSpec(
        num_scalar_prefetch=1,
        grid=(bsz * nc,),
        in_specs=[blk(0), blk(1), blk(2), blk(3), blk(4),
                  pl.BlockSpec((1, chunk, GATE_LANES), lambda s, pt: (s // nc, s % nc, 0)),
                  pl.BlockSpec((1, 8, chunk), lambda s, pt: (s // nc, 0, s % nc)),
                  st4(A_DH, A_DH), st4(1, A_DH), st4(1, GATE_LANES),
                  pl.BlockSpec((rows, GATE_LANES), lambda s, pt: (0, 0)),
                  seq_spec, seq_spec, new_spec, new_spec]
                 + [page_spec(i) for i in range(n_group)] * 2,
        out_specs=[pl.BlockSpec((1, chunk, A_WIDTH), lambda s, pt: (s // nc, s % nc, 0)),
                   st4(A_DH, A_DH), st4(1, A_DH), st4(1, GATE_LANES),
                   seq_spec],
        scratch_shapes=[pltpu.VMEM((hd, A_DH, A_DH), F32),
                        pltpu.VMEM((hd, 1, A_DH), F32),
                        pltpu.VMEM((hd, 1, GATE_LANES), F32),
                        pltpu.VMEM((rows, B_WIDTH), BF16),
                        pltpu.VMEM((rows, B_WIDTH), F32),
                        pltpu.VMEM((rows, GATE_LANES), F32),
                        pltpu.VMEM((n_group * PAGE_SIZE, B_WIDTH), BF16),
                        pltpu.VMEM((n_group * PAGE_SIZE, B_WIDTH), BF16)],
    )
    outs = pl.pallas_call(
        functools.partial(_mlstm_attn_kernel, chunk=chunk, n_group=n_group, n_chunks=nc, n_page_steps=steps),
        grid_spec=grid_spec,
        out_shape=[jax.ShapeDtypeStruct((bsz, t_len, A_WIDTH), BF16),
                   jax.ShapeDtypeStruct((bsz, hd, A_DH, A_DH), F32),
                   jax.ShapeDtypeStruct((bsz, hd, 1, A_DH), F32),
                   jax.ShapeDtypeStruct((bsz, hd, 1, GATE_LANES), F32),
                   jax.ShapeDtypeStruct((n_seq, SAMPLE_PAD, B_WIDTH), F32)],
        compiler_params=_params("arbitrary"),
        name="mlstm_attn_sample",
    )(page_table, qkvog, qkvog, qkvog, qkvog, qkvog, gcol, grow, c0, n0, m0,
      bias_rows, q, g, k_new, v_new, *([cache_k] * n_group), *([cache_v] * n_group))
    return outs[:4], outs[4]


def _proj_act_kernel(a_ref, w_ref, vg_ref, o_ref, *rest, act, row_chunk):
    maybe_f32, wbf = rest[:-1], rest[-1]

    @pl.when(pl.program_id(1) == 0)
    def _():
        wbf[...] = w_ref[...].astype(BF16)

    for r in range(a_ref.shape[0] // row_chunk):
        rows = slice(r * row_chunk, (r + 1) * row_chunk)
        y = _dot(a_ref[rows, :], wbf[...])
        if act == "gelu":
            y = _gelu_tanh(y)
        elif act == "silu":
            y = _silu(y)
        else:
            y = _rms(_gelu_tanh(y), vg_ref[...])
        o_ref[rows, :] = y.astype(o_ref.dtype)
        if maybe_f32:
            maybe_f32[0][rows, :] = y


def proj_act(h, w, col0, v_gain, act, tm, tn, out_dtype, emit_f32=False):
    m = h.shape[0]
    j0 = col0 // tn
    n_col = C_WIDTH // tn
    out_spec = pl.BlockSpec((tm, tn), lambda j, i: (i, j))
    n_out = 2 if emit_f32 else 1
    w_mode = dict(pipeline_mode=pl.Buffered(1)) if n_col == 1 else {}
    return pl.pallas_call(
        functools.partial(_proj_act_kernel, act=act, row_chunk=min(tm, 256)),
        grid=(n_col, m // tm),
        in_specs=[pl.BlockSpec((tm, D_MODEL), lambda j, i: (i, 0)),
                  pl.BlockSpec((D_MODEL, tn), lambda j, i: (0, j0 + j), **w_mode),
                  pl.BlockSpec((1, tn), lambda j, i: (0, j))],
        out_specs=[out_spec] * n_out,
        out_shape=[jax.ShapeDtypeStruct((m, C_WIDTH), out_dtype)]
                  + ([jax.ShapeDtypeStruct((m, C_WIDTH), F32)] if emit_f32 else []),
        scratch_shapes=[pltpu.VMEM((D_MODEL, tn), BF16)],
        compiler_params=_params("arbitrary", "arbitrary"),
        name="proj_" + act,
    )(h, w, v_gain.reshape(1, C_WIDTH))


def odd_in(h, w, v_gain, tm, act_dtype, emit_v32):
    u = proj_act(h, w, 0, v_gain, "gelu", tm, 1024, act_dtype)[0]
    v = proj_act(h, w, C_WIDTH, v_gain, "gelu_rms", min(tm, 512), C_WIDTH, act_dtype, emit_v32)
    g = proj_act(h, w, 2 * C_WIDTH, v_gain, "silu", tm, 1024, act_dtype)[0]
    return [u, v[0], g] + ([v[1]] if emit_v32 else [])


def _spatial_kernel(u_ref, v_ref, g_ref, ws_ref, bs_ref, y_ref, *, chunk, n_chunks):
    tt = lax.broadcasted_iota(jnp.int32, (chunk, chunk), 0)
    ss = lax.broadcasted_iota(jnp.int32, (chunk, chunk), 1)
    causal = ss <= tt
    for grp in range(C_GROUPS):
        wm = jnp.where(causal, ws_ref[grp], 0.0)
        bcol = bs_ref[:, grp:grp + 1]
        cols = slice(grp * C_GDIM, (grp + 1) * C_GDIM)
        for c in range(n_chunks):
            rows = slice(c * chunk, (c + 1) * chunk)
            vv = v_ref[rows, cols]
            if chunk >= 128:
                sv = _dot(wm.astype(BF16), vv)
            else:
                vf = vv.astype(F32)
                sv = jnp.zeros((chunk, C_GDIM), F32)
                for s in range(chunk):
                    sv = sv + wm[:, s:s + 1] * vf[s:s + 1, :]
            sv = sv + bcol
            y = u_ref[rows, cols].astype(F32) * sv * g_ref[rows, cols].astype(F32)
            y_ref[rows, cols] = y.astype(y_ref.dtype)


def spatial_gate(u, v, g, w_s, b_s_t, chunk, n_chunks):
    m = u.shape[0]
    tm = chunk * n_chunks
    row_spec = pl.BlockSpec((tm, C_WIDTH), lambda i: (i, 0))
    return pl.pallas_call(
        functools.partial(_spatial_kernel, chunk=chunk, n_chunks=n_chunks),
        grid=(m // tm,),
        in_specs=[row_spec, row_spec, row_spec,
                  pl.BlockSpec((C_GROUPS, chunk, chunk), lambda i: (0, 0, 0)),
                  pl.BlockSpec((chunk, C_GROUPS), lambda i: (0, 0))],
        out_specs=row_spec,
        out_shape=jax.ShapeDtypeStruct((m, C_WIDTH), u.dtype),
        compiler_params=_params("arbitrary"),
        name="spatial_gate",
    )(u, v, g, w_s, b_s_t)


def _even_weights(w_in, b_i, b_f):
    gate0 = 5 * A_WIDTH
    b0 = gate0 + 2 * A_HEADS
    wt = jnp.swapaxes(w_in, 0, 1)
    w_gate = jnp.pad(w_in[:, gate0:b0], ((0, 0), (0, GATE_LANES - 2 * A_HEADS)))
    bias = jnp.pad(jnp.concatenate([b_i, b_f]), (0, GATE_LANES - 2 * A_HEADS)).reshape(1, GATE_LANES)
    return wt, b0, w_gate, bias


def _even_front(x, bsz, t_len, valid_len, ew, g_norm, act_dtype, tm):
    wt, b0, w_gate, bias = ew
    h, gates = norm_gates(x, g_norm, w_gate, bias, min(tm, 512))
    tn = 1024
    qkvog = proj(h, wt, 0, 5 * A_WIDTH, BF16, tm, tn).reshape(bsz, t_len, 5 * A_WIDTH)
    q_b = proj(h, wt, b0, B_WIDTH, act_dtype, tm, tn).reshape(bsz, t_len, B_WIDTH)
    g_b = proj(h, wt, b0 + 3 * B_WIDTH, B_WIDTH, act_dtype, tm, tn).reshape(bsz, t_len, B_WIDTH)
    k_new, k_bf = kv_proj(h, wt, b0 + B_WIDTH, min(tm, 512))
    v_new, v_bf = kv_proj(h, wt, b0 + 2 * B_WIDTH, min(tm, 512))

    gates = gates.reshape(bsz, t_len, GATE_LANES)
    t_pad = -(-t_len // A_CHUNK) * A_CHUNK
    pad = ((0, 0), (0, t_pad - t_len), (0, 0))
    if valid_len < t_pad:
        qkvog, gates = jnp.pad(qkvog, pad), jnp.pad(gates, pad)
        pos = jnp.arange(t_pad)[None, :, None]
        lane = jnp.arange(GATE_LANES)[None, None, :]
        gates = jnp.where((pos >= valid_len) & (lane < A_HEADS), NEG_BIG, gates)
        gates = jnp.where((pos >= valid_len) & (lane >= A_HEADS), 0.0, gates)
    grow = gates[:, :, :2 * A_HEADS].transpose(0, 2, 1)
    return dict(qkvog=qkvog, gates=gates, grow=grow, q_b=q_b, g_b=g_b,
                k_new=k_new, v_new=v_new, k_bf=k_bf, v_bf=v_bf)


def _even_back(x, h_a, h_b, w_out, next_gain, tm):
    m_rows = x.shape[0]
    w_out_b = w_out.astype(BF16)
    return out_proj_norm(
        [h_a.reshape(m_rows, A_WIDTH), h_b.reshape(m_rows, B_WIDTH)],
        [w_out_b[:A_WIDTH], w_out_b[A_WIDTH:]], x, next_gain, min(tm, 512), True, BF16)


def _odd_layer(x, h, w_in_b, v_gain, w_s, b_s, w_out_b, final_gain, tm, chunk, n_chunks, act_dtype, emit_v32):
    outs = odd_in(h, w_in_b, v_gain, min(2 * tm, h.shape[0]), act_dtype, emit_v32)
    u, v, g = outs[:3]
    y = spatial_gate(u, v, g, w_s[:, :chunk, :chunk], b_s[:, :chunk].T, chunk, n_chunks)
    (y_out,) = out_proj_norm([y], [w_out_b], x, final_gain, tm, False, F32)
    return y_out, (outs[3] if emit_v32 else None)


def kernel(x_prompt, x_sample, state_a_C, state_a_n, state_a_m, cache_b_k, cache_b_v, page_table,
           even_norm, even_w_in, even_b_i, even_b_f, even_b_sb, even_w_out,
           odd_norm, odd_w_in, odd_v_gain, odd_w_s, odd_b_s, odd_w_out, final_norm):
    bsz, seq, _ = x_prompt.shape
    n_seq, dec_seq, _ = x_sample.shape
    n_pool = cache_b_k.shape[1]

    ew = _even_weights(even_w_in[0], even_b_i[0], even_b_f[0])
    odd_w_in_b = odd_w_in[0]
    odd_w_out_b = odd_w_out[0].astype(BF16)

    xp = x_prompt.reshape(bsz * seq, D_MODEL)
    xs = jnp.pad(x_sample, ((0, 0), (0, SAMPLE_PAD - dec_seq), (0, 0))).reshape(n_seq * SAMPLE_PAD, D_MODEL)
    m_s = n_seq * SAMPLE_PAD
    fp = _even_front(xp, bsz, seq, seq, ew, even_norm[0], BF16, 1024)
    fs = _even_front(xs, n_seq, SAMPLE_PAD, dec_seq, ew, even_norm[0], F32, m_s)

    zero_state = (jnp.zeros((bsz, A_HEADS, A_DH, A_DH), F32),
                  jnp.zeros((bsz, A_HEADS, 1, A_DH), F32),
                  jnp.zeros((bsz, A_HEADS, 1, GATE_LANES), F32))
    assert cache_b_k.shape[0] == 1 and cache_b_v.shape[0] == 1
    page_view = (1, n_pool, PAGE_SIZE * B_HEADS, B_DH)
    kv_pad = ((0, 0), (0, (PAGE_SIZE - SAMPLE_PAD) * B_HEADS), (0, 0))
    bias_rows = jnp.broadcast_to(jnp.repeat(even_b_sb[0], SAMPLE_PAD)[:, None],
                                 (B_HEADS * SAMPLE_PAD, GATE_LANES))
    (ha_p, c_p, n_p, m_p), hb_s = mlstm_with_attn_sample(
        fp["qkvog"], fp["gates"], fp["grow"], *zero_state, A_CHUNK,
        fs["q_b"], fs["g_b"],
        jnp.pad(fs["k_new"].reshape(n_seq, SAMPLE_PAD * B_HEADS, B_DH), kv_pad),
        jnp.pad(fs["v_new"].reshape(n_seq, SAMPLE_PAD * B_HEADS, B_DH), kv_pad),
        cache_b_k.reshape(page_view), cache_b_v.reshape(page_view), page_table, bias_rows, PAGES_PER_STEP)

    hb_p = attn_prompt(fp["q_b"], fp["g_b"], fp["k_bf"].reshape(bsz, seq, B_WIDTH),
                       fp["v_bf"].reshape(bsz, seq, B_WIDTH), even_b_sb[0],
                       ATTN_BQ, ATTN_BK, ATTN_HEADS_PER_STEP)
    st_in = (state_a_C[0], state_a_n[0][:, :, None, :],
             jnp.broadcast_to(state_a_m[0][:, :, None, None], (n_seq, A_HEADS, 1, GATE_LANES)))
    ha_s, c_s, n_s, m_s_new = mlstm(fs["qkvog"], fs["gates"], fs["grow"], *st_in, A_CHUNK)
    ha_s = ha_s[:, :SAMPLE_PAD]

    xp1, hp1 = _even_back(xp, ha_p, hb_p, even_w_out[0], odd_norm[0], 512)
    xs1, hs1 = _even_back(xs, ha_s, hb_s, even_w_out[0], odd_norm[0], m_s)
    y_p, _ = _odd_layer(xp1, hp1, odd_w_in_b, odd_v_gain[0], odd_w_s[0], odd_b_s[0], odd_w_out_b,
                        final_norm, 512, C_CHUNK, 4, BF16, False)
    y_s, v_rows = _odd_layer(xs1, hs1, odd_w_in_b, odd_v_gain[0], odd_w_s[0], odd_b_s[0], odd_w_out_b,
                             final_norm, m_s, SAMPLE_PAD, n_seq, F32, True)

    def sample_rows(a, *dims):
        return a.reshape((n_seq, SAMPLE_PAD) + dims)[:, :dec_seq]

    return (y_p.reshape(bsz, seq, D_MODEL),
            sample_rows(y_s, D_MODEL),
            c_p[None], n_p[:, :, 0, :][None], m_p[:, :, 0, 0][None],
            c_s[None], n_s[:, :, 0, :][None], m_s_new[:, :, 0, 0][None],
            fp["k_new"].reshape(1, bsz, seq, B_HEADS, B_DH), fp["v_new"].reshape(1, bsz, seq, B_HEADS, B_DH),
            sample_rows(fs["k_new"], B_HEADS, B_DH)[None], sample_rows(fs["v_new"], B_HEADS, B_DH)[None],
            sample_rows(v_rows, C_WIDTH)[None])
```

```python
import functools

import jax
import jax.numpy as jnp
from jax import lax
from jax.experimental import pallas as pl
from jax.experimental.pallas import tpu as pltpu

F32 = jnp.float32
BF16 = jnp.bfloat16

D_MODEL = 2048
PAGE_SIZE = 128
A_HEADS = 4
A_DH = 256
A_WIDTH = A_HEADS * A_DH
A_CHUNK = 128
B_HEADS = 8
B_DH = 128
B_WIDTH = B_HEADS * B_DH
C_WIDTH = D_MODEL
C_GROUPS = 8
C_GDIM = C_WIDTH // C_GROUPS
C_CHUNK = 128
RMS_EPS = 1e-6
GATE_LANES = 128
NEG_BIG = -1e30
SAMPLE_PAD = 8
ATTN_BQ = 512
ATTN_BK = 512
CUMSUM_BLOCK = 256
ATTN_HEADS_PER_STEP = 2
PAGES_PER_STEP = 8
RIDER_PARTS = 1
HOST_ROW_CHUNK = 1024

VMEM_LIMIT_BYTES = 56 * 1024 * 1024


def _params(*sem, flags=None):
    return pltpu.CompilerParams(dimension_semantics=sem, vmem_limit_bytes=VMEM_LIMIT_BYTES, flags=flags)


def _dot(a, b):
    return jnp.dot(a, b, preferred_element_type=F32)


def _dot_nt(a, b):
    return lax.dot_general(a, b, (((1,), (1,)), ((), ())), preferred_element_type=F32)


def _dot_tn(a, b):
    return lax.dot_general(a, b, (((0,), (0,)), ((), ())), preferred_element_type=F32)


def _softplus(z):
    return jnp.maximum(z, 0.0) + jnp.log(1.0 + jnp.exp(-jnp.abs(z)))


def _sigmoid(z):
    return 1.0 / (1.0 + jnp.exp(-z))


def _silu(z):
    return z * _sigmoid(z)


def _gelu_tanh(x):
    c = 0.7978845608028654
    return x * (0.5 * (1.0 + jnp.tanh(c * (x + 0.044715 * (x * x * x)))))


def _rms(x, g):
    return x * lax.rsqrt(jnp.mean(x * x, axis=-1, keepdims=True) + RMS_EPS) * g


def _split_hi_lo(x):
    hi = x.astype(BF16)
    lo = (x - hi.astype(F32)).astype(BF16)
    return hi, lo


def _norm_gates_kernel(x_ref, g_ref, whi_ref, wlo_ref, bias_ref, h_ref, gate_ref):
    h = _rms(x_ref[...], g_ref[...])
    h_hi, h_lo = _split_hi_lo(h)
    h_ref[...] = h_hi
    pre = (_dot(h_hi, whi_ref[...]) + _dot(h_hi, wlo_ref[...]) + _dot(h_lo, whi_ref[...])
           + bias_ref[...])
    lane = lax.broadcasted_iota(jnp.int32, pre.shape, 1)
    is_forget = (lane >= A_HEADS) & (lane < 2 * A_HEADS)
    gate_ref[...] = jnp.where(is_forget, -_softplus(-pre), pre)


def norm_gates(x, gain, w_gate, bias, tm):
    m = x.shape[0]
    whi, wlo = _split_hi_lo(w_gate)
    return pl.pallas_call(
        _norm_gates_kernel,
        grid=(m // tm,),
        in_specs=[pl.BlockSpec((tm, D_MODEL), lambda i: (i, 0)),
                  pl.BlockSpec((1, D_MODEL), lambda i: (0, 0)),
                  pl.BlockSpec((D_MODEL, GATE_LANES), lambda i: (0, 0)),
                  pl.BlockSpec((D_MODEL, GATE_LANES), lambda i: (0, 0)),
                  pl.BlockSpec((1, GATE_LANES), lambda i: (0, 0))],
        out_specs=[pl.BlockSpec((tm, D_MODEL), lambda i: (i, 0)),
                   pl.BlockSpec((tm, GATE_LANES), lambda i: (i, 0))],
        out_shape=[jax.ShapeDtypeStruct((m, D_MODEL), BF16),
                   jax.ShapeDtypeStruct((m, GATE_LANES), F32)],
        compiler_params=_params("arbitrary"),
        name="norm_gates",
    )(x, gain.reshape(1, D_MODEL), whi, wlo, bias)


def _hosted_kernel(*refs, n_in, n_out, phases_fn, lin, rank, rider):
    if rider is not None:
        refs = refs[1:]
    r_in, r_out = (rider.n_in, rider.n_out) if rider is not None else (0, 0)
    ins, refs = refs[:n_in], refs[n_in:]
    r_ins, refs = refs[:r_in], refs[r_in:]
    outs, refs = refs[:n_out], refs[n_out:]
    r_outs, refs = refs[:r_out], refs[r_out:]
    n_sc = len(refs) - (rider.n_scratch if rider is not None else 0)
    scratch, r_scratch = refs[:n_sc], refs[n_sc:]
    sets = [phases_fn(ins, outs, scratch)]
    if rider is not None:
        step = lin(*[pl.program_id(d) for d in range(rank)])
        sets.insert(0, rider.phases(r_ins, r_outs, r_scratch, step))
    for pre, _, _ in sets:
        for cond, fn in pre:
            pl.when(cond)(fn)
    for k in range(max(len(parts) for _, parts, _ in sets)):
        for _, parts, _ in sets:
            if k < len(parts):
                parts[k]()
    for _, _, post in sets:
        for cond, fn in post:
            pl.when(cond)(fn)


def _hosted_call(name, grid, lin, in_specs, out_specs, out_shape, scratch, phases_fn, args, rider=None):
    n_in, n_out = len(in_specs), len(out_specs)
    aliases = {}
    if rider is not None:
        r = rider.specs(lin, len(grid))
        aliases = {1 + n_in + i: n_out + o for i, o in r["aliases"].items()}
        in_specs, out_specs = in_specs + r["in_specs"], out_specs + r["out_specs"]
        out_shape, scratch = out_shape + r["out_shape"], scratch + r["scratch"]
        args = [rider.page_table] + list(args) + r["args"]
    kern = functools.partial(_hosted_kernel, n_in=n_in, n_out=n_out, phases_fn=phases_fn,
                             lin=lin, rank=len(grid), rider=rider)
    grid_spec = pltpu.PrefetchScalarGridSpec(
        num_scalar_prefetch=0 if rider is None else 1, grid=grid,
        in_specs=in_specs, out_specs=out_specs, scratch_shapes=scratch)
    outs = pl.pallas_call(
        kern, grid_spec=grid_spec, out_shape=out_shape, input_output_aliases=aliases,
        compiler_params=_params(*(["arbitrary"] * len(grid))), name=name,
    )(*args)
    return outs[:n_out], outs[n_out:]


def proj(a, wt, row0, n_out, out_dtype, tm, tn, rider=None):
    m, k = a.shape
    n_i = m // tm

    def phases(ins, outs, scratch):
        (a_ref, wt_ref), (o_ref,), (wbf,) = ins, outs, scratch

        def cast():
            wbf[...] = wt_ref[...].astype(BF16)

        def chunk(rows):
            def run():
                o_ref[rows, :] = _dot_nt(a_ref[rows, :], wbf[...]).astype(o_ref.dtype)
            return run

        row_chunk = min(tm, HOST_ROW_CHUNK)
        parts = [chunk(slice(r, r + row_chunk)) for r in range(0, tm, row_chunk)]
        return [(pl.program_id(1) == 0, cast)], parts, []

    w_mode = dict(pipeline_mode=pl.Buffered(1)) if rider is not None else {}
    (out,), rest = _hosted_call(
        "proj", (n_out // tn, n_i), lambda j, i: j * n_i + i,
        [pl.BlockSpec((tm, k), lambda j, i, *_: (i, 0)),
         pl.BlockSpec((pl.Element(tn), pl.Element(k)),
                      lambda j, i, *_: (pl.multiple_of(row0 + j * tn, 8), 0), **w_mode)],
        [pl.BlockSpec((tm, tn), lambda j, i, *_: (i, j))],
        [jax.ShapeDtypeStruct((m, n_out), out_dtype)],
        [pltpu.VMEM((tn, k), BF16)], phases, [a, wt], rider)
    return out if rider is None else (out, rest)


def kv_proj(a, wt, row0, tm, rider=None):
    m, k = a.shape

    def phases(ins, outs, scratch):
        (a_ref, wt_ref), (o_ref, obf_ref), (wbf,) = ins, outs, scratch

        def cast():
            wbf[...] = wt_ref[...].astype(BF16)

        def chunk(r0, n_rows):
            def run():
                y = _dot_nt(a_ref[r0:r0 + n_rows, :], wbf[...])
                obf_ref[r0:r0 + n_rows, :] = y.astype(BF16)
                for h in range(B_HEADS):
                    o_ref[pl.ds(r0 * B_HEADS + h, n_rows, stride=B_HEADS), :] = y[:, h * B_DH:(h + 1) * B_DH]
            return run

        row_chunk = min(tm, HOST_ROW_CHUNK)
        parts = [chunk(r, row_chunk) for r in range(0, tm, row_chunk)]
        return [(pl.program_id(0) == 0, cast)], parts, []

    outs, rest = _hosted_call(
        "kv_proj", (m // tm,), lambda i: i,
        [pl.BlockSpec((tm, k), lambda i, *_: (i, 0)),
         pl.BlockSpec((pl.Element(B_WIDTH), pl.Element(k)), lambda i, *_: (row0, 0))],
        [pl.BlockSpec((tm * B_HEADS, B_DH), lambda i, *_: (i, 0)),
         pl.BlockSpec((tm, B_WIDTH), lambda i, *_: (i, 0))],
        [jax.ShapeDtypeStruct((m * B_HEADS, B_DH), F32), jax.ShapeDtypeStruct((m, B_WIDTH), BF16)],
        [pltpu.VMEM((B_WIDTH, k), BF16)], phases, [a, wt], rider)
    return tuple(outs) if rider is None else (tuple(outs), rest)


def _out_proj_kernel(*refs, n_lhs, emit_x):
    a_refs = refs[:n_lhs]
    w_refs = refs[n_lhs:2 * n_lhs]
    x_ref, g_ref = refs[2 * n_lhs], refs[2 * n_lhs + 1]
    out_refs = refs[2 * n_lhs + 2:]
    tm = x_ref.shape[0]
    row_chunk = min(tm, 256)
    for r in range(tm // row_chunk):
        rows = slice(r * row_chunk, (r + 1) * row_chunk)
        y = x_ref[rows, :]
        for a_ref, w_ref in zip(a_refs, w_refs):
            y = y + _dot(a_ref[rows, :].astype(BF16), w_ref[...])
        if emit_x:
            out_refs[0][rows, :] = y
        out_refs[-1][rows, :] = _rms(y, g_ref[...]).astype(out_refs[-1].dtype)


def out_proj_norm(lhs, ws, x, gain, tm, emit_x, norm_dtype):
    m = x.shape[0]
    n_lhs = len(lhs)
    in_specs = ([pl.BlockSpec((tm, a.shape[1]), lambda i: (i, 0)) for a in lhs]
                + [pl.BlockSpec(w.shape, lambda i: (0, 0)) for w in ws]
                + [pl.BlockSpec((tm, D_MODEL), lambda i: (i, 0)),
                   pl.BlockSpec((1, D_MODEL), lambda i: (0, 0))])
    out_specs = [pl.BlockSpec((tm, D_MODEL), lambda i: (i, 0))]
    out_shape = [jax.ShapeDtypeStruct((m, D_MODEL), norm_dtype)]
    if emit_x:
        out_specs = [pl.BlockSpec((tm, D_MODEL), lambda i: (i, 0))] + out_specs
        out_shape = [jax.ShapeDtypeStruct((m, D_MODEL), F32)] + out_shape
    return pl.pallas_call(
        functools.partial(_out_proj_kernel, n_lhs=n_lhs, emit_x=emit_x),
        grid=(m // tm,),
        in_specs=in_specs,
        out_specs=out_specs,
        out_shape=out_shape,
        compiler_params=_params("arbitrary"),
        name="out_proj_norm",
    )(*lhs, *ws, x, gain.reshape(1, D_MODEL))


def _mlstm_chunk(q_ref, k_ref, v_ref, og_ref, gg_ref, gcol_ref, grow_ref, h_ref, c_sc, n_sc, m_sc, chunk, heads):
    L = chunk
    gcol = gcol_ref[0]
    grow = grow_ref[0]
    tt = lax.broadcasted_iota(jnp.int32, (L, L), 0)
    ss = lax.broadcasted_iota(jnp.int32, (L, L), 1)
    causal = ss <= tt

    for head in heads:
        cols = slice(head * A_DH, (head + 1) * A_DH)
        q = q_ref[0, :, cols]
        ks = k_ref[0, :, cols] * jnp.asarray(A_DH ** -0.5, BF16)
        v = v_ref[0, :, cols]
        ig_col = gcol[:, head:head + 1]
        lf_col = gcol[:, head + A_HEADS:head + A_HEADS + 1]
        ig_row = grow[head:head + 1, :]
        lf_row = grow[head + A_HEADS:head + A_HEADS + 1, :]
        b_col = jnp.sum(jnp.where(causal, lf_row, 0.0), axis=1, keepdims=True)
        b_row = jnp.sum(jnp.where(tt <= ss, lf_col, 0.0), axis=0, keepdims=True)
        b_last = jnp.sum(lf_row, axis=1, keepdims=True)

        m0 = m_sc[head][:, :1]
        n0 = n_sc[head]
        c0 = c_sc[head]

        d = jnp.where(causal, b_col - b_row + ig_row, NEG_BIG)
        m_carry = b_col + m0
        m = jnp.maximum(m_carry, jnp.max(d, axis=1, keepdims=True))
        w_intra = jnp.exp(d - m)
        w_carry = jnp.exp(m_carry - m)
        s = _dot_nt(q, ks) * w_intra
        qf = q.astype(F32)
        num = _dot(s.astype(BF16), v) + w_carry * _dot_nt(q, c0.astype(BF16))
        den = jnp.sum(s, axis=1, keepdims=True) + w_carry * jnp.sum(qf * n0, axis=1, keepdims=True)
        h = num / jnp.maximum(jnp.abs(den), jnp.exp(-m))
        gated = h * _sigmoid(og_ref[0, :, cols].astype(F32)) * _silu(gg_ref[0, :, cols].astype(F32))
        h_ref[0, :, cols] = gated.astype(h_ref.dtype)

        m_carry_last = b_last + m0
        d_last_row = b_last - b_row + ig_row
        m_new = jnp.maximum(m_carry_last, jnp.max(d_last_row, axis=1, keepdims=True))
        wc_last = jnp.exp(m_carry_last - m_new)
        w_last_col = jnp.exp(b_last - b_col + ig_col - m_new)
        vw = (v.astype(F32) * w_last_col).astype(BF16)
        c_new = wc_last * c0 + _dot_tn(vw, ks)
        n_new = wc_last * n0 + jnp.sum(ks.astype(F32) * w_last_col, axis=0, keepdims=True)
        c_sc[head] = c_new
        n_sc[head] = n_new
        m_sc[head] = jnp.broadcast_to(m_new, (1, GATE_LANES))


def mlstm(qkvog, gcol, grow, c0, n0, m0, chunk, rider=None):
    bsz, t_len, _ = qkvog.shape
    nc = t_len // chunk
    hd = A_HEADS

    def phases(ins, outs, state):
        c0_ref, n0_ref, m0_ref = ins[7:]
        h_ref, c_out_ref, n_out_ref, m_out_ref = outs
        c_sc, n_sc, m_sc = state
        ci = pl.program_id(1)

        def init():
            c_sc[...] = c0_ref[0]
            n_sc[...] = n0_ref[0]
            m_sc[...] = m0_ref[0]

        def head_part(head):
            return lambda: _mlstm_chunk(*ins[:7], h_ref, c_sc, n_sc, m_sc, chunk, (head,))

        parts = [head_part(head) for head in range(A_HEADS)]

        def final():
            c_out_ref[0] = c_sc[...]
            n_out_ref[0] = n_sc[...]
            m_out_ref[0] = m_sc[...]

        return [(ci == 0, init)], parts, [(ci == nc - 1, final)]

    blk = lambda seg: pl.BlockSpec((1, chunk, A_WIDTH), lambda b, c, *_, seg=seg: (b, c, seg))
    st4 = lambda r, w: pl.BlockSpec((1, hd, r, w), lambda b, c, *_: (b, 0, 0, 0))
    outs, rest = _hosted_call(
        "mlstm", (bsz, nc), lambda b, c: b * nc + c,
        [blk(0), blk(1), blk(2), blk(3), blk(4),
         pl.BlockSpec((1, chunk, GATE_LANES), lambda b, c, *_: (b, c, 0)),
         pl.BlockSpec((1, 8, chunk), lambda b, c, *_: (b, 0, c)),
         st4(A_DH, A_DH), st4(1, A_DH), st4(1, GATE_LANES)],
        [pl.BlockSpec((1, chunk, A_WIDTH), lambda b, c, *_: (b, c, 0)),
         st4(A_DH, A_DH), st4(1, A_DH), st4(1, GATE_LANES)],
        [jax.ShapeDtypeStruct((bsz, t_len, A_WIDTH), BF16),
         jax.ShapeDtypeStruct((bsz, hd, A_DH, A_DH), F32),
         jax.ShapeDtypeStruct((bsz, hd, 1, A_DH), F32),
         jax.ShapeDtypeStruct((bsz, hd, 1, GATE_LANES), F32)],
        [pltpu.VMEM((hd, A_DH, A_DH), F32), pltpu.VMEM((hd, 1, A_DH), F32),
         pltpu.VMEM((hd, 1, GATE_LANES), F32)],
        phases, [qkvog, qkvog, qkvog, qkvog, qkvog, gcol, grow, c0, n0, m0], rider)
    return tuple(outs) if rider is None else (tuple(outs), rest)


def _stick_block(q, kb, vb, bias, run, mask, upper):
    rows = q.shape[0]
    sub = upper.shape[0]
    n_sub = kb.shape[0] // sub
    z = _dot_nt(q, kb) * (B_DH ** -0.5) + bias
    sp = _softplus(z)
    spm = sp if mask is None else jnp.where(mask, sp, 0.0)
    hi, lo = _split_hi_lo(spm)
    laters = [None] * n_sub
    total = None
    for i in reversed(range(n_sub)):
        ln = slice(i * sub, (i + 1) * sub)
        both = _dot(jnp.concatenate([hi[:, ln], lo[:, ln]], axis=0), upper)
        carry = run if total is None else run + total
        laters[i] = both[:rows] + both[rows:] + carry
        part = jnp.sum(spm[:, ln], axis=1, keepdims=True)
        total = part if total is None else total + part
    later = laters[0] if n_sub == 1 else jnp.concatenate(laters, axis=1)
    a = jnp.exp(z - sp - later)
    if mask is not None:
        a = jnp.where(mask, a, 0.0)
    return _dot(a.astype(BF16), vb), total


def _strict_upper(n):
    j = lax.broadcasted_iota(jnp.int32, (n, n), 0)
    s = lax.broadcasted_iota(jnp.int32, (n, n), 1)
    return jnp.where(j > s, 1.0, 0.0).astype(BF16)


def _attn_prompt_kernel(bias_ref, q_ref, k_ref, v_ref, g_ref, o_ref, *, bq, bk, n_heads):
    head0 = pl.program_id(1) * n_heads
    qi = pl.program_id(2)
    kbf = k_ref.at[0]
    vbf = v_ref.at[0]
    upper = _strict_upper(min(bk, CUMSUM_BLOCK))
    row = lax.broadcasted_iota(jnp.int32, (bq, bk), 0)
    col = lax.broadcasted_iota(jnp.int32, (bq, bk), 1)
    lanes = [slice(h * B_DH, (h + 1) * B_DH) for h in range(n_heads)]
    qs = [q_ref[0, :, ln] for ln in lanes]
    biases = [bias_ref[head0 + h] for h in range(n_heads)]

    def blocks(kj, runs, mask):
        start = pl.multiple_of(kj * bk, bk)
        return [_stick_block(qs[h], kbf[pl.ds(start, bk), ln], vbf[pl.ds(start, bk), ln],
                             biases[h], runs[h], mask, upper) for h, ln in enumerate(lanes)]

    q0 = qi * bq
    n_full = q0 // bk
    accs = [jnp.zeros((bq, B_DH), F32)] * n_heads
    runs = [jnp.zeros((bq, 1), F32)] * n_heads
    for m in reversed(range(max(1, bq // bk))):
        kj = n_full + m
        res = blocks(kj, runs, col + (kj * bk - q0) < row)
        accs = [a + c for a, (c, _) in zip(accs, res)]
        runs = [r + t for r, (_, t) in zip(runs, res)]

    def body(it, carry):
        accs, runs = carry
        res = blocks(n_full - 1 - it, runs, None)
        return (tuple(a + c for a, (c, _) in zip(accs, res)),
                tuple(r + t for r, (_, t) in zip(runs, res)))

    accs, runs = lax.fori_loop(0, n_full, body, (tuple(accs), tuple(runs)))
    for h, ln in enumerate(lanes):
        o_ref[0, :, ln] = (accs[h] * _silu(g_ref[0, :, ln].astype(F32))).astype(o_ref.dtype)


def attn_prompt(q, g, k, v, b_sb, bq, bk, n_heads):
    bsz, t_len, _ = k.shape
    width = n_heads * B_DH
    q_spec = pl.BlockSpec((1, bq, width), lambda b, h, i: (b, i, h))
    kv_spec = pl.BlockSpec((1, t_len, width), lambda b, h, i: (b, 0, h))
    return pl.pallas_call(
        functools.partial(_attn_prompt_kernel, bq=bq, bk=bk, n_heads=n_heads),
        grid=(bsz, B_HEADS // n_heads, t_len // bq),
        in_specs=[pl.BlockSpec(memory_space=pltpu.SMEM), q_spec, kv_spec, kv_spec, q_spec],
        out_specs=q_spec,
        out_shape=jax.ShapeDtypeStruct((bsz, t_len, B_WIDTH), BF16),
        compiler_params=_params("arbitrary", "arbitrary", "arbitrary"),
        name="attn_prompt",
    )(b_sb, q, k, v, g)


def _attn_sample_phases(ins, outs, scratch, n_group, n_parts):
    bias_ref, q_ref, knew_ref, vnew_ref, acc_in_ref, run_in_ref = ins[:6]
    k_refs = ins[6:6 + n_group]
    v_refs = ins[6 + n_group:]
    acc_out_ref, run_out_ref = outs
    qbd, acc, run, kcat, vcat = scratch
    rows = B_HEADS * SAMPLE_PAD
    upper = _strict_upper(PAGE_SIZE)
    bias = bias_ref[...][:, :1]

    def repack(page, dst, i):
        for h in range(B_HEADS):
            dst[i * PAGE_SIZE:(i + 1) * PAGE_SIZE, h * B_DH:(h + 1) * B_DH] = page(h).astype(BF16)

    def step(slot0, n_blk, mask):
        keys = slice(slot0 * PAGE_SIZE, (slot0 + n_blk) * PAGE_SIZE)
        z = _dot_nt(qbd[...], kcat[keys, :]) * (B_DH ** -0.5) + bias
        sp = _softplus(z)
        spm = sp if mask is None else jnp.where(mask, sp, 0.0)
        hi, lo = _split_hi_lo(spm)
        carry = run[...][:, :1]
        laters = []
        for i in range(n_blk):
            ln = slice(i * PAGE_SIZE, (i + 1) * PAGE_SIZE)
            both = _dot(jnp.concatenate([hi[:, ln], lo[:, ln]], axis=0), upper)
            laters.append(both[:rows] + both[rows:] + carry)
            carry = carry + jnp.sum(spm[:, ln], axis=1, keepdims=True)
        later = laters[0] if n_blk == 1 else jnp.concatenate(laters, axis=1)
        a = jnp.exp(z - sp - later)
        if mask is not None:
            a = jnp.where(mask, a, 0.0)
        acc[...] += _dot(a.astype(BF16), vcat[keys, :])
        run[...] = jnp.broadcast_to(carry, run.shape)

    def build_queries():
        r = lax.broadcasted_iota(jnp.int32, (rows, B_WIDTH), 0)
        c = lax.broadcasted_iota(jnp.int32, (rows, B_WIDTH), 1)
        q_rep = jnp.concatenate([q_ref[0].astype(F32)] * B_HEADS, axis=0)
        qbd[...] = jnp.where((r // SAMPLE_PAD) == (c // B_DH), q_rep, 0.0).astype(BF16)

    def start_sequence():
        build_queries()
        acc[...] = jnp.zeros_like(acc)
        run[...] = jnp.zeros_like(run)
        repack(lambda h: knew_ref[0, pl.ds(h, PAGE_SIZE, stride=B_HEADS), :], kcat, 0)
        repack(lambda h: vnew_ref[0, pl.ds(h, PAGE_SIZE, stride=B_HEADS), :], vcat, 0)
        t = lax.broadcasted_iota(jnp.int32, (rows, PAGE_SIZE), 0) % SAMPLE_PAD
        s = lax.broadcasted_iota(jnp.int32, (rows, PAGE_SIZE), 1)
        step(0, 1, s < t)

    def resume_sequence():
        build_queries()
        acc[...] = acc_in_ref[0]
        run[...] = run_in_ref[0]

    def pages_part(slot0, n_blk, last):
        def run_part():
            for i in range(slot0, slot0 + n_blk):
                repack(lambda h, r=k_refs[i]: r[0, 0, pl.ds(h, PAGE_SIZE, stride=B_HEADS), :], kcat, i)
                repack(lambda h, r=v_refs[i]: r[0, 0, pl.ds(h, PAGE_SIZE, stride=B_HEADS), :], vcat, i)
            step(slot0, n_blk, None)
            if last:
                acc_out_ref[0] = acc[...]
                run_out_ref[0] = run[...]
        return run_part

    per_part = n_group // n_parts
    parts = [pages_part(k * per_part, per_part, k == n_parts - 1) for k in range(n_parts)]
    return start_sequence, resume_sequence, parts


class _PagedAttnRider:
    n_out = 2
    n_scratch = 5

    def __init__(self, q, k_new, v_new, cache_k, cache_v, page_table, bias_rows, acc, run, first_step, n_group):
        self.arrays = (bias_rows, q, k_new, v_new, acc, run)
        self.caches = (cache_k, cache_v)
        self.page_table = page_table
        self.first_step = first_step
        self.n_group = n_group
        self.n_in = 6 + 2 * n_group
        self.steps_per_seq = page_table.shape[1] // n_group

    def specs(self, lin, rank):
        n_group, spq = self.n_group, self.steps_per_seq
        n_pages = self.page_table.shape[1]
        rows = B_HEADS * SAMPLE_PAD
        page_rows = PAGE_SIZE * B_HEADS
        gstep = lambda a: self.first_step + lin(*a[:rank])
        seq_map = lambda *a: (gstep(a) // spq, 0, 0)

        def page_spec(i):
            def index(*a):
                g, pt = gstep(a), a[rank]
                return (0, pt[g // spq, n_pages - 1 - ((g % spq) * n_group + i)], 0, 0)
            return pl.BlockSpec((1, 1, page_rows, B_DH), index)

        acc_spec = pl.BlockSpec((1, rows, B_WIDTH), seq_map)
        run_spec = pl.BlockSpec((1, rows, GATE_LANES), seq_map)
        new_spec = pl.BlockSpec((1, page_rows, B_DH), seq_map)
        acc, run = self.arrays[4:]
        return dict(
            in_specs=[pl.BlockSpec((rows, GATE_LANES), lambda *a: (0, 0)),
                      pl.BlockSpec((1, SAMPLE_PAD, B_WIDTH), seq_map), new_spec, new_spec,
                      acc_spec, run_spec] + [page_spec(i) for i in range(n_group)] * 2,
            out_specs=[acc_spec, run_spec],
            out_shape=[jax.ShapeDtypeStruct(acc.shape, F32), jax.ShapeDtypeStruct(run.shape, F32)],
            scratch=[pltpu.VMEM((rows, B_WIDTH), BF16), pltpu.VMEM((rows, B_WIDTH), F32),
                     pltpu.VMEM((rows, GATE_LANES), F32),
                     pltpu.VMEM((n_group * PAGE_SIZE, B_WIDTH), BF16),
                     pltpu.VMEM((n_group * PAGE_SIZE, B_WIDTH), BF16)],
            args=list(self.arrays) + [self.caches[0]] * n_group + [self.caches[1]] * n_group,
            aliases={4: 0, 5: 1},
        )

    def phases(self, ins, outs, scratch, local_step):
        start, resume, parts = _attn_sample_phases(ins, outs, scratch, self.n_group, RIDER_PARTS)
        p = (self.first_step + local_step) % self.steps_per_seq
        return [(p == 0, start), ((local_step == 0) & (p != 0), resume)], parts, []


def attn_sample_finish(acc, g):
    n_seq = acc.shape[0]

    def kern(acc_ref, g_ref, o_ref):
        a = acc_ref[0]
        c = lax.broadcasted_iota(jnp.int32, (SAMPLE_PAD, B_WIDTH), 1) // B_DH
        out = jnp.zeros((SAMPLE_PAD, B_WIDTH), F32)
        for h in range(B_HEADS):
            out = out + jnp.where(c == h, a[h * SAMPLE_PAD:(h + 1) * SAMPLE_PAD, :], 0.0)
        o_ref[0] = out * _silu(g_ref[0])

    return pl.pallas_call(
        kern, grid=(n_seq,),
        in_specs=[pl.BlockSpec((1, B_HEADS * SAMPLE_PAD, B_WIDTH), lambda s: (s, 0, 0)),
                  pl.BlockSpec((1, SAMPLE_PAD, B_WIDTH), lambda s: (s, 0, 0))],
        out_specs=pl.BlockSpec((1, SAMPLE_PAD, B_WIDTH), lambda s: (s, 0, 0)),
        out_shape=jax.ShapeDtypeStruct((n_seq, SAMPLE_PAD, B_WIDTH), F32),
        compiler_params=_params("arbitrary"), name="attn_sample_finish",
    )(acc, g)


def _proj_act_kernel(a_ref, w_ref, vg_ref, o_ref, *rest, act, row_chunk):
    maybe_f32, wbf = rest[:-1], rest[-1]

    @pl.when(pl.program_id(1) == 0)
    def _():
        wbf[...] = w_ref[...].astype(BF16)

    for r in range(a_ref.shape[0] // row_chunk):
        rows = slice(r * row_chunk, (r + 1) * row_chunk)
        y = _dot(a_ref[rows, :], wbf[...])
        if act == "gelu":
            y = _gelu_tanh(y)
        elif act == "silu":
            y = _silu(y)
        else:
            y = _rms(_gelu_tanh(y), vg_ref[...])
        o_ref[rows, :] = y.astype(o_ref.dtype)
        if maybe_f32:
            maybe_f32[0][rows, :] = y


def proj_act(h, w, col0, v_gain, act, tm, tn, out_dtype, emit_f32=False):
    m = h.shape[0]
    j0 = col0 // tn
    n_col = C_WIDTH // tn
    out_spec = pl.BlockSpec((tm, tn), lambda j, i: (i, j))
    n_out = 2 if emit_f32 else 1
    w_mode = dict(pipeline_mode=pl.Buffered(1)) if n_col == 1 else {}
    return pl.pallas_call(
        functools.partial(_proj_act_kernel, act=act, row_chunk=min(tm, 256)),
        grid=(n_col, m // tm),
        in_specs=[pl.BlockSpec((tm, D_MODEL), lambda j, i: (i, 0)),
                  pl.BlockSpec((D_MODEL, tn), lambda j, i: (0, j0 + j), **w_mode),
                  pl.BlockSpec((1, tn), lambda j, i: (0, j))],
        out_specs=[out_spec] * n_out,
        out_shape=[jax.ShapeDtypeStruct((m, C_WIDTH), out_dtype)]
                  + ([jax.ShapeDtypeStruct((m, C_WIDTH), F32)] if emit_f32 else []),
        scratch_shapes=[pltpu.VMEM((D_MODEL, tn), BF16)],
        compiler_params=_params("arbitrary", "arbitrary"),
        name="proj_" + act,
    )(h, w, v_gain.reshape(1, C_WIDTH))


def odd_in(h, w, v_gain, tm, act_dtype, emit_v32):
    u = proj_act(h, w, 0, v_gain, "gelu", tm, 1024, act_dtype)[0]
    v = proj_act(h, w, C_WIDTH, v_gain, "gelu_rms", min(tm, 512), C_WIDTH, act_dtype, emit_v32)
    g = proj_act(h, w, 2 * C_WIDTH, v_gain, "silu", tm, 1024, act_dtype)[0]
    return [u, v[0], g] + ([v[1]] if emit_v32 else [])


def _spatial_kernel(u_ref, v_ref, g_ref, ws_ref, bs_ref, y_ref, *, chunk, n_chunks):
    tt = lax.broadcasted_iota(jnp.int32, (chunk, chunk), 0)
    ss = lax.broadcasted_iota(jnp.int32, (chunk, chunk), 1)
    causal = ss <= tt
    for grp in range(C_GROUPS):
        wm = jnp.where(causal, ws_ref[grp], 0.0)
        bcol = bs_ref[:, grp:grp + 1]
        cols = slice(grp * C_GDIM, (grp + 1) * C_GDIM)
        for c in range(n_chunks):
            rows = slice(c * chunk, (c + 1) * chunk)
            vv = v_ref[rows, cols]
            if chunk >= 128:
                sv = _dot(wm.astype(BF16), vv)
            else:
                vf = vv.astype(F32)
                sv = jnp.zeros((chunk, C_GDIM), F32)
                for s in range(chunk):
                    sv = sv + wm[:, s:s + 1] * vf[s:s + 1, :]
            sv = sv + bcol
            y = u_ref[rows, cols].astype(F32) * sv * g_ref[rows, cols].astype(F32)
            y_ref[rows, cols] = y.astype(y_ref.dtype)


def spatial_gate(u, v, g, w_s, b_s_t, chunk, n_chunks):
    m = u.shape[0]
    tm = chunk * n_chunks
    row_spec = pl.BlockSpec((tm, C_WIDTH), lambda i: (i, 0))
    return pl.pallas_call(
        functools.partial(_spatial_kernel, chunk=chunk, n_chunks=n_chunks),
        grid=(m // tm,),
        in_specs=[row_spec, row_spec, row_spec,
                  pl.BlockSpec((C_GROUPS, chunk, chunk), lambda i: (0, 0, 0)),
                  pl.BlockSpec((chunk, C_GROUPS), lambda i: (0, 0))],
        out_specs=row_spec,
        out_shape=jax.ShapeDtypeStruct((m, C_WIDTH), u.dtype),
        compiler_params=_params("arbitrary"),
        name="spatial_gate",
    )(u, v, g, w_s, b_s_t)


def _even_weights(w_in, b_i, b_f):
    gate0 = 5 * A_WIDTH
    b0 = gate0 + 2 * A_HEADS
    wt = jnp.swapaxes(w_in, 0, 1)
    w_gate = jnp.pad(w_in[:, gate0:b0], ((0, 0), (0, GATE_LANES - 2 * A_HEADS)))
    bias = jnp.pad(jnp.concatenate([b_i, b_f]), (0, GATE_LANES - 2 * A_HEADS)).reshape(1, GATE_LANES)
    return wt, b0, w_gate, bias


class _PageStream:
    def __init__(self, make_rider, acc, run, total_steps):
        self.make_rider, self.state, self.step, self.total = make_rider, (acc, run), 0, total_steps

    def host(self, n_steps, fn, *args):
        if self.step + n_steps > self.total:
            return fn(*args)
        out, state = fn(*args, rider=self.make_rider(self.step, *self.state))
        self.state, self.step = tuple(state), self.step + n_steps
        return out


def _even_front(x, bsz, t_len, valid_len, ew, g_norm, act_dtype, tm, stream=None):
    wt, b0, w_gate, bias = ew
    m = bsz * t_len
    host = stream.host if stream is not None else (lambda n_steps, fn, *args: fn(*args))
    h, gates = norm_gates(x, g_norm, w_gate, bias, min(tm, 512))
    tn, tm_kv = 1024, min(tm, 512)
    qkvog = host(5 * (m // tm), proj, h, wt, 0, 5 * A_WIDTH, BF16, tm, tn).reshape(bsz, t_len, 5 * A_WIDTH)
    q_b = host(m // tm, proj, h, wt, b0, B_WIDTH, act_dtype, tm, tn).reshape(bsz, t_len, B_WIDTH)
    g_b = proj(h, wt, b0 + 3 * B_WIDTH, B_WIDTH, act_dtype, tm, tn).reshape(bsz, t_len, B_WIDTH)
    k_new, k_bf = host(m // tm_kv, kv_proj, h, wt, b0 + B_WIDTH, tm_kv)
    v_new, v_bf = kv_proj(h, wt, b0 + 2 * B_WIDTH, tm_kv)

    gates = gates.reshape(bsz, t_len, GATE_LANES)
    t_pad = -(-t_len // A_CHUNK) * A_CHUNK
    pad = ((0, 0), (0, t_pad - t_len), (0, 0))
    if valid_len < t_pad:
        qkvog, gates = jnp.pad(qkvog, pad), jnp.pad(gates, pad)
        pos = jnp.arange(t_pad)[None, :, None]
        lane = jnp.arange(GATE_LANES)[None, None, :]
        gates = jnp.where((pos >= valid_len) & (lane < A_HEADS), NEG_BIG, gates)
        gates = jnp.where((pos >= valid_len) & (lane >= A_HEADS), 0.0, gates)
    grow = gates[:, :, :2 * A_HEADS].transpose(0, 2, 1)
    return dict(qkvog=qkvog, gates=gates, grow=grow, q_b=q_b, g_b=g_b,
                k_new=k_new, v_new=v_new, k_bf=k_bf, v_bf=v_bf)


def _even_back(x, h_a, h_b, w_out, next_gain, tm):
    m_rows = x.shape[0]
    w_out_b = w_out.astype(BF16)
    return out_proj_norm(
        [h_a.reshape(m_rows, A_WIDTH), h_b.reshape(m_rows, B_WIDTH)],
        [w_out_b[:A_WIDTH], w_out_b[A_WIDTH:]], x, next_gain, min(tm, 512), True, BF16)


def _odd_layer(x, h, w_in_b, v_gain, w_s, b_s, w_out_b, final_gain, tm, chunk, n_chunks, act_dtype, emit_v32):
    outs = odd_in(h, w_in_b, v_gain, min(2 * tm, h.shape[0]), act_dtype, emit_v32)
    u, v, g = outs[:3]
    y = spatial_gate(u, v, g, w_s[:, :chunk, :chunk], b_s[:, :chunk].T, chunk, n_chunks)
    (y_out,) = out_proj_norm([y], [w_out_b], x, final_gain, tm, False, F32)
    return y_out, (outs[3] if emit_v32 else None)


def kernel(x_prompt, x_sample, state_a_C, state_a_n, state_a_m, cache_b_k, cache_b_v, page_table,
           even_norm, even_w_in, even_b_i, even_b_f, even_b_sb, even_w_out,
           odd_norm, odd_w_in, odd_v_gain, odd_w_s, odd_b_s, odd_w_out, final_norm):
    bsz, seq, _ = x_prompt.shape
    n_seq, dec_seq, _ = x_sample.shape
    n_pool = cache_b_k.shape[1]

    ew = _even_weights(even_w_in[0], even_b_i[0], even_b_f[0])
    odd_w_in_b = odd_w_in[0]
    odd_w_out_b = odd_w_out[0].astype(BF16)

    xp = x_prompt.reshape(bsz * seq, D_MODEL)
    xs = jnp.pad(x_sample, ((0, 0), (0, SAMPLE_PAD - dec_seq), (0, 0))).reshape(n_seq * SAMPLE_PAD, D_MODEL)
    m_s = n_seq * SAMPLE_PAD
    fs = _even_front(xs, n_seq, SAMPLE_PAD, dec_seq, ew, even_norm[0], F32, m_s)

    assert cache_b_k.shape[0] == 1 and cache_b_v.shape[0] == 1
    page_view = (1, n_pool, PAGE_SIZE * B_HEADS, B_DH)
    cache_k, cache_v = cache_b_k.reshape(page_view), cache_b_v.reshape(page_view)
    kv_pad = ((0, 0), (0, (PAGE_SIZE - SAMPLE_PAD) * B_HEADS), (0, 0))
    k_new_s = jnp.pad(fs["k_new"].reshape(n_seq, SAMPLE_PAD * B_HEADS, B_DH), kv_pad)
    v_new_s = jnp.pad(fs["v_new"].reshape(n_seq, SAMPLE_PAD * B_HEADS, B_DH), kv_pad)
    bias_rows = jnp.broadcast_to(jnp.repeat(even_b_sb[0], SAMPLE_PAD)[:, None],
                                 (B_HEADS * SAMPLE_PAD, GATE_LANES))
    att_rows = B_HEADS * SAMPLE_PAD
    stream = _PageStream(
        lambda first_step, acc, run: _PagedAttnRider(
            fs["q_b"], k_new_s, v_new_s, cache_k, cache_v, page_table, bias_rows, acc, run,
            first_step, PAGES_PER_STEP),
        jnp.zeros((n_seq, att_rows, B_WIDTH), F32), jnp.zeros((n_seq, att_rows, GATE_LANES), F32),
        n_seq * (page_table.shape[1] // PAGES_PER_STEP))
    fp = _even_front(xp, bsz, seq, seq, ew, even_norm[0], BF16, 1024, stream)
    zero_state = (jnp.zeros((bsz, A_HEADS, A_DH, A_DH), F32),
                  jnp.zeros((bsz, A_HEADS, 1, A_DH), F32),
                  jnp.zeros((bsz, A_HEADS, 1, GATE_LANES), F32))
    ha_p, c_p, n_p, m_p = stream.host(bsz * (seq // A_CHUNK), mlstm,
                                      fp["qkvog"], fp["gates"], fp["grow"], *zero_state, A_CHUNK)
    assert stream.step == stream.total
    hb_s = attn_sample_finish(stream.state[0], fs["g_b"])

    hb_p = attn_prompt(fp["q_b"], fp["g_b"], fp["k_bf"].reshape(bsz, seq, B_WIDTH),
                       fp["v_bf"].reshape(bsz, seq, B_WIDTH), even_b_sb[0],
                       ATTN_BQ, ATTN_BK, ATTN_HEADS_PER_STEP)
    st_in = (state_a_C[0], state_a_n[0][:, :, None, :],
             jnp.broadcast_to(state_a_m[0][:, :, None, None], (n_seq, A_HEADS, 1, GATE_LANES)))
    ha_s, c_s, n_s, m_s_new = mlstm(fs["qkvog"], fs["gates"], fs["grow"], *st_in, A_CHUNK)
    ha_s = ha_s[:, :SAMPLE_PAD]

    xp1, hp1 = _even_back(xp, ha_p, hb_p, even_w_out[0], odd_norm[0], 512)
    xs1, hs1 = _even_back(xs, ha_s, hb_s, even_w_out[0], odd_norm[0], m_s)
    y_p, _ = _odd_layer(xp1, hp1, odd_w_in_b, odd_v_gain[0], odd_w_s[0], odd_b_s[0], odd_w_out_b,
                        final_norm, 512, C_CHUNK, 4, BF16, False)
    y_s, v_rows = _odd_layer(xs1, hs1, odd_w_in_b, odd_v_gain[0], odd_w_s[0], odd_b_s[0], odd_w_out_b,
                             final_norm, m_s, SAMPLE_PAD, n_seq, F32, True)

    def sample_rows(a, *dims):
        return a.reshape((n_seq, SAMPLE_PAD) + dims)[:, :dec_seq]

    return (y_p.reshape(bsz, seq, D_MODEL),
            sample_rows(y_s, D_MODEL),
            c_p[None], n_p[:, :, 0, :][None], m_p[:, :, 0, 0][None],
            c_s[None], n_s[:, :, 0, :][None], m_s_new[:, :, 0, 0][None],
            fp["k_new"].reshape(1, bsz, seq, B_HEADS, B_DH), fp["v_new"].reshape(1, bsz, seq, B_HEADS, B_DH),
            sample_rows(fs["k_new"], B_HEADS, B_DH)[None], sample_rows(fs["v_new"], B_HEADS, B_DH)[None],
            sample_rows(v_rows, C_WIDTH)[None])
```

```python
import functools

import jax
import jax.numpy as jnp
from jax import lax
from jax.experimental import pallas as pl
from jax.experimental.pallas import tpu as pltpu

F32 = jnp.float32
BF16 = jnp.bfloat16

D_MODEL = 2048
PAGE_SIZE = 128
A_HEADS = 4
A_DH = 256
A_WIDTH = A_HEADS * A_DH
A_CHUNK = 128
B_HEADS = 8
B_DH = 128
B_WIDTH = B_HEADS * B_DH
C_WIDTH = D_MODEL
C_GROUPS = 8
C_GDIM = C_WIDTH // C_GROUPS
C_CHUNK = 128
RMS_EPS = 1e-6
GATE_LANES = 128
NEG_BIG = -1e30
SAMPLE_PAD = 8
ATTN_BQ = 512
ATTN_BK = 512
CUMSUM_BLOCK = 256
ATTN_HEADS_PER_STEP = 2
PAGES_PER_STEP = 16
RIDER_PARTS = 1
HOST_ROW_CHUNK = 1024

VMEM_LIMIT_BYTES = 56 * 1024 * 1024


def _params(*sem, flags=None):
    return pltpu.CompilerParams(dimension_semantics=sem, vmem_limit_bytes=VMEM_LIMIT_BYTES, flags=flags)


def _dot(a, b):
    return jnp.dot(a, b, preferred_element_type=F32)


def _dot_nt(a, b):
    return lax.dot_general(a, b, (((1,), (1,)), ((), ())), preferred_element_type=F32)


def _dot_tn(a, b):
    return lax.dot_general(a, b, (((0,), (0,)), ((), ())), preferred_element_type=F32)


def _softplus(z):
    return jnp.maximum(z, 0.0) + jnp.log(1.0 + jnp.exp(-jnp.abs(z)))


def _sigmoid(z):
    return 1.0 / (1.0 + jnp.exp(-z))


def _silu(z):
    return z * _sigmoid(z)


def _gelu_tanh(x):
    c = 0.7978845608028654
    return x * (0.5 * (1.0 + jnp.tanh(c * (x + 0.044715 * (x * x * x)))))


def _rms(x, g):
    return x * lax.rsqrt(jnp.mean(x * x, axis=-1, keepdims=True) + RMS_EPS) * g


def _split_hi_lo(x):
    hi = x.astype(BF16)
    lo = (x - hi.astype(F32)).astype(BF16)
    return hi, lo


def _norm_gates_kernel(x_ref, g_ref, whi_ref, wlo_ref, bias_ref, h_ref, gate_ref):
    h = _rms(x_ref[...], g_ref[...])
    h_hi, h_lo = _split_hi_lo(h)
    h_ref[...] = h_hi
    pre = (_dot(h_hi, whi_ref[...]) + _dot(h_hi, wlo_ref[...]) + _dot(h_lo, whi_ref[...])
           + bias_ref[...])
    lane = lax.broadcasted_iota(jnp.int32, pre.shape, 1)
    is_forget = (lane >= A_HEADS) & (lane < 2 * A_HEADS)
    gate_ref[...] = jnp.where(is_forget, -_softplus(-pre), pre)


def norm_gates(x, gain, w_gate, bias, tm):
    m = x.shape[0]
    whi, wlo = _split_hi_lo(w_gate)
    return pl.pallas_call(
        _norm_gates_kernel,
        grid=(m // tm,),
        in_specs=[pl.BlockSpec((tm, D_MODEL), lambda i: (i, 0)),
                  pl.BlockSpec((1, D_MODEL), lambda i: (0, 0)),
                  pl.BlockSpec((D_MODEL, GATE_LANES), lambda i: (0, 0)),
                  pl.BlockSpec((D_MODEL, GATE_LANES), lambda i: (0, 0)),
                  pl.BlockSpec((1, GATE_LANES), lambda i: (0, 0))],
        out_specs=[pl.BlockSpec((tm, D_MODEL), lambda i: (i, 0)),
                   pl.BlockSpec((tm, GATE_LANES), lambda i: (i, 0))],
        out_shape=[jax.ShapeDtypeStruct((m, D_MODEL), BF16),
                   jax.ShapeDtypeStruct((m, GATE_LANES), F32)],
        compiler_params=_params("arbitrary"),
        name="norm_gates",
    )(x, gain.reshape(1, D_MODEL), whi, wlo, bias)


def _hosted_kernel(*refs, n_in, n_out, phases_fn, lin, rank, rider):
    if rider is not None:
        refs = refs[1:]
    r_in, r_out = (rider.n_in, rider.n_out) if rider is not None else (0, 0)
    ins, refs = refs[:n_in], refs[n_in:]
    r_ins, refs = refs[:r_in], refs[r_in:]
    outs, refs = refs[:n_out], refs[n_out:]
    r_outs, refs = refs[:r_out], refs[r_out:]
    n_sc = len(refs) - (rider.n_scratch if rider is not None else 0)
    scratch, r_scratch = refs[:n_sc], refs[n_sc:]
    sets = [phases_fn(ins, outs, scratch)]
    if rider is not None:
        step = lin(*[pl.program_id(d) for d in range(rank)])
        sets.append(rider.phases(r_ins, r_outs, r_scratch, step))
    for pre, _, _ in sets:
        for cond, fn in pre:
            pl.when(cond)(fn)
    for _, parts, _ in sets:
        for part in parts:
            part()
    for _, _, post in sets:
        for cond, fn in post:
            pl.when(cond)(fn)


def _hosted_call(name, grid, lin, in_specs, out_specs, out_shape, scratch, phases_fn, args, rider=None):
    n_in, n_out = len(in_specs), len(out_specs)
    aliases = {}
    if rider is not None:
        r = rider.specs(lin, len(grid))
        aliases = {1 + n_in + i: n_out + o for i, o in r["aliases"].items()}
        in_specs, out_specs = in_specs + r["in_specs"], out_specs + r["out_specs"]
        out_shape, scratch = out_shape + r["out_shape"], scratch + r["scratch"]
        args = [rider.page_table] + list(args) + r["args"]
    kern = functools.partial(_hosted_kernel, n_in=n_in, n_out=n_out, phases_fn=phases_fn,
                             lin=lin, rank=len(grid), rider=rider)
    grid_spec = pltpu.PrefetchScalarGridSpec(
        num_scalar_prefetch=0 if rider is None else 1, grid=grid,
        in_specs=in_specs, out_specs=out_specs, scratch_shapes=scratch)
    outs = pl.pallas_call(
        kern, grid_spec=grid_spec, out_shape=out_shape, input_output_aliases=aliases,
        compiler_params=_params(*(["arbitrary"] * len(grid))), name=name,
    )(*args)
    return outs[:n_out], outs[n_out:]


def proj(a, wt, row0, n_out, out_dtype, tm, tn, rider=None, scale=None):
    m, k = a.shape
    n_i = m // tm

    def phases(ins, outs, scratch):
        (a_ref, wt_ref), (o_ref,), (wbf,) = ins, outs, scratch

        def cast():
            wbf[...] = wt_ref[...].astype(BF16)

        def chunk(rows):
            def run():
                y = _dot_nt(a_ref[rows, :], wbf[...])
                o_ref[rows, :] = (y if scale is None else y * scale).astype(o_ref.dtype)
            return run

        row_chunk = min(tm, HOST_ROW_CHUNK)
        parts = [chunk(slice(r, r + row_chunk)) for r in range(0, tm, row_chunk)]
        return [(pl.program_id(1) == 0, cast)], parts, []

    w_mode = dict(pipeline_mode=pl.Buffered(1)) if rider is not None else {}
    (out,), rest = _hosted_call(
        "proj", (n_out // tn, n_i), lambda j, i: j * n_i + i,
        [pl.BlockSpec((tm, k), lambda j, i, *_: (i, 0)),
         pl.BlockSpec((pl.Element(tn), pl.Element(k)),
                      lambda j, i, *_: (pl.multiple_of(row0 + j * tn, 8), 0), **w_mode)],
        [pl.BlockSpec((tm, tn), lambda j, i, *_: (i, j))],
        [jax.ShapeDtypeStruct((m, n_out), out_dtype)],
        [pltpu.VMEM((tn, k), BF16)], phases, [a, wt], rider)
    return out if rider is None else (out, rest)


def kv_proj(a, wt, row0, tm, rider=None):
    m, k = a.shape

    def phases(ins, outs, scratch):
        (a_ref, wt_ref), (o_ref, obf_ref), (wbf,) = ins, outs, scratch

        def cast():
            wbf[...] = wt_ref[...].astype(BF16)

        def chunk(r0, n_rows):
            def run():
                y = _dot_nt(a_ref[r0:r0 + n_rows, :], wbf[...])
                obf_ref[r0:r0 + n_rows, :] = y.astype(BF16)
                for h in range(B_HEADS):
                    o_ref[pl.ds(r0 * B_HEADS + h, n_rows, stride=B_HEADS), :] = y[:, h * B_DH:(h + 1) * B_DH]
            return run

        row_chunk = min(tm, HOST_ROW_CHUNK)
        parts = [chunk(r, row_chunk) for r in range(0, tm, row_chunk)]
        return [(pl.program_id(0) == 0, cast)], parts, []

    outs, rest = _hosted_call(
        "kv_proj", (m // tm,), lambda i: i,
        [pl.BlockSpec((tm, k), lambda i, *_: (i, 0)),
         pl.BlockSpec((pl.Element(B_WIDTH), pl.Element(k)), lambda i, *_: (row0, 0))],
        [pl.BlockSpec((tm * B_HEADS, B_DH), lambda i, *_: (i, 0)),
         pl.BlockSpec((tm, B_WIDTH), lambda i, *_: (i, 0))],
        [jax.ShapeDtypeStruct((m * B_HEADS, B_DH), F32), jax.ShapeDtypeStruct((m, B_WIDTH), BF16)],
        [pltpu.VMEM((B_WIDTH, k), BF16)], phases, [a, wt], rider)
    return tuple(outs) if rider is None else (tuple(outs), rest)


def _out_proj_kernel(*refs, n_lhs, emit_x):
    a_refs = refs[:n_lhs]
    w_refs = refs[n_lhs:2 * n_lhs]
    x_ref, g_ref = refs[2 * n_lhs], refs[2 * n_lhs + 1]
    out_refs = refs[2 * n_lhs + 2:]
    tm = x_ref.shape[0]
    row_chunk = min(tm, 256)
    for r in range(tm // row_chunk):
        rows = slice(r * row_chunk, (r + 1) * row_chunk)
        y = x_ref[rows, :]
        for a_ref, w_ref in zip(a_refs, w_refs):
            y = y + _dot(a_ref[rows, :].astype(BF16), w_ref[...])
        if emit_x:
            out_refs[0][rows, :] = y
        out_refs[-1][rows, :] = _rms(y, g_ref[...]).astype(out_refs[-1].dtype)


def out_proj_norm(lhs, ws, x, gain, tm, emit_x, norm_dtype):
    m = x.shape[0]
    n_lhs = len(lhs)
    in_specs = ([pl.BlockSpec((tm, a.shape[1]), lambda i: (i, 0)) for a in lhs]
                + [pl.BlockSpec(w.shape, lambda i: (0, 0)) for w in ws]
                + [pl.BlockSpec((tm, D_MODEL), lambda i: (i, 0)),
                   pl.BlockSpec((1, D_MODEL), lambda i: (0, 0))])
    out_specs = [pl.BlockSpec((tm, D_MODEL), lambda i: (i, 0))]
    out_shape = [jax.ShapeDtypeStruct((m, D_MODEL), norm_dtype)]
    if emit_x:
        out_specs = [pl.BlockSpec((tm, D_MODEL), lambda i: (i, 0))] + out_specs
        out_shape = [jax.ShapeDtypeStruct((m, D_MODEL), F32)] + out_shape
    return pl.pallas_call(
        functools.partial(_out_proj_kernel, n_lhs=n_lhs, emit_x=emit_x),
        grid=(m // tm,),
        in_specs=in_specs,
        out_specs=out_specs,
        out_shape=out_shape,
        compiler_params=_params("arbitrary"),
        name="out_proj_norm",
    )(*lhs, *ws, x, gain.reshape(1, D_MODEL))


def _mlstm_chunk(q_ref, k_ref, v_ref, og_ref, gg_ref, gcol_ref, grow_ref, h_ref, c_sc, n_sc, m_sc, chunk, heads):
    L = chunk
    gcol = gcol_ref[0]
    grow = grow_ref[0]
    tt = lax.broadcasted_iota(jnp.int32, (L, L), 0)
    ss = lax.broadcasted_iota(jnp.int32, (L, L), 1)
    causal = ss <= tt

    for head in heads:
        cols = slice(head * A_DH, (head + 1) * A_DH)
        q = q_ref[0, :, cols]
        ks = k_ref[0, :, cols] * jnp.asarray(A_DH ** -0.5, BF16)
        v = v_ref[0, :, cols]
        ig_col = gcol[:, head:head + 1]
        lf_col = gcol[:, head + A_HEADS:head + A_HEADS + 1]
        ig_row = grow[head:head + 1, :]
        lf_row = grow[head + A_HEADS:head + A_HEADS + 1, :]
        b_col = jnp.sum(jnp.where(causal, lf_row, 0.0), axis=1, keepdims=True)
        b_row = jnp.sum(jnp.where(tt <= ss, lf_col, 0.0), axis=0, keepdims=True)
        b_last = jnp.sum(lf_row, axis=1, keepdims=True)

        m0 = m_sc[head][:, :1]
        n0 = n_sc[head]
        c0 = c_sc[head]

        d = jnp.where(causal, b_col - b_row + ig_row, NEG_BIG)
        m_carry = b_col + m0
        m = jnp.maximum(m_carry, jnp.max(d, axis=1, keepdims=True))
        w_intra = jnp.exp(d - m)
        w_carry = jnp.exp(m_carry - m)
        s = _dot_nt(q, ks) * w_intra
        qf = q.astype(F32)
        num = _dot(s.astype(BF16), v) + w_carry * _dot_nt(q, c0.astype(BF16))
        den = jnp.sum(s, axis=1, keepdims=True) + w_carry * jnp.sum(qf * n0, axis=1, keepdims=True)
        h = num / jnp.maximum(jnp.abs(den), jnp.exp(-m))
        gated = h * _sigmoid(og_ref[0, :, cols].astype(F32)) * _silu(gg_ref[0, :, cols].astype(F32))
        h_ref[0, :, cols] = gated.astype(h_ref.dtype)

        m_carry_last = b_last + m0
        d_last_row = b_last - b_row + ig_row
        m_new = jnp.maximum(m_carry_last, jnp.max(d_last_row, axis=1, keepdims=True))
        wc_last = jnp.exp(m_carry_last - m_new)
        w_last_col = jnp.exp(b_last - b_col + ig_col - m_new)
        vw = (v.astype(F32) * w_last_col).astype(BF16)
        c_new = wc_last * c0 + _dot_tn(vw, ks)
        n_new = wc_last * n0 + jnp.sum(ks.astype(F32) * w_last_col, axis=0, keepdims=True)
        c_sc[head] = c_new
        n_sc[head] = n_new
        m_sc[head] = jnp.broadcast_to(m_new, (1, GATE_LANES))


def mlstm(qkvog, gcol, grow, c0, n0, m0, chunk, rider=None):
    bsz, t_len, _ = qkvog.shape
    nc = t_len // chunk
    hd = A_HEADS

    def phases(ins, outs, state):
        c0_ref, n0_ref, m0_ref = ins[7:]
        h_ref, c_out_ref, n_out_ref, m_out_ref = outs
        c_sc, n_sc, m_sc = state
        ci = pl.program_id(1)

        def init():
            c_sc[...] = c0_ref[0]
            n_sc[...] = n0_ref[0]
            m_sc[...] = m0_ref[0]

        def head_part(head):
            return lambda: _mlstm_chunk(*ins[:7], h_ref, c_sc, n_sc, m_sc, chunk, (head,))

        parts = [head_part(head) for head in range(A_HEADS)]

        def final():
            c_out_ref[0] = c_sc[...]
            n_out_ref[0] = n_sc[...]
            m_out_ref[0] = m_sc[...]

        return [(ci == 0, init)], parts, [(ci == nc - 1, final)]

    blk = lambda seg: pl.BlockSpec((1, chunk, A_WIDTH), lambda b, c, *_, seg=seg: (b, c, seg))
    st4 = lambda r, w: pl.BlockSpec((1, hd, r, w), lambda b, c, *_: (b, 0, 0, 0))
    outs, rest = _hosted_call(
        "mlstm", (bsz, nc), lambda b, c: b * nc + c,
        [blk(0), blk(1), blk(2), blk(3), blk(4),
         pl.BlockSpec((1, chunk, GATE_LANES), lambda b, c, *_: (b, c, 0)),
         pl.BlockSpec((1, 8, chunk), lambda b, c, *_: (b, 0, c)),
         st4(A_DH, A_DH), st4(1, A_DH), st4(1, GATE_LANES)],
        [pl.BlockSpec((1, chunk, A_WIDTH), lambda b, c, *_: (b, c, 0)),
         st4(A_DH, A_DH), st4(1, A_DH), st4(1, GATE_LANES)],
        [jax.ShapeDtypeStruct((bsz, t_len, A_WIDTH), BF16),
         jax.ShapeDtypeStruct((bsz, hd, A_DH, A_DH), F32),
         jax.ShapeDtypeStruct((bsz, hd, 1, A_DH), F32),
         jax.ShapeDtypeStruct((bsz, hd, 1, GATE_LANES), F32)],
        [pltpu.VMEM((hd, A_DH, A_DH), F32), pltpu.VMEM((hd, 1, A_DH), F32),
         pltpu.VMEM((hd, 1, GATE_LANES), F32)],
        phases, [qkvog, qkvog, qkvog, qkvog, qkvog, gcol, grow, c0, n0, m0], rider)
    return tuple(outs) if rider is None else (tuple(outs), rest)


def _stick_block(q, kb, vb, bias, run, mask, upper):
    rows = q.shape[0]
    sub = upper.shape[0]
    n_sub = kb.shape[0] // sub
    z = _dot_nt(q, kb) if bias is None else _dot_nt(q, kb) * (B_DH ** -0.5) + bias
    sp = _softplus(z)
    spm = sp if mask is None else jnp.where(mask, sp, 0.0)
    hi, lo = _split_hi_lo(spm)
    laters = [None] * n_sub
    total = None
    for i in reversed(range(n_sub)):
        ln = slice(i * sub, (i + 1) * sub)
        both = _dot(jnp.concatenate([hi[:, ln], lo[:, ln]], axis=0), upper)
        carry = run if total is None else run + total
        laters[i] = both[:rows] + both[rows:] + carry
        part = jnp.sum(spm[:, ln], axis=1, keepdims=True)
        total = part if total is None else total + part
    later = laters[0] if n_sub == 1 else jnp.concatenate(laters, axis=1)
    a = jnp.exp(z - sp - later)
    if mask is not None:
        a = jnp.where(mask, a, 0.0)
    return _dot(a.astype(BF16), vb), total


def _strict_upper(n):
    j = lax.broadcasted_iota(jnp.int32, (n, n), 0)
    s = lax.broadcasted_iota(jnp.int32, (n, n), 1)
    return jnp.where(j > s, 1.0, 0.0).astype(BF16)


def _attn_prompt_kernel(bias_ref, q_ref, k_ref, v_ref, g_ref, o_ref, *, bq, bk, n_heads):
    head0 = pl.program_id(1) * n_heads
    qi = pl.program_id(2)
    kbf = k_ref.at[0]
    vbf = v_ref.at[0]
    upper = _strict_upper(min(bk, CUMSUM_BLOCK))
    row = lax.broadcasted_iota(jnp.int32, (bq, bk), 0)
    col = lax.broadcasted_iota(jnp.int32, (bq, bk), 1)
    lanes = [slice(h * B_DH, (h + 1) * B_DH) for h in range(n_heads)]
    lane_q = lax.broadcasted_iota(jnp.int32, (bq, B_DH), 1)
    lane_k = lax.broadcasted_iota(jnp.int32, (bk, B_DH), 1)
    ones_cols = jnp.where(lane_q < 2, 1.0, 0.0).astype(BF16)
    qs = [jnp.concatenate([q_ref[0, :, ln], ones_cols], axis=1) for ln in lanes]
    bias_cols = []
    for h in range(n_heads):
        b = jnp.full((bk, B_DH), bias_ref[head0 + h], F32)
        b_hi = b.astype(BF16).astype(F32)
        bias_cols.append(jnp.where(lane_k == 0, b_hi, jnp.where(lane_k == 1, b - b_hi, 0.0)).astype(BF16))

    def blocks(kj, runs, mask):
        start = pl.multiple_of(kj * bk, bk)
        return [_stick_block(qs[h], jnp.concatenate([kbf[pl.ds(start, bk), ln], bias_cols[h]], axis=1),
                             vbf[pl.ds(start, bk), ln], None, runs[h], mask, upper)
                for h, ln in enumerate(lanes)]

    q0 = qi * bq
    n_full = q0 // bk
    accs = [jnp.zeros((bq, B_DH), F32)] * n_heads
    runs = [jnp.zeros((bq, 1), F32)] * n_heads
    for m in reversed(range(max(1, bq // bk))):
        kj = n_full + m
        res = blocks(kj, runs, col + (kj * bk - q0) < row)
        accs = [a + c for a, (c, _) in zip(accs, res)]
        runs = [r + t for r, (_, t) in zip(runs, res)]

    def body(it, carry):
        accs, runs = carry
        res = blocks(n_full - 1 - it, runs, None)
        return (tuple(a + c for a, (c, _) in zip(accs, res)),
                tuple(r + t for r, (_, t) in zip(runs, res)))

    accs, runs = lax.fori_loop(0, n_full, body, (tuple(accs), tuple(runs)))
    for h, ln in enumerate(lanes):
        o_ref[0, :, ln] = (accs[h] * _silu(g_ref[0, :, ln].astype(F32))).astype(o_ref.dtype)


def attn_prompt(q, g, k, v, b_sb, bq, bk, n_heads):
    bsz, t_len, _ = k.shape
    width = n_heads * B_DH
    q_spec = pl.BlockSpec((1, bq, width), lambda b, h, i: (b, i, h))
    kv_spec = pl.BlockSpec((1, t_len, width), lambda b, h, i: (b, 0, h))
    return pl.pallas_call(
        functools.partial(_attn_prompt_kernel, bq=bq, bk=bk, n_heads=n_heads),
        grid=(bsz, B_HEADS // n_heads, t_len // bq),
        in_specs=[pl.BlockSpec(memory_space=pltpu.SMEM), q_spec, kv_spec, kv_spec, q_spec],
        out_specs=q_spec,
        out_shape=jax.ShapeDtypeStruct((bsz, t_len, B_WIDTH), BF16),
        compiler_params=_params("arbitrary", "arbitrary", "arbitrary"),
        name="attn_prompt",
    )(b_sb, q, k, v, g)


def _attn_sample_phases(ins, outs, scratch, n_group, n_parts):
    bias_ref, q_ref, knew_ref, vnew_ref, acc_in_ref, run_in_ref = ins[:6]
    k_refs = ins[6:6 + n_group]
    v_refs = ins[6 + n_group:]
    acc_out_ref, run_out_ref = outs
    qbd, acc, run, kcat, vcat = scratch
    rows = B_HEADS * SAMPLE_PAD
    upper = _strict_upper(PAGE_SIZE)
    bias = bias_ref[...][:, :1]

    def repack(page, dst, i):
        for h in range(B_HEADS):
            dst[i * PAGE_SIZE:(i + 1) * PAGE_SIZE, h * B_DH:(h + 1) * B_DH] = page(h).astype(BF16)

    def step(slot0, n_blk, mask):
        keys = slice(slot0 * PAGE_SIZE, (slot0 + n_blk) * PAGE_SIZE)
        z = _dot_nt(qbd[...], kcat[keys, :]) * (B_DH ** -0.5) + bias
        sp = _softplus(z)
        spm = sp if mask is None else jnp.where(mask, sp, 0.0)
        hi, lo = _split_hi_lo(spm)
        carry = run[...][:, :1]
        laters = []
        for i in range(n_blk):
            ln = slice(i * PAGE_SIZE, (i + 1) * PAGE_SIZE)
            both = _dot(jnp.concatenate([hi[:, ln], lo[:, ln]], axis=0), upper)
            laters.append(both[:rows] + both[rows:] + carry)
            carry = carry + jnp.sum(spm[:, ln], axis=1, keepdims=True)
        later = laters[0] if n_blk == 1 else jnp.concatenate(laters, axis=1)
        a = jnp.exp(z - sp - later)
        if mask is not None:
            a = jnp.where(mask, a, 0.0)
        acc[...] += _dot(a.astype(BF16), vcat[keys, :])
        run[...] = jnp.broadcast_to(carry, run.shape)

    def build_queries():
        r = lax.broadcasted_iota(jnp.int32, (rows, B_WIDTH), 0)
        c = lax.broadcasted_iota(jnp.int32, (rows, B_WIDTH), 1)
        q_rep = jnp.concatenate([q_ref[0].astype(F32)] * B_HEADS, axis=0)
        qbd[...] = jnp.where((r // SAMPLE_PAD) == (c // B_DH), q_rep, 0.0).astype(BF16)

    def start_sequence():
        build_queries()
        acc[...] = jnp.zeros_like(acc)
        run[...] = jnp.zeros_like(run)
        repack(lambda h: knew_ref[0, pl.ds(h, PAGE_SIZE, stride=B_HEADS), :], kcat, 0)
        repack(lambda h: vnew_ref[0, pl.ds(h, PAGE_SIZE, stride=B_HEADS), :], vcat, 0)
        t = lax.broadcasted_iota(jnp.int32, (rows, PAGE_SIZE), 0) % SAMPLE_PAD
        s = lax.broadcasted_iota(jnp.int32, (rows, PAGE_SIZE), 1)
        step(0, 1, s < t)

    def resume_sequence():
        build_queries()
        acc[...] = acc_in_ref[0]
        run[...] = run_in_ref[0]

    def pages_part(slot0, n_blk, last):
        def run_part():
            for i in range(slot0, slot0 + n_blk):
                repack(lambda h, r=k_refs[i]: r[0, 0, pl.ds(h, PAGE_SIZE, stride=B_HEADS), :], kcat, i)
                repack(lambda h, r=v_refs[i]: r[0, 0, pl.ds(h, PAGE_SIZE, stride=B_HEADS), :], vcat, i)
            step(slot0, n_blk, None)
            if last:
                acc_out_ref[0] = acc[...]
                run_out_ref[0] = run[...]
        return run_part

    per_part = n_group // n_parts
    parts = [pages_part(k * per_part, per_part, k == n_parts - 1) for k in range(n_parts)]
    return start_sequence, resume_sequence, parts


class _PagedAttnRider:
    n_out = 2
    n_scratch = 5

    def __init__(self, q, k_new, v_new, cache_k, cache_v, page_table, bias_rows, acc, run, first_step, n_group):
        self.arrays = (bias_rows, q, k_new, v_new, acc, run)
        self.caches = (cache_k, cache_v)
        self.page_table = page_table
        self.first_step = first_step
        self.n_group = n_group
        self.n_in = 6 + 2 * n_group
        self.steps_per_seq = page_table.shape[1] // n_group

    def specs(self, lin, rank):
        n_group, spq = self.n_group, self.steps_per_seq
        n_pages = self.page_table.shape[1]
        rows = B_HEADS * SAMPLE_PAD
        page_rows = PAGE_SIZE * B_HEADS
        gstep = lambda a: self.first_step + lin(*a[:rank])
        seq_map = lambda *a: (gstep(a) // spq, 0, 0)

        def page_spec(i):
            def index(*a):
                g, pt = gstep(a), a[rank]
                return (0, pt[g // spq, n_pages - 1 - ((g % spq) * n_group + i)], 0, 0)
            return pl.BlockSpec((1, 1, page_rows, B_DH), index)

        acc_spec = pl.BlockSpec((1, rows, B_WIDTH), seq_map)
        run_spec = pl.BlockSpec((1, rows, GATE_LANES), seq_map)
        new_spec = pl.BlockSpec((1, page_rows, B_DH), seq_map)
        acc, run = self.arrays[4:]
        return dict(
            in_specs=[pl.BlockSpec((rows, GATE_LANES), lambda *a: (0, 0)),
                      pl.BlockSpec((1, SAMPLE_PAD, B_WIDTH), seq_map), new_spec, new_spec,
                      acc_spec, run_spec] + [page_spec(i) for i in range(n_group)] * 2,
            out_specs=[acc_spec, run_spec],
            out_shape=[jax.ShapeDtypeStruct(acc.shape, F32), jax.ShapeDtypeStruct(run.shape, F32)],
            scratch=[pltpu.VMEM((rows, B_WIDTH), BF16), pltpu.VMEM((rows, B_WIDTH), F32),
                     pltpu.VMEM((rows, GATE_LANES), F32),
                     pltpu.VMEM((n_group * PAGE_SIZE, B_WIDTH), BF16),
                     pltpu.VMEM((n_group * PAGE_SIZE, B_WIDTH), BF16)],
            args=list(self.arrays) + [self.caches[0]] * n_group + [self.caches[1]] * n_group,
            aliases={4: 0, 5: 1},
        )

    def phases(self, ins, outs, scratch, local_step):
        start, resume, parts = _attn_sample_phases(ins, outs, scratch, self.n_group, RIDER_PARTS)
        p = (self.first_step + local_step) % self.steps_per_seq
        return [(p == 0, start), ((local_step == 0) & (p != 0), resume)], parts, []


def attn_sample_finish(acc, g):
    n_seq = acc.shape[0]

    def kern(acc_ref, g_ref, o_ref):
        a = acc_ref[0]
        c = lax.broadcasted_iota(jnp.int32, (SAMPLE_PAD, B_WIDTH), 1) // B_DH
        out = jnp.zeros((SAMPLE_PAD, B_WIDTH), F32)
        for h in range(B_HEADS):
            out = out + jnp.where(c == h, a[h * SAMPLE_PAD:(h + 1) * SAMPLE_PAD, :], 0.0)
        o_ref[0] = out * _silu(g_ref[0])

    return pl.pallas_call(
        kern, grid=(n_seq,),
        in_specs=[pl.BlockSpec((1, B_HEADS * SAMPLE_PAD, B_WIDTH), lambda s: (s, 0, 0)),
                  pl.BlockSpec((1, SAMPLE_PAD, B_WIDTH), lambda s: (s, 0, 0))],
        out_specs=pl.BlockSpec((1, SAMPLE_PAD, B_WIDTH), lambda s: (s, 0, 0)),
        out_shape=jax.ShapeDtypeStruct((n_seq, SAMPLE_PAD, B_WIDTH), F32),
        compiler_params=_params("arbitrary"), name="attn_sample_finish",
    )(acc, g)


def _proj_act_kernel(a_ref, w_ref, vg_ref, o_ref, *rest, act, row_chunk):
    maybe_f32, wbf = rest[:-1], rest[-1]

    @pl.when(pl.program_id(1) == 0)
    def _():
        wbf[...] = w_ref[...].astype(BF16)

    for r in range(a_ref.shape[0] // row_chunk):
        rows = slice(r * row_chunk, (r + 1) * row_chunk)
        y = _dot(a_ref[rows, :], wbf[...])
        if act == "gelu":
            y = _gelu_tanh(y)
        elif act == "silu":
            y = _silu(y)
        else:
            y = _rms(_gelu_tanh(y), vg_ref[...])
        o_ref[rows, :] = y.astype(o_ref.dtype)
        if maybe_f32:
            maybe_f32[0][rows, :] = y


def proj_act(h, w, col0, v_gain, act, tm, tn, out_dtype, emit_f32=False):
    m = h.shape[0]
    j0 = col0 // tn
    n_col = C_WIDTH // tn
    out_spec = pl.BlockSpec((tm, tn), lambda j, i: (i, j))
    n_out = 2 if emit_f32 else 1
    w_mode = dict(pipeline_mode=pl.Buffered(1)) if n_col == 1 else {}
    return pl.pallas_call(
        functools.partial(_proj_act_kernel, act=act, row_chunk=min(tm, 256)),
        grid=(n_col, m // tm),
        in_specs=[pl.BlockSpec((tm, D_MODEL), lambda j, i: (i, 0)),
                  pl.BlockSpec((D_MODEL, tn), lambda j, i: (0, j0 + j), **w_mode),
                  pl.BlockSpec((1, tn), lambda j, i: (0, j))],
        out_specs=[out_spec] * n_out,
        out_shape=[jax.ShapeDtypeStruct((m, C_WIDTH), out_dtype)]
                  + ([jax.ShapeDtypeStruct((m, C_WIDTH), F32)] if emit_f32 else []),
        scratch_shapes=[pltpu.VMEM((D_MODEL, tn), BF16)],
        compiler_params=_params("arbitrary", "arbitrary"),
        name="proj_" + act,
    )(h, w, v_gain.reshape(1, C_WIDTH))


def odd_in(h, w, v_gain, tm, act_dtype, emit_v32):
    u = proj_act(h, w, 0, v_gain, "gelu", tm, 1024, act_dtype)[0]
    v = proj_act(h, w, C_WIDTH, v_gain, "gelu_rms", min(tm, 512), C_WIDTH, act_dtype, emit_v32)
    g = proj_act(h, w, 2 * C_WIDTH, v_gain, "silu", tm, 1024, act_dtype)[0]
    return [u, v[0], g] + ([v[1]] if emit_v32 else [])


def _spatial_kernel(u_ref, v_ref, g_ref, ws_ref, bs_ref, y_ref, *, chunk, n_chunks):
    tt = lax.broadcasted_iota(jnp.int32, (chunk, chunk), 0)
    ss = lax.broadcasted_iota(jnp.int32, (chunk, chunk), 1)
    causal = ss <= tt
    for grp in range(C_GROUPS):
        wm = jnp.where(causal, ws_ref[grp], 0.0)
        bcol = bs_ref[:, grp:grp + 1]
        cols = slice(grp * C_GDIM, (grp + 1) * C_GDIM)
        for c in range(n_chunks):
            rows = slice(c * chunk, (c + 1) * chunk)
            vv = v_ref[rows, cols]
            if chunk >= 128:
                sv = _dot(wm.astype(BF16), vv)
            else:
                vf = vv.astype(F32)
                sv = jnp.zeros((chunk, C_GDIM), F32)
                for s in range(chunk):
                    sv = sv + wm[:, s:s + 1] * vf[s:s + 1, :]
            sv = sv + bcol
            y = u_ref[rows, cols].astype(F32) * sv * g_ref[rows, cols].astype(F32)
            y_ref[rows, cols] = y.astype(y_ref.dtype)


def spatial_gate(u, v, g, w_s, b_s_t, chunk, n_chunks):
    m = u.shape[0]
    tm = chunk * n_chunks
    row_spec = pl.BlockSpec((tm, C_WIDTH), lambda i: (i, 0))
    return pl.pallas_call(
        functools.partial(_spatial_kernel, chunk=chunk, n_chunks=n_chunks),
        grid=(m // tm,),
        in_specs=[row_spec, row_spec, row_spec,
                  pl.BlockSpec((C_GROUPS, chunk, chunk), lambda i: (0, 0, 0)),
                  pl.BlockSpec((chunk, C_GROUPS), lambda i: (0, 0))],
        out_specs=row_spec,
        out_shape=jax.ShapeDtypeStruct((m, C_WIDTH), u.dtype),
        compiler_params=_params("arbitrary"),
        name="spatial_gate",
    )(u, v, g, w_s, b_s_t)


def _even_weights(w_in, b_i, b_f):
    gate0 = 5 * A_WIDTH
    b0 = gate0 + 2 * A_HEADS
    wt = jnp.swapaxes(w_in, 0, 1)
    w_gate = jnp.pad(w_in[:, gate0:b0], ((0, 0), (0, GATE_LANES - 2 * A_HEADS)))
    bias = jnp.pad(jnp.concatenate([b_i, b_f]), (0, GATE_LANES - 2 * A_HEADS)).reshape(1, GATE_LANES)
    return wt, b0, w_gate, bias


def _even_front(x, bsz, t_len, valid_len, ew, g_norm, act_dtype, tm, q_prescaled):
    wt, b0, w_gate, bias = ew
    h, gates = norm_gates(x, g_norm, w_gate, bias, min(tm, 512))
    tn, tm_kv = 1024, min(tm, 512)
    qkvog = proj(h, wt, 0, 5 * A_WIDTH, BF16, tm, tn).reshape(bsz, t_len, 5 * A_WIDTH)
    q_scale = dict(scale=B_DH ** -0.5) if q_prescaled else {}
    q_b = proj(h, wt, b0, B_WIDTH, act_dtype, tm, tn, **q_scale).reshape(bsz, t_len, B_WIDTH)
    g_b = proj(h, wt, b0 + 3 * B_WIDTH, B_WIDTH, act_dtype, tm, tn).reshape(bsz, t_len, B_WIDTH)
    k_new, k_bf = kv_proj(h, wt, b0 + B_WIDTH, tm_kv)
    v_new, v_bf = kv_proj(h, wt, b0 + 2 * B_WIDTH, tm_kv)

    gates = gates.reshape(bsz, t_len, GATE_LANES)
    t_pad = -(-t_len // A_CHUNK) * A_CHUNK
    pad = ((0, 0), (0, t_pad - t_len), (0, 0))
    if valid_len < t_pad:
        qkvog, gates = jnp.pad(qkvog, pad), jnp.pad(gates, pad)
        pos = jnp.arange(t_pad)[None, :, None]
        lane = jnp.arange(GATE_LANES)[None, None, :]
        gates = jnp.where((pos >= valid_len) & (lane < A_HEADS), NEG_BIG, gates)
        gates = jnp.where((pos >= valid_len) & (lane >= A_HEADS), 0.0, gates)
    grow = gates[:, :, :2 * A_HEADS].transpose(0, 2, 1)
    return dict(qkvog=qkvog, gates=gates, grow=grow, q_b=q_b, g_b=g_b,
                k_new=k_new, v_new=v_new, k_bf=k_bf, v_bf=v_bf)


def _even_back(x, h_a, h_b, w_out, next_gain, tm):
    m_rows = x.shape[0]
    w_out_b = w_out.astype(BF16)
    return out_proj_norm(
        [h_a.reshape(m_rows, A_WIDTH), h_b.reshape(m_rows, B_WIDTH)],
        [w_out_b[:A_WIDTH], w_out_b[A_WIDTH:]], x, next_gain, min(tm, 512), True, BF16)


def _odd_layer(x, h, w_in_b, v_gain, w_s, b_s, w_out_b, final_gain, tm, chunk, n_chunks, act_dtype, emit_v32):
    outs = odd_in(h, w_in_b, v_gain, min(2 * tm, h.shape[0]), act_dtype, emit_v32)
    u, v, g = outs[:3]
    y = spatial_gate(u, v, g, w_s[:, :chunk, :chunk], b_s[:, :chunk].T, chunk, n_chunks)
    (y_out,) = out_proj_norm([y], [w_out_b], x, final_gain, tm, False, F32)
    return y_out, (outs[3] if emit_v32 else None)


def kernel(x_prompt, x_sample, state_a_C, state_a_n, state_a_m, cache_b_k, cache_b_v, page_table,
           even_norm, even_w_in, even_b_i, even_b_f, even_b_sb, even_w_out,
           odd_norm, odd_w_in, odd_v_gain, odd_w_s, odd_b_s, odd_w_out, final_norm):
    bsz, seq, _ = x_prompt.shape
    n_seq, dec_seq, _ = x_sample.shape
    n_pool = cache_b_k.shape[1]

    ew = _even_weights(even_w_in[0], even_b_i[0], even_b_f[0])
    odd_w_in_b = odd_w_in[0]
    odd_w_out_b = odd_w_out[0].astype(BF16)

    xp = x_prompt.reshape(bsz * seq, D_MODEL)
    xs = jnp.pad(x_sample, ((0, 0), (0, SAMPLE_PAD - dec_seq), (0, 0))).reshape(n_seq * SAMPLE_PAD, D_MODEL)
    m_s = n_seq * SAMPLE_PAD
    fs = _even_front(xs, n_seq, SAMPLE_PAD, dec_seq, ew, even_norm[0], F32, m_s, False)
    fp = _even_front(xp, bsz, seq, seq, ew, even_norm[0], BF16, 1024, True)

    assert cache_b_k.shape[0] == 1 and cache_b_v.shape[0] == 1
    page_view = (1, n_pool, PAGE_SIZE * B_HEADS, B_DH)
    cache_k, cache_v = cache_b_k.reshape(page_view), cache_b_v.reshape(page_view)
    kv_pad = ((0, 0), (0, (PAGE_SIZE - SAMPLE_PAD) * B_HEADS), (0, 0))
    k_new_s = jnp.pad(fs["k_new"].reshape(n_seq, SAMPLE_PAD * B_HEADS, B_DH), kv_pad)
    v_new_s = jnp.pad(fs["v_new"].reshape(n_seq, SAMPLE_PAD * B_HEADS, B_DH), kv_pad)
    bias_rows = jnp.broadcast_to(jnp.repeat(even_b_sb[0], SAMPLE_PAD)[:, None],
                                 (B_HEADS * SAMPLE_PAD, GATE_LANES))
    att_rows = B_HEADS * SAMPLE_PAD
    rider = _PagedAttnRider(
        fs["q_b"], k_new_s, v_new_s, cache_k, cache_v, page_table, bias_rows,
        jnp.zeros((n_seq, att_rows, B_WIDTH), F32), jnp.zeros((n_seq, att_rows, GATE_LANES), F32),
        0, PAGES_PER_STEP)
    assert bsz * (seq // A_CHUNK) == n_seq * rider.steps_per_seq
    zero_state = (jnp.zeros((bsz, A_HEADS, A_DH, A_DH), F32),
                  jnp.zeros((bsz, A_HEADS, 1, A_DH), F32),
                  jnp.zeros((bsz, A_HEADS, 1, GATE_LANES), F32))
    (ha_p, c_p, n_p, m_p), (att_acc, _) = mlstm(
        fp["qkvog"], fp["gates"], fp["grow"], *zero_state, A_CHUNK, rider=rider)
    hb_s = attn_sample_finish(att_acc, fs["g_b"])

    hb_p = attn_prompt(fp["q_b"], fp["g_b"], fp["k_bf"].reshape(bsz, seq, B_WIDTH),
                       fp["v_bf"].reshape(bsz, seq, B_WIDTH), even_b_sb[0],
                       ATTN_BQ, ATTN_BK, ATTN_HEADS_PER_STEP)
    st_in = (state_a_C[0], state_a_n[0][:, :, None, :],
             jnp.broadcast_to(state_a_m[0][:, :, None, None], (n_seq, A_HEADS, 1, GATE_LANES)))
    ha_s, c_s, n_s, m_s_new = mlstm(fs["qkvog"], fs["gates"], fs["grow"], *st_in, A_CHUNK)
    ha_s = ha_s[:, :SAMPLE_PAD]

    xp1, hp1 = _even_back(xp, ha_p, hb_p, even_w_out[0], odd_norm[0], 512)
    xs1, hs1 = _even_back(xs, ha_s, hb_s, even_w_out[0], odd_norm[0], m_s)
    y_p, _ = _odd_layer(xp1, hp1, odd_w_in_b, odd_v_gain[0], odd_w_s[0], odd_b_s[0], odd_w_out_b,
                        final_norm, 512, C_CHUNK, 4, BF16, False)
    y_s, v_rows = _odd_layer(xs1, hs1, odd_w_in_b, odd_v_gain[0], odd_w_s[0], odd_b_s[0], odd_w_out_b,
                             final_norm, m_s, SAMPLE_PAD, n_seq, F32, True)

    def sample_rows(a, *dims):
        return a.reshape((n_seq, SAMPLE_PAD) + dims)[:, :dec_seq]

    return (y_p.reshape(bsz, seq, D_MODEL),
            sample_rows(y_s, D_MODEL),
            c_p[None], n_p[:, :, 0, :][None], m_p[:, :, 0, 0][None],
            c_s[None], n_s[:, :, 0, :][None], m_s_new[:, :, 0, 0][None],
            fp["k_new"].reshape(1, bsz, seq, B_HEADS, B_DH), fp["v_new"].reshape(1, bsz, seq, B_HEADS, B_DH),
            sample_rows(fs["k_new"], B_HEADS, B_DH)[None], sample_rows(fs["v_new"], B_HEADS, B_DH)[None],
            sample_rows(v_rows, C_WIDTH)[None])
```

```python
import functools

import jax
import jax.numpy as jnp
from jax import lax
from jax.experimental import pallas as pl
from jax.experimental.pallas import tpu as pltpu

F32 = jnp.float32
BF16 = jnp.bfloat16

D_MODEL = 2048
PAGE_SIZE = 128
A_HEADS = 4
A_DH = 256
A_WIDTH = A_HEADS * A_DH
A_CHUNK = 128
B_HEADS = 8
B_DH = 128
B_WIDTH = B_HEADS * B_DH
C_WIDTH = D_MODEL
C_GROUPS = 8
C_GDIM = C_WIDTH // C_GROUPS
C_CHUNK = 128
RMS_EPS = 1e-6
GATE_LANES = 128
NEG_BIG = -1e30
SAMPLE_PAD = 8
ATTN_BQ = 512
ATTN_BK = 512
CUMSUM_BLOCK = 256
ATTN_HEADS_PER_STEP = 2
PAGES_PER_STEP = 16
RIDER_PARTS = 1
HOST_ROW_CHUNK = 1024

VMEM_LIMIT_BYTES = 56 * 1024 * 1024


def _params(*sem, flags=None):
    return pltpu.CompilerParams(dimension_semantics=sem, vmem_limit_bytes=VMEM_LIMIT_BYTES, flags=flags)


def _dot(a, b):
    return jnp.dot(a, b, preferred_element_type=F32)


def _dot_nt(a, b):
    return lax.dot_general(a, b, (((1,), (1,)), ((), ())), preferred_element_type=F32)


def _dot_tn(a, b):
    return lax.dot_general(a, b, (((0,), (0,)), ((), ())), preferred_element_type=F32)


def _softplus(z):
    return jnp.maximum(z, 0.0) + jnp.log(1.0 + jnp.exp(-jnp.abs(z)))


def _sigmoid(z):
    return 1.0 / (1.0 + jnp.exp(-z))


def _silu(z):
    return z * _sigmoid(z)


def _gelu_tanh(x):
    c = 0.7978845608028654
    return x * (0.5 * (1.0 + jnp.tanh(c * (x + 0.044715 * (x * x * x)))))


def _rms(x, g):
    return x * lax.rsqrt(jnp.mean(x * x, axis=-1, keepdims=True) + RMS_EPS) * g


def _split_hi_lo(x):
    hi = x.astype(BF16)
    lo = (x - hi.astype(F32)).astype(BF16)
    return hi, lo


def _norm_gates_kernel(x_ref, g_ref, whi_ref, wlo_ref, bias_ref, h_ref, gate_ref):
    h = _rms(x_ref[...], g_ref[...])
    h_hi, h_lo = _split_hi_lo(h)
    h_ref[...] = h_hi
    pre = (_dot(h_hi, whi_ref[...]) + _dot(h_hi, wlo_ref[...]) + _dot(h_lo, whi_ref[...])
           + bias_ref[...])
    lane = lax.broadcasted_iota(jnp.int32, pre.shape, 1)
    is_forget = (lane >= A_HEADS) & (lane < 2 * A_HEADS)
    gate_ref[...] = jnp.where(is_forget, -_softplus(-pre), pre)


def norm_gates(x, gain, w_gate, bias, tm):
    m = x.shape[0]
    whi, wlo = _split_hi_lo(w_gate)
    return pl.pallas_call(
        _norm_gates_kernel,
        grid=(m // tm,),
        in_specs=[pl.BlockSpec((tm, D_MODEL), lambda i: (i, 0)),
                  pl.BlockSpec((1, D_MODEL), lambda i: (0, 0)),
                  pl.BlockSpec((D_MODEL, GATE_LANES), lambda i: (0, 0)),
                  pl.BlockSpec((D_MODEL, GATE_LANES), lambda i: (0, 0)),
                  pl.BlockSpec((1, GATE_LANES), lambda i: (0, 0))],
        out_specs=[pl.BlockSpec((tm, D_MODEL), lambda i: (i, 0)),
                   pl.BlockSpec((tm, GATE_LANES), lambda i: (i, 0))],
        out_shape=[jax.ShapeDtypeStruct((m, D_MODEL), BF16),
                   jax.ShapeDtypeStruct((m, GATE_LANES), F32)],
        compiler_params=_params("arbitrary"),
        name="norm_gates",
    )(x, gain.reshape(1, D_MODEL), whi, wlo, bias)


def _hosted_kernel(*refs, n_in, n_out, phases_fn, lin, rank, rider):
    if rider is not None:
        refs = refs[1:]
    r_in, r_out = (rider.n_in, rider.n_out) if rider is not None else (0, 0)
    ins, refs = refs[:n_in], refs[n_in:]
    r_ins, refs = refs[:r_in], refs[r_in:]
    outs, refs = refs[:n_out], refs[n_out:]
    r_outs, refs = refs[:r_out], refs[r_out:]
    n_sc = len(refs) - (rider.n_scratch if rider is not None else 0)
    scratch, r_scratch = refs[:n_sc], refs[n_sc:]
    sets = [phases_fn(ins, outs, scratch)]
    if rider is not None:
        step = lin(*[pl.program_id(d) for d in range(rank)])
        sets.append(rider.phases(r_ins, r_outs, r_scratch, step))
    for pre, _, _ in sets:
        for cond, fn in pre:
            pl.when(cond)(fn)
    for _, parts, _ in sets:
        for part in parts:
            part()
    for _, _, post in sets:
        for cond, fn in post:
            pl.when(cond)(fn)


def _hosted_call(name, grid, lin, in_specs, out_specs, out_shape, scratch, phases_fn, args, rider=None):
    n_in, n_out = len(in_specs), len(out_specs)
    aliases = {}
    if rider is not None:
        r = rider.specs(lin, len(grid))
        aliases = {1 + n_in + i: n_out + o for i, o in r["aliases"].items()}
        in_specs, out_specs = in_specs + r["in_specs"], out_specs + r["out_specs"]
        out_shape, scratch = out_shape + r["out_shape"], scratch + r["scratch"]
        args = [rider.page_table] + list(args) + r["args"]
    kern = functools.partial(_hosted_kernel, n_in=n_in, n_out=n_out, phases_fn=phases_fn,
                             lin=lin, rank=len(grid), rider=rider)
    grid_spec = pltpu.PrefetchScalarGridSpec(
        num_scalar_prefetch=0 if rider is None else 1, grid=grid,
        in_specs=in_specs, out_specs=out_specs, scratch_shapes=scratch)
    outs = pl.pallas_call(
        kern, grid_spec=grid_spec, out_shape=out_shape, input_output_aliases=aliases,
        compiler_params=_params(*(["arbitrary"] * len(grid))), name=name,
    )(*args)
    return outs[:n_out], outs[n_out:]


def _two_group_steps(n_i, has_second):
    if not has_second:
        return n_i, (lambda i: i)
    return n_i + 1, (lambda i: jnp.minimum(i, n_i - 1))


def proj(a, wt, row0, n_out, out_dtype, tm, tn, scale=None, second=None):
    m, k = a.shape
    n_i = m // tm
    n_steps, tile = _two_group_steps(n_i, second is not None)

    def phases(ins, outs, scratch):
        a_ref, wt_ref = ins[:2]
        o_ref, (wbf,) = outs[0], scratch
        i = pl.program_id(1)

        def cast():
            wbf[...] = wt_ref[...].astype(BF16)

        def main():
            y = _dot_nt(a_ref[...], wbf[...])
            o_ref[...] = (y if scale is None else y * scale).astype(o_ref.dtype)

        if second is None:
            return [(i == 0, cast)], [main], []

        def small():
            outs[1][...] = _dot_nt(ins[2][...], wbf[...]).astype(outs[1].dtype)

        return [(i == 0, cast), (i < n_i, main), (i == n_i, small)], [], []

    in_specs = [pl.BlockSpec((tm, k), lambda j, i, *_: (tile(i), 0)),
                pl.BlockSpec((pl.Element(tn), pl.Element(k)),
                             lambda j, i, *_: (pl.multiple_of(row0 + j * tn, 8), 0))]
    out_specs = [pl.BlockSpec((tm, tn), lambda j, i, *_: (tile(i), j))]
    out_shape = [jax.ShapeDtypeStruct((m, n_out), out_dtype)]
    args = [a, wt]
    if second is not None:
        a2, dtype2 = second
        in_specs.append(pl.BlockSpec(a2.shape, lambda j, i, *_: (0, 0)))
        out_specs.append(pl.BlockSpec((a2.shape[0], tn), lambda j, i, *_: (0, j)))
        out_shape.append(jax.ShapeDtypeStruct((a2.shape[0], n_out), dtype2))
        args.append(a2)
    outs, _ = _hosted_call("proj", (n_out // tn, n_steps), lambda j, i: j * n_steps + i,
                           in_specs, out_specs, out_shape, [pltpu.VMEM((tn, k), BF16)], phases, args)
    return outs[0] if second is None else tuple(outs)


def kv_proj(a, wt, row0, tm, second=None):
    m, k = a.shape
    n_i = m // tm
    n_steps, tile = _two_group_steps(n_i, second is not None)

    def phases(ins, outs, scratch):
        a_ref, wt_ref = ins[:2]
        (wbf,) = scratch
        i = pl.program_id(0)

        def cast():
            wbf[...] = wt_ref[...].astype(BF16)

        def rows_to(a_rows_ref, o_ref, obf_ref):
            def run():
                n_rows = a_rows_ref.shape[0]
                y = _dot_nt(a_rows_ref[...], wbf[...])
                obf_ref[...] = y.astype(BF16)
                for h in range(B_HEADS):
                    o_ref[pl.ds(h, n_rows, stride=B_HEADS), :] = y[:, h * B_DH:(h + 1) * B_DH]
            return run

        main = rows_to(a_ref, outs[0], outs[1])
        if second is None:
            return [(i == 0, cast)], [main], []
        return [(i == 0, cast), (i < n_i, main), (i == n_i, rows_to(ins[2], outs[2], outs[3]))], [], []

    def out_pair(rows, index):
        return ([pl.BlockSpec((rows * B_HEADS, B_DH), index), pl.BlockSpec((rows, B_WIDTH), index)],
                lambda total: [jax.ShapeDtypeStruct((total * B_HEADS, B_DH), F32),
                               jax.ShapeDtypeStruct((total, B_WIDTH), BF16)])

    in_specs = [pl.BlockSpec((tm, k), lambda i, *_: (tile(i), 0)),
                pl.BlockSpec((pl.Element(B_WIDTH), pl.Element(k)), lambda i, *_: (row0, 0))]
    out_specs, shapes = out_pair(tm, lambda i, *_: (tile(i), 0))
    out_shape = shapes(m)
    args = [a, wt]
    if second is not None:
        m2 = second.shape[0]
        in_specs.append(pl.BlockSpec(second.shape, lambda i, *_: (0, 0)))
        specs2, shapes2 = out_pair(m2, lambda i, *_: (0, 0))
        out_specs, out_shape = out_specs + specs2, out_shape + shapes2(m2)
        args.append(second)
    outs, _ = _hosted_call("kv_proj", (n_steps,), lambda i: i, in_specs, out_specs, out_shape,
                           [pltpu.VMEM((B_WIDTH, k), BF16)], phases, args)
    return tuple(outs)


def out_proj_norm(lhs, ws, x, gain, tm, emit_x, norm_dtype, second=None):
    m = x.shape[0]
    n_lhs = len(lhs)
    n_i = m // tm
    n_steps, tile = _two_group_steps(n_i, second is not None)
    n_out = 2 if emit_x else 1

    def phases(ins, outs, scratch):
        del scratch
        w_refs = ins[n_lhs:2 * n_lhs]
        g_ref = ins[2 * n_lhs + 1]
        i = pl.program_id(0)

        def group(a_refs, x_ref, out_refs, row_chunk):
            def run():
                for r in range(0, x_ref.shape[0], row_chunk):
                    rows = slice(r, r + row_chunk)
                    y = x_ref[rows, :]
                    for a_ref, w_ref in zip(a_refs, w_refs):
                        y = y + _dot(a_ref[rows, :].astype(BF16), w_ref[...])
                    if emit_x:
                        out_refs[0][rows, :] = y
                    out_refs[-1][rows, :] = _rms(y, g_ref[...]).astype(out_refs[-1].dtype)
            return run

        main = group(ins[:n_lhs], ins[2 * n_lhs], outs[:n_out], min(tm, 256))
        if second is None:
            return [], [main], []
        a2_refs, x2_ref = ins[2 * n_lhs + 2:3 * n_lhs + 2], ins[3 * n_lhs + 2]
        small = group(a2_refs, x2_ref, outs[n_out:], x2_ref.shape[0])
        return [(i < n_i, main), (i == n_i, small)], [], []

    row_spec = lambda width: pl.BlockSpec((tm, width), lambda i, *_: (tile(i), 0))
    whole = lambda arr: pl.BlockSpec(arr.shape, lambda i, *_: (0,) * arr.ndim)
    gain2d = gain.reshape(1, D_MODEL)
    in_specs = ([row_spec(a.shape[1]) for a in lhs] + [whole(w) for w in ws]
                + [row_spec(D_MODEL), whole(gain2d)])
    out_specs = [row_spec(D_MODEL)] * n_out
    out_shape = ([jax.ShapeDtypeStruct((m, D_MODEL), F32)] if emit_x else []) \
        + [jax.ShapeDtypeStruct((m, D_MODEL), norm_dtype)]
    args = [*lhs, *ws, x, gain2d]
    if second is not None:
        lhs2, x2, norm_dtype2 = second
        m2 = x2.shape[0]
        in_specs += [whole(a) for a in lhs2] + [whole(x2)]
        out_specs += [pl.BlockSpec((m2, D_MODEL), lambda i, *_: (0, 0))] * n_out
        out_shape += ([jax.ShapeDtypeStruct((m2, D_MODEL), F32)] if emit_x else []) \
            + [jax.ShapeDtypeStruct((m2, D_MODEL), norm_dtype2)]
        args += [*lhs2, x2]
    outs, _ = _hosted_call("out_proj_norm", (n_steps,), lambda i: i, in_specs, out_specs, out_shape,
                           [], phases, args)
    return list(outs)


def _mlstm_chunk(q_ref, k_ref, v_ref, og_ref, gg_ref, gcol_ref, grow_ref, h_ref, c_sc, n_sc, m_sc, chunk, heads):
    L = chunk
    gcol = gcol_ref[0]
    grow = grow_ref[0]
    tt = lax.broadcasted_iota(jnp.int32, (L, L), 0)
    ss = lax.broadcasted_iota(jnp.int32, (L, L), 1)
    causal = ss <= tt

    for head in heads:
        cols = slice(head * A_DH, (head + 1) * A_DH)
        q = q_ref[0, :, cols]
        ks = k_ref[0, :, cols] * jnp.asarray(A_DH ** -0.5, BF16)
        v = v_ref[0, :, cols]
        ig_col = gcol[:, head:head + 1]
        lf_col = gcol[:, head + A_HEADS:head + A_HEADS + 1]
        ig_row = grow[head:head + 1, :]
        lf_row = grow[head + A_HEADS:head + A_HEADS + 1, :]
        b_col = jnp.sum(jnp.where(causal, lf_row, 0.0), axis=1, keepdims=True)
        b_row = jnp.sum(jnp.where(tt <= ss, lf_col, 0.0), axis=0, keepdims=True)
        b_last = jnp.sum(lf_row, axis=1, keepdims=True)

        m0 = m_sc[head][:, :1]
        n0 = n_sc[head]
        c0 = c_sc[head]

        d = jnp.where(causal, b_col - b_row + ig_row, NEG_BIG)
        m_carry = b_col + m0
        m = jnp.maximum(m_carry, jnp.max(d, axis=1, keepdims=True))
        w_intra = jnp.exp(d - m)
        w_carry = jnp.exp(m_carry - m)
        s = _dot_nt(q, ks) * w_intra
        qf = q.astype(F32)
        num = _dot(s.astype(BF16), v) + w_carry * _dot_nt(q, c0.astype(BF16))
        den = jnp.sum(s, axis=1, keepdims=True) + w_carry * jnp.sum(qf * n0, axis=1, keepdims=True)
        h = num / jnp.maximum(jnp.abs(den), jnp.exp(-m))
        gated = h * _sigmoid(og_ref[0, :, cols].astype(F32)) * _silu(gg_ref[0, :, cols].astype(F32))
        h_ref[0, :, cols] = gated.astype(h_ref.dtype)

        m_carry_last = b_last + m0
        d_last_row = b_last - b_row + ig_row
        m_new = jnp.maximum(m_carry_last, jnp.max(d_last_row, axis=1, keepdims=True))
        wc_last = jnp.exp(m_carry_last - m_new)
        w_last_col = jnp.exp(b_last - b_col + ig_col - m_new)
        vw = (v.astype(F32) * w_last_col).astype(BF16)
        c_new = wc_last * c0 + _dot_tn(vw, ks)
        n_new = wc_last * n0 + jnp.sum(ks.astype(F32) * w_last_col, axis=0, keepdims=True)
        c_sc[head] = c_new
        n_sc[head] = n_new
        m_sc[head] = jnp.broadcast_to(m_new, (1, GATE_LANES))


def mlstm(qkvog, gcol, grow, c0, n0, m0, chunk, rider=None):
    bsz, t_len, _ = qkvog.shape
    nc = t_len // chunk
    hd = A_HEADS

    def phases(ins, outs, state):
        c0_ref, n0_ref, m0_ref = ins[7:]
        h_ref, c_out_ref, n_out_ref, m_out_ref = outs
        c_sc, n_sc, m_sc = state
        ci = pl.program_id(1)

        def init():
            c_sc[...] = c0_ref[0]
            n_sc[...] = n0_ref[0]
            m_sc[...] = m0_ref[0]

        def head_part(head):
            return lambda: _mlstm_chunk(*ins[:7], h_ref, c_sc, n_sc, m_sc, chunk, (head,))

        parts = [head_part(head) for head in range(A_HEADS)]

        def final():
            c_out_ref[0] = c_sc[...]
            n_out_ref[0] = n_sc[...]
            m_out_ref[0] = m_sc[...]

        return [(ci == 0, init)], parts, [(ci == nc - 1, final)]

    blk = lambda seg: pl.BlockSpec((1, chunk, A_WIDTH), lambda b, c, *_, seg=seg: (b, c, seg))
    st4 = lambda r, w: pl.BlockSpec((1, hd, r, w), lambda b, c, *_: (b, 0, 0, 0))
    outs, rest = _hosted_call(
        "mlstm", (bsz, nc), lambda b, c: b * nc + c,
        [blk(0), blk(1), blk(2), blk(3), blk(4),
         pl.BlockSpec((1, chunk, GATE_LANES), lambda b, c, *_: (b, c, 0)),
         pl.BlockSpec((1, 8, chunk), lambda b, c, *_: (b, 0, c)),
         st4(A_DH, A_DH), st4(1, A_DH), st4(1, GATE_LANES)],
        [pl.BlockSpec((1, chunk, A_WIDTH), lambda b, c, *_: (b, c, 0)),
         st4(A_DH, A_DH), st4(1, A_DH), st4(1, GATE_LANES)],
        [jax.ShapeDtypeStruct((bsz, t_len, A_WIDTH), BF16),
         jax.ShapeDtypeStruct((bsz, hd, A_DH, A_DH), F32),
         jax.ShapeDtypeStruct((bsz, hd, 1, A_DH), F32),
         jax.ShapeDtypeStruct((bsz, hd, 1, GATE_LANES), F32)],
        [pltpu.VMEM((hd, A_DH, A_DH), F32), pltpu.VMEM((hd, 1, A_DH), F32),
         pltpu.VMEM((hd, 1, GATE_LANES), F32)],
        phases, [qkvog, qkvog, qkvog, qkvog, qkvog, gcol, grow, c0, n0, m0], rider)
    return tuple(outs) if rider is None else (tuple(outs), rest)


def _stick_block(q, kb, vb, bias, run, mask, upper):
    rows = q.shape[0]
    sub = upper.shape[0]
    n_sub = kb.shape[0] // sub
    z = _dot_nt(q, kb) if bias is None else _dot_nt(q, kb) * (B_DH ** -0.5) + bias
    sp = _softplus(z)
    spm = sp if mask is None else jnp.where(mask, sp, 0.0)
    hi, lo = _split_hi_lo(spm)
    laters = [None] * n_sub
    total = None
    for i in reversed(range(n_sub)):
        ln = slice(i * sub, (i + 1) * sub)
        both = _dot(jnp.concatenate([hi[:, ln], lo[:, ln]], axis=0), upper)
        carry = run if total is None else run + total
        laters[i] = both[:rows] + both[rows:] + carry
        part = jnp.sum(spm[:, ln], axis=1, keepdims=True)
        total = part if total is None else total + part
    later = laters[0] if n_sub == 1 else jnp.concatenate(laters, axis=1)
    a = jnp.exp(z - sp - later)
    if mask is not None:
        a = jnp.where(mask, a, 0.0)
    return _dot(a.astype(BF16), vb), total


def _strict_upper(n):
    j = lax.broadcasted_iota(jnp.int32, (n, n), 0)
    s = lax.broadcasted_iota(jnp.int32, (n, n), 1)
    return jnp.where(j > s, 1.0, 0.0).astype(BF16)


def _attn_prompt_kernel(bias_ref, q_ref, k_ref, v_ref, g_ref, o_ref, *, bq, bk, n_heads):
    head0 = pl.program_id(1) * n_heads
    qi = pl.program_id(2)
    kbf = k_ref.at[0]
    vbf = v_ref.at[0]
    upper = _strict_upper(min(bk, CUMSUM_BLOCK))
    row = lax.broadcasted_iota(jnp.int32, (bq, bk), 0)
    col = lax.broadcasted_iota(jnp.int32, (bq, bk), 1)
    lanes = [slice(h * B_DH, (h + 1) * B_DH) for h in range(n_heads)]
    lane_q = lax.broadcasted_iota(jnp.int32, (bq, B_DH), 1)
    lane_k = lax.broadcasted_iota(jnp.int32, (bk, B_DH), 1)
    ones_cols = jnp.where(lane_q < 2, 1.0, 0.0).astype(BF16)
    qs = [jnp.concatenate([q_ref[0, :, ln], ones_cols], axis=1) for ln in lanes]
    bias_cols = []
    for h in range(n_heads):
        b = jnp.full((bk, B_DH), bias_ref[head0 + h], F32)
        b_hi = b.astype(BF16).astype(F32)
        bias_cols.append(jnp.where(lane_k == 0, b_hi, jnp.where(lane_k == 1, b - b_hi, 0.0)).astype(BF16))

    def blocks(kj, runs, mask):
        start = pl.multiple_of(kj * bk, bk)
        return [_stick_block(qs[h], jnp.concatenate([kbf[pl.ds(start, bk), ln], bias_cols[h]], axis=1),
                             vbf[pl.ds(start, bk), ln], None, runs[h], mask, upper)
                for h, ln in enumerate(lanes)]

    q0 = qi * bq
    n_full = q0 // bk
    accs = [jnp.zeros((bq, B_DH), F32)] * n_heads
    runs = [jnp.zeros((bq, 1), F32)] * n_heads
    for m in reversed(range(max(1, bq // bk))):
        kj = n_full + m
        res = blocks(kj, runs, col + (kj * bk - q0) < row)
        accs = [a + c for a, (c, _) in zip(accs, res)]
        runs = [r + t for r, (_, t) in zip(runs, res)]

    def body(it, carry):
        accs, runs = carry
        res = blocks(n_full - 1 - it, runs, None)
        return (tuple(a + c for a, (c, _) in zip(accs, res)),
                tuple(r + t for r, (_, t) in zip(runs, res)))

    accs, runs = lax.fori_loop(0, n_full, body, (tuple(accs), tuple(runs)))
    for h, ln in enumerate(lanes):
        o_ref[0, :, ln] = (accs[h] * _silu(g_ref[0, :, ln].astype(F32))).astype(o_ref.dtype)


def attn_prompt(q, g, k, v, b_sb, bq, bk, n_heads):
    bsz, t_len, _ = k.shape
    width = n_heads * B_DH
    q_spec = pl.BlockSpec((1, bq, width), lambda b, h, i: (b, i, h))
    kv_spec = pl.BlockSpec((1, t_len, width), lambda b, h, i: (b, 0, h))
    return pl.pallas_call(
        functools.partial(_attn_prompt_kernel, bq=bq, bk=bk, n_heads=n_heads),
        grid=(bsz, B_HEADS // n_heads, t_len // bq),
        in_specs=[pl.BlockSpec(memory_space=pltpu.SMEM), q_spec, kv_spec, kv_spec, q_spec],
        out_specs=q_spec,
        out_shape=jax.ShapeDtypeStruct((bsz, t_len, B_WIDTH), BF16),
        compiler_params=_params("arbitrary", "arbitrary", "arbitrary"),
        name="attn_prompt",
    )(b_sb, q, k, v, g)


def _attn_sample_phases(ins, outs, scratch, n_group, n_parts):
    bias_ref, q_ref, knew_ref, vnew_ref, acc_in_ref, run_in_ref = ins[:6]
    k_refs = ins[6:6 + n_group]
    v_refs = ins[6 + n_group:]
    acc_out_ref, run_out_ref = outs
    qbd, acc, run, kcat, vcat = scratch
    rows = B_HEADS * SAMPLE_PAD
    upper = _strict_upper(PAGE_SIZE)
    bias = bias_ref[...][:, :1]

    def repack(page, dst, i):
        for h in range(B_HEADS):
            dst[i * PAGE_SIZE:(i + 1) * PAGE_SIZE, h * B_DH:(h + 1) * B_DH] = page(h).astype(BF16)

    def step(slot0, n_blk, mask):
        keys = slice(slot0 * PAGE_SIZE, (slot0 + n_blk) * PAGE_SIZE)
        z = _dot_nt(qbd[...], kcat[keys, :]) * (B_DH ** -0.5) + bias
        sp = _softplus(z)
        spm = sp if mask is None else jnp.where(mask, sp, 0.0)
        hi, lo = _split_hi_lo(spm)
        carry = run[...][:, :1]
        laters = []
        for i in range(n_blk):
            ln = slice(i * PAGE_SIZE, (i + 1) * PAGE_SIZE)
            both = _dot(jnp.concatenate([hi[:, ln], lo[:, ln]], axis=0), upper)
            laters.append(both[:rows] + both[rows:] + carry)
            carry = carry + jnp.sum(spm[:, ln], axis=1, keepdims=True)
        later = laters[0] if n_blk == 1 else jnp.concatenate(laters, axis=1)
        a = jnp.exp(z - sp - later)
        if mask is not None:
            a = jnp.where(mask, a, 0.0)
        acc[...] += _dot(a.astype(BF16), vcat[keys, :])
        run[...] = jnp.broadcast_to(carry, run.shape)

    def build_queries():
        r = lax.broadcasted_iota(jnp.int32, (rows, B_WIDTH), 0)
        c = lax.broadcasted_iota(jnp.int32, (rows, B_WIDTH), 1)
        q_rep = jnp.concatenate([q_ref[0].astype(F32)] * B_HEADS, axis=0)
        qbd[...] = jnp.where((r // SAMPLE_PAD) == (c // B_DH), q_rep, 0.0).astype(BF16)

    def start_sequence():
        build_queries()
        acc[...] = jnp.zeros_like(acc)
        run[...] = jnp.zeros_like(run)
        repack(lambda h: knew_ref[0, pl.ds(h, PAGE_SIZE, stride=B_HEADS), :], kcat, 0)
        repack(lambda h: vnew_ref[0, pl.ds(h, PAGE_SIZE, stride=B_HEADS), :], vcat, 0)
        t = lax.broadcasted_iota(jnp.int32, (rows, PAGE_SIZE), 0) % SAMPLE_PAD
        s = lax.broadcasted_iota(jnp.int32, (rows, PAGE_SIZE), 1)
        step(0, 1, s < t)

    def resume_sequence():
        build_queries()
        acc[...] = acc_in_ref[0]
        run[...] = run_in_ref[0]

    def pages_part(slot0, n_blk, last):
        def run_part():
            for i in range(slot0, slot0 + n_blk):
                repack(lambda h, r=k_refs[i]: r[0, 0, pl.ds(h, PAGE_SIZE, stride=B_HEADS), :], kcat, i)
                repack(lambda h, r=v_refs[i]: r[0, 0, pl.ds(h, PAGE_SIZE, stride=B_HEADS), :], vcat, i)
            step(slot0, n_blk, None)
            if last:
                acc_out_ref[0] = acc[...]
                run_out_ref[0] = run[...]
        return run_part

    per_part = n_group // n_parts
    parts = [pages_part(k * per_part, per_part, k == n_parts - 1) for k in range(n_parts)]
    return start_sequence, resume_sequence, parts


class _PagedAttnRider:
    n_out = 2
    n_scratch = 5

    def __init__(self, q, k_new, v_new, cache_k, cache_v, page_table, bias_rows, acc, run, first_step, n_group):
        self.arrays = (bias_rows, q, k_new, v_new, acc, run)
        self.caches = (cache_k, cache_v)
        self.page_table = page_table
        self.first_step = first_step
        self.n_group = n_group
        self.n_in = 6 + 2 * n_group
        self.steps_per_seq = page_table.shape[1] // n_group

    def specs(self, lin, rank):
        n_group, spq = self.n_group, self.steps_per_seq
        n_pages = self.page_table.shape[1]
        rows = B_HEADS * SAMPLE_PAD
        page_rows = PAGE_SIZE * B_HEADS
        gstep = lambda a: self.first_step + lin(*a[:rank])
        seq_map = lambda *a: (gstep(a) // spq, 0, 0)

        def page_spec(i):
            def index(*a):
                g, pt = gstep(a), a[rank]
                return (0, pt[g // spq, n_pages - 1 - ((g % spq) * n_group + i)], 0, 0)
            return pl.BlockSpec((1, 1, page_rows, B_DH), index)

        acc_spec = pl.BlockSpec((1, rows, B_WIDTH), seq_map)
        run_spec = pl.BlockSpec((1, rows, GATE_LANES), seq_map)
        new_spec = pl.BlockSpec((1, page_rows, B_DH), seq_map)
        acc, run = self.arrays[4:]
        return dict(
            in_specs=[pl.BlockSpec((rows, GATE_LANES), lambda *a: (0, 0)),
                      pl.BlockSpec((1, SAMPLE_PAD, B_WIDTH), seq_map), new_spec, new_spec,
                      acc_spec, run_spec] + [page_spec(i) for i in range(n_group)] * 2,
            out_specs=[acc_spec, run_spec],
            out_shape=[jax.ShapeDtypeStruct(acc.shape, F32), jax.ShapeDtypeStruct(run.shape, F32)],
            scratch=[pltpu.VMEM((rows, B_WIDTH), BF16), pltpu.VMEM((rows, B_WIDTH), F32),
                     pltpu.VMEM((rows, GATE_LANES), F32),
                     pltpu.VMEM((n_group * PAGE_SIZE, B_WIDTH), BF16),
                     pltpu.VMEM((n_group * PAGE_SIZE, B_WIDTH), BF16)],
            args=list(self.arrays) + [self.caches[0]] * n_group + [self.caches[1]] * n_group,
            aliases={4: 0, 5: 1},
        )

    def phases(self, ins, outs, scratch, local_step):
        start, resume, parts = _attn_sample_phases(ins, outs, scratch, self.n_group, RIDER_PARTS)
        p = (self.first_step + local_step) % self.steps_per_seq
        return [(p == 0, start), ((local_step == 0) & (p != 0), resume)], parts, []


def attn_sample_finish(acc, g):
    n_seq = acc.shape[0]

    def kern(acc_ref, g_ref, o_ref):
        a = acc_ref[0]
        c = lax.broadcasted_iota(jnp.int32, (SAMPLE_PAD, B_WIDTH), 1) // B_DH
        out = jnp.zeros((SAMPLE_PAD, B_WIDTH), F32)
        for h in range(B_HEADS):
            out = out + jnp.where(c == h, a[h * SAMPLE_PAD:(h + 1) * SAMPLE_PAD, :], 0.0)
        o_ref[0] = out * _silu(g_ref[0])

    return pl.pallas_call(
        kern, grid=(n_seq,),
        in_specs=[pl.BlockSpec((1, B_HEADS * SAMPLE_PAD, B_WIDTH), lambda s: (s, 0, 0)),
                  pl.BlockSpec((1, SAMPLE_PAD, B_WIDTH), lambda s: (s, 0, 0))],
        out_specs=pl.BlockSpec((1, SAMPLE_PAD, B_WIDTH), lambda s: (s, 0, 0)),
        out_shape=jax.ShapeDtypeStruct((n_seq, SAMPLE_PAD, B_WIDTH), F32),
        compiler_params=_params("arbitrary"), name="attn_sample_finish",
    )(acc, g)


def proj_act(h, w, col0, v_gain, act, tm, tn, out_dtype, second=None):
    m = h.shape[0]
    j0 = col0 // tn
    n_col = C_WIDTH // tn
    n_i = m // tm
    n_steps, tile = _two_group_steps(n_i, second is not None)

    def phases(ins, outs, scratch):
        a_ref, w_ref, vg_ref = ins[:3]
        (wbf,) = scratch
        i = pl.program_id(1)

        def cast():
            wbf[...] = w_ref[...].astype(BF16)

        def group(rows_ref, o_ref, row_chunk):
            def run():
                for r in range(0, rows_ref.shape[0], row_chunk):
                    rows = slice(r, r + row_chunk)
                    y = _dot(rows_ref[rows, :], wbf[...])
                    if act == "gelu":
                        y = _gelu_tanh(y)
                    elif act == "silu":
                        y = _silu(y)
                    else:
                        y = _rms(_gelu_tanh(y), vg_ref[...])
                    o_ref[rows, :] = y.astype(o_ref.dtype)
            return run

        main = group(a_ref, outs[0], min(tm, 256))
        if second is None:
            return [(i == 0, cast)], [main], []
        small = group(ins[3], outs[1], ins[3].shape[0])
        return [(i == 0, cast), (i < n_i, main), (i == n_i, small)], [], []

    w_mode = dict(pipeline_mode=pl.Buffered(1)) if n_col == 1 else {}
    in_specs = [pl.BlockSpec((tm, D_MODEL), lambda j, i, *_: (tile(i), 0)),
                pl.BlockSpec((D_MODEL, tn), lambda j, i, *_: (0, j0 + j), **w_mode),
                pl.BlockSpec((1, tn), lambda j, i, *_: (0, j))]
    out_specs = [pl.BlockSpec((tm, tn), lambda j, i, *_: (tile(i), j))]
    out_shape = [jax.ShapeDtypeStruct((m, C_WIDTH), out_dtype)]
    args = [h, w, v_gain.reshape(1, C_WIDTH)]
    if second is not None:
        h2, dtype2 = second
        in_specs.append(pl.BlockSpec(h2.shape, lambda j, i, *_: (0, 0)))
        out_specs.append(pl.BlockSpec((h2.shape[0], tn), lambda j, i, *_: (0, j)))
        out_shape.append(jax.ShapeDtypeStruct((h2.shape[0], C_WIDTH), dtype2))
        args.append(h2)
    outs, _ = _hosted_call("proj_" + act, (n_col, n_steps), lambda j, i: j * n_steps + i,
                           in_specs, out_specs, out_shape, [pltpu.VMEM((D_MODEL, tn), BF16)], phases, args)
    return outs[0] if second is None else tuple(outs)


def odd_in(h, w, v_gain, tm, act_dtype, second=None):
    u = proj_act(h, w, 0, v_gain, "gelu", tm, 1024, act_dtype, second)
    v = proj_act(h, w, C_WIDTH, v_gain, "gelu_rms", min(tm, 512), C_WIDTH, act_dtype, second)
    g = proj_act(h, w, 2 * C_WIDTH, v_gain, "silu", tm, 1024, act_dtype, second)
    if second is None:
        return u, v, g
    return (u[0], v[0], g[0]), (u[1], v[1], g[1])


def _spatial_kernel(u_ref, v_ref, g_ref, ws_ref, bs_ref, y_ref, *, chunk, n_chunks):
    tt = lax.broadcasted_iota(jnp.int32, (chunk, chunk), 0)
    ss = lax.broadcasted_iota(jnp.int32, (chunk, chunk), 1)
    causal = ss <= tt
    for grp in range(C_GROUPS):
        wm = jnp.where(causal, ws_ref[grp], 0.0)
        bcol = bs_ref[:, grp:grp + 1]
        cols = slice(grp * C_GDIM, (grp + 1) * C_GDIM)
        for c in range(n_chunks):
            rows = slice(c * chunk, (c + 1) * chunk)
            vv = v_ref[rows, cols]
            if chunk >= 128:
                sv = _dot(wm.astype(BF16), vv)
            else:
                vf = vv.astype(F32)
                sv = jnp.zeros((chunk, C_GDIM), F32)
                for s in range(chunk):
                    sv = sv + wm[:, s:s + 1] * vf[s:s + 1, :]
            sv = sv + bcol
            y = u_ref[rows, cols].astype(F32) * sv * g_ref[rows, cols].astype(F32)
            y_ref[rows, cols] = y.astype(y_ref.dtype)


def spatial_gate(u, v, g, w_s, b_s_t, chunk, n_chunks):
    m = u.shape[0]
    tm = chunk * n_chunks
    row_spec = pl.BlockSpec((tm, C_WIDTH), lambda i: (i, 0))
    return pl.pallas_call(
        functools.partial(_spatial_kernel, chunk=chunk, n_chunks=n_chunks),
        grid=(m // tm,),
        in_specs=[row_spec, row_spec, row_spec,
                  pl.BlockSpec((C_GROUPS, chunk, chunk), lambda i: (0, 0, 0)),
                  pl.BlockSpec((chunk, C_GROUPS), lambda i: (0, 0))],
        out_specs=row_spec,
        out_shape=jax.ShapeDtypeStruct((m, C_WIDTH), u.dtype),
        compiler_params=_params("arbitrary"),
        name="spatial_gate",
    )(u, v, g, w_s, b_s_t)


def _even_weights(w_in, b_i, b_f):
    gate0 = 5 * A_WIDTH
    b0 = gate0 + 2 * A_HEADS
    wt = jnp.swapaxes(w_in, 0, 1)
    w_gate = jnp.pad(w_in[:, gate0:b0], ((0, 0), (0, GATE_LANES - 2 * A_HEADS)))
    bias = jnp.pad(jnp.concatenate([b_i, b_f]), (0, GATE_LANES - 2 * A_HEADS)).reshape(1, GATE_LANES)
    return wt, b0, w_gate, bias


def _mlstm_inputs(qkvog, gates, bsz, t_len, valid_len):
    qkvog = qkvog.reshape(bsz, t_len, 5 * A_WIDTH)
    gates = gates.reshape(bsz, t_len, GATE_LANES)
    t_pad = -(-t_len // A_CHUNK) * A_CHUNK
    pad = ((0, 0), (0, t_pad - t_len), (0, 0))
    if valid_len < t_pad:
        qkvog, gates = jnp.pad(qkvog, pad), jnp.pad(gates, pad)
        pos = jnp.arange(t_pad)[None, :, None]
        lane = jnp.arange(GATE_LANES)[None, None, :]
        gates = jnp.where((pos >= valid_len) & (lane < A_HEADS), NEG_BIG, gates)
        gates = jnp.where((pos >= valid_len) & (lane >= A_HEADS), 0.0, gates)
    return qkvog, gates, gates[:, :, :2 * A_HEADS].transpose(0, 2, 1)


def _even_front(xp, xs, ew, g_norm, tm):
    wt, b0, w_gate, bias = ew
    hp, gates_p = norm_gates(xp, g_norm, w_gate, bias, min(tm, 512))
    hs, gates_s = norm_gates(xs, g_norm, w_gate, bias, xs.shape[0])
    tn, tm_kv = 1024, min(tm, 512)
    qkvog = proj(hp, wt, 0, 5 * A_WIDTH, BF16, tm, tn, second=(hs, BF16))
    q_b = proj(hp, wt, b0, B_WIDTH, BF16, tm, tn, scale=B_DH ** -0.5, second=(hs, F32))
    g_b = proj(hp, wt, b0 + 3 * B_WIDTH, B_WIDTH, BF16, tm, tn, second=(hs, F32))
    k = kv_proj(hp, wt, b0 + B_WIDTH, tm_kv, second=hs)
    v = kv_proj(hp, wt, b0 + 2 * B_WIDTH, tm_kv, second=hs)
    groups = []
    for i, gates in enumerate((gates_p, gates_s)):
        groups.append(dict(qkvog=qkvog[i], gates=gates, q_b=q_b[i], g_b=g_b[i],
                           k_new=k[2 * i], k_bf=k[2 * i + 1], v_new=v[2 * i], v_bf=v[2 * i + 1]))
    return groups


def _even_back(xp, xs, mix_p, mix_s, w_out, next_gain, tm):
    w_out_b = w_out.astype(BF16)
    flat = lambda mix, m: [mix[0].reshape(m, A_WIDTH), mix[1].reshape(m, B_WIDTH)]
    return out_proj_norm(flat(mix_p, xp.shape[0]), [w_out_b[:A_WIDTH], w_out_b[A_WIDTH:]], xp, next_gain,
                         tm, True, BF16, second=(flat(mix_s, xs.shape[0]), xs, BF16))


def _odd_layer(xp, hp, xs, hs, w_in, v_gain, w_s, b_s, w_out_b, final_gain, tm, n_seq):
    (u, v, g), (u2, v2, g2) = odd_in(hp, w_in, v_gain, 2 * tm, BF16, second=(hs, F32))
    y = spatial_gate(u, v, g, w_s, b_s.T, C_CHUNK, tm // C_CHUNK)
    y2 = spatial_gate(u2, v2, g2, w_s[:, :SAMPLE_PAD, :SAMPLE_PAD], b_s[:, :SAMPLE_PAD].T, SAMPLE_PAD, n_seq)
    y_p, y_s = out_proj_norm([y], [w_out_b], xp, final_gain, tm, False, F32, second=([y2], xs, F32))
    return y_p, y_s, v2


def kernel(x_prompt, x_sample, state_a_C, state_a_n, state_a_m, cache_b_k, cache_b_v, page_table,
           even_norm, even_w_in, even_b_i, even_b_f, even_b_sb, even_w_out,
           odd_norm, odd_w_in, odd_v_gain, odd_w_s, odd_b_s, odd_w_out, final_norm):
    bsz, seq, _ = x_prompt.shape
    n_seq, dec_seq, _ = x_sample.shape
    n_pool = cache_b_k.shape[1]

    ew = _even_weights(even_w_in[0], even_b_i[0], even_b_f[0])
    odd_w_in_b = odd_w_in[0]
    odd_w_out_b = odd_w_out[0].astype(BF16)

    xp = x_prompt.reshape(bsz * seq, D_MODEL)
    xs = jnp.pad(x_sample, ((0, 0), (0, SAMPLE_PAD - dec_seq), (0, 0))).reshape(n_seq * SAMPLE_PAD, D_MODEL)
    fp, fs = _even_front(xp, xs, ew, even_norm[0], 1024)
    qkvog_p, gates_p, grow_p = _mlstm_inputs(fp["qkvog"], fp["gates"], bsz, seq, seq)
    qkvog_s, gates_s, grow_s = _mlstm_inputs(fs["qkvog"], fs["gates"], n_seq, SAMPLE_PAD, dec_seq)
    q_s, g_s = (fs[name].reshape(n_seq, SAMPLE_PAD, B_WIDTH) for name in ("q_b", "g_b"))

    assert cache_b_k.shape[0] == 1 and cache_b_v.shape[0] == 1
    page_view = (1, n_pool, PAGE_SIZE * B_HEADS, B_DH)
    cache_k, cache_v = cache_b_k.reshape(page_view), cache_b_v.reshape(page_view)
    kv_pad = ((0, 0), (0, (PAGE_SIZE - SAMPLE_PAD) * B_HEADS), (0, 0))
    k_new_s = jnp.pad(fs["k_new"].reshape(n_seq, SAMPLE_PAD * B_HEADS, B_DH), kv_pad)
    v_new_s = jnp.pad(fs["v_new"].reshape(n_seq, SAMPLE_PAD * B_HEADS, B_DH), kv_pad)
    bias_rows = jnp.broadcast_to(jnp.repeat(even_b_sb[0], SAMPLE_PAD)[:, None],
                                 (B_HEADS * SAMPLE_PAD, GATE_LANES))
    att_rows = B_HEADS * SAMPLE_PAD
    rider = _PagedAttnRider(
        q_s, k_new_s, v_new_s, cache_k, cache_v, page_table, bias_rows,
        jnp.zeros((n_seq, att_rows, B_WIDTH), F32), jnp.zeros((n_seq, att_rows, GATE_LANES), F32),
        0, PAGES_PER_STEP)
    assert bsz * (seq // A_CHUNK) == n_seq * rider.steps_per_seq
    zero_state = (jnp.zeros((bsz, A_HEADS, A_DH, A_DH), F32),
                  jnp.zeros((bsz, A_HEADS, 1, A_DH), F32),
                  jnp.zeros((bsz, A_HEADS, 1, GATE_LANES), F32))
    (ha_p, c_p, n_p, m_p), (att_acc, _) = mlstm(
        qkvog_p, gates_p, grow_p, *zero_state, A_CHUNK, rider=rider)
    hb_s = attn_sample_finish(att_acc, g_s)

    as_seq = lambda a: a.reshape(bsz, seq, B_WIDTH)
    hb_p = attn_prompt(as_seq(fp["q_b"]), as_seq(fp["g_b"]), as_seq(fp["k_bf"]), as_seq(fp["v_bf"]),
                       even_b_sb[0], ATTN_BQ, ATTN_BK, ATTN_HEADS_PER_STEP)
    st_in = (state_a_C[0], state_a_n[0][:, :, None, :],
             jnp.broadcast_to(state_a_m[0][:, :, None, None], (n_seq, A_HEADS, 1, GATE_LANES)))
    ha_s, c_s, n_s, m_s_new = mlstm(qkvog_s, gates_s, grow_s, *st_in, A_CHUNK)
    ha_s = ha_s[:, :SAMPLE_PAD]

    xp1, hp1, xs1, hs1 = _even_back(xp, xs, (ha_p, hb_p), (ha_s, hb_s), even_w_out[0], odd_norm[0], 512)
    y_p, y_s, v_rows = _odd_layer(xp1, hp1, xs1, hs1, odd_w_in_b, odd_v_gain[0], odd_w_s[0], odd_b_s[0],
                                  odd_w_out_b, final_norm, 512, n_seq)

    def sample_rows(a, *dims):
        return a.reshape((n_seq, SAMPLE_PAD) + dims)[:, :dec_seq]

    return (y_p.reshape(bsz, seq, D_MODEL),
            sample_rows(y_s, D_MODEL),
            c_p[None], n_p[:, :, 0, :][None], m_p[:, :, 0, 0][None],
            c_s[None], n_s[:, :, 0, :][None], m_s_new[:, :, 0, 0][None],
            fp["k_new"].reshape(1, bsz, seq, B_HEADS, B_DH), fp["v_new"].reshape(1, bsz, seq, B_HEADS, B_DH),
            sample_rows(fs["k_new"], B_HEADS, B_DH)[None], sample_rows(fs["v_new"], B_HEADS, B_DH)[None],
            sample_rows(v_rows, C_WIDTH)[None])
```

```python
import functools

import jax
import jax.numpy as jnp
from jax import lax
from jax.experimental import pallas as pl
from jax.experimental.pallas import tpu as pltpu

F32 = jnp.float32
BF16 = jnp.bfloat16

D_MODEL = 2048
PAGE_SIZE = 128
A_HEADS = 4
A_DH = 256
A_WIDTH = A_HEADS * A_DH
A_CHUNK = 128
B_HEADS = 8
B_DH = 128
B_WIDTH = B_HEADS * B_DH
C_WIDTH = D_MODEL
C_GROUPS = 8
C_GDIM = C_WIDTH // C_GROUPS
C_CHUNK = 128
RMS_EPS = 1e-6
GATE_LANES = 128
NEG_BIG = -1e30
SAMPLE_PAD = 8
ATTN_BQ = 512
ATTN_BK = 512
CUMSUM_BLOCK = 256
ATTN_HEADS_PER_STEP = 2
PAGES_PER_STEP = 16
RIDER_PARTS = 1
HOST_ROW_CHUNK = 1024

VMEM_LIMIT_BYTES = 56 * 1024 * 1024


def _params(*sem, flags=None):
    return pltpu.CompilerParams(dimension_semantics=sem, vmem_limit_bytes=VMEM_LIMIT_BYTES, flags=flags)


def _dot(a, b):
    return jnp.dot(a, b, preferred_element_type=F32)


def _dot_nt(a, b):
    return lax.dot_general(a, b, (((1,), (1,)), ((), ())), preferred_element_type=F32)


def _dot_tn(a, b):
    return lax.dot_general(a, b, (((0,), (0,)), ((), ())), preferred_element_type=F32)


def _softplus(z):
    return jnp.maximum(z, 0.0) + jnp.log(1.0 + jnp.exp(-jnp.abs(z)))


def _sigmoid(z):
    return 1.0 / (1.0 + jnp.exp(-z))


def _silu(z):
    return z * _sigmoid(z)


def _gelu_tanh(x):
    c = 0.7978845608028654
    return x * (0.5 * (1.0 + jnp.tanh(c * (x + 0.044715 * (x * x * x)))))


def _rms(x, g):
    return x * lax.rsqrt(jnp.mean(x * x, axis=-1, keepdims=True) + RMS_EPS) * g


def _split_hi_lo(x):
    hi = x.astype(BF16)
    lo = (x - hi.astype(F32)).astype(BF16)
    return hi, lo


def _norm_gates_kernel(x_ref, g_ref, whi_ref, wlo_ref, bias_ref, h_ref, gate_ref):
    h = _rms(x_ref[...], g_ref[...])
    h_hi, h_lo = _split_hi_lo(h)
    h_ref[...] = h_hi
    pre = (_dot(h_hi, whi_ref[...]) + _dot(h_hi, wlo_ref[...]) + _dot(h_lo, whi_ref[...])
           + bias_ref[...])
    lane = lax.broadcasted_iota(jnp.int32, pre.shape, 1)
    is_forget = (lane >= A_HEADS) & (lane < 2 * A_HEADS)
    gate_ref[...] = jnp.where(is_forget, -_softplus(-pre), pre)


def norm_gates(x, gain, w_gate, bias, tm):
    m = x.shape[0]
    whi, wlo = _split_hi_lo(w_gate)
    return pl.pallas_call(
        _norm_gates_kernel,
        grid=(m // tm,),
        in_specs=[pl.BlockSpec((tm, D_MODEL), lambda i: (i, 0)),
                  pl.BlockSpec((1, D_MODEL), lambda i: (0, 0)),
                  pl.BlockSpec((D_MODEL, GATE_LANES), lambda i: (0, 0)),
                  pl.BlockSpec((D_MODEL, GATE_LANES), lambda i: (0, 0)),
                  pl.BlockSpec((1, GATE_LANES), lambda i: (0, 0))],
        out_specs=[pl.BlockSpec((tm, D_MODEL), lambda i: (i, 0)),
                   pl.BlockSpec((tm, GATE_LANES), lambda i: (i, 0))],
        out_shape=[jax.ShapeDtypeStruct((m, D_MODEL), BF16),
                   jax.ShapeDtypeStruct((m, GATE_LANES), F32)],
        compiler_params=_params("arbitrary"),
        name="norm_gates",
    )(x, gain.reshape(1, D_MODEL), whi, wlo, bias)


def _hosted_kernel(*refs, n_in, n_out, phases_fn, lin, rank, rider):
    if rider is not None:
        refs = refs[1:]
    r_in, r_out = (rider.n_in, rider.n_out) if rider is not None else (0, 0)
    ins, refs = refs[:n_in], refs[n_in:]
    r_ins, refs = refs[:r_in], refs[r_in:]
    outs, refs = refs[:n_out], refs[n_out:]
    r_outs, refs = refs[:r_out], refs[r_out:]
    n_sc = len(refs) - (rider.n_scratch if rider is not None else 0)
    scratch, r_scratch = refs[:n_sc], refs[n_sc:]
    sets = [phases_fn(ins, outs, scratch)]
    if rider is not None:
        step = lin(*[pl.program_id(d) for d in range(rank)])
        sets.append(rider.phases(r_ins, r_outs, r_scratch, step))
    for pre, _, _ in sets:
        for cond, fn in pre:
            pl.when(cond)(fn)
    for _, parts, _ in sets:
        for part in parts:
            part()
    for _, _, post in sets:
        for cond, fn in post:
            pl.when(cond)(fn)


def _hosted_call(name, grid, lin, in_specs, out_specs, out_shape, scratch, phases_fn, args, rider=None):
    n_in, n_out = len(in_specs), len(out_specs)
    aliases = {}
    if rider is not None:
        r = rider.specs(lin, len(grid))
        aliases = {1 + n_in + i: n_out + o for i, o in r["aliases"].items()}
        in_specs, out_specs = in_specs + r["in_specs"], out_specs + r["out_specs"]
        out_shape, scratch = out_shape + r["out_shape"], scratch + r["scratch"]
        args = [rider.page_table] + list(args) + r["args"]
    kern = functools.partial(_hosted_kernel, n_in=n_in, n_out=n_out, phases_fn=phases_fn,
                             lin=lin, rank=len(grid), rider=rider)
    grid_spec = pltpu.PrefetchScalarGridSpec(
        num_scalar_prefetch=0 if rider is None else 1, grid=grid,
        in_specs=in_specs, out_specs=out_specs, scratch_shapes=scratch)
    outs = pl.pallas_call(
        kern, grid_spec=grid_spec, out_shape=out_shape, input_output_aliases=aliases,
        compiler_params=_params(*(["arbitrary"] * len(grid))), name=name,
    )(*args)
    return outs[:n_out], outs[n_out:]


def _two_group_steps(n_i, has_second):
    if not has_second:
        return n_i, (lambda i: i)
    return n_i + 1, (lambda i: jnp.maximum(i - 1, 0))


def proj(a, wt, row0, n_out, out_dtype, tm, tn, scale=None, second=None):
    m, k = a.shape
    n_i = m // tm
    n_steps, tile = _two_group_steps(n_i, second is not None)

    def phases(ins, outs, scratch):
        a_ref, wt_ref = ins[:2]
        o_ref, (wbf,) = outs[0], scratch
        i = pl.program_id(1)

        def cast():
            wbf[...] = wt_ref[...].astype(BF16)

        def main():
            y = _dot_nt(a_ref[...], wbf[...])
            o_ref[...] = (y if scale is None else y * scale).astype(o_ref.dtype)

        if second is None:
            return [(i == 0, cast)], [main], []

        def small():
            outs[1][...] = _dot_nt(ins[2][...], wbf[...]).astype(outs[1].dtype)

        return [(i == 0, cast), (i == 0, small), (i > 0, main)], [], []

    in_specs = [pl.BlockSpec((tm, k), lambda j, i, *_: (tile(i), 0)),
                pl.BlockSpec((pl.Element(tn), pl.Element(k)),
                             lambda j, i, *_: (pl.multiple_of(row0 + j * tn, 8), 0))]
    out_specs = [pl.BlockSpec((tm, tn), lambda j, i, *_: (tile(i), j))]
    out_shape = [jax.ShapeDtypeStruct((m, n_out), out_dtype)]
    args = [a, wt]
    if second is not None:
        a2, dtype2 = second
        in_specs.append(pl.BlockSpec(a2.shape, lambda j, i, *_: (0, 0)))
        out_specs.append(pl.BlockSpec((a2.shape[0], tn), lambda j, i, *_: (0, j)))
        out_shape.append(jax.ShapeDtypeStruct((a2.shape[0], n_out), dtype2))
        args.append(a2)
    outs, _ = _hosted_call("proj", (n_out // tn, n_steps), lambda j, i: j * n_steps + i,
                           in_specs, out_specs, out_shape, [pltpu.VMEM((tn, k), BF16)], phases, args)
    return outs[0] if second is None else tuple(outs)


def kv_proj(a, wt, row0, tm, second=None):
    m, k = a.shape
    n_i = m // tm
    n_steps, tile = _two_group_steps(n_i, second is not None)

    def phases(ins, outs, scratch):
        a_ref, wt_ref = ins[:2]
        (wbf,) = scratch
        i = pl.program_id(0)

        def cast():
            wbf[...] = wt_ref[...].astype(BF16)

        def rows_to(a_rows_ref, o_ref, obf_ref):
            def run():
                n_rows = a_rows_ref.shape[0]
                y = _dot_nt(a_rows_ref[...], wbf[...])
                obf_ref[...] = y.astype(BF16)
                for h in range(B_HEADS):
                    o_ref[pl.ds(h, n_rows, stride=B_HEADS), :] = y[:, h * B_DH:(h + 1) * B_DH]
            return run

        main = rows_to(a_ref, outs[0], outs[1])
        if second is None:
            return [(i == 0, cast)], [main], []
        return [(i == 0, cast), (i == 0, rows_to(ins[2], outs[2], outs[3])), (i > 0, main)], [], []

    def out_pair(rows, index):
        return ([pl.BlockSpec((rows * B_HEADS, B_DH), index), pl.BlockSpec((rows, B_WIDTH), index)],
                lambda total: [jax.ShapeDtypeStruct((total * B_HEADS, B_DH), F32),
                               jax.ShapeDtypeStruct((total, B_WIDTH), BF16)])

    in_specs = [pl.BlockSpec((tm, k), lambda i, *_: (tile(i), 0)),
                pl.BlockSpec((pl.Element(B_WIDTH), pl.Element(k)), lambda i, *_: (row0, 0))]
    out_specs, shapes = out_pair(tm, lambda i, *_: (tile(i), 0))
    out_shape = shapes(m)
    args = [a, wt]
    if second is not None:
        m2 = second.shape[0]
        in_specs.append(pl.BlockSpec(second.shape, lambda i, *_: (0, 0)))
        specs2, shapes2 = out_pair(m2, lambda i, *_: (0, 0))
        out_specs, out_shape = out_specs + specs2, out_shape + shapes2(m2)
        args.append(second)
    outs, _ = _hosted_call("kv_proj", (n_steps,), lambda i: i, in_specs, out_specs, out_shape,
                           [pltpu.VMEM((B_WIDTH, k), BF16)], phases, args)
    return tuple(outs)


def out_proj_norm(lhs, ws, x, gain, tm, emit_x, norm_dtype, second=None, spatial=None):
    m = x.shape[0]
    n_lhs, n_w = len(lhs), len(ws)
    n_i = m // tm
    n_steps, tile = _two_group_steps(n_i, second is not None)
    n_out = 2 if emit_x else 1

    def phases(ins, outs, scratch):
        w_refs = ins[n_lhs:n_lhs + n_w]
        x_ref, g_ref = ins[n_lhs + n_w], ins[n_lhs + n_w + 1]
        rest = ins[n_lhs + n_w + 2:]
        i = pl.program_id(0)

        def group(make_lhs, a_refs, x_ref, out_refs, row_chunk):
            def run():
                if make_lhs is not None:
                    make_lhs()
                for r in range(0, x_ref.shape[0], row_chunk):
                    rows = slice(r, r + row_chunk)
                    y = x_ref[rows, :]
                    for a_ref, w_ref in zip(a_refs, w_refs):
                        y = y + _dot(a_ref[rows, :].astype(BF16), w_ref[...])
                    if emit_x:
                        out_refs[0][rows, :] = y
                    out_refs[-1][rows, :] = _rms(y, g_ref[...]).astype(out_refs[-1].dtype)
            return run

        if second is not None:
            a2_refs, x2_ref, rest = rest[:n_w], rest[n_w], rest[n_w + 1:]
        if spatial is None:
            main = group(None, ins[:n_lhs], x_ref, outs[:n_out], min(tm, 256))
        else:
            (y_sc,) = scratch
            fill = functools.partial(_spatial_kernel, *rest[:5], y_sc, chunk=C_CHUNK, n_chunks=tm // C_CHUNK)
            main = group(fill, [y_sc], x_ref, outs[:n_out], min(tm, 256))
        if second is None:
            return [], [main], []
        small = group(None, a2_refs, x2_ref, outs[n_out:], x2_ref.shape[0])
        return [(i == 0, small), (i > 0, main)], [], []

    row_spec = lambda width: pl.BlockSpec((tm, width), lambda i, *_: (tile(i), 0))
    whole = lambda arr: pl.BlockSpec(arr.shape, lambda i, *_: (0,) * arr.ndim)
    gain2d = gain.reshape(1, D_MODEL)
    in_specs = ([row_spec(a.shape[1]) for a in lhs] + [whole(w) for w in ws]
                + [row_spec(D_MODEL), whole(gain2d)])
    out_specs = [row_spec(D_MODEL)] * n_out
    out_shape = ([jax.ShapeDtypeStruct((m, D_MODEL), F32)] if emit_x else []) \
        + [jax.ShapeDtypeStruct((m, D_MODEL), norm_dtype)]
    args = [*lhs, *ws, x, gain2d]
    scratch = []
    if second is not None:
        lhs2, x2, norm_dtype2 = second
        m2 = x2.shape[0]
        in_specs += [whole(a) for a in lhs2] + [whole(x2)]
        out_specs += [pl.BlockSpec((m2, D_MODEL), lambda i, *_: (0, 0))] * n_out
        out_shape += ([jax.ShapeDtypeStruct((m2, D_MODEL), F32)] if emit_x else []) \
            + [jax.ShapeDtypeStruct((m2, D_MODEL), norm_dtype2)]
        args += [*lhs2, x2]
    if spatial is not None:
        u, v, gate, w_s, b_s_t = spatial
        in_specs += [row_spec(C_WIDTH)] * 3 + [whole(w_s), whole(b_s_t)]
        args += [u, v, gate, w_s, b_s_t]
        scratch = [pltpu.VMEM((tm, C_WIDTH), BF16)]
    outs, _ = _hosted_call("out_proj_norm", (n_steps,), lambda i: i, in_specs, out_specs, out_shape,
                           scratch, phases, args)
    return list(outs)


def _mlstm_chunk(q_ref, k_ref, v_ref, og_ref, gg_ref, gcol_ref, grow_ref, h_ref, c_sc, n_sc, m_sc, chunk, heads):
    L = chunk
    gcol = gcol_ref[0]
    grow = grow_ref[0]
    tt = lax.broadcasted_iota(jnp.int32, (L, L), 0)
    ss = lax.broadcasted_iota(jnp.int32, (L, L), 1)
    causal = ss <= tt

    for head in heads:
        cols = slice(head * A_DH, (head + 1) * A_DH)
        q = q_ref[0, :, cols]
        ks = k_ref[0, :, cols] * jnp.asarray(A_DH ** -0.5, BF16)
        v = v_ref[0, :, cols]
        ig_col = gcol[:, head:head + 1]
        lf_col = gcol[:, head + A_HEADS:head + A_HEADS + 1]
        ig_row = grow[head:head + 1, :]
        lf_row = grow[head + A_HEADS:head + A_HEADS + 1, :]
        b_col = jnp.sum(jnp.where(causal, lf_row, 0.0), axis=1, keepdims=True)
        b_row = jnp.sum(jnp.where(tt <= ss, lf_col, 0.0), axis=0, keepdims=True)
        b_last = jnp.sum(lf_row, axis=1, keepdims=True)

        m0 = m_sc[head][:, :1]
        n0 = n_sc[head]
        c0 = c_sc[head]

        d = jnp.where(causal, b_col - b_row + ig_row, NEG_BIG)
        m_carry = b_col + m0
        m = jnp.maximum(m_carry, jnp.max(d, axis=1, keepdims=True))
        w_intra = jnp.exp(d - m)
        w_carry = jnp.exp(m_carry - m)
        s = _dot_nt(q, ks) * w_intra
        qf = q.astype(F32)
        num = _dot(s.astype(BF16), v) + w_carry * _dot_nt(q, c0.astype(BF16))
        den = jnp.sum(s, axis=1, keepdims=True) + w_carry * jnp.sum(qf * n0, axis=1, keepdims=True)
        h = num / jnp.maximum(jnp.abs(den), jnp.exp(-m))
        gated = h * _sigmoid(og_ref[0, :, cols].astype(F32)) * _silu(gg_ref[0, :, cols].astype(F32))
        h_ref[0, :, cols] = gated.astype(h_ref.dtype)

        m_carry_last = b_last + m0
        d_last_row = b_last - b_row + ig_row
        m_new = jnp.maximum(m_carry_last, jnp.max(d_last_row, axis=1, keepdims=True))
        wc_last = jnp.exp(m_carry_last - m_new)
        w_last_col = jnp.exp(b_last - b_col + ig_col - m_new)
        vw = (v.astype(F32) * w_last_col).astype(BF16)
        c_new = wc_last * c0 + _dot_tn(vw, ks)
        n_new = wc_last * n0 + jnp.sum(ks.astype(F32) * w_last_col, axis=0, keepdims=True)
        c_sc[head] = c_new
        n_sc[head] = n_new
        m_sc[head] = jnp.broadcast_to(m_new, (1, GATE_LANES))


def mlstm(qkvog, gcol, grow, c0, n0, m0, chunk, rider=None):
    bsz, t_len, _ = qkvog.shape
    nc = t_len // chunk
    hd = A_HEADS

    def phases(ins, outs, state):
        c0_ref, n0_ref, m0_ref = ins[7:]
        h_ref, c_out_ref, n_out_ref, m_out_ref = outs
        c_sc, n_sc, m_sc = state
        ci = pl.program_id(1)

        def init():
            c_sc[...] = c0_ref[0]
            n_sc[...] = n0_ref[0]
            m_sc[...] = m0_ref[0]

        def head_part(head):
            return lambda: _mlstm_chunk(*ins[:7], h_ref, c_sc, n_sc, m_sc, chunk, (head,))

        parts = [head_part(head) for head in range(A_HEADS)]

        def final():
            c_out_ref[0] = c_sc[...]
            n_out_ref[0] = n_sc[...]
            m_out_ref[0] = m_sc[...]

        return [(ci == 0, init)], parts, [(ci == nc - 1, final)]

    blk = lambda seg: pl.BlockSpec((1, chunk, A_WIDTH), lambda b, c, *_, seg=seg: (b, c, seg))
    st4 = lambda r, w: pl.BlockSpec((1, hd, r, w), lambda b, c, *_: (b, 0, 0, 0))
    outs, rest = _hosted_call(
        "mlstm", (bsz, nc), lambda b, c: b * nc + c,
        [blk(0), blk(1), blk(2), blk(3), blk(4),
         pl.BlockSpec((1, chunk, GATE_LANES), lambda b, c, *_: (b, c, 0)),
         pl.BlockSpec((1, 8, chunk), lambda b, c, *_: (b, 0, c)),
         st4(A_DH, A_DH), st4(1, A_DH), st4(1, GATE_LANES)],
        [pl.BlockSpec((1, chunk, A_WIDTH), lambda b, c, *_: (b, c, 0)),
         st4(A_DH, A_DH), st4(1, A_DH), st4(1, GATE_LANES)],
        [jax.ShapeDtypeStruct((bsz, t_len, A_WIDTH), BF16),
         jax.ShapeDtypeStruct((bsz, hd, A_DH, A_DH), F32),
         jax.ShapeDtypeStruct((bsz, hd, 1, A_DH), F32),
         jax.ShapeDtypeStruct((bsz, hd, 1, GATE_LANES), F32)],
        [pltpu.VMEM((hd, A_DH, A_DH), F32), pltpu.VMEM((hd, 1, A_DH), F32),
         pltpu.VMEM((hd, 1, GATE_LANES), F32)],
        phases, [qkvog, qkvog, qkvog, qkvog, qkvog, gcol, grow, c0, n0, m0], rider)
    return tuple(outs) if rider is None else (tuple(outs), rest)


def _stick_block(q, kb, vb, bias, run, mask, upper):
    rows = q.shape[0]
    sub = upper.shape[0]
    n_sub = kb.shape[0] // sub
    z = _dot_nt(q, kb) if bias is None else _dot_nt(q, kb) * (B_DH ** -0.5) + bias
    sp = _softplus(z)
    spm = sp if mask is None else jnp.where(mask, sp, 0.0)
    hi, lo = _split_hi_lo(spm)
    laters = [None] * n_sub
    total = None
    for i in reversed(range(n_sub)):
        ln = slice(i * sub, (i + 1) * sub)
        both = _dot(jnp.concatenate([hi[:, ln], lo[:, ln]], axis=0), upper)
        carry = run if total is None else run + total
        laters[i] = both[:rows] + both[rows:] + carry
        part = jnp.sum(spm[:, ln], axis=1, keepdims=True)
        total = part if total is None else total + part
    later = laters[0] if n_sub == 1 else jnp.concatenate(laters, axis=1)
    a = jnp.exp(z - sp - later)
    if mask is not None:
        a = jnp.where(mask, a, 0.0)
    return _dot(a.astype(BF16), vb), total


def _strict_upper(n):
    j = lax.broadcasted_iota(jnp.int32, (n, n), 0)
    s = lax.broadcasted_iota(jnp.int32, (n, n), 1)
    return jnp.where(j > s, 1.0, 0.0).astype(BF16)


def _attn_prompt_kernel(bias_ref, q_ref, k_ref, v_ref, g_ref, o_ref, *, bq, bk, n_heads):
    head0 = pl.program_id(1) * n_heads
    qi = pl.program_id(2)
    kbf = k_ref.at[0]
    vbf = v_ref.at[0]
    upper = _strict_upper(min(bk, CUMSUM_BLOCK))
    row = lax.broadcasted_iota(jnp.int32, (bq, bk), 0)
    col = lax.broadcasted_iota(jnp.int32, (bq, bk), 1)
    lanes = [slice(h * B_DH, (h + 1) * B_DH) for h in range(n_heads)]
    lane_q = lax.broadcasted_iota(jnp.int32, (bq, B_DH), 1)
    lane_k = lax.broadcasted_iota(jnp.int32, (bk, B_DH), 1)
    ones_cols = jnp.where(lane_q < 2, 1.0, 0.0).astype(BF16)
    qs = [jnp.concatenate([q_ref[0, :, ln], ones_cols], axis=1) for ln in lanes]
    bias_cols = []
    for h in range(n_heads):
        b = jnp.full((bk, B_DH), bias_ref[head0 + h], F32)
        b_hi = b.astype(BF16).astype(F32)
        bias_cols.append(jnp.where(lane_k == 0, b_hi, jnp.where(lane_k == 1, b - b_hi, 0.0)).astype(BF16))

    def blocks(kj, runs, mask):
        start = pl.multiple_of(kj * bk, bk)
        return [_stick_block(qs[h], jnp.concatenate([kbf[pl.ds(start, bk), ln], bias_cols[h]], axis=1),
                             vbf[pl.ds(start, bk), ln], None, runs[h], mask, upper)
                for h, ln in enumerate(lanes)]

    q0 = qi * bq
    n_full = q0 // bk
    accs = [jnp.zeros((bq, B_DH), F32)] * n_heads
    runs = [jnp.zeros((bq, 1), F32)] * n_heads
    for m in reversed(range(max(1, bq // bk))):
        kj = n_full + m
        res = blocks(kj, runs, col + (kj * bk - q0) < row)
        accs = [a + c for a, (c, _) in zip(accs, res)]
        runs = [r + t for r, (_, t) in zip(runs, res)]

    def body(it, carry):
        accs, runs = carry
        res = blocks(n_full - 1 - it, runs, None)
        return (tuple(a + c for a, (c, _) in zip(accs, res)),
                tuple(r + t for r, (_, t) in zip(runs, res)))

    accs, runs = lax.fori_loop(0, n_full, body, (tuple(accs), tuple(runs)))
    for h, ln in enumerate(lanes):
        o_ref[0, :, ln] = (accs[h] * _silu(g_ref[0, :, ln].astype(F32))).astype(o_ref.dtype)


def attn_prompt(q, g, k, v, b_sb, bq, bk, n_heads):
    bsz, t_len, _ = k.shape
    width = n_heads * B_DH
    q_spec = pl.BlockSpec((1, bq, width), lambda b, h, i: (b, i, h))
    kv_spec = pl.BlockSpec((1, t_len, width), lambda b, h, i: (b, 0, h))
    return pl.pallas_call(
        functools.partial(_attn_prompt_kernel, bq=bq, bk=bk, n_heads=n_heads),
        grid=(bsz, B_HEADS // n_heads, t_len // bq),
        in_specs=[pl.BlockSpec(memory_space=pltpu.SMEM), q_spec, kv_spec, kv_spec, q_spec],
        out_specs=q_spec,
        out_shape=jax.ShapeDtypeStruct((bsz, t_len, B_WIDTH), BF16),
        compiler_params=_params("arbitrary", "arbitrary", "arbitrary"),
        name="attn_prompt",
    )(b_sb, q, k, v, g)


def _attn_sample_phases(ins, outs, scratch, n_group, n_parts):
    bias_ref, q_ref, knew_ref, vnew_ref, acc_in_ref, run_in_ref = ins[:6]
    k_refs = ins[6:6 + n_group]
    v_refs = ins[6 + n_group:]
    acc_out_ref, run_out_ref = outs
    qbd, acc, run, kcat, vcat = scratch
    rows = B_HEADS * SAMPLE_PAD
    upper = _strict_upper(PAGE_SIZE)
    bias = bias_ref[...][:, :1]

    def repack(page, dst, i):
        for h in range(B_HEADS):
            dst[i * PAGE_SIZE:(i + 1) * PAGE_SIZE, h * B_DH:(h + 1) * B_DH] = page(h).astype(BF16)

    def step(slot0, n_blk, mask):
        keys = slice(slot0 * PAGE_SIZE, (slot0 + n_blk) * PAGE_SIZE)
        z = _dot_nt(qbd[...], kcat[keys, :]) * (B_DH ** -0.5) + bias
        sp = _softplus(z)
        spm = sp if mask is None else jnp.where(mask, sp, 0.0)
        hi, lo = _split_hi_lo(spm)
        carry = run[...][:, :1]
        laters = []
        for i in range(n_blk):
            ln = slice(i * PAGE_SIZE, (i + 1) * PAGE_SIZE)
            both = _dot(jnp.concatenate([hi[:, ln], lo[:, ln]], axis=0), upper)
            laters.append(both[:rows] + both[rows:] + carry)
            carry = carry + jnp.sum(spm[:, ln], axis=1, keepdims=True)
        later = laters[0] if n_blk == 1 else jnp.concatenate(laters, axis=1)
        a = jnp.exp(z - sp - later)
        if mask is not None:
            a = jnp.where(mask, a, 0.0)
        acc[...] += _dot(a.astype(BF16), vcat[keys, :])
        run[...] = jnp.broadcast_to(carry, run.shape)

    def build_queries():
        r = lax.broadcasted_iota(jnp.int32, (rows, B_WIDTH), 0)
        c = lax.broadcasted_iota(jnp.int32, (rows, B_WIDTH), 1)
        q_rep = jnp.concatenate([q_ref[0].astype(F32)] * B_HEADS, axis=0)
        qbd[...] = jnp.where((r // SAMPLE_PAD) == (c // B_DH), q_rep, 0.0).astype(BF16)

    def start_sequence():
        build_queries()
        acc[...] = jnp.zeros_like(acc)
        run[...] = jnp.zeros_like(run)
        repack(lambda h: knew_ref[0, pl.ds(h, PAGE_SIZE, stride=B_HEADS), :], kcat, 0)
        repack(lambda h: vnew_ref[0, pl.ds(h, PAGE_SIZE, stride=B_HEADS), :], vcat, 0)
        t = lax.broadcasted_iota(jnp.int32, (rows, PAGE_SIZE), 0) % SAMPLE_PAD
        s = lax.broadcasted_iota(jnp.int32, (rows, PAGE_SIZE), 1)
        step(0, 1, s < t)

    def resume_sequence():
        build_queries()
        acc[...] = acc_in_ref[0]
        run[...] = run_in_ref[0]

    def pages_part(slot0, n_blk, last):
        def run_part():
            for i in range(slot0, slot0 + n_blk):
                repack(lambda h, r=k_refs[i]: r[0, 0, pl.ds(h, PAGE_SIZE, stride=B_HEADS), :], kcat, i)
                repack(lambda h, r=v_refs[i]: r[0, 0, pl.ds(h, PAGE_SIZE, stride=B_HEADS), :], vcat, i)
            step(slot0, n_blk, None)
            if last:
                acc_out_ref[0] = acc[...]
                run_out_ref[0] = run[...]
        return run_part

    per_part = n_group // n_parts
    parts = [pages_part(k * per_part, per_part, k == n_parts - 1) for k in range(n_parts)]
    return start_sequence, resume_sequence, parts


class _PagedAttnRider:
    n_out = 2
    n_scratch = 5

    def __init__(self, q, k_new, v_new, cache_k, cache_v, page_table, bias_rows, acc, run, first_step, n_group):
        self.arrays = (bias_rows, q, k_new, v_new, acc, run)
        self.caches = (cache_k, cache_v)
        self.page_table = page_table
        self.first_step = first_step
        self.n_group = n_group
        self.n_in = 6 + 2 * n_group
        self.steps_per_seq = page_table.shape[1] // n_group

    def specs(self, lin, rank):
        n_group, spq = self.n_group, self.steps_per_seq
        n_pages = self.page_table.shape[1]
        rows = B_HEADS * SAMPLE_PAD
        page_rows = PAGE_SIZE * B_HEADS
        gstep = lambda a: self.first_step + lin(*a[:rank])
        seq_map = lambda *a: (gstep(a) // spq, 0, 0)

        def page_spec(i):
            def index(*a):
                g, pt = gstep(a), a[rank]
                return (0, pt[g // spq, n_pages - 1 - ((g % spq) * n_group + i)], 0, 0)
            return pl.BlockSpec((1, 1, page_rows, B_DH), index)

        acc_spec = pl.BlockSpec((1, rows, B_WIDTH), seq_map)
        run_spec = pl.BlockSpec((1, rows, GATE_LANES), seq_map)
        new_spec = pl.BlockSpec((1, page_rows, B_DH), seq_map)
        acc, run = self.arrays[4:]
        return dict(
            in_specs=[pl.BlockSpec((rows, GATE_LANES), lambda *a: (0, 0)),
                      pl.BlockSpec((1, SAMPLE_PAD, B_WIDTH), seq_map), new_spec, new_spec,
                      acc_spec, run_spec] + [page_spec(i) for i in range(n_group)] * 2,
            out_specs=[acc_spec, run_spec],
            out_shape=[jax.ShapeDtypeStruct(acc.shape, F32), jax.ShapeDtypeStruct(run.shape, F32)],
            scratch=[pltpu.VMEM((rows, B_WIDTH), BF16), pltpu.VMEM((rows, B_WIDTH), F32),
                     pltpu.VMEM((rows, GATE_LANES), F32),
                     pltpu.VMEM((n_group * PAGE_SIZE, B_WIDTH), BF16),
                     pltpu.VMEM((n_group * PAGE_SIZE, B_WIDTH), BF16)],
            args=list(self.arrays) + [self.caches[0]] * n_group + [self.caches[1]] * n_group,
            aliases={4: 0, 5: 1},
        )

    def phases(self, ins, outs, scratch, local_step):
        start, resume, parts = _attn_sample_phases(ins, outs, scratch, self.n_group, RIDER_PARTS)
        p = (self.first_step + local_step) % self.steps_per_seq
        return [(p == 0, start), ((local_step == 0) & (p != 0), resume)], parts, []


def attn_sample_finish(acc, g):
    n_seq = acc.shape[0]

    def kern(acc_ref, g_ref, o_ref):
        a = acc_ref[0]
        c = lax.broadcasted_iota(jnp.int32, (SAMPLE_PAD, B_WIDTH), 1) // B_DH
        out = jnp.zeros((SAMPLE_PAD, B_WIDTH), F32)
        for h in range(B_HEADS):
            out = out + jnp.where(c == h, a[h * SAMPLE_PAD:(h + 1) * SAMPLE_PAD, :], 0.0)
        o_ref[0] = out * _silu(g_ref[0])

    return pl.pallas_call(
        kern, grid=(n_seq,),
        in_specs=[pl.BlockSpec((1, B_HEADS * SAMPLE_PAD, B_WIDTH), lambda s: (s, 0, 0)),
                  pl.BlockSpec((1, SAMPLE_PAD, B_WIDTH), lambda s: (s, 0, 0))],
        out_specs=pl.BlockSpec((1, SAMPLE_PAD, B_WIDTH), lambda s: (s, 0, 0)),
        out_shape=jax.ShapeDtypeStruct((n_seq, SAMPLE_PAD, B_WIDTH), F32),
        compiler_params=_params("arbitrary"), name="attn_sample_finish",
    )(acc, g)


def proj_act(h, w, col0, v_gain, act, tm, tn, out_dtype, second=None):
    m = h.shape[0]
    j0 = col0 // tn
    n_col = C_WIDTH // tn
    n_i = m // tm
    n_steps, tile = _two_group_steps(n_i, second is not None)

    def phases(ins, outs, scratch):
        a_ref, w_ref, vg_ref = ins[:3]
        (wbf,) = scratch
        i = pl.program_id(1)

        def cast():
            wbf[...] = w_ref[...].astype(BF16)

        def group(rows_ref, o_ref, row_chunk):
            def run():
                for r in range(0, rows_ref.shape[0], row_chunk):
                    rows = slice(r, r + row_chunk)
                    y = _dot(rows_ref[rows, :], wbf[...])
                    if act == "gelu":
                        y = _gelu_tanh(y)
                    elif act == "silu":
                        y = _silu(y)
                    else:
                        y = _rms(_gelu_tanh(y), vg_ref[...])
                    o_ref[rows, :] = y.astype(o_ref.dtype)
            return run

        main = group(a_ref, outs[0], min(tm, 256))
        if second is None:
            return [(i == 0, cast)], [main], []
        small = group(ins[3], outs[1], ins[3].shape[0])
        return [(i == 0, cast), (i == 0, small), (i > 0, main)], [], []

    w_mode = dict(pipeline_mode=pl.Buffered(1)) if n_col == 1 else {}
    in_specs = [pl.BlockSpec((tm, D_MODEL), lambda j, i, *_: (tile(i), 0)),
                pl.BlockSpec((D_MODEL, tn), lambda j, i, *_: (0, j0 + j), **w_mode),
                pl.BlockSpec((1, tn), lambda j, i, *_: (0, j))]
    out_specs = [pl.BlockSpec((tm, tn), lambda j, i, *_: (tile(i), j))]
    out_shape = [jax.ShapeDtypeStruct((m, C_WIDTH), out_dtype)]
    args = [h, w, v_gain.reshape(1, C_WIDTH)]
    if second is not None:
        h2, dtype2 = second
        in_specs.append(pl.BlockSpec(h2.shape, lambda j, i, *_: (0, 0)))
        out_specs.append(pl.BlockSpec((h2.shape[0], tn), lambda j, i, *_: (0, j)))
        out_shape.append(jax.ShapeDtypeStruct((h2.shape[0], C_WIDTH), dtype2))
        args.append(h2)
    outs, _ = _hosted_call("proj_" + act, (n_col, n_steps), lambda j, i: j * n_steps + i,
                           in_specs, out_specs, out_shape, [pltpu.VMEM((D_MODEL, tn), BF16)], phases, args)
    return outs[0] if second is None else tuple(outs)


def odd_in(h, w, v_gain, tm, act_dtype, second=None):
    u = proj_act(h, w, 0, v_gain, "gelu", tm, 1024, act_dtype, second)
    v = proj_act(h, w, C_WIDTH, v_gain, "gelu_rms", min(tm, 512), C_WIDTH, act_dtype, second)
    g = proj_act(h, w, 2 * C_WIDTH, v_gain, "silu", tm, 1024, act_dtype, second)
    if second is None:
        return u, v, g
    return (u[0], v[0], g[0]), (u[1], v[1], g[1])


def _spatial_kernel(u_ref, v_ref, g_ref, ws_ref, bs_ref, y_ref, *, chunk, n_chunks):
    tt = lax.broadcasted_iota(jnp.int32, (chunk, chunk), 0)
    ss = lax.broadcasted_iota(jnp.int32, (chunk, chunk), 1)
    causal = ss <= tt
    for grp in range(C_GROUPS):
        wm = jnp.where(causal, ws_ref[grp], 0.0)
        bcol = bs_ref[:, grp:grp + 1]
        cols = slice(grp * C_GDIM, (grp + 1) * C_GDIM)
        for c in range(n_chunks):
            rows = slice(c * chunk, (c + 1) * chunk)
            vv = v_ref[rows, cols]
            if chunk >= 128:
                sv = _dot(wm.astype(BF16), vv)
            else:
                vf = vv.astype(F32)
                sv = jnp.zeros((chunk, C_GDIM), F32)
                for s in range(chunk):
                    sv = sv + wm[:, s:s + 1] * vf[s:s + 1, :]
            sv = sv + bcol
            y = u_ref[rows, cols].astype(F32) * sv * g_ref[rows, cols].astype(F32)
            y_ref[rows, cols] = y.astype(y_ref.dtype)


def spatial_gate(u, v, g, w_s, b_s_t, chunk, n_chunks):
    m = u.shape[0]
    tm = chunk * n_chunks
    row_spec = pl.BlockSpec((tm, C_WIDTH), lambda i: (i, 0))
    return pl.pallas_call(
        functools.partial(_spatial_kernel, chunk=chunk, n_chunks=n_chunks),
        grid=(m // tm,),
        in_specs=[row_spec, row_spec, row_spec,
                  pl.BlockSpec((C_GROUPS, chunk, chunk), lambda i: (0, 0, 0)),
                  pl.BlockSpec((chunk, C_GROUPS), lambda i: (0, 0))],
        out_specs=row_spec,
        out_shape=jax.ShapeDtypeStruct((m, C_WIDTH), u.dtype),
        compiler_params=_params("arbitrary"),
        name="spatial_gate",
    )(u, v, g, w_s, b_s_t)


def _even_weights(w_in, b_i, b_f):
    gate0 = 5 * A_WIDTH
    b0 = gate0 + 2 * A_HEADS
    wt = jnp.swapaxes(w_in, 0, 1)
    w_gate = jnp.pad(w_in[:, gate0:b0], ((0, 0), (0, GATE_LANES - 2 * A_HEADS)))
    bias = jnp.pad(jnp.concatenate([b_i, b_f]), (0, GATE_LANES - 2 * A_HEADS)).reshape(1, GATE_LANES)
    return wt, b0, w_gate, bias


def _mlstm_inputs(qkvog, gates, bsz, t_len, valid_len):
    qkvog = qkvog.reshape(bsz, t_len, 5 * A_WIDTH)
    gates = gates.reshape(bsz, t_len, GATE_LANES)
    t_pad = -(-t_len // A_CHUNK) * A_CHUNK
    pad = ((0, 0), (0, t_pad - t_len), (0, 0))
    if valid_len < t_pad:
        qkvog, gates = jnp.pad(qkvog, pad), jnp.pad(gates, pad)
        pos = jnp.arange(t_pad)[None, :, None]
        lane = jnp.arange(GATE_LANES)[None, None, :]
        gates = jnp.where((pos >= valid_len) & (lane < A_HEADS), NEG_BIG, gates)
        gates = jnp.where((pos >= valid_len) & (lane >= A_HEADS), 0.0, gates)
    return qkvog, gates, gates[:, :, :2 * A_HEADS].transpose(0, 2, 1)


def _even_front(xp, xs, ew, g_norm, tm):
    wt, b0, w_gate, bias = ew
    hp, gates_p = norm_gates(xp, g_norm, w_gate, bias, min(tm, 512))
    hs, gates_s = norm_gates(xs, g_norm, w_gate, bias, xs.shape[0])
    tn, tm_kv = 1024, min(tm, 512)
    qkvog = proj(hp, wt, 0, 5 * A_WIDTH, BF16, tm, tn, second=(hs, BF16))
    q_b = proj(hp, wt, b0, B_WIDTH, BF16, tm, tn, scale=B_DH ** -0.5, second=(hs, F32))
    g_b = proj(hp, wt, b0 + 3 * B_WIDTH, B_WIDTH, BF16, tm, tn, second=(hs, F32))
    k = kv_proj(hp, wt, b0 + B_WIDTH, tm_kv, second=hs)
    v = kv_proj(hp, wt, b0 + 2 * B_WIDTH, tm_kv, second=hs)
    groups = []
    for i, gates in enumerate((gates_p, gates_s)):
        groups.append(dict(qkvog=qkvog[i], gates=gates, q_b=q_b[i], g_b=g_b[i],
                           k_new=k[2 * i], k_bf=k[2 * i + 1], v_new=v[2 * i], v_bf=v[2 * i + 1]))
    return groups


def _even_back(xp, xs, mix_p, mix_s, w_out, next_gain, tm):
    w_out_b = w_out.astype(BF16)
    flat = lambda mix, m: [mix[0].reshape(m, A_WIDTH), mix[1].reshape(m, B_WIDTH)]
    return out_proj_norm(flat(mix_p, xp.shape[0]), [w_out_b[:A_WIDTH], w_out_b[A_WIDTH:]], xp, next_gain,
                         tm, True, BF16, second=(flat(mix_s, xs.shape[0]), xs, BF16))


def _odd_layer(xp, hp, xs, hs, w_in, v_gain, w_s, b_s, w_out_b, final_gain, tm, n_seq):
    (u, v, g), (u2, v2, g2) = odd_in(hp, w_in, v_gain, 2 * tm, BF16, second=(hs, F32))
    y2 = spatial_gate(u2, v2, g2, w_s[:, :SAMPLE_PAD, :SAMPLE_PAD], b_s[:, :SAMPLE_PAD].T, SAMPLE_PAD, n_seq)
    y_p, y_s = out_proj_norm([], [w_out_b], xp, final_gain, tm, False, F32, second=([y2], xs, F32),
                             spatial=(u, v, g, w_s, b_s.T))
    return y_p, y_s, v2


def kernel(x_prompt, x_sample, state_a_C, state_a_n, state_a_m, cache_b_k, cache_b_v, page_table,
           even_norm, even_w_in, even_b_i, even_b_f, even_b_sb, even_w_out,
           odd_norm, odd_w_in, odd_v_gain, odd_w_s, odd_b_s, odd_w_out, final_norm):
    bsz, seq, _ = x_prompt.shape
    n_seq, dec_seq, _ = x_sample.shape
    n_pool = cache_b_k.shape[1]

    ew = _even_weights(even_w_in[0], even_b_i[0], even_b_f[0])
    odd_w_in_b = odd_w_in[0]
    odd_w_out_b = odd_w_out[0].astype(BF16)

    xp = x_prompt.reshape(bsz * seq, D_MODEL)
    xs = jnp.pad(x_sample, ((0, 0), (0, SAMPLE_PAD - dec_seq), (0, 0))).reshape(n_seq * SAMPLE_PAD, D_MODEL)
    fp, fs = _even_front(xp, xs, ew, even_norm[0], 1024)
    qkvog_p, gates_p, grow_p = _mlstm_inputs(fp["qkvog"], fp["gates"], bsz, seq, seq)
    qkvog_s, gates_s, grow_s = _mlstm_inputs(fs["qkvog"], fs["gates"], n_seq, SAMPLE_PAD, dec_seq)
    q_s, g_s = (fs[name].reshape(n_seq, SAMPLE_PAD, B_WIDTH) for name in ("q_b", "g_b"))

    assert cache_b_k.shape[0] == 1 and cache_b_v.shape[0] == 1
    page_view = (1, n_pool, PAGE_SIZE * B_HEADS, B_DH)
    cache_k, cache_v = cache_b_k.reshape(page_view), cache_b_v.reshape(page_view)
    kv_pad = ((0, 0), (0, (PAGE_SIZE - SAMPLE_PAD) * B_HEADS), (0, 0))
    k_new_s = jnp.pad(fs["k_new"].reshape(n_seq, SAMPLE_PAD * B_HEADS, B_DH), kv_pad)
    v_new_s = jnp.pad(fs["v_new"].reshape(n_seq, SAMPLE_PAD * B_HEADS, B_DH), kv_pad)
    bias_rows = jnp.broadcast_to(jnp.repeat(even_b_sb[0], SAMPLE_PAD)[:, None],
                                 (B_HEADS * SAMPLE_PAD, GATE_LANES))
    att_rows = B_HEADS * SAMPLE_PAD
    rider = _PagedAttnRider(
        q_s, k_new_s, v_new_s, cache_k, cache_v, page_table, bias_rows,
        jnp.zeros((n_seq, att_rows, B_WIDTH), F32), jnp.zeros((n_seq, att_rows, GATE_LANES), F32),
        0, PAGES_PER_STEP)
    assert bsz * (seq // A_CHUNK) == n_seq * rider.steps_per_seq
    zero_state = (jnp.zeros((bsz, A_HEADS, A_DH, A_DH), F32),
                  jnp.zeros((bsz, A_HEADS, 1, A_DH), F32),
                  jnp.zeros((bsz, A_HEADS, 1, GATE_LANES), F32))
    (ha_p, c_p, n_p, m_p), (att_acc, _) = mlstm(
        qkvog_p, gates_p, grow_p, *zero_state, A_CHUNK, rider=rider)
    hb_s = attn_sample_finish(att_acc, g_s)

    as_seq = lambda a: a.reshape(bsz, seq, B_WIDTH)
    hb_p = attn_prompt(as_seq(fp["q_b"]), as_seq(fp["g_b"]), as_seq(fp["k_bf"]), as_seq(fp["v_bf"]),
                       even_b_sb[0], ATTN_BQ, ATTN_BK, ATTN_HEADS_PER_STEP)
    st_in = (state_a_C[0], state_a_n[0][:, :, None, :],
             jnp.broadcast_to(state_a_m[0][:, :, None, None], (n_seq, A_HEADS, 1, GATE_LANES)))
    ha_s, c_s, n_s, m_s_new = mlstm(qkvog_s, gates_s, grow_s, *st_in, A_CHUNK)
    ha_s = ha_s[:, :SAMPLE_PAD]

    xp1, hp1, xs1, hs1 = _even_back(xp, xs, (ha_p, hb_p), (ha_s, hb_s), even_w_out[0], odd_norm[0], 512)
    y_p, y_s, v_rows = _odd_layer(xp1, hp1, xs1, hs1, odd_w_in_b, odd_v_gain[0], odd_w_s[0], odd_b_s[0],
                                  odd_w_out_b, final_norm, 512, n_seq)

    def sample_rows(a, *dims):
        return a.reshape((n_seq, SAMPLE_PAD) + dims)[:, :dec_seq]

    return (y_p.reshape(bsz, seq, D_MODEL),
            sample_rows(y_s, D_MODEL),
            c_p[None], n_p[:, :, 0, :][None], m_p[:, :, 0, 0][None],
            c_s[None], n_s[:, :, 0, :][None], m_s_new[:, :, 0, 0][None],
            fp["k_new"].reshape(1, bsz, seq, B_HEADS, B_DH), fp["v_new"].reshape(1, bsz, seq, B_HEADS, B_DH),
            sample_rows(fs["k_new"], B_HEADS, B_DH)[None], sample_rows(fs["v_new"], B_HEADS, B_DH)[None],
            sample_rows(v_rows, C_WIDTH)[None])
```

```python
import functools

import jax
import jax.numpy as jnp
from jax import lax
from jax.experimental import pallas as pl
from jax.experimental.pallas import tpu as pltpu

F32 = jnp.float32
BF16 = jnp.bfloat16

D_MODEL = 2048
PAGE_SIZE = 128
A_HEADS = 4
A_DH = 256
A_WIDTH = A_HEADS * A_DH
A_CHUNK = 128
B_HEADS = 8
B_DH = 128
B_WIDTH = B_HEADS * B_DH
C_WIDTH = D_MODEL
C_GROUPS = 8
C_GDIM = C_WIDTH // C_GROUPS
C_CHUNK = 128
RMS_EPS = 1e-6
GATE_LANES = 128
NEG_BIG = -1e30
SAMPLE_PAD = 8
ATTN_BQ = 512
ATTN_BK = 512
CUMSUM_BLOCK = 256
ATTN_HEADS_PER_STEP = 1
PAGES_PER_STEP = 8
RIDER_PARTS = 1
HOST_ROW_CHUNK = 1024

VMEM_LIMIT_BYTES = 56 * 1024 * 1024


def _params(*sem, flags=None):
    return pltpu.CompilerParams(dimension_semantics=sem, vmem_limit_bytes=VMEM_LIMIT_BYTES, flags=flags)


def _dot(a, b):
    return jnp.dot(a, b, preferred_element_type=F32)


def _dot_nt(a, b):
    return lax.dot_general(a, b, (((1,), (1,)), ((), ())), preferred_element_type=F32)


def _dot_tn(a, b):
    return lax.dot_general(a, b, (((0,), (0,)), ((), ())), preferred_element_type=F32)


def _softplus(z):
    return jnp.maximum(z, 0.0) + jnp.log(1.0 + jnp.exp(-jnp.abs(z)))


def _sigmoid(z):
    return 1.0 / (1.0 + jnp.exp(-z))


def _silu(z):
    return z * _sigmoid(z)


def _gelu_tanh(x):
    c = 0.7978845608028654
    return x * (0.5 * (1.0 + jnp.tanh(c * (x + 0.044715 * (x * x * x)))))


def _rms(x, g):
    return x * lax.rsqrt(jnp.mean(x * x, axis=-1, keepdims=True) + RMS_EPS) * g


def _split_hi_lo(x):
    hi = x.astype(BF16)
    lo = (x - hi.astype(F32)).astype(BF16)
    return hi, lo


def _norm_gates_kernel(x_ref, g_ref, whi_ref, wlo_ref, bias_ref, h_ref, gate_ref):
    h = _rms(x_ref[...], g_ref[...])
    h_hi, h_lo = _split_hi_lo(h)
    h_ref[...] = h_hi
    pre = (_dot(h_hi, whi_ref[...]) + _dot(h_hi, wlo_ref[...]) + _dot(h_lo, whi_ref[...])
           + bias_ref[...])
    lane = lax.broadcasted_iota(jnp.int32, pre.shape, 1)
    is_forget = (lane >= A_HEADS) & (lane < 2 * A_HEADS)
    gate_ref[...] = jnp.where(is_forget, -_softplus(-pre), pre)


def norm_gates(x, gain, w_gate, bias, tm):
    m = x.shape[0]
    whi, wlo = _split_hi_lo(w_gate)
    return pl.pallas_call(
        _norm_gates_kernel,
        grid=(m // tm,),
        in_specs=[pl.BlockSpec((tm, D_MODEL), lambda i: (i, 0)),
                  pl.BlockSpec((1, D_MODEL), lambda i: (0, 0)),
                  pl.BlockSpec((D_MODEL, GATE_LANES), lambda i: (0, 0)),
                  pl.BlockSpec((D_MODEL, GATE_LANES), lambda i: (0, 0)),
                  pl.BlockSpec((1, GATE_LANES), lambda i: (0, 0))],
        out_specs=[pl.BlockSpec((tm, D_MODEL), lambda i: (i, 0)),
                   pl.BlockSpec((tm, GATE_LANES), lambda i: (i, 0))],
        out_shape=[jax.ShapeDtypeStruct((m, D_MODEL), BF16),
                   jax.ShapeDtypeStruct((m, GATE_LANES), F32)],
        compiler_params=_params("arbitrary"),
        name="norm_gates",
    )(x, gain.reshape(1, D_MODEL), whi, wlo, bias)


def _hosted_kernel(*refs, n_in, n_out, phases_fn, lin, rank, rider):
    if rider is not None:
        refs = refs[1:]
    r_in, r_out = (rider.n_in, rider.n_out) if rider is not None else (0, 0)
    ins, refs = refs[:n_in], refs[n_in:]
    r_ins, refs = refs[:r_in], refs[r_in:]
    outs, refs = refs[:n_out], refs[n_out:]
    r_outs, refs = refs[:r_out], refs[r_out:]
    n_sc = len(refs) - (rider.n_scratch if rider is not None else 0)
    scratch, r_scratch = refs[:n_sc], refs[n_sc:]
    sets = [phases_fn(ins, outs, scratch)]
    if rider is not None:
        step = lin(*[pl.program_id(d) for d in range(rank)])
        sets.append(rider.phases(r_ins, r_outs, r_scratch, step))
    for pre, _, _ in sets:
        for cond, fn in pre:
            pl.when(cond)(fn)
    for _, parts, _ in sets:
        for part in parts:
            part()
    for _, _, post in sets:
        for cond, fn in post:
            pl.when(cond)(fn)


def _hosted_call(name, grid, lin, in_specs, out_specs, out_shape, scratch, phases_fn, args, rider=None):
    n_in, n_out = len(in_specs), len(out_specs)
    aliases = {}
    if rider is not None:
        r = rider.specs(lin, len(grid))
        aliases = {1 + n_in + i: n_out + o for i, o in r["aliases"].items()}
        in_specs, out_specs = in_specs + r["in_specs"], out_specs + r["out_specs"]
        out_shape, scratch = out_shape + r["out_shape"], scratch + r["scratch"]
        args = [rider.page_table] + list(args) + r["args"]
    kern = functools.partial(_hosted_kernel, n_in=n_in, n_out=n_out, phases_fn=phases_fn,
                             lin=lin, rank=len(grid), rider=rider)
    grid_spec = pltpu.PrefetchScalarGridSpec(
        num_scalar_prefetch=0 if rider is None else 1, grid=grid,
        in_specs=in_specs, out_specs=out_specs, scratch_shapes=scratch)
    outs = pl.pallas_call(
        kern, grid_spec=grid_spec, out_shape=out_shape, input_output_aliases=aliases,
        compiler_params=_params(*(["arbitrary"] * len(grid))), name=name,
    )(*args)
    return outs[:n_out], outs[n_out:]


def _two_group_steps(n_i, has_second):
    if not has_second:
        return n_i, (lambda i: i)
    return n_i + 1, (lambda i: jnp.maximum(i - 1, 0))


def proj(a, wt, row0, n_out, out_dtype, tm, tn, scale=None, second=None):
    m, k = a.shape
    n_i = m // tm
    n_steps, tile = _two_group_steps(n_i, second is not None)

    def phases(ins, outs, scratch):
        a_ref, wt_ref = ins[:2]
        o_ref, (wbf,) = outs[0], scratch
        i = pl.program_id(1)

        def cast():
            wbf[...] = wt_ref[...].astype(BF16)

        def main():
            y = _dot_nt(a_ref[...], wbf[...])
            o_ref[...] = (y if scale is None else y * scale).astype(o_ref.dtype)

        if second is None:
            return [(i == 0, cast)], [main], []

        def small():
            outs[1][...] = _dot_nt(ins[2][...], wbf[...]).astype(outs[1].dtype)

        return [(i == 0, cast), (i == 0, small), (i > 0, main)], [], []

    in_specs = [pl.BlockSpec((tm, k), lambda j, i, *_: (tile(i), 0)),
                pl.BlockSpec((pl.Element(tn), pl.Element(k)),
                             lambda j, i, *_: (pl.multiple_of(row0 + j * tn, 8), 0))]
    out_specs = [pl.BlockSpec((tm, tn), lambda j, i, *_: (tile(i), j))]
    out_shape = [jax.ShapeDtypeStruct((m, n_out), out_dtype)]
    args = [a, wt]
    if second is not None:
        a2, dtype2 = second
        in_specs.append(pl.BlockSpec(a2.shape, lambda j, i, *_: (0, 0)))
        out_specs.append(pl.BlockSpec((a2.shape[0], tn), lambda j, i, *_: (0, j)))
        out_shape.append(jax.ShapeDtypeStruct((a2.shape[0], n_out), dtype2))
        args.append(a2)
    outs, _ = _hosted_call("proj", (n_out // tn, n_steps), lambda j, i: j * n_steps + i,
                           in_specs, out_specs, out_shape, [pltpu.VMEM((tn, k), BF16)], phases, args)
    return outs[0] if second is None else tuple(outs)


def kv_proj(a, wt, row0, tm, second=None):
    m, k = a.shape
    n_i = m // tm
    n_steps, tile = _two_group_steps(n_i, second is not None)

    def phases(ins, outs, scratch):
        a_ref, wt_ref = ins[:2]
        (wbf,) = scratch
        i = pl.program_id(0)

        def cast():
            wbf[...] = wt_ref[...].astype(BF16)

        def rows_to(a_rows_ref, o_ref, obf_ref):
            def run():
                n_rows = a_rows_ref.shape[0]
                y = _dot_nt(a_rows_ref[...], wbf[...])
                obf_ref[...] = y.astype(BF16)
                for h in range(B_HEADS):
                    o_ref[pl.ds(h, n_rows, stride=B_HEADS), :] = y[:, h * B_DH:(h + 1) * B_DH]
            return run

        main = rows_to(a_ref, outs[0], outs[1])
        if second is None:
            return [(i == 0, cast)], [main], []
        return [(i == 0, cast), (i == 0, rows_to(ins[2], outs[2], outs[3])), (i > 0, main)], [], []

    def out_pair(rows, index):
        return ([pl.BlockSpec((rows * B_HEADS, B_DH), index), pl.BlockSpec((rows, B_WIDTH), index)],
                lambda total: [jax.ShapeDtypeStruct((total * B_HEADS, B_DH), F32),
                               jax.ShapeDtypeStruct((total, B_WIDTH), BF16)])

    in_specs = [pl.BlockSpec((tm, k), lambda i, *_: (tile(i), 0)),
                pl.BlockSpec((pl.Element(B_WIDTH), pl.Element(k)), lambda i, *_: (row0, 0))]
    out_specs, shapes = out_pair(tm, lambda i, *_: (tile(i), 0))
    out_shape = shapes(m)
    args = [a, wt]
    if second is not None:
        m2 = second.shape[0]
        in_specs.append(pl.BlockSpec(second.shape, lambda i, *_: (0, 0)))
        specs2, shapes2 = out_pair(m2, lambda i, *_: (0, 0))
        out_specs, out_shape = out_specs + specs2, out_shape + shapes2(m2)
        args.append(second)
    outs, _ = _hosted_call("kv_proj", (n_steps,), lambda i: i, in_specs, out_specs, out_shape,
                           [pltpu.VMEM((B_WIDTH, k), BF16)], phases, args)
    return tuple(outs)


def out_proj_norm(lhs, ws, x, gain, tm, emit_x, norm_dtype, second=None, spatial=None):
    m = x.shape[0]
    n_lhs, n_w = len(lhs), len(ws)
    n_i = m // tm
    n_steps, tile = _two_group_steps(n_i, second is not None)
    n_out = 2 if emit_x else 1

    def phases(ins, outs, scratch):
        w_refs = ins[n_lhs:n_lhs + n_w]
        x_ref, g_ref = ins[n_lhs + n_w], ins[n_lhs + n_w + 1]
        rest = ins[n_lhs + n_w + 2:]
        i = pl.program_id(0)

        def group(make_lhs, a_refs, x_ref, out_refs, row_chunk):
            def run():
                if make_lhs is not None:
                    make_lhs()
                for r in range(0, x_ref.shape[0], row_chunk):
                    rows = slice(r, r + row_chunk)
                    y = x_ref[rows, :]
                    for a_ref, w_ref in zip(a_refs, w_refs):
                        y = y + _dot(a_ref[rows, :].astype(BF16), w_ref[...])
                    if emit_x:
                        out_refs[0][rows, :] = y
                    out_refs[-1][rows, :] = _rms(y, g_ref[...]).astype(out_refs[-1].dtype)
            return run

        if second is not None:
            a2_refs, x2_ref, rest = rest[:n_w], rest[n_w], rest[n_w + 1:]
        if spatial is None:
            main = group(None, ins[:n_lhs], x_ref, outs[:n_out], min(tm, 256))
        else:
            (y_sc,) = scratch
            fill = functools.partial(_spatial_kernel, *rest[:5], y_sc, chunk=C_CHUNK, n_chunks=tm // C_CHUNK)
            main = group(fill, [y_sc], x_ref, outs[:n_out], min(tm, 256))
        if second is None:
            return [], [main], []
        small = group(None, a2_refs, x2_ref, outs[n_out:], x2_ref.shape[0])
        return [(i == 0, small), (i > 0, main)], [], []

    row_spec = lambda width: pl.BlockSpec((tm, width), lambda i, *_: (tile(i), 0))
    whole = lambda arr: pl.BlockSpec(arr.shape, lambda i, *_: (0,) * arr.ndim)
    gain2d = gain.reshape(1, D_MODEL)
    in_specs = ([row_spec(a.shape[1]) for a in lhs] + [whole(w) for w in ws]
                + [row_spec(D_MODEL), whole(gain2d)])
    out_specs = [row_spec(D_MODEL)] * n_out
    out_shape = ([jax.ShapeDtypeStruct((m, D_MODEL), F32)] if emit_x else []) \
        + [jax.ShapeDtypeStruct((m, D_MODEL), norm_dtype)]
    args = [*lhs, *ws, x, gain2d]
    scratch = []
    if second is not None:
        lhs2, x2, norm_dtype2 = second
        m2 = x2.shape[0]
        in_specs += [whole(a) for a in lhs2] + [whole(x2)]
        out_specs += [pl.BlockSpec((m2, D_MODEL), lambda i, *_: (0, 0))] * n_out
        out_shape += ([jax.ShapeDtypeStruct((m2, D_MODEL), F32)] if emit_x else []) \
            + [jax.ShapeDtypeStruct((m2, D_MODEL), norm_dtype2)]
        args += [*lhs2, x2]
    if spatial is not None:
        u, v, gate, w_s, b_s_t = spatial
        in_specs += [row_spec(C_WIDTH)] * 3 + [whole(w_s), whole(b_s_t)]
        args += [u, v, gate, w_s, b_s_t]
        scratch = [pltpu.VMEM((tm, C_WIDTH), BF16)]
    outs, _ = _hosted_call("out_proj_norm", (n_steps,), lambda i: i, in_specs, out_specs, out_shape,
                           scratch, phases, args)
    return list(outs)


def _mlstm_chunk(q_ref, k_ref, v_ref, og_ref, gg_ref, gcol_ref, grow_ref, h_ref, c_sc, n_sc, m_sc, chunk, heads):
    L = chunk
    gcol = gcol_ref[0]
    grow = grow_ref[0]
    tt = lax.broadcasted_iota(jnp.int32, (L, L), 0)
    ss = lax.broadcasted_iota(jnp.int32, (L, L), 1)
    causal = ss <= tt

    for head in heads:
        cols = slice(head * A_DH, (head + 1) * A_DH)
        q = q_ref[0, :, cols]
        ks = k_ref[0, :, cols] * jnp.asarray(A_DH ** -0.5, BF16)
        v = v_ref[0, :, cols]
        ig_col = gcol[:, head:head + 1]
        lf_col = gcol[:, head + A_HEADS:head + A_HEADS + 1]
        ig_row = grow[head:head + 1, :]
        lf_row = grow[head + A_HEADS:head + A_HEADS + 1, :]
        b_col = jnp.sum(jnp.where(causal, lf_row, 0.0), axis=1, keepdims=True)
        b_row = jnp.sum(jnp.where(tt <= ss, lf_col, 0.0), axis=0, keepdims=True)
        b_last = jnp.sum(lf_row, axis=1, keepdims=True)

        m0 = m_sc[head][:, :1]
        n0 = n_sc[head]
        c0 = c_sc[head]

        d = jnp.where(causal, b_col - b_row + ig_row, NEG_BIG)
        m_carry = b_col + m0
        m = jnp.maximum(m_carry, jnp.max(d, axis=1, keepdims=True))
        w_intra = jnp.exp(d - m)
        w_carry = jnp.exp(m_carry - m)
        s = _dot_nt(q, ks) * w_intra
        qf = q.astype(F32)
        num = _dot(s.astype(BF16), v) + w_carry * _dot_nt(q, c0.astype(BF16))
        den = jnp.sum(s, axis=1, keepdims=True) + w_carry * jnp.sum(qf * n0, axis=1, keepdims=True)
        h = num / jnp.maximum(jnp.abs(den), jnp.exp(-m))
        gated = h * _sigmoid(og_ref[0, :, cols].astype(F32)) * _silu(gg_ref[0, :, cols].astype(F32))
        h_ref[0, :, cols] = gated.astype(h_ref.dtype)

        m_carry_last = b_last + m0
        d_last_row = b_last - b_row + ig_row
        m_new = jnp.maximum(m_carry_last, jnp.max(d_last_row, axis=1, keepdims=True))
        wc_last = jnp.exp(m_carry_last - m_new)
        w_last_col = jnp.exp(b_last - b_col + ig_col - m_new)
        vw = (v.astype(F32) * w_last_col).astype(BF16)
        c_new = wc_last * c0 + _dot_tn(vw, ks)
        n_new = wc_last * n0 + jnp.sum(ks.astype(F32) * w_last_col, axis=0, keepdims=True)
        c_sc[head] = c_new
        n_sc[head] = n_new
        m_sc[head] = jnp.broadcast_to(m_new, (1, GATE_LANES))


def mlstm(qkvog, gcol, grow, c0, n0, m0, chunk, rider=None):
    bsz, t_len, _ = qkvog.shape
    nc = t_len // chunk
    hd = A_HEADS

    def phases(ins, outs, state):
        c0_ref, n0_ref, m0_ref = ins[7:]
        h_ref, c_out_ref, n_out_ref, m_out_ref = outs
        c_sc, n_sc, m_sc = state
        ci = pl.program_id(1)

        def init():
            c_sc[...] = c0_ref[0]
            n_sc[...] = n0_ref[0]
            m_sc[...] = m0_ref[0]

        def head_part(head):
            return lambda: _mlstm_chunk(*ins[:7], h_ref, c_sc, n_sc, m_sc, chunk, (head,))

        parts = [head_part(head) for head in range(A_HEADS)]

        def final():
            c_out_ref[0] = c_sc[...]
            n_out_ref[0] = n_sc[...]
            m_out_ref[0] = m_sc[...]

        return [(ci == 0, init)], parts, [(ci == nc - 1, final)]

    blk = lambda seg: pl.BlockSpec((1, chunk, A_WIDTH), lambda b, c, *_, seg=seg: (b, c, seg))
    st4 = lambda r, w: pl.BlockSpec((1, hd, r, w), lambda b, c, *_: (b, 0, 0, 0))
    outs, rest = _hosted_call(
        "mlstm", (bsz, nc), lambda b, c: b * nc + c,
        [blk(0), blk(1), blk(2), blk(3), blk(4),
         pl.BlockSpec((1, chunk, GATE_LANES), lambda b, c, *_: (b, c, 0)),
         pl.BlockSpec((1, 8, chunk), lambda b, c, *_: (b, 0, c)),
         st4(A_DH, A_DH), st4(1, A_DH), st4(1, GATE_LANES)],
        [pl.BlockSpec((1, chunk, A_WIDTH), lambda b, c, *_: (b, c, 0)),
         st4(A_DH, A_DH), st4(1, A_DH), st4(1, GATE_LANES)],
        [jax.ShapeDtypeStruct((bsz, t_len, A_WIDTH), BF16),
         jax.ShapeDtypeStruct((bsz, hd, A_DH, A_DH), F32),
         jax.ShapeDtypeStruct((bsz, hd, 1, A_DH), F32),
         jax.ShapeDtypeStruct((bsz, hd, 1, GATE_LANES), F32)],
        [pltpu.VMEM((hd, A_DH, A_DH), F32), pltpu.VMEM((hd, 1, A_DH), F32),
         pltpu.VMEM((hd, 1, GATE_LANES), F32)],
        phases, [qkvog, qkvog, qkvog, qkvog, qkvog, gcol, grow, c0, n0, m0], rider)
    return tuple(outs) if rider is None else (tuple(outs), rest)


def _stick_block(q, kb, vb, bias, run, mask, upper):
    rows = q.shape[0]
    sub = upper.shape[0]
    n_sub = kb.shape[0] // sub
    z = _dot_nt(q, kb) if bias is None else _dot_nt(q, kb) * (B_DH ** -0.5) + bias
    sp = _softplus(z)
    spm = sp if mask is None else jnp.where(mask, sp, 0.0)
    hi, lo = _split_hi_lo(spm)
    laters = [None] * n_sub
    total = None
    for i in reversed(range(n_sub)):
        ln = slice(i * sub, (i + 1) * sub)
        both = _dot(jnp.concatenate([hi[:, ln], lo[:, ln]], axis=0), upper)
        carry = run if total is None else run + total
        laters[i] = both[:rows] + both[rows:] + carry
        part = jnp.sum(spm[:, ln], axis=1, keepdims=True)
        total = part if total is None else total + part
    later = laters[0] if n_sub == 1 else jnp.concatenate(laters, axis=1)
    a = jnp.exp(z - sp - later)
    if mask is not None:
        a = jnp.where(mask, a, 0.0)
    return _dot(a.astype(BF16), vb), total


def _strict_upper(n):
    j = lax.broadcasted_iota(jnp.int32, (n, n), 0)
    s = lax.broadcasted_iota(jnp.int32, (n, n), 1)
    return jnp.where(j > s, 1.0, 0.0).astype(BF16)


def _attn_prompt_kernel(bias_ref, q_ref, k_ref, v_ref, g_ref, o_ref, *, bq, bk, n_heads):
    head0 = pl.program_id(1) * n_heads
    qi = pl.program_id(2)
    kbf = k_ref.at[0]
    vbf = v_ref.at[0]
    upper = _strict_upper(min(bk, CUMSUM_BLOCK))
    row = lax.broadcasted_iota(jnp.int32, (bq, bk), 0)
    col = lax.broadcasted_iota(jnp.int32, (bq, bk), 1)
    lanes = [slice(h * B_DH, (h + 1) * B_DH) for h in range(n_heads)]
    lane_q = lax.broadcasted_iota(jnp.int32, (bq, B_DH), 1)
    lane_k = lax.broadcasted_iota(jnp.int32, (bk, B_DH), 1)
    ones_cols = jnp.where(lane_q < 2, 1.0, 0.0).astype(BF16)
    qs = [jnp.concatenate([q_ref[0, :, ln], ones_cols], axis=1) for ln in lanes]
    bias_cols = []
    for h in range(n_heads):
        b = jnp.full((bk, B_DH), bias_ref[head0 + h], F32)
        b_hi = b.astype(BF16).astype(F32)
        bias_cols.append(jnp.where(lane_k == 0, b_hi, jnp.where(lane_k == 1, b - b_hi, 0.0)).astype(BF16))

    def blocks(kj, runs, mask):
        start = pl.multiple_of(kj * bk, bk)
        return [_stick_block(qs[h], jnp.concatenate([kbf[pl.ds(start, bk), ln], bias_cols[h]], axis=1),
                             vbf[pl.ds(start, bk), ln], None, runs[h], mask, upper)
                for h, ln in enumerate(lanes)]

    q0 = qi * bq
    n_full = q0 // bk
    accs = [jnp.zeros((bq, B_DH), F32)] * n_heads
    runs = [jnp.zeros((bq, 1), F32)] * n_heads
    for m in reversed(range(max(1, bq // bk))):
        kj = n_full + m
        res = blocks(kj, runs, col + (kj * bk - q0) < row)
        accs = [a + c for a, (c, _) in zip(accs, res)]
        runs = [r + t for r, (_, t) in zip(runs, res)]

    def body(it, carry):
        accs, runs = carry
        res = blocks(n_full - 1 - it, runs, None)
        return (tuple(a + c for a, (c, _) in zip(accs, res)),
                tuple(r + t for r, (_, t) in zip(runs, res)))

    accs, runs = lax.fori_loop(0, n_full, body, (tuple(accs), tuple(runs)))
    for h, ln in enumerate(lanes):
        o_ref[0, :, ln] = (accs[h] * _silu(g_ref[0, :, ln].astype(F32))).astype(o_ref.dtype)


def attn_prompt(q, g, k, v, b_sb, bq, bk, n_heads, rider=None):
    bsz, t_len, _ = k.shape
    width = n_heads * B_DH
    n_hgrp, n_q = B_HEADS // n_heads, t_len // bq
    q_spec = pl.BlockSpec((1, bq, width), lambda b, h, i, *_: (b, i, h))
    kv_spec = pl.BlockSpec((1, t_len, width), lambda b, h, i, *_: (b, 0, h))

    def phases(ins, outs, scratch):
        del scratch
        body = functools.partial(_attn_prompt_kernel, *ins, *outs, bq=bq, bk=bk, n_heads=n_heads)
        return [], [body], []

    (out,), rest = _hosted_call(
        "attn_prompt", (bsz, n_hgrp, n_q), lambda b, h, i: (b * n_hgrp + h) * n_q + i,
        [pl.BlockSpec(memory_space=pltpu.SMEM), q_spec, kv_spec, kv_spec, q_spec], [q_spec],
        [jax.ShapeDtypeStruct((bsz, t_len, B_WIDTH), BF16)], [], phases, [b_sb, q, k, v, g], rider)
    return out if rider is None else (out, rest)


def _attn_sample_phases(ins, outs, scratch, n_group, n_parts):
    bias_ref, q_ref, knew_ref, vnew_ref, acc_in_ref, run_in_ref = ins[:6]
    k_refs = ins[6:6 + n_group]
    v_refs = ins[6 + n_group:]
    acc_out_ref, run_out_ref = outs
    qbd, acc, run, kcat, vcat = scratch
    rows = B_HEADS * SAMPLE_PAD
    upper = _strict_upper(PAGE_SIZE)
    bias = bias_ref[...][:, :1]

    def repack(page, dst, i):
        for h in range(B_HEADS):
            dst[i * PAGE_SIZE:(i + 1) * PAGE_SIZE, h * B_DH:(h + 1) * B_DH] = page(h).astype(BF16)

    def step(slot0, n_blk, mask):
        keys = slice(slot0 * PAGE_SIZE, (slot0 + n_blk) * PAGE_SIZE)
        z = _dot_nt(qbd[...], kcat[keys, :]) * (B_DH ** -0.5) + bias
        sp = _softplus(z)
        spm = sp if mask is None else jnp.where(mask, sp, 0.0)
        hi, lo = _split_hi_lo(spm)
        carry = run[...][:, :1]
        laters = []
        for i in range(n_blk):
            ln = slice(i * PAGE_SIZE, (i + 1) * PAGE_SIZE)
            both = _dot(jnp.concatenate([hi[:, ln], lo[:, ln]], axis=0), upper)
            laters.append(both[:rows] + both[rows:] + carry)
            carry = carry + jnp.sum(spm[:, ln], axis=1, keepdims=True)
        later = laters[0] if n_blk == 1 else jnp.concatenate(laters, axis=1)
        a = jnp.exp(z - sp - later)
        if mask is not None:
            a = jnp.where(mask, a, 0.0)
        acc[...] += _dot(a.astype(BF16), vcat[keys, :])
        run[...] = jnp.broadcast_to(carry, run.shape)

    def build_queries():
        r = lax.broadcasted_iota(jnp.int32, (rows, B_WIDTH), 0)
        c = lax.broadcasted_iota(jnp.int32, (rows, B_WIDTH), 1)
        q_rep = jnp.concatenate([q_ref[0].astype(F32)] * B_HEADS, axis=0)
        qbd[...] = jnp.where((r // SAMPLE_PAD) == (c // B_DH), q_rep, 0.0).astype(BF16)

    def start_sequence():
        build_queries()
        acc[...] = jnp.zeros_like(acc)
        run[...] = jnp.zeros_like(run)
        repack(lambda h: knew_ref[0, pl.ds(h, PAGE_SIZE, stride=B_HEADS), :], kcat, 0)
        repack(lambda h: vnew_ref[0, pl.ds(h, PAGE_SIZE, stride=B_HEADS), :], vcat, 0)
        t = lax.broadcasted_iota(jnp.int32, (rows, PAGE_SIZE), 0) % SAMPLE_PAD
        s = lax.broadcasted_iota(jnp.int32, (rows, PAGE_SIZE), 1)
        step(0, 1, s < t)

    def resume_sequence():
        build_queries()
        acc[...] = acc_in_ref[0]
        run[...] = run_in_ref[0]

    def pages_part(slot0, n_blk, last):
        def run_part():
            for i in range(slot0, slot0 + n_blk):
                repack(lambda h, r=k_refs[i]: r[0, 0, pl.ds(h, PAGE_SIZE, stride=B_HEADS), :], kcat, i)
                repack(lambda h, r=v_refs[i]: r[0, 0, pl.ds(h, PAGE_SIZE, stride=B_HEADS), :], vcat, i)
            step(slot0, n_blk, None)
            if last:
                acc_out_ref[0] = acc[...]
                run_out_ref[0] = run[...]
        return run_part

    per_part = n_group // n_parts
    parts = [pages_part(k * per_part, per_part, k == n_parts - 1) for k in range(n_parts)]
    return start_sequence, resume_sequence, parts


class _PagedAttnRider:
    n_out = 2
    n_scratch = 5

    def __init__(self, q, k_new, v_new, cache_k, cache_v, page_table, bias_rows, acc, run, first_step, n_group):
        self.arrays = (bias_rows, q, k_new, v_new, acc, run)
        self.caches = (cache_k, cache_v)
        self.page_table = page_table
        self.first_step = first_step
        self.n_group = n_group
        self.n_in = 6 + 2 * n_group
        self.steps_per_seq = page_table.shape[1] // n_group

    def specs(self, lin, rank):
        n_group, spq = self.n_group, self.steps_per_seq
        n_pages = self.page_table.shape[1]
        rows = B_HEADS * SAMPLE_PAD
        page_rows = PAGE_SIZE * B_HEADS
        gstep = lambda a: self.first_step + lin(*a[:rank])
        seq_map = lambda *a: (gstep(a) // spq, 0, 0)

        def page_spec(i):
            def index(*a):
                g, pt = gstep(a), a[rank]
                return (0, pt[g // spq, n_pages - 1 - ((g % spq) * n_group + i)], 0, 0)
            return pl.BlockSpec((1, 1, page_rows, B_DH), index)

        acc_spec = pl.BlockSpec((1, rows, B_WIDTH), seq_map)
        run_spec = pl.BlockSpec((1, rows, GATE_LANES), seq_map)
        new_spec = pl.BlockSpec((1, page_rows, B_DH), seq_map)
        acc, run = self.arrays[4:]
        return dict(
            in_specs=[pl.BlockSpec((rows, GATE_LANES), lambda *a: (0, 0)),
                      pl.BlockSpec((1, SAMPLE_PAD, B_WIDTH), seq_map), new_spec, new_spec,
                      acc_spec, run_spec] + [page_spec(i) for i in range(n_group)] * 2,
            out_specs=[acc_spec, run_spec],
            out_shape=[jax.ShapeDtypeStruct(acc.shape, F32), jax.ShapeDtypeStruct(run.shape, F32)],
            scratch=[pltpu.VMEM((rows, B_WIDTH), BF16), pltpu.VMEM((rows, B_WIDTH), F32),
                     pltpu.VMEM((rows, GATE_LANES), F32),
                     pltpu.VMEM((n_group * PAGE_SIZE, B_WIDTH), BF16),
                     pltpu.VMEM((n_group * PAGE_SIZE, B_WIDTH), BF16)],
            args=list(self.arrays) + [self.caches[0]] * n_group + [self.caches[1]] * n_group,
            aliases={4: 0, 5: 1},
        )

    def phases(self, ins, outs, scratch, local_step):
        start, resume, parts = _attn_sample_phases(ins, outs, scratch, self.n_group, RIDER_PARTS)
        p = (self.first_step + local_step) % self.steps_per_seq
        return [(p == 0, start), ((local_step == 0) & (p != 0), resume)], parts, []


def attn_sample_finish(acc, g):
    n_seq = acc.shape[0]

    def kern(acc_ref, g_ref, o_ref):
        a = acc_ref[0]
        c = lax.broadcasted_iota(jnp.int32, (SAMPLE_PAD, B_WIDTH), 1) // B_DH
        out = jnp.zeros((SAMPLE_PAD, B_WIDTH), F32)
        for h in range(B_HEADS):
            out = out + jnp.where(c == h, a[h * SAMPLE_PAD:(h + 1) * SAMPLE_PAD, :], 0.0)
        o_ref[0] = out * _silu(g_ref[0])

    return pl.pallas_call(
        kern, grid=(n_seq,),
        in_specs=[pl.BlockSpec((1, B_HEADS * SAMPLE_PAD, B_WIDTH), lambda s: (s, 0, 0)),
                  pl.BlockSpec((1, SAMPLE_PAD, B_WIDTH), lambda s: (s, 0, 0))],
        out_specs=pl.BlockSpec((1, SAMPLE_PAD, B_WIDTH), lambda s: (s, 0, 0)),
        out_shape=jax.ShapeDtypeStruct((n_seq, SAMPLE_PAD, B_WIDTH), F32),
        compiler_params=_params("arbitrary"), name="attn_sample_finish",
    )(acc, g)


def proj_act(h, w, col0, v_gain, act, tm, tn, out_dtype, second=None):
    m = h.shape[0]
    j0 = col0 // tn
    n_col = C_WIDTH // tn
    n_i = m // tm
    n_steps, tile = _two_group_steps(n_i, second is not None)

    def phases(ins, outs, scratch):
        a_ref, w_ref, vg_ref = ins[:3]
        (wbf,) = scratch
        i = pl.program_id(1)

        def cast():
            wbf[...] = w_ref[...].astype(BF16)

        def group(rows_ref, o_ref, row_chunk):
            def run():
                for r in range(0, rows_ref.shape[0], row_chunk):
                    rows = slice(r, r + row_chunk)
                    y = _dot(rows_ref[rows, :], wbf[...])
                    if act == "gelu":
                        y = _gelu_tanh(y)
                    elif act == "silu":
                        y = _silu(y)
                    else:
                        y = _rms(_gelu_tanh(y), vg_ref[...])
                    o_ref[rows, :] = y.astype(o_ref.dtype)
            return run

        main = group(a_ref, outs[0], min(tm, 256))
        if second is None:
            return [(i == 0, cast)], [main], []
        small = group(ins[3], outs[1], ins[3].shape[0])
        return [(i == 0, cast), (i == 0, small), (i > 0, main)], [], []

    w_mode = dict(pipeline_mode=pl.Buffered(1)) if n_col == 1 else {}
    in_specs = [pl.BlockSpec((tm, D_MODEL), lambda j, i, *_: (tile(i), 0)),
                pl.BlockSpec((D_MODEL, tn), lambda j, i, *_: (0, j0 + j), **w_mode),
                pl.BlockSpec((1, tn), lambda j, i, *_: (0, j))]
    out_specs = [pl.BlockSpec((tm, tn), lambda j, i, *_: (tile(i), j))]
    out_shape = [jax.ShapeDtypeStruct((m, C_WIDTH), out_dtype)]
    args = [h, w, v_gain.reshape(1, C_WIDTH)]
    if second is not None:
        h2, dtype2 = second
        in_specs.append(pl.BlockSpec(h2.shape, lambda j, i, *_: (0, 0)))
        out_specs.append(pl.BlockSpec((h2.shape[0], tn), lambda j, i, *_: (0, j)))
        out_shape.append(jax.ShapeDtypeStruct((h2.shape[0], C_WIDTH), dtype2))
        args.append(h2)
    outs, _ = _hosted_call("proj_" + act, (n_col, n_steps), lambda j, i: j * n_steps + i,
                           in_specs, out_specs, out_shape, [pltpu.VMEM((D_MODEL, tn), BF16)], phases, args)
    return outs[0] if second is None else tuple(outs)


def odd_in(h, w, v_gain, tm, act_dtype, second=None):
    u = proj_act(h, w, 0, v_gain, "gelu", tm, 1024, act_dtype, second)
    v = proj_act(h, w, C_WIDTH, v_gain, "gelu_rms", min(tm, 512), C_WIDTH, act_dtype, second)
    g = proj_act(h, w, 2 * C_WIDTH, v_gain, "silu", tm, 1024, act_dtype, second)
    if second is None:
        return u, v, g
    return (u[0], v[0], g[0]), (u[1], v[1], g[1])


def _spatial_kernel(u_ref, v_ref, g_ref, ws_ref, bs_ref, y_ref, *, chunk, n_chunks):
    tt = lax.broadcasted_iota(jnp.int32, (chunk, chunk), 0)
    ss = lax.broadcasted_iota(jnp.int32, (chunk, chunk), 1)
    causal = ss <= tt
    for grp in range(C_GROUPS):
        wm = jnp.where(causal, ws_ref[grp], 0.0)
        bcol = bs_ref[:, grp:grp + 1]
        cols = slice(grp * C_GDIM, (grp + 1) * C_GDIM)
        for c in range(n_chunks):
            rows = slice(c * chunk, (c + 1) * chunk)
            vv = v_ref[rows, cols]
            if chunk >= 128:
                sv = _dot(wm.astype(BF16), vv)
            else:
                vf = vv.astype(F32)
                sv = jnp.zeros((chunk, C_GDIM), F32)
                for s in range(chunk):
                    sv = sv + wm[:, s:s + 1] * vf[s:s + 1, :]
            sv = sv + bcol
            y = u_ref[rows, cols].astype(F32) * sv * g_ref[rows, cols].astype(F32)
            y_ref[rows, cols] = y.astype(y_ref.dtype)


def spatial_gate(u, v, g, w_s, b_s_t, chunk, n_chunks):
    m = u.shape[0]
    tm = chunk * n_chunks
    row_spec = pl.BlockSpec((tm, C_WIDTH), lambda i: (i, 0))
    return pl.pallas_call(
        functools.partial(_spatial_kernel, chunk=chunk, n_chunks=n_chunks),
        grid=(m // tm,),
        in_specs=[row_spec, row_spec, row_spec,
                  pl.BlockSpec((C_GROUPS, chunk, chunk), lambda i: (0, 0, 0)),
                  pl.BlockSpec((chunk, C_GROUPS), lambda i: (0, 0))],
        out_specs=row_spec,
        out_shape=jax.ShapeDtypeStruct((m, C_WIDTH), u.dtype),
        compiler_params=_params("arbitrary"),
        name="spatial_gate",
    )(u, v, g, w_s, b_s_t)


def _even_weights(w_in, b_i, b_f):
    gate0 = 5 * A_WIDTH
    b0 = gate0 + 2 * A_HEADS
    wt = jnp.swapaxes(w_in, 0, 1)
    w_gate = jnp.pad(w_in[:, gate0:b0], ((0, 0), (0, GATE_LANES - 2 * A_HEADS)))
    bias = jnp.pad(jnp.concatenate([b_i, b_f]), (0, GATE_LANES - 2 * A_HEADS)).reshape(1, GATE_LANES)
    return wt, b0, w_gate, bias


def _mlstm_inputs(qkvog, gates, bsz, t_len, valid_len):
    qkvog = qkvog.reshape(bsz, t_len, 5 * A_WIDTH)
    gates = gates.reshape(bsz, t_len, GATE_LANES)
    t_pad = -(-t_len // A_CHUNK) * A_CHUNK
    pad = ((0, 0), (0, t_pad - t_len), (0, 0))
    if valid_len < t_pad:
        qkvog, gates = jnp.pad(qkvog, pad), jnp.pad(gates, pad)
        pos = jnp.arange(t_pad)[None, :, None]
        lane = jnp.arange(GATE_LANES)[None, None, :]
        gates = jnp.where((pos >= valid_len) & (lane < A_HEADS), NEG_BIG, gates)
        gates = jnp.where((pos >= valid_len) & (lane >= A_HEADS), 0.0, gates)
    return qkvog, gates, gates[:, :, :2 * A_HEADS].transpose(0, 2, 1)


def _even_front(xp, xs, ew, g_norm, tm):
    wt, b0, w_gate, bias = ew
    hp, gates_p = norm_gates(xp, g_norm, w_gate, bias, min(tm, 512))
    hs, gates_s = norm_gates(xs, g_norm, w_gate, bias, xs.shape[0])
    tn, tm_kv = 1024, min(tm, 512)
    qkvog = proj(hp, wt, 0, 5 * A_WIDTH, BF16, tm, tn, second=(hs, BF16))
    q_b = proj(hp, wt, b0, B_WIDTH, BF16, tm, tn, scale=B_DH ** -0.5, second=(hs, F32))
    g_b = proj(hp, wt, b0 + 3 * B_WIDTH, B_WIDTH, BF16, tm, tn, second=(hs, F32))
    k = kv_proj(hp, wt, b0 + B_WIDTH, tm_kv, second=hs)
    v = kv_proj(hp, wt, b0 + 2 * B_WIDTH, tm_kv, second=hs)
    groups = []
    for i, gates in enumerate((gates_p, gates_s)):
        groups.append(dict(qkvog=qkvog[i], gates=gates, q_b=q_b[i], g_b=g_b[i],
                           k_new=k[2 * i], k_bf=k[2 * i + 1], v_new=v[2 * i], v_bf=v[2 * i + 1]))
    return groups


def _even_back(xp, xs, mix_p, mix_s, w_out, next_gain, tm):
    w_out_b = w_out.astype(BF16)
    flat = lambda mix, m: [mix[0].reshape(m, A_WIDTH), mix[1].reshape(m, B_WIDTH)]
    return out_proj_norm(flat(mix_p, xp.shape[0]), [w_out_b[:A_WIDTH], w_out_b[A_WIDTH:]], xp, next_gain,
                         tm, True, BF16, second=(flat(mix_s, xs.shape[0]), xs, BF16))


def _odd_layer(xp, hp, xs, hs, w_in, v_gain, w_s, b_s, w_out_b, final_gain, tm, n_seq):
    (u, v, g), (u2, v2, g2) = odd_in(hp, w_in, v_gain, 2 * tm, BF16, second=(hs, F32))
    y2 = spatial_gate(u2, v2, g2, w_s[:, :SAMPLE_PAD, :SAMPLE_PAD], b_s[:, :SAMPLE_PAD].T, SAMPLE_PAD, n_seq)
    y_p, y_s = out_proj_norm([], [w_out_b], xp, final_gain, tm, False, F32, second=([y2], xs, F32),
                             spatial=(u, v, g, w_s, b_s.T))
    return y_p, y_s, v2


def kernel(x_prompt, x_sample, state_a_C, state_a_n, state_a_m, cache_b_k, cache_b_v, page_table,
           even_norm, even_w_in, even_b_i, even_b_f, even_b_sb, even_w_out,
           odd_norm, odd_w_in, odd_v_gain, odd_w_s, odd_b_s, odd_w_out, final_norm):
    bsz, seq, _ = x_prompt.shape
    n_seq, dec_seq, _ = x_sample.shape
    n_pool = cache_b_k.shape[1]

    ew = _even_weights(even_w_in[0], even_b_i[0], even_b_f[0])
    odd_w_in_b = odd_w_in[0]
    odd_w_out_b = odd_w_out[0].astype(BF16)

    xp = x_prompt.reshape(bsz * seq, D_MODEL)
    xs = jnp.pad(x_sample, ((0, 0), (0, SAMPLE_PAD - dec_seq), (0, 0))).reshape(n_seq * SAMPLE_PAD, D_MODEL)
    fp, fs = _even_front(xp, xs, ew, even_norm[0], 1024)
    qkvog_p, gates_p, grow_p = _mlstm_inputs(fp["qkvog"], fp["gates"], bsz, seq, seq)
    qkvog_s, gates_s, grow_s = _mlstm_inputs(fs["qkvog"], fs["gates"], n_seq, SAMPLE_PAD, dec_seq)
    q_s, g_s = (fs[name].reshape(n_seq, SAMPLE_PAD, B_WIDTH) for name in ("q_b", "g_b"))

    assert cache_b_k.shape[0] == 1 and cache_b_v.shape[0] == 1
    page_view = (1, n_pool, PAGE_SIZE * B_HEADS, B_DH)
    cache_k, cache_v = cache_b_k.reshape(page_view), cache_b_v.reshape(page_view)
    kv_pad = ((0, 0), (0, (PAGE_SIZE - SAMPLE_PAD) * B_HEADS), (0, 0))
    k_new_s = jnp.pad(fs["k_new"].reshape(n_seq, SAMPLE_PAD * B_HEADS, B_DH), kv_pad)
    v_new_s = jnp.pad(fs["v_new"].reshape(n_seq, SAMPLE_PAD * B_HEADS, B_DH), kv_pad)
    bias_rows = jnp.broadcast_to(jnp.repeat(even_b_sb[0], SAMPLE_PAD)[:, None],
                                 (B_HEADS * SAMPLE_PAD, GATE_LANES))
    att_rows = B_HEADS * SAMPLE_PAD
    rider = _PagedAttnRider(
        q_s, k_new_s, v_new_s, cache_k, cache_v, page_table, bias_rows,
        jnp.zeros((n_seq, att_rows, B_WIDTH), F32), jnp.zeros((n_seq, att_rows, GATE_LANES), F32),
        0, PAGES_PER_STEP)
    assert bsz * (B_HEADS // ATTN_HEADS_PER_STEP) * (seq // ATTN_BQ) == n_seq * rider.steps_per_seq
    as_seq = lambda a: a.reshape(bsz, seq, B_WIDTH)
    hb_p, (att_acc, _) = attn_prompt(
        as_seq(fp["q_b"]), as_seq(fp["g_b"]), as_seq(fp["k_bf"]), as_seq(fp["v_bf"]),
        even_b_sb[0], ATTN_BQ, ATTN_BK, ATTN_HEADS_PER_STEP, rider=rider)
    hb_s = attn_sample_finish(att_acc, g_s)

    zero_state = (jnp.zeros((bsz, A_HEADS, A_DH, A_DH), F32),
                  jnp.zeros((bsz, A_HEADS, 1, A_DH), F32),
                  jnp.zeros((bsz, A_HEADS, 1, GATE_LANES), F32))
    ha_p, c_p, n_p, m_p = mlstm(qkvog_p, gates_p, grow_p, *zero_state, A_CHUNK)
    st_in = (state_a_C[0], state_a_n[0][:, :, None, :],
             jnp.broadcast_to(state_a_m[0][:, :, None, None], (n_seq, A_HEADS, 1, GATE_LANES)))
    ha_s, c_s, n_s, m_s_new = mlstm(qkvog_s, gates_s, grow_s, *st_in, A_CHUNK)
    ha_s = ha_s[:, :SAMPLE_PAD]

    xp1, hp1, xs1, hs1 = _even_back(xp, xs, (ha_p, hb_p), (ha_s, hb_s), even_w_out[0], odd_norm[0], 512)
    y_p, y_s, v_rows = _odd_layer(xp1, hp1, xs1, hs1, odd_w_in_b, odd_v_gain[0], odd_w_s[0], odd_b_s[0],
                                  odd_w_out_b, final_norm, 512, n_seq)

    def sample_rows(a, *dims):
        return a.reshape((n_seq, SAMPLE_PAD) + dims)[:, :dec_seq]

    return (y_p.reshape(bsz, seq, D_MODEL),
            sample_rows(y_s, D_MODEL),
            c_p[None], n_p[:, :, 0, :][None], m_p[:, :, 0, 0][None],
            c_s[None], n_s[:, :, 0, :][None], m_s_new[:, :, 0, 0][None],
            fp["k_new"].reshape(1, bsz, seq, B_HEADS, B_DH), fp["v_new"].reshape(1, bsz, seq, B_HEADS, B_DH),
            sample_rows(fs["k_new"], B_HEADS, B_DH)[None], sample_rows(fs["v_new"], B_HEADS, B_DH)[None],
            sample_rows(v_rows, C_WIDTH)[None])
```

```python
import functools

import jax
import jax.numpy as jnp
from jax import lax
from jax.experimental import pallas as pl
from jax.experimental.pallas import tpu as pltpu

F32 = jnp.float32
BF16 = jnp.bfloat16

D_MODEL = 2048
PAGE_SIZE = 128
A_HEADS = 4
A_DH = 256
A_WIDTH = A_HEADS * A_DH
A_CHUNK = 128
B_HEADS = 8
B_DH = 128
B_WIDTH = B_HEADS * B_DH
C_WIDTH = D_MODEL
C_GROUPS = 8
C_GDIM = C_WIDTH // C_GROUPS
C_CHUNK = 128
RMS_EPS = 1e-6
GATE_LANES = 128
NEG_BIG = -1e30
SAMPLE_PAD = 8
ATTN_BQ = 512
ATTN_BK = 512
CUMSUM_BLOCK = 256
ATTN_HEADS_PER_STEP = 2
PAGES_PER_STEP = 16
RIDER_PARTS = 1
HOST_ROW_CHUNK = 1024

VMEM_LIMIT_BYTES = 56 * 1024 * 1024


def _params(*sem, flags=None):
    return pltpu.CompilerParams(dimension_semantics=sem, vmem_limit_bytes=VMEM_LIMIT_BYTES, flags=flags)


def _dot(a, b):
    return jnp.dot(a, b, preferred_element_type=F32)


def _dot_nt(a, b):
    return lax.dot_general(a, b, (((1,), (1,)), ((), ())), preferred_element_type=F32)


def _dot_tn(a, b):
    return lax.dot_general(a, b, (((0,), (0,)), ((), ())), preferred_element_type=F32)


def _softplus(z):
    return jnp.maximum(z, 0.0) + jnp.log(1.0 + jnp.exp(-jnp.abs(z)))


def _sigmoid(z):
    return 1.0 / (1.0 + jnp.exp(-z))


def _silu(z):
    return z * _sigmoid(z)


def _gelu_tanh(x):
    c = 0.7978845608028654
    return x * (0.5 * (1.0 + jnp.tanh(c * (x + 0.044715 * (x * x * x)))))


def _rms(x, g):
    return x * lax.rsqrt(jnp.mean(x * x, axis=-1, keepdims=True) + RMS_EPS) * g


def _split_hi_lo(x):
    hi = x.astype(BF16)
    lo = (x - hi.astype(F32)).astype(BF16)
    return hi, lo


def _norm_gates_kernel(x_ref, g_ref, whi_ref, wlo_ref, bias_ref, h_ref, gate_ref):
    h = _rms(x_ref[...], g_ref[...])
    h_hi, h_lo = _split_hi_lo(h)
    h_ref[...] = h_hi
    pre = (_dot(h_hi, whi_ref[...]) + _dot(h_hi, wlo_ref[...]) + _dot(h_lo, whi_ref[...])
           + bias_ref[...])
    lane = lax.broadcasted_iota(jnp.int32, pre.shape, 1)
    is_forget = (lane >= A_HEADS) & (lane < 2 * A_HEADS)
    gate_ref[...] = jnp.where(is_forget, -_softplus(-pre), pre)


def norm_gates(x, gain, w_gate, bias, tm):
    m = x.shape[0]
    whi, wlo = _split_hi_lo(w_gate)
    return pl.pallas_call(
        _norm_gates_kernel,
        grid=(m // tm,),
        in_specs=[pl.BlockSpec((tm, D_MODEL), lambda i: (i, 0)),
                  pl.BlockSpec((1, D_MODEL), lambda i: (0, 0)),
                  pl.BlockSpec((D_MODEL, GATE_LANES), lambda i: (0, 0)),
                  pl.BlockSpec((D_MODEL, GATE_LANES), lambda i: (0, 0)),
                  pl.BlockSpec((1, GATE_LANES), lambda i: (0, 0))],
        out_specs=[pl.BlockSpec((tm, D_MODEL), lambda i: (i, 0)),
                   pl.BlockSpec((tm, GATE_LANES), lambda i: (i, 0))],
        out_shape=[jax.ShapeDtypeStruct((m, D_MODEL), BF16),
                   jax.ShapeDtypeStruct((m, GATE_LANES), F32)],
        compiler_params=_params("arbitrary"),
        name="norm_gates",
    )(x, gain.reshape(1, D_MODEL), whi, wlo, bias)


def _hosted_kernel(*refs, n_in, n_out, phases_fn, lin, rank, rider):
    if rider is not None:
        refs = refs[1:]
    r_in, r_out = (rider.n_in, rider.n_out) if rider is not None else (0, 0)
    ins, refs = refs[:n_in], refs[n_in:]
    r_ins, refs = refs[:r_in], refs[r_in:]
    outs, refs = refs[:n_out], refs[n_out:]
    r_outs, refs = refs[:r_out], refs[r_out:]
    n_sc = len(refs) - (rider.n_scratch if rider is not None else 0)
    scratch, r_scratch = refs[:n_sc], refs[n_sc:]
    sets = [phases_fn(ins, outs, scratch)]
    if rider is not None:
        step = lin(*[pl.program_id(d) for d in range(rank)])
        sets.append(rider.phases(r_ins, r_outs, r_scratch, step))
    for pre, _, _ in sets:
        for cond, fn in pre:
            pl.when(cond)(fn)
    for _, parts, _ in sets:
        for part in parts:
            part()
    for _, _, post in sets:
        for cond, fn in post:
            pl.when(cond)(fn)


def _hosted_call(name, grid, lin, in_specs, out_specs, out_shape, scratch, phases_fn, args, rider=None):
    n_in, n_out = len(in_specs), len(out_specs)
    aliases = {}
    if rider is not None:
        r = rider.specs(lin, len(grid))
        aliases = {1 + n_in + i: n_out + o for i, o in r["aliases"].items()}
        in_specs, out_specs = in_specs + r["in_specs"], out_specs + r["out_specs"]
        out_shape, scratch = out_shape + r["out_shape"], scratch + r["scratch"]
        args = [rider.page_table] + list(args) + r["args"]
    kern = functools.partial(_hosted_kernel, n_in=n_in, n_out=n_out, phases_fn=phases_fn,
                             lin=lin, rank=len(grid), rider=rider)
    grid_spec = pltpu.PrefetchScalarGridSpec(
        num_scalar_prefetch=0 if rider is None else 1, grid=grid,
        in_specs=in_specs, out_specs=out_specs, scratch_shapes=scratch)
    outs = pl.pallas_call(
        kern, grid_spec=grid_spec, out_shape=out_shape, input_output_aliases=aliases,
        compiler_params=_params(*(["arbitrary"] * len(grid))), name=name,
    )(*args)
    return outs[:n_out], outs[n_out:]


def _two_group_steps(n_i, has_second):
    if not has_second:
        return n_i, (lambda i: i)
    return n_i + 1, (lambda i: jnp.maximum(i - 1, 0))


def proj(a, wt, row0, n_out, out_dtype, tm, tn, scale=None, second=None):
    m, k = a.shape
    n_i = m // tm
    n_steps, tile = _two_group_steps(n_i, second is not None)

    def phases(ins, outs, scratch):
        a_ref, wt_ref = ins[:2]
        o_ref, (wbf,) = outs[0], scratch
        i = pl.program_id(1)

        def cast():
            wbf[...] = wt_ref[...].astype(BF16)

        def main():
            y = _dot_nt(a_ref[...], wbf[...])
            o_ref[...] = (y if scale is None else y * scale).astype(o_ref.dtype)

        if second is None:
            return [(i == 0, cast)], [main], []

        def small():
            outs[1][...] = _dot_nt(ins[2][...], wbf[...]).astype(outs[1].dtype)

        return [(i == 0, cast), (i == 0, small), (i > 0, main)], [], []

    in_specs = [pl.BlockSpec((tm, k), lambda j, i, *_: (tile(i), 0)),
                pl.BlockSpec((pl.Element(tn), pl.Element(k)),
                             lambda j, i, *_: (pl.multiple_of(row0 + j * tn, 8), 0))]
    out_specs = [pl.BlockSpec((tm, tn), lambda j, i, *_: (tile(i), j))]
    out_shape = [jax.ShapeDtypeStruct((m, n_out), out_dtype)]
    args = [a, wt]
    if second is not None:
        a2, dtype2 = second
        in_specs.append(pl.BlockSpec(a2.shape, lambda j, i, *_: (0, 0)))
        out_specs.append(pl.BlockSpec((a2.shape[0], tn), lambda j, i, *_: (0, j)))
        out_shape.append(jax.ShapeDtypeStruct((a2.shape[0], n_out), dtype2))
        args.append(a2)
    outs, _ = _hosted_call("proj", (n_out // tn, n_steps), lambda j, i: j * n_steps + i,
                           in_specs, out_specs, out_shape, [pltpu.VMEM((tn, k), BF16)], phases, args)
    return outs[0] if second is None else tuple(outs)


def kv_proj(a, wt, row0, tm, second=None):
    m, k = a.shape
    n_i = m // tm
    n_steps, tile = _two_group_steps(n_i, second is not None)

    def phases(ins, outs, scratch):
        a_ref, wt_ref = ins[:2]
        (wbf,) = scratch
        i = pl.program_id(0)

        def cast():
            wbf[...] = wt_ref[...].astype(BF16)

        def rows_to(a_rows_ref, o_ref, obf_ref):
            def run():
                n_rows = a_rows_ref.shape[0]
                y = _dot_nt(a_rows_ref[...], wbf[...])
                obf_ref[...] = y.astype(BF16)
                for h in range(B_HEADS):
                    o_ref[pl.ds(h, n_rows, stride=B_HEADS), :] = y[:, h * B_DH:(h + 1) * B_DH]
            return run

        main = rows_to(a_ref, outs[0], outs[1])
        if second is None:
            return [(i == 0, cast)], [main], []
        return [(i == 0, cast), (i == 0, rows_to(ins[2], outs[2], outs[3])), (i > 0, main)], [], []

    def out_pair(rows, index):
        return ([pl.BlockSpec((rows * B_HEADS, B_DH), index), pl.BlockSpec((rows, B_WIDTH), index)],
                lambda total: [jax.ShapeDtypeStruct((total * B_HEADS, B_DH), F32),
                               jax.ShapeDtypeStruct((total, B_WIDTH), BF16)])

    in_specs = [pl.BlockSpec((tm, k), lambda i, *_: (tile(i), 0)),
                pl.BlockSpec((pl.Element(B_WIDTH), pl.Element(k)), lambda i, *_: (row0, 0))]
    out_specs, shapes = out_pair(tm, lambda i, *_: (tile(i), 0))
    out_shape = shapes(m)
    args = [a, wt]
    if second is not None:
        m2 = second.shape[0]
        in_specs.append(pl.BlockSpec(second.shape, lambda i, *_: (0, 0)))
        specs2, shapes2 = out_pair(m2, lambda i, *_: (0, 0))
        out_specs, out_shape = out_specs + specs2, out_shape + shapes2(m2)
        args.append(second)
    outs, _ = _hosted_call("kv_proj", (n_steps,), lambda i: i, in_specs, out_specs, out_shape,
                           [pltpu.VMEM((B_WIDTH, k), BF16)], phases, args)
    return tuple(outs)


def out_proj_norm(lhs, ws, x, gain, tm, emit_x, norm_dtype, second=None, spatial=None):
    m = x.shape[0]
    n_lhs, n_w = len(lhs), len(ws)
    n_i = m // tm
    n_steps, tile = _two_group_steps(n_i, second is not None)
    n_out = 2 if emit_x else 1

    def phases(ins, outs, scratch):
        w_refs = ins[n_lhs:n_lhs + n_w]
        x_ref, g_ref = ins[n_lhs + n_w], ins[n_lhs + n_w + 1]
        rest = ins[n_lhs + n_w + 2:]
        i = pl.program_id(0)

        def group(make_lhs, a_refs, x_ref, out_refs, row_chunk):
            def run():
                if make_lhs is not None:
                    make_lhs()
                for r in range(0, x_ref.shape[0], row_chunk):
                    rows = slice(r, r + row_chunk)
                    y = x_ref[rows, :]
                    for a_ref, w_ref in zip(a_refs, w_refs):
                        y = y + _dot(a_ref[rows, :].astype(BF16), w_ref[...])
                    if emit_x:
                        out_refs[0][rows, :] = y
                    out_refs[-1][rows, :] = _rms(y, g_ref[...]).astype(out_refs[-1].dtype)
            return run

        if second is not None:
            a2_refs, x2_ref, rest = rest[:n_w], rest[n_w], rest[n_w + 1:]
        if spatial is None:
            main = group(None, ins[:n_lhs], x_ref, outs[:n_out], min(tm, 256))
        else:
            (y_sc,) = scratch
            fill = functools.partial(_spatial_kernel, *rest[:5], y_sc, chunk=C_CHUNK, n_chunks=tm // C_CHUNK)
            main = group(fill, [y_sc], x_ref, outs[:n_out], min(tm, 256))
        if second is None:
            return [], [main], []
        small = group(None, a2_refs, x2_ref, outs[n_out:], x2_ref.shape[0])
        return [(i == 0, small), (i > 0, main)], [], []

    row_spec = lambda width: pl.BlockSpec((tm, width), lambda i, *_: (tile(i), 0))
    whole = lambda arr: pl.BlockSpec(arr.shape, lambda i, *_: (0,) * arr.ndim)
    gain2d = gain.reshape(1, D_MODEL)
    in_specs = ([row_spec(a.shape[1]) for a in lhs] + [whole(w) for w in ws]
                + [row_spec(D_MODEL), whole(gain2d)])
    out_specs = [row_spec(D_MODEL)] * n_out
    out_shape = ([jax.ShapeDtypeStruct((m, D_MODEL), F32)] if emit_x else []) \
        + [jax.ShapeDtypeStruct((m, D_MODEL), norm_dtype)]
    args = [*lhs, *ws, x, gain2d]
    scratch = []
    if second is not None:
        lhs2, x2, norm_dtype2 = second
        m2 = x2.shape[0]
        in_specs += [whole(a) for a in lhs2] + [whole(x2)]
        out_specs += [pl.BlockSpec((m2, D_MODEL), lambda i, *_: (0, 0))] * n_out
        out_shape += ([jax.ShapeDtypeStruct((m2, D_MODEL), F32)] if emit_x else []) \
            + [jax.ShapeDtypeStruct((m2, D_MODEL), norm_dtype2)]
        args += [*lhs2, x2]
    if spatial is not None:
        u, v, gate, w_s, b_s_t = spatial
        in_specs += [row_spec(C_WIDTH)] * 3 + [whole(w_s), whole(b_s_t)]
        args += [u, v, gate, w_s, b_s_t]
        scratch = [pltpu.VMEM((tm, C_WIDTH), BF16)]
    outs, _ = _hosted_call("out_proj_norm", (n_steps,), lambda i: i, in_specs, out_specs, out_shape,
                           scratch, phases, args)
    return list(outs)


def _mlstm_chunk(q_ref, k_ref, v_ref, og_ref, gg_ref, gcol_ref, grow_ref, h_ref, c_sc, n_sc, m_sc, chunk, heads):
    L = chunk
    gcol = gcol_ref[0]
    grow = grow_ref[0]
    tt = lax.broadcasted_iota(jnp.int32, (L, L), 0)
    ss = lax.broadcasted_iota(jnp.int32, (L, L), 1)
    causal = ss <= tt

    for head in heads:
        cols = slice(head * A_DH, (head + 1) * A_DH)
        q = q_ref[0, :, cols]
        ks = k_ref[0, :, cols] * jnp.asarray(A_DH ** -0.5, BF16)
        v = v_ref[0, :, cols]
        ig_col = gcol[:, head:head + 1]
        lf_col = gcol[:, head + A_HEADS:head + A_HEADS + 1]
        ig_row = grow[head:head + 1, :]
        lf_row = grow[head + A_HEADS:head + A_HEADS + 1, :]
        b_col = jnp.sum(jnp.where(causal, lf_row, 0.0), axis=1, keepdims=True)
        b_row = jnp.sum(jnp.where(tt <= ss, lf_col, 0.0), axis=0, keepdims=True)
        b_last = jnp.sum(lf_row, axis=1, keepdims=True)

        m0 = m_sc[head][:, :1]
        n0 = n_sc[head]
        c0 = c_sc[head]

        d = jnp.where(causal, b_col - b_row + ig_row, NEG_BIG)
        m_carry = b_col + m0
        m = jnp.maximum(m_carry, jnp.max(d, axis=1, keepdims=True))
        w_intra = jnp.exp(d - m)
        w_carry = jnp.exp(m_carry - m)
        s = _dot_nt(q, ks) * w_intra
        qf = q.astype(F32)
        num = _dot(s.astype(BF16), v) + w_carry * _dot_nt(q, c0.astype(BF16))
        den = jnp.sum(s, axis=1, keepdims=True) + w_carry * jnp.sum(qf * n0, axis=1, keepdims=True)
        h = num / jnp.maximum(jnp.abs(den), jnp.exp(-m))
        gated = h * _sigmoid(og_ref[0, :, cols].astype(F32)) * _silu(gg_ref[0, :, cols].astype(F32))
        h_ref[0, :, cols] = gated.astype(h_ref.dtype)

        m_carry_last = b_last + m0
        d_last_row = b_last - b_row + ig_row
        m_new = jnp.maximum(m_carry_last, jnp.max(d_last_row, axis=1, keepdims=True))
        wc_last = jnp.exp(m_carry_last - m_new)
        w_last_col = jnp.exp(b_last - b_col + ig_col - m_new)
        vw = (v.astype(F32) * w_last_col).astype(BF16)
        c_new = wc_last * c0 + _dot_tn(vw, ks)
        n_new = wc_last * n0 + jnp.sum(ks.astype(F32) * w_last_col, axis=0, keepdims=True)
        c_sc[head] = c_new
        n_sc[head] = n_new
        m_sc[head] = jnp.broadcast_to(m_new, (1, GATE_LANES))


def mlstm(qkvog, gcol, grow, c0, n0, m0, chunk, rider=None):
    bsz, t_len, _ = qkvog.shape
    nc = t_len // chunk
    hd = A_HEADS

    def phases(ins, outs, state):
        c0_ref, n0_ref, m0_ref = ins[7:]
        h_ref, c_out_ref, n_out_ref, m_out_ref = outs
        c_sc, n_sc, m_sc = state
        ci = pl.program_id(1)

        def init():
            c_sc[...] = c0_ref[0]
            n_sc[...] = n0_ref[0]
            m_sc[...] = m0_ref[0]

        def head_part(head):
            return lambda: _mlstm_chunk(*ins[:7], h_ref, c_sc, n_sc, m_sc, chunk, (head,))

        parts = [head_part(head) for head in range(A_HEADS)]

        def final():
            c_out_ref[0] = c_sc[...]
            n_out_ref[0] = n_sc[...]
            m_out_ref[0] = m_sc[...]

        return [(ci == 0, init)], parts, [(ci == nc - 1, final)]

    blk = lambda seg: pl.BlockSpec((1, chunk, A_WIDTH), lambda b, c, *_, seg=seg: (b, c, seg))
    st4 = lambda r, w: pl.BlockSpec((1, hd, r, w), lambda b, c, *_: (b, 0, 0, 0))
    outs, rest = _hosted_call(
        "mlstm", (bsz, nc), lambda b, c: b * nc + c,
        [blk(0), blk(1), blk(2), blk(3), blk(4),
         pl.BlockSpec((1, chunk, GATE_LANES), lambda b, c, *_: (b, c, 0)),
         pl.BlockSpec((1, 8, chunk), lambda b, c, *_: (b, 0, c)),
         st4(A_DH, A_DH), st4(1, A_DH), st4(1, GATE_LANES)],
        [pl.BlockSpec((1, chunk, A_WIDTH), lambda b, c, *_: (b, c, 0)),
         st4(A_DH, A_DH), st4(1, A_DH), st4(1, GATE_LANES)],
        [jax.ShapeDtypeStruct((bsz, t_len, A_WIDTH), BF16),
         jax.ShapeDtypeStruct((bsz, hd, A_DH, A_DH), F32),
         jax.ShapeDtypeStruct((bsz, hd, 1, A_DH), F32),
         jax.ShapeDtypeStruct((bsz, hd, 1, GATE_LANES), F32)],
        [pltpu.VMEM((hd, A_DH, A_DH), F32), pltpu.VMEM((hd, 1, A_DH), F32),
         pltpu.VMEM((hd, 1, GATE_LANES), F32)],
        phases, [qkvog, qkvog, qkvog, qkvog, qkvog, gcol, grow, c0, n0, m0], rider)
    return tuple(outs) if rider is None else (tuple(outs), rest)


def _stick_block(q, kb, vb, bias, run, mask, upper):
    rows = q.shape[0]
    sub = upper.shape[0]
    n_sub = kb.shape[0] // sub
    z = _dot_nt(q, kb) if bias is None else _dot_nt(q, kb) * (B_DH ** -0.5) + bias
    sp = _softplus(z)
    spm = sp if mask is None else jnp.where(mask, sp, 0.0)
    hi, lo = _split_hi_lo(spm)
    laters = [None] * n_sub
    total = None
    for i in reversed(range(n_sub)):
        ln = slice(i * sub, (i + 1) * sub)
        both = _dot(jnp.concatenate([hi[:, ln], lo[:, ln]], axis=0), upper)
        carry = run if total is None else run + total
        laters[i] = both[:rows] + both[rows:] + carry
        part = jnp.sum(spm[:, ln], axis=1, keepdims=True)
        total = part if total is None else total + part
    later = laters[0] if n_sub == 1 else jnp.concatenate(laters, axis=1)
    a = jnp.exp(z - sp - later)
    if mask is not None:
        a = jnp.where(mask, a, 0.0)
    return _dot(a.astype(BF16), vb), total


def _strict_upper(n):
    j = lax.broadcasted_iota(jnp.int32, (n, n), 0)
    s = lax.broadcasted_iota(jnp.int32, (n, n), 1)
    return jnp.where(j > s, 1.0, 0.0).astype(BF16)


def _attn_prompt_kernel(bias_ref, q_ref, k_ref, v_ref, g_ref, o_ref, *, bq, bk, n_heads):
    head0 = pl.program_id(1) * n_heads
    qi = pl.program_id(2)
    kbf = k_ref.at[0]
    vbf = v_ref.at[0]
    upper = _strict_upper(min(bk, CUMSUM_BLOCK))
    row = lax.broadcasted_iota(jnp.int32, (bq, bk), 0)
    col = lax.broadcasted_iota(jnp.int32, (bq, bk), 1)
    lanes = [slice(h * B_DH, (h + 1) * B_DH) for h in range(n_heads)]
    lane_q = lax.broadcasted_iota(jnp.int32, (bq, B_DH), 1)
    lane_k = lax.broadcasted_iota(jnp.int32, (bk, B_DH), 1)
    ones_cols = jnp.where(lane_q < 2, 1.0, 0.0).astype(BF16)
    qs = [jnp.concatenate([q_ref[0, :, ln], ones_cols], axis=1) for ln in lanes]
    bias_cols = []
    for h in range(n_heads):
        b = jnp.full((bk, B_DH), bias_ref[head0 + h], F32)
        b_hi = b.astype(BF16).astype(F32)
        bias_cols.append(jnp.where(lane_k == 0, b_hi, jnp.where(lane_k == 1, b - b_hi, 0.0)).astype(BF16))

    def blocks(kj, runs, mask):
        start = pl.multiple_of(kj * bk, bk)
        return [_stick_block(qs[h], jnp.concatenate([kbf[pl.ds(start, bk), ln], bias_cols[h]], axis=1),
                             vbf[pl.ds(start, bk), ln], None, runs[h], mask, upper)
                for h, ln in enumerate(lanes)]

    q0 = qi * bq
    n_full = q0 // bk
    accs = [jnp.zeros((bq, B_DH), F32)] * n_heads
    runs = [jnp.zeros((bq, 1), F32)] * n_heads
    for m in reversed(range(max(1, bq // bk))):
        kj = n_full + m
        res = blocks(kj, runs, col + (kj * bk - q0) < row)
        accs = [a + c for a, (c, _) in zip(accs, res)]
        runs = [r + t for r, (_, t) in zip(runs, res)]

    def body(it, carry):
        accs, runs = carry
        res = blocks(n_full - 1 - it, runs, None)
        return (tuple(a + c for a, (c, _) in zip(accs, res)),
                tuple(r + t for r, (_, t) in zip(runs, res)))

    accs, runs = lax.fori_loop(0, n_full, body, (tuple(accs), tuple(runs)))
    for h, ln in enumerate(lanes):
        o_ref[0, :, ln] = (accs[h] * _silu(g_ref[0, :, ln].astype(F32))).astype(o_ref.dtype)


def attn_prompt(q, g, k, v, b_sb, bq, bk, n_heads):
    bsz, t_len, _ = k.shape
    width = n_heads * B_DH
    q_spec = pl.BlockSpec((1, bq, width), lambda b, h, i: (b, i, h))
    kv_spec = pl.BlockSpec((1, t_len, width), lambda b, h, i: (b, 0, h))
    return pl.pallas_call(
        functools.partial(_attn_prompt_kernel, bq=bq, bk=bk, n_heads=n_heads),
        grid=(bsz, B_HEADS // n_heads, t_len // bq),
        in_specs=[pl.BlockSpec(memory_space=pltpu.SMEM), q_spec, kv_spec, kv_spec, q_spec],
        out_specs=q_spec,
        out_shape=jax.ShapeDtypeStruct((bsz, t_len, B_WIDTH), BF16),
        compiler_params=_params("arbitrary", "arbitrary", "arbitrary"),
        name="attn_prompt",
    )(b_sb, q, k, v, g)


def _attn_sample_phases(ins, outs, scratch, n_group, n_parts):
    bias_ref, q_ref, knew_ref, vnew_ref, acc_in_ref, run_in_ref = ins[:6]
    k_refs = ins[6:6 + n_group]
    v_refs = ins[6 + n_group:]
    acc_out_ref, run_out_ref = outs
    qbd, acc, run, kcat, vcat = scratch
    rows = B_HEADS * SAMPLE_PAD
    upper = _strict_upper(PAGE_SIZE)
    bias = bias_ref[...][:, :1]

    def repack(page, dst, i):
        for h in range(B_HEADS):
            dst[i * PAGE_SIZE:(i + 1) * PAGE_SIZE, h * B_DH:(h + 1) * B_DH] = page(h).astype(BF16)

    def step(slot0, n_blk, mask):
        keys = slice(slot0 * PAGE_SIZE, (slot0 + n_blk) * PAGE_SIZE)
        z = _dot_nt(qbd[...], kcat[keys, :]) * (B_DH ** -0.5) + bias
        sp = _softplus(z)
        spm = sp if mask is None else jnp.where(mask, sp, 0.0)
        hi, lo = _split_hi_lo(spm)
        carry = run[...][:, :1]
        laters = []
        for i in range(n_blk):
            ln = slice(i * PAGE_SIZE, (i + 1) * PAGE_SIZE)
            both = _dot(jnp.concatenate([hi[:, ln], lo[:, ln]], axis=0), upper)
            laters.append(both[:rows] + both[rows:] + carry)
            carry = carry + jnp.sum(spm[:, ln], axis=1, keepdims=True)
        later = laters[0] if n_blk == 1 else jnp.concatenate(laters, axis=1)
        a = jnp.exp(z - sp - later)
        if mask is not None:
            a = jnp.where(mask, a, 0.0)
        acc[...] += _dot(a.astype(BF16), vcat[keys, :])
        run[...] = jnp.broadcast_to(carry, run.shape)

    def build_queries():
        r = lax.broadcasted_iota(jnp.int32, (rows, B_WIDTH), 0)
        c = lax.broadcasted_iota(jnp.int32, (rows, B_WIDTH), 1)
        q_rep = jnp.concatenate([q_ref[0].astype(F32)] * B_HEADS, axis=0)
        qbd[...] = jnp.where((r // SAMPLE_PAD) == (c // B_DH), q_rep, 0.0).astype(BF16)

    def start_sequence():
        build_queries()
        acc[...] = jnp.zeros_like(acc)
        run[...] = jnp.zeros_like(run)
        repack(lambda h: knew_ref[0, pl.ds(h, PAGE_SIZE, stride=B_HEADS), :], kcat, 0)
        repack(lambda h: vnew_ref[0, pl.ds(h, PAGE_SIZE, stride=B_HEADS), :], vcat, 0)
        t = lax.broadcasted_iota(jnp.int32, (rows, PAGE_SIZE), 0) % SAMPLE_PAD
        s = lax.broadcasted_iota(jnp.int32, (rows, PAGE_SIZE), 1)
        step(0, 1, s < t)

    def resume_sequence():
        build_queries()
        acc[...] = acc_in_ref[0]
        run[...] = run_in_ref[0]

    def pages_part(slot0, n_blk, last):
        def run_part():
            for i in range(slot0, slot0 + n_blk):
                repack(lambda h, r=k_refs[i]: r[0, 0, pl.ds(h, PAGE_SIZE, stride=B_HEADS), :], kcat, i)
                repack(lambda h, r=v_refs[i]: r[0, 0, pl.ds(h, PAGE_SIZE, stride=B_HEADS), :], vcat, i)
            step(slot0, n_blk, None)
            if last:
                acc_out_ref[0] = acc[...]
                run_out_ref[0] = run[...]
        return run_part

    per_part = n_group // n_parts
    parts = [pages_part(k * per_part, per_part, k == n_parts - 1) for k in range(n_parts)]
    return start_sequence, resume_sequence, parts


class _PagedAttnRider:
    n_out = 2
    n_scratch = 5

    def __init__(self, q, k_new, v_new, cache_k, cache_v, page_table, bias_rows, acc, run, first_step, n_group):
        self.arrays = (bias_rows, q, k_new, v_new, acc, run)
        self.caches = (cache_k, cache_v)
        self.page_table = page_table
        self.first_step = first_step
        self.n_group = n_group
        self.n_in = 6 + 2 * n_group
        self.steps_per_seq = page_table.shape[1] // n_group

    def specs(self, lin, rank):
        n_group, spq = self.n_group, self.steps_per_seq
        n_pages = self.page_table.shape[1]
        rows = B_HEADS * SAMPLE_PAD
        page_rows = PAGE_SIZE * B_HEADS
        gstep = lambda a: self.first_step + lin(*a[:rank])
        seq_map = lambda *a: (gstep(a) // spq, 0, 0)

        def page_spec(i):
            def index(*a):
                g, pt = gstep(a), a[rank]
                return (0, pt[g // spq, n_pages - 1 - ((g % spq) * n_group + i)], 0, 0)
            return pl.BlockSpec((1, 1, page_rows, B_DH), index)

        acc_spec = pl.BlockSpec((1, rows, B_WIDTH), seq_map)
        run_spec = pl.BlockSpec((1, rows, GATE_LANES), seq_map)
        new_spec = pl.BlockSpec((1, page_rows, B_DH), seq_map)
        acc, run = self.arrays[4:]
        return dict(
            in_specs=[pl.BlockSpec((rows, GATE_LANES), lambda *a: (0, 0)),
                      pl.BlockSpec((1, SAMPLE_PAD, B_WIDTH), seq_map), new_spec, new_spec,
                      acc_spec, run_spec] + [page_spec(i) for i in range(n_group)] * 2,
            out_specs=[acc_spec, run_spec],
            out_shape=[jax.ShapeDtypeStruct(acc.shape, F32), jax.ShapeDtypeStruct(run.shape, F32)],
            scratch=[pltpu.VMEM((rows, B_WIDTH), BF16), pltpu.VMEM((rows, B_WIDTH), F32),
                     pltpu.VMEM((rows, GATE_LANES), F32),
                     pltpu.VMEM((n_group * PAGE_SIZE, B_WIDTH), BF16),
                     pltpu.VMEM((n_group * PAGE_SIZE, B_WIDTH), BF16)],
            args=list(self.arrays) + [self.caches[0]] * n_group + [self.caches[1]] * n_group,
            aliases={4: 0, 5: 1},
        )

    def phases(self, ins, outs, scratch, local_step):
        start, resume, parts = _attn_sample_phases(ins, outs, scratch, self.n_group, RIDER_PARTS)
        p = (self.first_step + local_step) % self.steps_per_seq
        return [(p == 0, start), ((local_step == 0) & (p != 0), resume)], parts, []


def attn_sample_finish(acc, g):
    n_seq = acc.shape[0]

    def kern(acc_ref, g_ref, o_ref):
        a = acc_ref[0]
        c = lax.broadcasted_iota(jnp.int32, (SAMPLE_PAD, B_WIDTH), 1) // B_DH
        out = jnp.zeros((SAMPLE_PAD, B_WIDTH), F32)
        for h in range(B_HEADS):
            out = out + jnp.where(c == h, a[h * SAMPLE_PAD:(h + 1) * SAMPLE_PAD, :], 0.0)
        o_ref[0] = out * _silu(g_ref[0])

    return pl.pallas_call(
        kern, grid=(n_seq,),
        in_specs=[pl.BlockSpec((1, B_HEADS * SAMPLE_PAD, B_WIDTH), lambda s: (s, 0, 0)),
                  pl.BlockSpec((1, SAMPLE_PAD, B_WIDTH), lambda s: (s, 0, 0))],
        out_specs=pl.BlockSpec((1, SAMPLE_PAD, B_WIDTH), lambda s: (s, 0, 0)),
        out_shape=jax.ShapeDtypeStruct((n_seq, SAMPLE_PAD, B_WIDTH), F32),
        compiler_params=_params("arbitrary"), name="attn_sample_finish",
    )(acc, g)


def proj_act(h, w, col0, v_gain, act, tm, tn, out_dtype, second=None):
    m = h.shape[0]
    j0 = col0 // tn
    n_col = C_WIDTH // tn
    n_i = m // tm
    n_steps, tile = _two_group_steps(n_i, second is not None)

    def phases(ins, outs, scratch):
        a_ref, w_ref, vg_ref = ins[:3]
        (wbf,) = scratch
        i = pl.program_id(1)

        def cast():
            wbf[...] = w_ref[...].astype(BF16)

        def group(rows_ref, o_ref, row_chunk):
            def run():
                for r in range(0, rows_ref.shape[0], row_chunk):
                    rows = slice(r, r + row_chunk)
                    y = _dot(rows_ref[rows, :], wbf[...])
                    if act == "gelu":
                        y = _gelu_tanh(y)
                    elif act == "silu":
                        y = _silu(y)
                    else:
                        y = _rms(_gelu_tanh(y), vg_ref[...])
                    o_ref[rows, :] = y.astype(o_ref.dtype)
            return run

        main = group(a_ref, outs[0], min(tm, 256))
        if second is None:
            return [(i == 0, cast)], [main], []
        small = group(ins[3], outs[1], ins[3].shape[0])
        return [(i == 0, cast), (i == 0, small), (i > 0, main)], [], []

    w_mode = dict(pipeline_mode=pl.Buffered(1)) if n_col == 1 else {}
    in_specs = [pl.BlockSpec((tm, D_MODEL), lambda j, i, *_: (tile(i), 0)),
                pl.BlockSpec((D_MODEL, tn), lambda j, i, *_: (0, j0 + j), **w_mode),
                pl.BlockSpec((1, tn), lambda j, i, *_: (0, j))]
    out_specs = [pl.BlockSpec((tm, tn), lambda j, i, *_: (tile(i), j))]
    out_shape = [jax.ShapeDtypeStruct((m, C_WIDTH), out_dtype)]
    args = [h, w, v_gain.reshape(1, C_WIDTH)]
    if second is not None:
        h2, dtype2 = second
        in_specs.append(pl.BlockSpec(h2.shape, lambda j, i, *_: (0, 0)))
        out_specs.append(pl.BlockSpec((h2.shape[0], tn), lambda j, i, *_: (0, j)))
        out_shape.append(jax.ShapeDtypeStruct((h2.shape[0], C_WIDTH), dtype2))
        args.append(h2)
    outs, _ = _hosted_call("proj_" + act, (n_col, n_steps), lambda j, i: j * n_steps + i,
                           in_specs, out_specs, out_shape, [pltpu.VMEM((D_MODEL, tn), BF16)], phases, args)
    return outs[0] if second is None else tuple(outs)


def odd_in(h, w, v_gain, tm, act_dtype, second=None):
    u = proj_act(h, w, 0, v_gain, "gelu", tm, 1024, act_dtype, second)
    v = proj_act(h, w, C_WIDTH, v_gain, "gelu_rms", min(tm, 512), C_WIDTH, act_dtype, second)
    g = proj_act(h, w, 2 * C_WIDTH, v_gain, "silu", tm, 1024, act_dtype, second)
    if second is None:
        return u, v, g
    return (u[0], v[0], g[0]), (u[1], v[1], g[1])


def _spatial_kernel(u_ref, v_ref, g_ref, ws_ref, bs_ref, y_ref, *, chunk, n_chunks):
    tt = lax.broadcasted_iota(jnp.int32, (chunk, chunk), 0)
    ss = lax.broadcasted_iota(jnp.int32, (chunk, chunk), 1)
    causal = ss <= tt
    for grp in range(C_GROUPS):
        wm = jnp.where(causal, ws_ref[grp], 0.0)
        bcol = bs_ref[:, grp:grp + 1]
        cols = slice(grp * C_GDIM, (grp + 1) * C_GDIM)
        for c in range(n_chunks):
            rows = slice(c * chunk, (c + 1) * chunk)
            vv = v_ref[rows, cols]
            if chunk >= 128:
                sv = _dot(wm.astype(BF16), vv)
            else:
                vf = vv.astype(F32)
                sv = jnp.zeros((chunk, C_GDIM), F32)
                for s in range(chunk):
                    sv = sv + wm[:, s:s + 1] * vf[s:s + 1, :]
            sv = sv + bcol
            y = u_ref[rows, cols].astype(F32) * sv * g_ref[rows, cols].astype(F32)
            y_ref[rows, cols] = y.astype(y_ref.dtype)


def spatial_gate(u, v, g, w_s, b_s_t, chunk, n_chunks):
    m = u.shape[0]
    tm = chunk * n_chunks
    row_spec = pl.BlockSpec((tm, C_WIDTH), lambda i: (i, 0))
    return pl.pallas_call(
        functools.partial(_spatial_kernel, chunk=chunk, n_chunks=n_chunks),
        grid=(m // tm,),
        in_specs=[row_spec, row_spec, row_spec,
                  pl.BlockSpec((C_GROUPS, chunk, chunk), lambda i: (0, 0, 0)),
                  pl.BlockSpec((chunk, C_GROUPS), lambda i: (0, 0))],
        out_specs=row_spec,
        out_shape=jax.ShapeDtypeStruct((m, C_WIDTH), u.dtype),
        compiler_params=_params("arbitrary"),
        name="spatial_gate",
    )(u, v, g, w_s, b_s_t)


def _even_weights(w_in, b_i, b_f):
    gate0 = 5 * A_WIDTH
    b0 = gate0 + 2 * A_HEADS
    wt = jnp.swapaxes(w_in, 0, 1)
    w_gate = jnp.pad(w_in[:, gate0:b0], ((0, 0), (0, GATE_LANES - 2 * A_HEADS)))
    bias = jnp.pad(jnp.concatenate([b_i, b_f]), (0, GATE_LANES - 2 * A_HEADS)).reshape(1, GATE_LANES)
    return wt, b0, w_gate, bias


def _mlstm_inputs(qkvog, gates, bsz, t_len, valid_len):
    qkvog = qkvog.reshape(bsz, t_len, 5 * A_WIDTH)
    gates = gates.reshape(bsz, t_len, GATE_LANES)
    t_pad = -(-t_len // A_CHUNK) * A_CHUNK
    pad = ((0, 0), (0, t_pad - t_len), (0, 0))
    if valid_len < t_pad:
        qkvog, gates = jnp.pad(qkvog, pad), jnp.pad(gates, pad)
        pos = jnp.arange(t_pad)[None, :, None]
        lane = jnp.arange(GATE_LANES)[None, None, :]
        gates = jnp.where((pos >= valid_len) & (lane < A_HEADS), NEG_BIG, gates)
        gates = jnp.where((pos >= valid_len) & (lane >= A_HEADS), 0.0, gates)
    return qkvog, gates, gates[:, :, :2 * A_HEADS].transpose(0, 2, 1)


def _even_front(xp, xs, ew, g_norm, tm):
    wt, b0, w_gate, bias = ew
    hp, gates_p = norm_gates(xp, g_norm, w_gate, bias, min(tm, 512))
    hs, gates_s = norm_gates(xs, g_norm, w_gate, bias, xs.shape[0])
    tn, tm_kv = 1024, tm
    qkvog = proj(hp, wt, 0, 5 * A_WIDTH, BF16, 2 * tm, tn, second=(hs, BF16))
    q_b = proj(hp, wt, b0, B_WIDTH, BF16, 2 * tm, tn, scale=B_DH ** -0.5, second=(hs, F32))
    g_b = proj(hp, wt, b0 + 3 * B_WIDTH, B_WIDTH, BF16, 2 * tm, tn, second=(hs, F32))
    k = kv_proj(hp, wt, b0 + B_WIDTH, tm_kv, second=hs)
    v = kv_proj(hp, wt, b0 + 2 * B_WIDTH, tm_kv, second=hs)
    groups = []
    for i, gates in enumerate((gates_p, gates_s)):
        groups.append(dict(qkvog=qkvog[i], gates=gates, q_b=q_b[i], g_b=g_b[i],
                           k_new=k[2 * i], k_bf=k[2 * i + 1], v_new=v[2 * i], v_bf=v[2 * i + 1]))
    return groups


def _even_back(xp, xs, mix_p, mix_s, w_out, next_gain, tm):
    w_out_b = w_out.astype(BF16)
    flat = lambda mix, m: [mix[0].reshape(m, A_WIDTH), mix[1].reshape(m, B_WIDTH)]
    return out_proj_norm(flat(mix_p, xp.shape[0]), [w_out_b[:A_WIDTH], w_out_b[A_WIDTH:]], xp, next_gain,
                         tm, True, BF16, second=(flat(mix_s, xs.shape[0]), xs, BF16))


def _odd_layer(xp, hp, xs, hs, w_in, v_gain, w_s, b_s, w_out_b, final_gain, tm, n_seq):
    (u, v, g), (u2, v2, g2) = odd_in(hp, w_in, v_gain, 4 * tm, BF16, second=(hs, F32))
    y2 = spatial_gate(u2, v2, g2, w_s[:, :SAMPLE_PAD, :SAMPLE_PAD], b_s[:, :SAMPLE_PAD].T, SAMPLE_PAD, n_seq)
    y_p, y_s = out_proj_norm([], [w_out_b], xp, final_gain, tm, False, F32, second=([y2], xs, F32),
                             spatial=(u, v, g, w_s, b_s.T))
    return y_p, y_s, v2


def kernel(x_prompt, x_sample, state_a_C, state_a_n, state_a_m, cache_b_k, cache_b_v, page_table,
           even_norm, even_w_in, even_b_i, even_b_f, even_b_sb, even_w_out,
           odd_norm, odd_w_in, odd_v_gain, odd_w_s, odd_b_s, odd_w_out, final_norm):
    bsz, seq, _ = x_prompt.shape
    n_seq, dec_seq, _ = x_sample.shape
    n_pool = cache_b_k.shape[1]

    ew = _even_weights(even_w_in[0], even_b_i[0], even_b_f[0])
    odd_w_in_b = odd_w_in[0]
    odd_w_out_b = odd_w_out[0].astype(BF16)

    xp = x_prompt.reshape(bsz * seq, D_MODEL)
    xs = jnp.pad(x_sample, ((0, 0), (0, SAMPLE_PAD - dec_seq), (0, 0))).reshape(n_seq * SAMPLE_PAD, D_MODEL)
    fp, fs = _even_front(xp, xs, ew, even_norm[0], 1024)
    qkvog_p, gates_p, grow_p = _mlstm_inputs(fp["qkvog"], fp["gates"], bsz, seq, seq)
    qkvog_s, gates_s, grow_s = _mlstm_inputs(fs["qkvog"], fs["gates"], n_seq, SAMPLE_PAD, dec_seq)
    q_s, g_s = (fs[name].reshape(n_seq, SAMPLE_PAD, B_WIDTH) for name in ("q_b", "g_b"))

    assert cache_b_k.shape[0] == 1 and cache_b_v.shape[0] == 1
    page_view = (1, n_pool, PAGE_SIZE * B_HEADS, B_DH)
    cache_k, cache_v = cache_b_k.reshape(page_view), cache_b_v.reshape(page_view)
    kv_pad = ((0, 0), (0, (PAGE_SIZE - SAMPLE_PAD) * B_HEADS), (0, 0))
    k_new_s = jnp.pad(fs["k_new"].reshape(n_seq, SAMPLE_PAD * B_HEADS, B_DH), kv_pad)
    v_new_s = jnp.pad(fs["v_new"].reshape(n_seq, SAMPLE_PAD * B_HEADS, B_DH), kv_pad)
    bias_rows = jnp.broadcast_to(jnp.repeat(even_b_sb[0], SAMPLE_PAD)[:, None],
                                 (B_HEADS * SAMPLE_PAD, GATE_LANES))
    att_rows = B_HEADS * SAMPLE_PAD
    rider = _PagedAttnRider(
        q_s, k_new_s, v_new_s, cache_k, cache_v, page_table, bias_rows,
        jnp.zeros((n_seq, att_rows, B_WIDTH), F32), jnp.zeros((n_seq, att_rows, GATE_LANES), F32),
        0, PAGES_PER_STEP)
    assert bsz * (seq // A_CHUNK) == n_seq * rider.steps_per_seq
    zero_state = (jnp.zeros((bsz, A_HEADS, A_DH, A_DH), F32),
                  jnp.zeros((bsz, A_HEADS, 1, A_DH), F32),
                  jnp.zeros((bsz, A_HEADS, 1, GATE_LANES), F32))
    (ha_p, c_p, n_p, m_p), (att_acc, _) = mlstm(
        qkvog_p, gates_p, grow_p, *zero_state, A_CHUNK, rider=rider)
    hb_s = attn_sample_finish(att_acc, g_s)

    as_seq = lambda a: a.reshape(bsz, seq, B_WIDTH)
    hb_p = attn_prompt(as_seq(fp["q_b"]), as_seq(fp["g_b"]), as_seq(fp["k_bf"]), as_seq(fp["v_bf"]),
                       even_b_sb[0], ATTN_BQ, ATTN_BK, ATTN_HEADS_PER_STEP)
    st_in = (state_a_C[0], state_a_n[0][:, :, None, :],
             jnp.broadcast_to(state_a_m[0][:, :, None, None], (n_seq, A_HEADS, 1, GATE_LANES)))
    ha_s, c_s, n_s, m_s_new = mlstm(qkvog_s, gates_s, grow_s, *st_in, A_CHUNK)
    ha_s = ha_s[:, :SAMPLE_PAD]

    xp1, hp1, xs1, hs1 = _even_back(xp, xs, (ha_p, hb_p), (ha_s, hb_s), even_w_out[0], odd_norm[0], 512)
    y_p, y_s, v_rows = _odd_layer(xp1, hp1, xs1, hs1, odd_w_in_b, odd_v_gain[0], odd_w_s[0], odd_b_s[0],
                                  odd_w_out_b, final_norm, 512, n_seq)

    def sample_rows(a, *dims):
        return a.reshape((n_seq, SAMPLE_PAD) + dims)[:, :dec_seq]

    return (y_p.reshape(bsz, seq, D_MODEL),
            sample_rows(y_s, D_MODEL),
            c_p[None], n_p[:, :, 0, :][None], m_p[:, :, 0, 0][None],
            c_s[None], n_s[:, :, 0, :][None], m_s_new[:, :, 0, 0][None],
            fp["k_new"].reshape(1, bsz, seq, B_HEADS, B_DH), fp["v_new"].reshape(1, bsz, seq, B_HEADS, B_DH),
            sample_rows(fs["k_new"], B_HEADS, B_DH)[None], sample_rows(fs["v_new"], B_HEADS, B_DH)[None],
            sample_rows(v_rows, C_WIDTH)[None])
```

```python
import functools

import jax
import jax.numpy as jnp
from jax import lax
from jax.experimental import pallas as pl
from jax.experimental.pallas import tpu as pltpu

F32 = jnp.float32
BF16 = jnp.bfloat16

D_MODEL = 2048
PAGE_SIZE = 128
A_HEADS = 4
A_DH = 256
A_WIDTH = A_HEADS * A_DH
A_CHUNK = 128
B_HEADS = 8
B_DH = 128
B_WIDTH = B_HEADS * B_DH
C_WIDTH = D_MODEL
C_GROUPS = 8
C_GDIM = C_WIDTH // C_GROUPS
C_CHUNK = 128
RMS_EPS = 1e-6
GATE_LANES = 128
NEG_BIG = -1e30
SAMPLE_PAD = 8
ATTN_BQ = 512
ATTN_BK = 512
CUMSUM_BLOCK = 256
ATTN_HEADS_PER_STEP = 2
PAGES_PER_STEP = 16

VMEM_LIMIT_BYTES = 56 * 1024 * 1024


def _params(*sem):
    return pltpu.CompilerParams(dimension_semantics=sem, vmem_limit_bytes=VMEM_LIMIT_BYTES)


def _dot(a, b):
    return jnp.dot(a, b, preferred_element_type=F32)


def _dot_nt(a, b):
    return lax.dot_general(a, b, (((1,), (1,)), ((), ())), preferred_element_type=F32)


def _dot_tn(a, b):
    return lax.dot_general(a, b, (((0,), (0,)), ((), ())), preferred_element_type=F32)


def _softplus(z):
    return jnp.maximum(z, 0.0) + jnp.log(1.0 + jnp.exp(-jnp.abs(z)))


def _sigmoid(z):
    return 1.0 / (1.0 + jnp.exp(-z))


def _silu(z):
    return z * _sigmoid(z)


def _gelu_tanh(x):
    c = 0.7978845608028654
    return x * (0.5 * (1.0 + jnp.tanh(c * (x + 0.044715 * (x * x * x)))))


def _rms(x, g):
    return x * lax.rsqrt(jnp.mean(x * x, axis=-1, keepdims=True) + RMS_EPS) * g


def _split_hi_lo(x):
    hi = x.astype(BF16)
    lo = (x - hi.astype(F32)).astype(BF16)
    return hi, lo


def _norm_gates_kernel(x_ref, g_ref, whi_ref, wlo_ref, bias_ref, h_ref, gate_ref):
    h = _rms(x_ref[...], g_ref[...])
    h_hi, h_lo = _split_hi_lo(h)
    h_ref[...] = h_hi
    pre = (_dot(h_hi, whi_ref[...]) + _dot(h_hi, wlo_ref[...]) + _dot(h_lo, whi_ref[...])
           + bias_ref[...])
    lane = lax.broadcasted_iota(jnp.int32, pre.shape, 1)
    is_forget = (lane >= A_HEADS) & (lane < 2 * A_HEADS)
    gate_ref[...] = jnp.where(is_forget, -_softplus(-pre), pre)


def norm_gates(x, gain, w_gate, bias, tm):
    m = x.shape[0]
    whi, wlo = _split_hi_lo(w_gate)
    return pl.pallas_call(
        _norm_gates_kernel,
        grid=(m // tm,),
        in_specs=[pl.BlockSpec((tm, D_MODEL), lambda i: (i, 0)),
                  pl.BlockSpec((1, D_MODEL), lambda i: (0, 0)),
                  pl.BlockSpec((D_MODEL, GATE_LANES), lambda i: (0, 0)),
                  pl.BlockSpec((D_MODEL, GATE_LANES), lambda i: (0, 0)),
                  pl.BlockSpec((1, GATE_LANES), lambda i: (0, 0))],
        out_specs=[pl.BlockSpec((tm, D_MODEL), lambda i: (i, 0)),
                   pl.BlockSpec((tm, GATE_LANES), lambda i: (i, 0))],
        out_shape=[jax.ShapeDtypeStruct((m, D_MODEL), BF16),
                   jax.ShapeDtypeStruct((m, GATE_LANES), F32)],
        compiler_params=_params("arbitrary"),
        name="norm_gates",
    )(x, gain.reshape(1, D_MODEL), whi, wlo, bias)


def _hosted_kernel(*refs, n_in, n_out, phases_fn, lin, rank, rider):
    if rider is not None:
        refs = refs[1:]
    r_in, r_out = (rider.n_in, rider.n_out) if rider is not None else (0, 0)
    ins, refs = refs[:n_in], refs[n_in:]
    r_ins, refs = refs[:r_in], refs[r_in:]
    outs, refs = refs[:n_out], refs[n_out:]
    r_outs, refs = refs[:r_out], refs[r_out:]
    n_sc = len(refs) - (rider.n_scratch if rider is not None else 0)
    scratch, r_scratch = refs[:n_sc], refs[n_sc:]
    sets = [phases_fn(ins, outs, scratch)]
    if rider is not None:
        step = lin(*[pl.program_id(d) for d in range(rank)])
        sets.append(rider.phases(r_ins, r_outs, r_scratch, step))
    for pre, _, _ in sets:
        for cond, fn in pre:
            pl.when(cond)(fn)
    for _, parts, _ in sets:
        for part in parts:
            part()
    for _, _, post in sets:
        for cond, fn in post:
            pl.when(cond)(fn)


def _hosted_call(name, grid, lin, in_specs, out_specs, out_shape, scratch, phases_fn, args, rider=None):
    n_in, n_out = len(in_specs), len(out_specs)
    aliases = {}
    if rider is not None:
        r = rider.specs(lin, len(grid))
        aliases = {1 + n_in + i: n_out + o for i, o in r["aliases"].items()}
        in_specs, out_specs = in_specs + r["in_specs"], out_specs + r["out_specs"]
        out_shape, scratch = out_shape + r["out_shape"], scratch + r["scratch"]
        args = [rider.page_table] + list(args) + r["args"]
    kern = functools.partial(_hosted_kernel, n_in=n_in, n_out=n_out, phases_fn=phases_fn,
                             lin=lin, rank=len(grid), rider=rider)
    grid_spec = pltpu.PrefetchScalarGridSpec(
        num_scalar_prefetch=0 if rider is None else 1, grid=grid,
        in_specs=in_specs, out_specs=out_specs, scratch_shapes=scratch)
    outs = pl.pallas_call(
        kern, grid_spec=grid_spec, out_shape=out_shape, input_output_aliases=aliases,
        compiler_params=_params(*(["arbitrary"] * len(grid))), name=name,
    )(*args)
    return outs[:n_out], outs[n_out:]


def _two_group_steps(n_i, has_second):
    if not has_second:
        return n_i, (lambda i: i)
    return n_i + 1, (lambda i: jnp.maximum(i - 1, 0))


def proj(a, wt, row0, n_out, out_dtype, tm, tn, scale=None, second=None):
    m, k = a.shape
    n_i = m // tm
    n_steps, tile = _two_group_steps(n_i, second is not None)

    def phases(ins, outs, scratch):
        a_ref, wt_ref = ins[:2]
        o_ref, (wbf,) = outs[0], scratch
        i = pl.program_id(1)

        def cast():
            wbf[...] = wt_ref[...].astype(BF16)

        def main():
            y = _dot_nt(a_ref[...], wbf[...])
            o_ref[...] = (y if scale is None else y * scale).astype(o_ref.dtype)

        if second is None:
            return [(i == 0, cast)], [main], []

        def small():
            outs[1][...] = _dot_nt(ins[2][...], wbf[...]).astype(outs[1].dtype)

        return [(i == 0, cast), (i == 0, small), (i > 0, main)], [], []

    in_specs = [pl.BlockSpec((tm, k), lambda j, i, *_: (tile(i), 0)),
                pl.BlockSpec((pl.Element(tn), pl.Element(k)),
                             lambda j, i, *_: (pl.multiple_of(row0 + j * tn, 8), 0))]
    out_specs = [pl.BlockSpec((tm, tn), lambda j, i, *_: (tile(i), j))]
    out_shape = [jax.ShapeDtypeStruct((m, n_out), out_dtype)]
    args = [a, wt]
    if second is not None:
        a2, dtype2 = second
        in_specs.append(pl.BlockSpec(a2.shape, lambda j, i, *_: (0, 0)))
        out_specs.append(pl.BlockSpec((a2.shape[0], tn), lambda j, i, *_: (0, j)))
        out_shape.append(jax.ShapeDtypeStruct((a2.shape[0], n_out), dtype2))
        args.append(a2)
    outs, _ = _hosted_call("proj", (n_out // tn, n_steps), lambda j, i: j * n_steps + i,
                           in_specs, out_specs, out_shape, [pltpu.VMEM((tn, k), BF16)], phases, args)
    return outs[0] if second is None else tuple(outs)


def kv_proj(a, wt, row0, tm, second=None):
    m, k = a.shape
    n_i = m // tm
    n_steps, tile = _two_group_steps(n_i, second is not None)

    def phases(ins, outs, scratch):
        a_ref, wt_ref = ins[:2]
        (wbf,) = scratch
        i = pl.program_id(0)

        def cast():
            wbf[...] = wt_ref[...].astype(BF16)

        def rows_to(a_rows_ref, o_ref, obf_ref):
            def run():
                n_rows = a_rows_ref.shape[0]
                y = _dot_nt(a_rows_ref[...], wbf[...])
                obf_ref[...] = y.astype(BF16)
                for h in range(B_HEADS):
                    o_ref[pl.ds(h, n_rows, stride=B_HEADS), :] = y[:, h * B_DH:(h + 1) * B_DH]
            return run

        main = rows_to(a_ref, outs[0], outs[1])
        if second is None:
            return [(i == 0, cast)], [main], []
        return [(i == 0, cast), (i == 0, rows_to(ins[2], outs[2], outs[3])), (i > 0, main)], [], []

    def out_pair(rows, index):
        return ([pl.BlockSpec((rows * B_HEADS, B_DH), index), pl.BlockSpec((rows, B_WIDTH), index)],
                lambda total: [jax.ShapeDtypeStruct((total * B_HEADS, B_DH), F32),
                               jax.ShapeDtypeStruct((total, B_WIDTH), BF16)])

    in_specs = [pl.BlockSpec((tm, k), lambda i, *_: (tile(i), 0)),
                pl.BlockSpec((pl.Element(B_WIDTH), pl.Element(k)), lambda i, *_: (row0, 0))]
    out_specs, shapes = out_pair(tm, lambda i, *_: (tile(i), 0))
    out_shape = shapes(m)
    args = [a, wt]
    if second is not None:
        m2 = second.shape[0]
        in_specs.append(pl.BlockSpec(second.shape, lambda i, *_: (0, 0)))
        specs2, shapes2 = out_pair(m2, lambda i, *_: (0, 0))
        out_specs, out_shape = out_specs + specs2, out_shape + shapes2(m2)
        args.append(second)
    outs, _ = _hosted_call("kv_proj", (n_steps,), lambda i: i, in_specs, out_specs, out_shape,
                           [pltpu.VMEM((B_WIDTH, k), BF16)], phases, args)
    return tuple(outs)


def out_proj_norm(lhs, ws, x, gain, tm, emit_x, norm_dtype, second=None, spatial=None):
    m = x.shape[0]
    n_lhs, n_w = len(lhs), len(ws)
    n_i = m // tm
    n_steps, tile = _two_group_steps(n_i, second is not None)
    n_out = 2 if emit_x else 1

    def phases(ins, outs, scratch):
        w_refs = ins[n_lhs:n_lhs + n_w]
        x_ref, g_ref = ins[n_lhs + n_w], ins[n_lhs + n_w + 1]
        rest = ins[n_lhs + n_w + 2:]
        i = pl.program_id(0)

        def group(make_lhs, a_refs, x_ref, out_refs, row_chunk):
            def run():
                if make_lhs is not None:
                    make_lhs()
                for r in range(0, x_ref.shape[0], row_chunk):
                    rows = slice(r, r + row_chunk)
                    y = x_ref[rows, :]
                    for a_ref, w_ref in zip(a_refs, w_refs):
                        y = y + _dot(a_ref[rows, :].astype(BF16), w_ref[...])
                    if emit_x:
                        out_refs[0][rows, :] = y
                    out_refs[-1][rows, :] = _rms(y, g_ref[...]).astype(out_refs[-1].dtype)
            return run

        if second is not None:
            a2_refs, x2_ref, rest = rest[:n_w], rest[n_w], rest[n_w + 1:]
        if spatial is None:
            main = group(None, ins[:n_lhs], x_ref, outs[:n_out], min(tm, 256))
        else:
            (y_sc,) = scratch
            fill = functools.partial(_spatial_kernel, *rest[:5], y_sc, chunk=C_CHUNK, n_chunks=tm // C_CHUNK)
            main = group(fill, [y_sc], x_ref, outs[:n_out], min(tm, 256))
        if second is None:
            return [], [main], []
        small = group(None, a2_refs, x2_ref, outs[n_out:], x2_ref.shape[0])
        return [(i == 0, small), (i > 0, main)], [], []

    row_spec = lambda width: pl.BlockSpec((tm, width), lambda i, *_: (tile(i), 0))
    whole = lambda arr: pl.BlockSpec(arr.shape, lambda i, *_: (0,) * arr.ndim)
    gain2d = gain.reshape(1, D_MODEL)
    in_specs = ([row_spec(a.shape[1]) for a in lhs] + [whole(w) for w in ws]
                + [row_spec(D_MODEL), whole(gain2d)])
    out_specs = [row_spec(D_MODEL)] * n_out
    out_shape = ([jax.ShapeDtypeStruct((m, D_MODEL), F32)] if emit_x else []) \
        + [jax.ShapeDtypeStruct((m, D_MODEL), norm_dtype)]
    args = [*lhs, *ws, x, gain2d]
    scratch = []
    if second is not None:
        lhs2, x2, norm_dtype2 = second
        m2 = x2.shape[0]
        in_specs += [whole(a) for a in lhs2] + [whole(x2)]
        out_specs += [pl.BlockSpec((m2, D_MODEL), lambda i, *_: (0, 0))] * n_out
        out_shape += ([jax.ShapeDtypeStruct((m2, D_MODEL), F32)] if emit_x else []) \
            + [jax.ShapeDtypeStruct((m2, D_MODEL), norm_dtype2)]
        args += [*lhs2, x2]
    if spatial is not None:
        u, v, gate, w_s, b_s_t = spatial
        in_specs += [row_spec(C_WIDTH)] * 3 + [whole(w_s), whole(b_s_t)]
        args += [u, v, gate, w_s, b_s_t]
        scratch = [pltpu.VMEM((tm, C_WIDTH), BF16)]
    outs, _ = _hosted_call("out_proj_norm", (n_steps,), lambda i: i, in_specs, out_specs, out_shape,
                           scratch, phases, args)
    return list(outs)


def _mlstm_chunk(q_ref, k_ref, v_ref, og_ref, gg_ref, gcol_ref, grow_ref, h_ref, c_sc, n_sc, m_sc, chunk, heads):
    L = chunk
    gcol = gcol_ref[0]
    grow = grow_ref[0]
    tt = lax.broadcasted_iota(jnp.int32, (L, L), 0)
    ss = lax.broadcasted_iota(jnp.int32, (L, L), 1)
    causal = ss <= tt

    for head in heads:
        cols = slice(head * A_DH, (head + 1) * A_DH)
        q = q_ref[0, :, cols]
        ks = k_ref[0, :, cols] * jnp.asarray(A_DH ** -0.5, BF16)
        v = v_ref[0, :, cols]
        ig_col = gcol[:, head:head + 1]
        lf_col = gcol[:, head + A_HEADS:head + A_HEADS + 1]
        ig_row = grow[head:head + 1, :]
        lf_row = grow[head + A_HEADS:head + A_HEADS + 1, :]
        b_col = jnp.sum(jnp.where(causal, lf_row, 0.0), axis=1, keepdims=True)
        b_row = jnp.sum(jnp.where(tt <= ss, lf_col, 0.0), axis=0, keepdims=True)
        b_last = jnp.sum(lf_row, axis=1, keepdims=True)

        m0 = m_sc[head][:, :1]
        n0 = n_sc[head]
        c0 = c_sc[head]

        d = jnp.where(causal, b_col - b_row + ig_row, NEG_BIG)
        m_carry = b_col + m0
        m = jnp.maximum(m_carry, jnp.max(d, axis=1, keepdims=True))
        w_intra = jnp.exp(d - m)
        w_carry = jnp.exp(m_carry - m)
        s = _dot_nt(q, ks) * w_intra
        qf = q.astype(F32)
        num = _dot(s.astype(BF16), v) + w_carry * _dot_nt(q, c0.astype(BF16))
        den = jnp.sum(s, axis=1, keepdims=True) + w_carry * jnp.sum(qf * n0, axis=1, keepdims=True)
        h = num / jnp.maximum(jnp.abs(den), jnp.exp(-m))
        gated = h * _sigmoid(og_ref[0, :, cols].astype(F32)) * _silu(gg_ref[0, :, cols].astype(F32))
        h_ref[0, :, cols] = gated.astype(h_ref.dtype)

        m_carry_last = b_last + m0
        d_last_row = b_last - b_row + ig_row
        m_new = jnp.maximum(m_carry_last, jnp.max(d_last_row, axis=1, keepdims=True))
        wc_last = jnp.exp(m_carry_last - m_new)
        w_last_col = jnp.exp(b_last - b_col + ig_col - m_new)
        vw = (v.astype(F32) * w_last_col).astype(BF16)
        c_new = wc_last * c0 + _dot_tn(vw, ks)
        n_new = wc_last * n0 + jnp.sum(ks.astype(F32) * w_last_col, axis=0, keepdims=True)
        c_sc[head] = c_new
        n_sc[head] = n_new
        m_sc[head] = jnp.broadcast_to(m_new, (1, GATE_LANES))


def mlstm(qkvog, gcol, grow, c0, n0, m0, chunk, rider=None):
    bsz, t_len, _ = qkvog.shape
    nc = t_len // chunk
    hd = A_HEADS

    def phases(ins, outs, state):
        c0_ref, n0_ref, m0_ref = ins[7:]
        h_ref, c_out_ref, n_out_ref, m_out_ref = outs
        c_sc, n_sc, m_sc = state
        ci = pl.program_id(1)

        def init():
            c_sc[...] = c0_ref[0]
            n_sc[...] = n0_ref[0]
            m_sc[...] = m0_ref[0]

        def head_part(head):
            return lambda: _mlstm_chunk(*ins[:7], h_ref, c_sc, n_sc, m_sc, chunk, (head,))

        parts = [head_part(head) for head in range(A_HEADS)]

        def final():
            c_out_ref[0] = c_sc[...]
            n_out_ref[0] = n_sc[...]
            m_out_ref[0] = m_sc[...]

        return [(ci == 0, init)], parts, [(ci == nc - 1, final)]

    blk = lambda seg: pl.BlockSpec((1, chunk, A_WIDTH), lambda b, c, *_, seg=seg: (b, c, seg))
    st4 = lambda r, w: pl.BlockSpec((1, hd, r, w), lambda b, c, *_: (b, 0, 0, 0))
    outs, rest = _hosted_call(
        "mlstm", (bsz, nc), lambda b, c: b * nc + c,
        [blk(0), blk(1), blk(2), blk(3), blk(4),
         pl.BlockSpec((1, chunk, GATE_LANES), lambda b, c, *_: (b, c, 0)),
         pl.BlockSpec((1, 8, chunk), lambda b, c, *_: (b, 0, c)),
         st4(A_DH, A_DH), st4(1, A_DH), st4(1, GATE_LANES)],
        [pl.BlockSpec((1, chunk, A_WIDTH), lambda b, c, *_: (b, c, 0)),
         st4(A_DH, A_DH), st4(1, A_DH), st4(1, GATE_LANES)],
        [jax.ShapeDtypeStruct((bsz, t_len, A_WIDTH), BF16),
         jax.ShapeDtypeStruct((bsz, hd, A_DH, A_DH), F32),
         jax.ShapeDtypeStruct((bsz, hd, 1, A_DH), F32),
         jax.ShapeDtypeStruct((bsz, hd, 1, GATE_LANES), F32)],
        [pltpu.VMEM((hd, A_DH, A_DH), F32), pltpu.VMEM((hd, 1, A_DH), F32),
         pltpu.VMEM((hd, 1, GATE_LANES), F32)],
        phases, [qkvog, qkvog, qkvog, qkvog, qkvog, gcol, grow, c0, n0, m0], rider)
    return tuple(outs) if rider is None else (tuple(outs), rest)


def _stick_block(q, kb, vb, bias, run, mask, upper):
    rows = q.shape[0]
    sub = upper.shape[0]
    n_sub = kb.shape[0] // sub
    z = _dot_nt(q, kb) if bias is None else _dot_nt(q, kb) * (B_DH ** -0.5) + bias
    sp = _softplus(z)
    spm = sp if mask is None else jnp.where(mask, sp, 0.0)
    hi, lo = _split_hi_lo(spm)
    laters = [None] * n_sub
    total = None
    for i in reversed(range(n_sub)):
        ln = slice(i * sub, (i + 1) * sub)
        both = _dot(jnp.concatenate([hi[:, ln], lo[:, ln]], axis=0), upper)
        carry = run if total is None else run + total
        laters[i] = both[:rows] + both[rows:] + carry
        part = jnp.sum(spm[:, ln], axis=1, keepdims=True)
        total = part if total is None else total + part
    later = laters[0] if n_sub == 1 else jnp.concatenate(laters, axis=1)
    a = jnp.exp(z - sp - later)
    if mask is not None:
        a = jnp.where(mask, a, 0.0)
    return _dot(a.astype(BF16), vb), total


def _strict_upper(n):
    j = lax.broadcasted_iota(jnp.int32, (n, n), 0)
    s = lax.broadcasted_iota(jnp.int32, (n, n), 1)
    return jnp.where(j > s, 1.0, 0.0).astype(BF16)


def _attn_prompt_kernel(bias_ref, q_ref, k_ref, v_ref, g_ref, o_ref, *, bq, bk, n_heads):
    head0 = pl.program_id(1) * n_heads
    qi = pl.program_id(2)
    kbf = k_ref.at[0]
    vbf = v_ref.at[0]
    upper = _strict_upper(min(bk, CUMSUM_BLOCK))
    row = lax.broadcasted_iota(jnp.int32, (bq, bk), 0)
    col = lax.broadcasted_iota(jnp.int32, (bq, bk), 1)
    lanes = [slice(h * B_DH, (h + 1) * B_DH) for h in range(n_heads)]
    lane_q = lax.broadcasted_iota(jnp.int32, (bq, B_DH), 1)
    lane_k = lax.broadcasted_iota(jnp.int32, (bk, B_DH), 1)
    ones_cols = jnp.where(lane_q < 2, 1.0, 0.0).astype(BF16)
    qs = [jnp.concatenate([q_ref[0, :, ln], ones_cols], axis=1) for ln in lanes]
    bias_cols = []
    for h in range(n_heads):
        b = jnp.full((bk, B_DH), bias_ref[head0 + h], F32)
        b_hi = b.astype(BF16).astype(F32)
        bias_cols.append(jnp.where(lane_k == 0, b_hi, jnp.where(lane_k == 1, b - b_hi, 0.0)).astype(BF16))

    def blocks(kj, runs, mask):
        start = pl.multiple_of(kj * bk, bk)
        return [_stick_block(qs[h], jnp.concatenate([kbf[pl.ds(start, bk), ln], bias_cols[h]], axis=1),
                             vbf[pl.ds(start, bk), ln], None, runs[h], mask, upper)
                for h, ln in enumerate(lanes)]

    q0 = qi * bq
    n_full = q0 // bk
    if bq == bk:
        half = bq // 2
        start = pl.multiple_of(qi * bq, bq)
        accs, runs = [], []
        for h, ln in enumerate(lanes):
            kb = jnp.concatenate([kbf[pl.ds(start, bk), ln], bias_cols[h]], axis=1)
            vb = vbf[pl.ds(start, bk), ln]
            zero = jnp.zeros((half, 1), F32)
            iota = lambda shape, dim: lax.broadcasted_iota(jnp.int32, shape, dim)
            top = _stick_block(qs[h][:half], kb[:half], vb[:half], None, zero,
                               iota((half, half), 1) < iota((half, half), 0), upper)
            bot = _stick_block(qs[h][half:], kb, vb, None, zero,
                               iota((half, bk), 1) < iota((half, bk), 0) + half, upper)
            accs.append(jnp.concatenate([top[0], bot[0]], axis=0))
            runs.append(jnp.concatenate([top[1], bot[1]], axis=0))
    else:
        accs = [jnp.zeros((bq, B_DH), F32)] * n_heads
        runs = [jnp.zeros((bq, 1), F32)] * n_heads
        for m in reversed(range(max(1, bq // bk))):
            kj = n_full + m
            res = blocks(kj, runs, col + (kj * bk - q0) < row)
            accs = [a + c for a, (c, _) in zip(accs, res)]
            runs = [r + t for r, (_, t) in zip(runs, res)]

    def body(it, carry):
        accs, runs = carry
        res = blocks(n_full - 1 - it, runs, None)
        return (tuple(a + c for a, (c, _) in zip(accs, res)),
                tuple(r + t for r, (_, t) in zip(runs, res)))

    accs, runs = lax.fori_loop(0, n_full, body, (tuple(accs), tuple(runs)))
    for h, ln in enumerate(lanes):
        o_ref[0, :, ln] = (accs[h] * _silu(g_ref[0, :, ln].astype(F32))).astype(o_ref.dtype)


def attn_prompt(q, g, k, v, b_sb, bq, bk, n_heads):
    bsz, t_len, _ = k.shape
    width = n_heads * B_DH
    q_spec = pl.BlockSpec((1, bq, width), lambda b, h, i: (b, i, h))
    kv_spec = pl.BlockSpec((1, t_len, width), lambda b, h, i: (b, 0, h))
    return pl.pallas_call(
        functools.partial(_attn_prompt_kernel, bq=bq, bk=bk, n_heads=n_heads),
        grid=(bsz, B_HEADS // n_heads, t_len // bq),
        in_specs=[pl.BlockSpec(memory_space=pltpu.SMEM), q_spec, kv_spec, kv_spec, q_spec],
        out_specs=q_spec,
        out_shape=jax.ShapeDtypeStruct((bsz, t_len, B_WIDTH), BF16),
        compiler_params=_params("arbitrary", "arbitrary", "arbitrary"),
        name="attn_prompt",
    )(b_sb, q, k, v, g)


def _attn_sample_phases(ins, outs, scratch, n_group):
    bias_ref, q_ref, knew_ref, vnew_ref, acc_in_ref, run_in_ref = ins[:6]
    k_refs = ins[6:6 + n_group]
    v_refs = ins[6 + n_group:]
    acc_out_ref, run_out_ref = outs
    qbd, acc, run, kcat, vcat = scratch
    rows = B_HEADS * SAMPLE_PAD
    upper = _strict_upper(PAGE_SIZE)
    bias = bias_ref[...][:, :1]

    def repack(page, dst, i):
        for h in range(B_HEADS):
            dst[i * PAGE_SIZE:(i + 1) * PAGE_SIZE, h * B_DH:(h + 1) * B_DH] = page(h).astype(BF16)

    def step(slot0, n_blk, mask):
        keys = slice(slot0 * PAGE_SIZE, (slot0 + n_blk) * PAGE_SIZE)
        z = _dot_nt(qbd[...], kcat[keys, :]) * (B_DH ** -0.5) + bias
        sp = _softplus(z)
        spm = sp if mask is None else jnp.where(mask, sp, 0.0)
        hi, lo = _split_hi_lo(spm)
        carry = run[...][:, :1]
        laters = []
        for i in range(n_blk):
            ln = slice(i * PAGE_SIZE, (i + 1) * PAGE_SIZE)
            both = _dot(jnp.concatenate([hi[:, ln], lo[:, ln]], axis=0), upper)
            laters.append(both[:rows] + both[rows:] + carry)
            carry = carry + jnp.sum(spm[:, ln], axis=1, keepdims=True)
        later = laters[0] if n_blk == 1 else jnp.concatenate(laters, axis=1)
        a = jnp.exp(z - sp - later)
        if mask is not None:
            a = jnp.where(mask, a, 0.0)
        acc[...] += _dot(a.astype(BF16), vcat[keys, :])
        run[...] = jnp.broadcast_to(carry, run.shape)

    def build_queries():
        r = lax.broadcasted_iota(jnp.int32, (rows, B_WIDTH), 0)
        c = lax.broadcasted_iota(jnp.int32, (rows, B_WIDTH), 1)
        q_rep = jnp.concatenate([q_ref[0].astype(F32)] * B_HEADS, axis=0)
        qbd[...] = jnp.where((r // SAMPLE_PAD) == (c // B_DH), q_rep, 0.0).astype(BF16)

    def start_sequence():
        build_queries()
        acc[...] = jnp.zeros_like(acc)
        run[...] = jnp.zeros_like(run)
        repack(lambda h: knew_ref[0, pl.ds(h, PAGE_SIZE, stride=B_HEADS), :], kcat, 0)
        repack(lambda h: vnew_ref[0, pl.ds(h, PAGE_SIZE, stride=B_HEADS), :], vcat, 0)
        t = lax.broadcasted_iota(jnp.int32, (rows, PAGE_SIZE), 0) % SAMPLE_PAD
        s = lax.broadcasted_iota(jnp.int32, (rows, PAGE_SIZE), 1)
        step(0, 1, s < t)

    def resume_sequence():
        build_queries()
        acc[...] = acc_in_ref[0]
        run[...] = run_in_ref[0]

    def pages():
        for i in range(n_group):
            repack(lambda h, r=k_refs[i]: r[0, 0, pl.ds(h, PAGE_SIZE, stride=B_HEADS), :], kcat, i)
            repack(lambda h, r=v_refs[i]: r[0, 0, pl.ds(h, PAGE_SIZE, stride=B_HEADS), :], vcat, i)
        step(0, n_group, None)
        acc_out_ref[0] = acc[...]
        run_out_ref[0] = run[...]

    return start_sequence, resume_sequence, pages


class _PagedAttnRider:
    n_out = 2
    n_scratch = 5

    def __init__(self, q, k_new, v_new, cache_k, cache_v, page_table, bias_rows, acc, run, first_step, n_group):
        self.arrays = (bias_rows, q, k_new, v_new, acc, run)
        self.caches = (cache_k, cache_v)
        self.page_table = page_table
        self.first_step = first_step
        self.n_group = n_group
        self.n_in = 6 + 2 * n_group
        self.steps_per_seq = page_table.shape[1] // n_group

    def specs(self, lin, rank):
        n_group, spq = self.n_group, self.steps_per_seq
        n_pages = self.page_table.shape[1]
        rows = B_HEADS * SAMPLE_PAD
        page_rows = PAGE_SIZE * B_HEADS
        gstep = lambda a: self.first_step + lin(*a[:rank])
        seq_map = lambda *a: (gstep(a) // spq, 0, 0)

        def page_spec(i):
            def index(*a):
                g, pt = gstep(a), a[rank]
                return (0, pt[g // spq, n_pages - 1 - ((g % spq) * n_group + i)], 0, 0)
            return pl.BlockSpec((1, 1, page_rows, B_DH), index)

        acc_spec = pl.BlockSpec((1, rows, B_WIDTH), seq_map)
        run_spec = pl.BlockSpec((1, rows, GATE_LANES), seq_map)
        new_spec = pl.BlockSpec((1, page_rows, B_DH), seq_map)
        acc, run = self.arrays[4:]
        return dict(
            in_specs=[pl.BlockSpec((rows, GATE_LANES), lambda *a: (0, 0)),
                      pl.BlockSpec((1, SAMPLE_PAD, B_WIDTH), seq_map), new_spec, new_spec,
                      acc_spec, run_spec] + [page_spec(i) for i in range(n_group)] * 2,
            out_specs=[acc_spec, run_spec],
            out_shape=[jax.ShapeDtypeStruct(acc.shape, F32), jax.ShapeDtypeStruct(run.shape, F32)],
            scratch=[pltpu.VMEM((rows, B_WIDTH), BF16), pltpu.VMEM((rows, B_WIDTH), F32),
                     pltpu.VMEM((rows, GATE_LANES), F32),
                     pltpu.VMEM((n_group * PAGE_SIZE, B_WIDTH), BF16),
                     pltpu.VMEM((n_group * PAGE_SIZE, B_WIDTH), BF16)],
            args=list(self.arrays) + [self.caches[0]] * n_group + [self.caches[1]] * n_group,
            aliases={4: 0, 5: 1},
        )

    def phases(self, ins, outs, scratch, local_step):
        start, resume, pages = _attn_sample_phases(ins, outs, scratch, self.n_group)
        p = (self.first_step + local_step) % self.steps_per_seq
        return [(p == 0, start), ((local_step == 0) & (p != 0), resume)], [pages], []


def attn_sample_finish(acc, g):
    n_seq = acc.shape[0]

    def kern(acc_ref, g_ref, o_ref):
        a = acc_ref[0]
        c = lax.broadcasted_iota(jnp.int32, (SAMPLE_PAD, B_WIDTH), 1) // B_DH
        out = jnp.zeros((SAMPLE_PAD, B_WIDTH), F32)
        for h in range(B_HEADS):
            out = out + jnp.where(c == h, a[h * SAMPLE_PAD:(h + 1) * SAMPLE_PAD, :], 0.0)
        o_ref[0] = out * _silu(g_ref[0])

    return pl.pallas_call(
        kern, grid=(n_seq,),
        in_specs=[pl.BlockSpec((1, B_HEADS * SAMPLE_PAD, B_WIDTH), lambda s: (s, 0, 0)),
                  pl.BlockSpec((1, SAMPLE_PAD, B_WIDTH), lambda s: (s, 0, 0))],
        out_specs=pl.BlockSpec((1, SAMPLE_PAD, B_WIDTH), lambda s: (s, 0, 0)),
        out_shape=jax.ShapeDtypeStruct((n_seq, SAMPLE_PAD, B_WIDTH), F32),
        compiler_params=_params("arbitrary"), name="attn_sample_finish",
    )(acc, g)


def proj_act(h, w, col0, v_gain, act, tm, tn, out_dtype, second=None):
    m = h.shape[0]
    j0 = col0 // tn
    n_col = C_WIDTH // tn
    n_i = m // tm
    n_steps, tile = _two_group_steps(n_i, second is not None)

    def phases(ins, outs, scratch):
        a_ref, w_ref, vg_ref = ins[:3]
        (wbf,) = scratch
        i = pl.program_id(1)

        def cast():
            wbf[...] = w_ref[...].astype(BF16)

        def group(rows_ref, o_ref, row_chunk):
            def run():
                for r in range(0, rows_ref.shape[0], row_chunk):
                    rows = slice(r, r + row_chunk)
                    y = _dot(rows_ref[rows, :], wbf[...])
                    if act == "gelu":
                        y = _gelu_tanh(y)
                    elif act == "silu":
                        y = _silu(y)
                    else:
                        y = _rms(_gelu_tanh(y), vg_ref[...])
                    o_ref[rows, :] = y.astype(o_ref.dtype)
            return run

        main = group(a_ref, outs[0], min(tm, 256))
        if second is None:
            return [(i == 0, cast)], [main], []
        small = group(ins[3], outs[1], ins[3].shape[0])
        return [(i == 0, cast), (i == 0, small), (i > 0, main)], [], []

    w_mode = dict(pipeline_mode=pl.Buffered(1)) if n_col == 1 else {}
    in_specs = [pl.BlockSpec((tm, D_MODEL), lambda j, i, *_: (tile(i), 0)),
                pl.BlockSpec((D_MODEL, tn), lambda j, i, *_: (0, j0 + j), **w_mode),
                pl.BlockSpec((1, tn), lambda j, i, *_: (0, j))]
    out_specs = [pl.BlockSpec((tm, tn), lambda j, i, *_: (tile(i), j))]
    out_shape = [jax.ShapeDtypeStruct((m, C_WIDTH), out_dtype)]
    args = [h, w, v_gain.reshape(1, C_WIDTH)]
    if second is not None:
        h2, dtype2 = second
        in_specs.append(pl.BlockSpec(h2.shape, lambda j, i, *_: (0, 0)))
        out_specs.append(pl.BlockSpec((h2.shape[0], tn), lambda j, i, *_: (0, j)))
        out_shape.append(jax.ShapeDtypeStruct((h2.shape[0], C_WIDTH), dtype2))
        args.append(h2)
    outs, _ = _hosted_call("proj_" + act, (n_col, n_steps), lambda j, i: j * n_steps + i,
                           in_specs, out_specs, out_shape, [pltpu.VMEM((D_MODEL, tn), BF16)], phases, args)
    return outs[0] if second is None else tuple(outs)


def odd_in(h, w, v_gain, tm, act_dtype, second=None):
    u = proj_act(h, w, 0, v_gain, "gelu", tm, 1024, act_dtype, second)
    v = proj_act(h, w, C_WIDTH, v_gain, "gelu_rms", min(tm, 512), C_WIDTH, act_dtype, second)
    g = proj_act(h, w, 2 * C_WIDTH, v_gain, "silu", tm, 1024, act_dtype, second)
    if second is None:
        return u, v, g
    return (u[0], v[0], g[0]), (u[1], v[1], g[1])


def _spatial_kernel(u_ref, v_ref, g_ref, ws_ref, bs_ref, y_ref, *, chunk, n_chunks):
    tt = lax.broadcasted_iota(jnp.int32, (chunk, chunk), 0)
    ss = lax.broadcasted_iota(jnp.int32, (chunk, chunk), 1)
    causal = ss <= tt
    for grp in range(C_GROUPS):
        wm = jnp.where(causal, ws_ref[grp], 0.0)
        bcol = bs_ref[:, grp:grp + 1]
        cols = slice(grp * C_GDIM, (grp + 1) * C_GDIM)
        for c in range(n_chunks):
            rows = slice(c * chunk, (c + 1) * chunk)
            vv = v_ref[rows, cols]
            if chunk >= 128:
                sv = _dot(wm.astype(BF16), vv)
            else:
                vf = vv.astype(F32)
                sv = jnp.zeros((chunk, C_GDIM), F32)
                for s in range(chunk):
                    sv = sv + wm[:, s:s + 1] * vf[s:s + 1, :]
            sv = sv + bcol
            y = u_ref[rows, cols].astype(F32) * sv * g_ref[rows, cols].astype(F32)
            y_ref[rows, cols] = y.astype(y_ref.dtype)


def spatial_gate(u, v, g, w_s, b_s_t, chunk, n_chunks):
    m = u.shape[0]
    tm = chunk * n_chunks
    row_spec = pl.BlockSpec((tm, C_WIDTH), lambda i: (i, 0))
    return pl.pallas_call(
        functools.partial(_spatial_kernel, chunk=chunk, n_chunks=n_chunks),
        grid=(m // tm,),
        in_specs=[row_spec, row_spec, row_spec,
                  pl.BlockSpec((C_GROUPS, chunk, chunk), lambda i: (0, 0, 0)),
                  pl.BlockSpec((chunk, C_GROUPS), lambda i: (0, 0))],
        out_specs=row_spec,
        out_shape=jax.ShapeDtypeStruct((m, C_WIDTH), u.dtype),
        compiler_params=_params("arbitrary"),
        name="spatial_gate",
    )(u, v, g, w_s, b_s_t)


def _even_weights(w_in, b_i, b_f):
    gate0 = 5 * A_WIDTH
    b0 = gate0 + 2 * A_HEADS
    wt = jnp.swapaxes(w_in, 0, 1)
    w_gate = jnp.pad(w_in[:, gate0:b0], ((0, 0), (0, GATE_LANES - 2 * A_HEADS)))
    bias = jnp.pad(jnp.concatenate([b_i, b_f]), (0, GATE_LANES - 2 * A_HEADS)).reshape(1, GATE_LANES)
    return wt, b0, w_gate, bias


def _mlstm_inputs(qkvog, gates, bsz, t_len, valid_len):
    qkvog = qkvog.reshape(bsz, t_len, 5 * A_WIDTH)
    gates = gates.reshape(bsz, t_len, GATE_LANES)
    t_pad = -(-t_len // A_CHUNK) * A_CHUNK
    pad = ((0, 0), (0, t_pad - t_len), (0, 0))
    if valid_len < t_pad:
        qkvog, gates = jnp.pad(qkvog, pad), jnp.pad(gates, pad)
        pos = jnp.arange(t_pad)[None, :, None]
        lane = jnp.arange(GATE_LANES)[None, None, :]
        gates = jnp.where((pos >= valid_len) & (lane < A_HEADS), NEG_BIG, gates)
        gates = jnp.where((pos >= valid_len) & (lane >= A_HEADS), 0.0, gates)
    return qkvog, gates, gates[:, :, :2 * A_HEADS].transpose(0, 2, 1)


def _even_front(xp, xs, ew, g_norm, tm):
    wt, b0, w_gate, bias = ew
    hp, gates_p = norm_gates(xp, g_norm, w_gate, bias, min(tm, 512))
    hs, gates_s = norm_gates(xs, g_norm, w_gate, bias, xs.shape[0])
    tn, tm_kv = 1024, tm
    qkvog = proj(hp, wt, 0, 5 * A_WIDTH, BF16, 2 * tm, tn, second=(hs, BF16))
    q_b = proj(hp, wt, b0, B_WIDTH, BF16, 2 * tm, tn, scale=B_DH ** -0.5, second=(hs, F32))
    g_b = proj(hp, wt, b0 + 3 * B_WIDTH, B_WIDTH, BF16, 2 * tm, tn, second=(hs, F32))
    k = kv_proj(hp, wt, b0 + B_WIDTH, tm_kv, second=hs)
    v = kv_proj(hp, wt, b0 + 2 * B_WIDTH, tm_kv, second=hs)
    groups = []
    for i, gates in enumerate((gates_p, gates_s)):
        groups.append(dict(qkvog=qkvog[i], gates=gates, q_b=q_b[i], g_b=g_b[i],
                           k_new=k[2 * i], k_bf=k[2 * i + 1], v_new=v[2 * i], v_bf=v[2 * i + 1]))
    return groups


def _even_back(xp, xs, mix_p, mix_s, w_out, next_gain, tm):
    w_out_b = w_out.astype(BF16)
    flat = lambda mix, m: [mix[0].reshape(m, A_WIDTH), mix[1].reshape(m, B_WIDTH)]
    return out_proj_norm(flat(mix_p, xp.shape[0]), [w_out_b[:A_WIDTH], w_out_b[A_WIDTH:]], xp, next_gain,
                         tm, True, BF16, second=(flat(mix_s, xs.shape[0]), xs, BF16))


def _odd_layer(xp, hp, xs, hs, w_in, v_gain, w_s, b_s, w_out_b, final_gain, tm, n_seq):
    (u, v, g), (u2, v2, g2) = odd_in(hp, w_in, v_gain, 4 * tm, BF16, second=(hs, F32))
    y2 = spatial_gate(u2, v2, g2, w_s[:, :SAMPLE_PAD, :SAMPLE_PAD], b_s[:, :SAMPLE_PAD].T, SAMPLE_PAD, n_seq)
    y_p, y_s = out_proj_norm([], [w_out_b], xp, final_gain, tm, False, F32, second=([y2], xs, F32),
                             spatial=(u, v, g, w_s, b_s.T))
    return y_p, y_s, v2


def kernel(x_prompt, x_sample, state_a_C, state_a_n, state_a_m, cache_b_k, cache_b_v, page_table,
           even_norm, even_w_in, even_b_i, even_b_f, even_b_sb, even_w_out,
           odd_norm, odd_w_in, odd_v_gain, odd_w_s, odd_b_s, odd_w_out, final_norm):
    bsz, seq, _ = x_prompt.shape
    n_seq, dec_seq, _ = x_sample.shape
    n_pool = cache_b_k.shape[1]

    ew = _even_weights(even_w_in[0], even_b_i[0], even_b_f[0])
    odd_w_in_b = odd_w_in[0]
    odd_w_out_b = odd_w_out[0].astype(BF16)

    xp = x_prompt.reshape(bsz * seq, D_MODEL)
    xs = jnp.pad(x_sample, ((0, 0), (0, SAMPLE_PAD - dec_seq), (0, 0))).reshape(n_seq * SAMPLE_PAD, D_MODEL)
    fp, fs = _even_front(xp, xs, ew, even_norm[0], 1024)
    qkvog_p, gates_p, grow_p = _mlstm_inputs(fp["qkvog"], fp["gates"], bsz, seq, seq)
    qkvog_s, gates_s, grow_s = _mlstm_inputs(fs["qkvog"], fs["gates"], n_seq, SAMPLE_PAD, dec_seq)
    q_s, g_s = (fs[name].reshape(n_seq, SAMPLE_PAD, B_WIDTH) for name in ("q_b", "g_b"))

    assert cache_b_k.shape[0] == 1 and cache_b_v.shape[0] == 1
    page_view = (1, n_pool, PAGE_SIZE * B_HEADS, B_DH)
    cache_k, cache_v = cache_b_k.reshape(page_view), cache_b_v.reshape(page_view)
    kv_pad = ((0, 0), (0, (PAGE_SIZE - SAMPLE_PAD) * B_HEADS), (0, 0))
    k_new_s = jnp.pad(fs["k_new"].reshape(n_seq, SAMPLE_PAD * B_HEADS, B_DH), kv_pad)
    v_new_s = jnp.pad(fs["v_new"].reshape(n_seq, SAMPLE_PAD * B_HEADS, B_DH), kv_pad)
    bias_rows = jnp.broadcast_to(jnp.repeat(even_b_sb[0], SAMPLE_PAD)[:, None],
                                 (B_HEADS * SAMPLE_PAD, GATE_LANES))
    att_rows = B_HEADS * SAMPLE_PAD
    rider = _PagedAttnRider(
        q_s, k_new_s, v_new_s, cache_k, cache_v, page_table, bias_rows,
        jnp.zeros((n_seq, att_rows, B_WIDTH), F32), jnp.zeros((n_seq, att_rows, GATE_LANES), F32),
        0, PAGES_PER_STEP)
    assert bsz * (seq // A_CHUNK) == n_seq * rider.steps_per_seq
    zero_state = (jnp.zeros((bsz, A_HEADS, A_DH, A_DH), F32),
                  jnp.zeros((bsz, A_HEADS, 1, A_DH), F32),
                  jnp.zeros((bsz, A_HEADS, 1, GATE_LANES), F32))
    (ha_p, c_p, n_p, m_p), (att_acc, _) = mlstm(
        qkvog_p, gates_p, grow_p, *zero_state, A_CHUNK, rider=rider)
    hb_s = attn_sample_finish(att_acc, g_s)

    as_seq = lambda a: a.reshape(bsz, seq, B_WIDTH)
    hb_p = attn_prompt(as_seq(fp["q_b"]), as_seq(fp["g_b"]), as_seq(fp["k_bf"]), as_seq(fp["v_bf"]),
                       even_b_sb[0], ATTN_BQ, ATTN_BK, ATTN_HEADS_PER_STEP)
    st_in = (state_a_C[0], state_a_n[0][:, :, None, :],
             jnp.broadcast_to(state_a_m[0][:, :, None, None], (n_seq, A_HEADS, 1, GATE_LANES)))
    ha_s, c_s, n_s, m_s_new = mlstm(qkvog_s, gates_s, grow_s, *st_in, A_CHUNK)
    ha_s = ha_s[:, :SAMPLE_PAD]

    xp1, hp1, xs1, hs1 = _even_back(xp, xs, (ha_p, hb_p), (ha_s, hb_s), even_w_out[0], odd_norm[0], 512)
    y_p, y_s, v_rows = _odd_layer(xp1, hp1, xs1, hs1, odd_w_in_b, odd_v_gain[0], odd_w_s[0], odd_b_s[0],
                                  odd_w_out_b, final_norm, 512, n_seq)

    def sample_rows(a, *dims):
        return a.reshape((n_seq, SAMPLE_PAD) + dims)[:, :dec_seq]

    return (y_p.reshape(bsz, seq, D_MODEL),
            sample_rows(y_s, D_MODEL),
            c_p[None], n_p[:, :, 0, :][None], m_p[:, :, 0, 0][None],
            c_s[None], n_s[:, :, 0, :][None], m_s_new[:, :, 0, 0][None],
            fp["k_new"].reshape(1, bsz, seq, B_HEADS, B_DH), fp["v_new"].reshape(1, bsz, seq, B_HEADS, B_DH),
            sample_rows(fs["k_new"], B_HEADS, B_DH)[None], sample_rows(fs["v_new"], B_HEADS, B_DH)[None],
            sample_rows(v_rows, C_WIDTH)[None])
```

```python
import functools

import jax
import jax.numpy as jnp
from jax import lax
from jax.experimental import pallas as pl
from jax.experimental.pallas import tpu as pltpu

F32 = jnp.float32
BF16 = jnp.bfloat16

D_MODEL = 2048
PAGE_SIZE = 128
A_HEADS = 4
A_DH = 256
A_WIDTH = A_HEADS * A_DH
A_CHUNK = 128
B_HEADS = 8
B_DH = 128
B_WIDTH = B_HEADS * B_DH
C_WIDTH = D_MODEL
C_GROUPS = 8
C_GDIM = C_WIDTH // C_GROUPS
C_CHUNK = 128
RMS_EPS = 1e-6
GATE_LANES = 128
NEG_BIG = -1e30
SAMPLE_PAD = 8
ATTN_BQ = 512
ATTN_BK = 512
CUMSUM_BLOCK = 256
ATTN_HEADS_PER_STEP = 2
PAGES_PER_STEP = 16

VMEM_LIMIT_BYTES = 56 * 1024 * 1024


def _params(*sem):
    return pltpu.CompilerParams(dimension_semantics=sem, vmem_limit_bytes=VMEM_LIMIT_BYTES)


def _dot(a, b):
    return jnp.dot(a, b, preferred_element_type=F32)


def _dot_nt(a, b):
    return lax.dot_general(a, b, (((1,), (1,)), ((), ())), preferred_element_type=F32)


def _dot_tn(a, b):
    return lax.dot_general(a, b, (((0,), (0,)), ((), ())), preferred_element_type=F32)


def _softplus(z):
    return jnp.maximum(z, 0.0) + jnp.log(1.0 + jnp.exp(-jnp.abs(z)))


def _sigmoid(z):
    return 1.0 / (1.0 + jnp.exp(-z))


def _silu(z):
    return z * _sigmoid(z)


def _gelu_tanh(x):
    c = 0.7978845608028654
    return x * (0.5 * (1.0 + jnp.tanh(c * (x + 0.044715 * (x * x * x)))))


def _rms(x, g):
    return x * lax.rsqrt(jnp.mean(x * x, axis=-1, keepdims=True) + RMS_EPS) * g


def _split_hi_lo(x):
    hi = x.astype(BF16)
    lo = (x - hi.astype(F32)).astype(BF16)
    return hi, lo


def _norm_gates_kernel(x_ref, g_ref, w_ref, bias_ref, h_ref, gate_ref):
    h = _rms(x_ref[...], g_ref[...])
    h_hi, h_lo = _split_hi_lo(h)
    h_ref[...] = h_hi
    a = _dot(h_hi, w_ref[...])
    b = _dot(h_lo, w_ref[...])
    pre = a[:, :GATE_LANES] + a[:, GATE_LANES:] + b[:, :GATE_LANES] + bias_ref[...]
    lane = lax.broadcasted_iota(jnp.int32, pre.shape, 1)
    is_forget = (lane >= A_HEADS) & (lane < 2 * A_HEADS)
    gate_ref[...] = jnp.where(is_forget, -_softplus(-pre), pre)


def norm_gates(x, gain, w_gate, bias, tm):
    m = x.shape[0]
    w_hi_lo = jnp.concatenate(_split_hi_lo(w_gate), axis=1)
    return pl.pallas_call(
        _norm_gates_kernel,
        grid=(m // tm,),
        in_specs=[pl.BlockSpec((tm, D_MODEL), lambda i: (i, 0)),
                  pl.BlockSpec((1, D_MODEL), lambda i: (0, 0)),
                  pl.BlockSpec((D_MODEL, 2 * GATE_LANES), lambda i: (0, 0)),
                  pl.BlockSpec((1, GATE_LANES), lambda i: (0, 0))],
        out_specs=[pl.BlockSpec((tm, D_MODEL), lambda i: (i, 0)),
                   pl.BlockSpec((tm, GATE_LANES), lambda i: (i, 0))],
        out_shape=[jax.ShapeDtypeStruct((m, D_MODEL), BF16),
                   jax.ShapeDtypeStruct((m, GATE_LANES), F32)],
        compiler_params=_params("arbitrary"),
        name="norm_gates",
    )(x, gain.reshape(1, D_MODEL), w_hi_lo, bias)


def _hosted_kernel(*refs, n_in, n_out, phases_fn, lin, rank, rider):
    if rider is not None:
        refs = refs[1:]
    r_in, r_out = (rider.n_in, rider.n_out) if rider is not None else (0, 0)
    ins, refs = refs[:n_in], refs[n_in:]
    r_ins, refs = refs[:r_in], refs[r_in:]
    outs, refs = refs[:n_out], refs[n_out:]
    r_outs, refs = refs[:r_out], refs[r_out:]
    n_sc = len(refs) - (rider.n_scratch if rider is not None else 0)
    scratch, r_scratch = refs[:n_sc], refs[n_sc:]
    sets = [phases_fn(ins, outs, scratch)]
    if rider is not None:
        step = lin(*[pl.program_id(d) for d in range(rank)])
        sets.append(rider.phases(r_ins, r_outs, r_scratch, step))
    for pre, _, _ in sets:
        for cond, fn in pre:
            pl.when(cond)(fn)
    for _, parts, _ in sets:
        for part in parts:
            part()
    for _, _, post in sets:
        for cond, fn in post:
            pl.when(cond)(fn)


def _hosted_call(name, grid, lin, in_specs, out_specs, out_shape, scratch, phases_fn, args, rider=None):
    n_in, n_out = len(in_specs), len(out_specs)
    aliases = {}
    if rider is not None:
        r = rider.specs(lin, len(grid))
        aliases = {1 + n_in + i: n_out + o for i, o in r["aliases"].items()}
        in_specs, out_specs = in_specs + r["in_specs"], out_specs + r["out_specs"]
        out_shape, scratch = out_shape + r["out_shape"], scratch + r["scratch"]
        args = [rider.page_table] + list(args) + r["args"]
    kern = functools.partial(_hosted_kernel, n_in=n_in, n_out=n_out, phases_fn=phases_fn,
                             lin=lin, rank=len(grid), rider=rider)
    grid_spec = pltpu.PrefetchScalarGridSpec(
        num_scalar_prefetch=0 if rider is None else 1, grid=grid,
        in_specs=in_specs, out_specs=out_specs, scratch_shapes=scratch)
    outs = pl.pallas_call(
        kern, grid_spec=grid_spec, out_shape=out_shape, input_output_aliases=aliases,
        compiler_params=_params(*(["arbitrary"] * len(grid))), name=name,
    )(*args)
    return outs[:n_out], outs[n_out:]


def _two_group_steps(n_i, has_second):
    if not has_second:
        return n_i, (lambda i: i)
    return n_i + 1, (lambda i: jnp.maximum(i - 1, 0))


def proj(a, wt, row0, n_out, out_dtype, tm, tn, scale=None, second=None):
    m, k = a.shape
    n_i = m // tm
    n_steps, tile = _two_group_steps(n_i, second is not None)

    def phases(ins, outs, scratch):
        a_ref, wt_ref = ins[:2]
        o_ref, (wbf,) = outs[0], scratch
        i = pl.program_id(1)

        def cast():
            wbf[...] = wt_ref[...].astype(BF16)

        def main():
            y = _dot_nt(a_ref[...], wbf[...])
            o_ref[...] = (y if scale is None else y * scale).astype(o_ref.dtype)

        if second is None:
            return [(i == 0, cast)], [main], []

        def small():
            outs[1][...] = _dot_nt(ins[2][...], wbf[...]).astype(outs[1].dtype)

        return [(i == 0, cast), (i == 0, small), (i > 0, main)], [], []

    in_specs = [pl.BlockSpec((tm, k), lambda j, i, *_: (tile(i), 0)),
                pl.BlockSpec((pl.Element(tn), pl.Element(k)),
                             lambda j, i, *_: (pl.multiple_of(row0 + j * tn, 8), 0))]
    out_specs = [pl.BlockSpec((tm, tn), lambda j, i, *_: (tile(i), j))]
    out_shape = [jax.ShapeDtypeStruct((m, n_out), out_dtype)]
    args = [a, wt]
    if second is not None:
        a2, dtype2 = second
        in_specs.append(pl.BlockSpec(a2.shape, lambda j, i, *_: (0, 0)))
        out_specs.append(pl.BlockSpec((a2.shape[0], tn), lambda j, i, *_: (0, j)))
        out_shape.append(jax.ShapeDtypeStruct((a2.shape[0], n_out), dtype2))
        args.append(a2)
    outs, _ = _hosted_call("proj", (n_out // tn, n_steps), lambda j, i: j * n_steps + i,
                           in_specs, out_specs, out_shape, [pltpu.VMEM((tn, k), BF16)], phases, args)
    return outs[0] if second is None else tuple(outs)


def kv_proj(a, wt, row0, tm, second=None):
    m, k = a.shape
    n_i = m // tm
    n_steps, tile = _two_group_steps(n_i, second is not None)

    def phases(ins, outs, scratch):
        a_ref, wt_ref = ins[:2]
        (wbf,) = scratch
        i = pl.program_id(0)

        def cast():
            wbf[...] = wt_ref[...].astype(BF16)

        def rows_to(a_rows_ref, o_ref, obf_ref):
            def run():
                n_rows = a_rows_ref.shape[0]
                y = _dot_nt(a_rows_ref[...], wbf[...])
                obf_ref[...] = y.astype(BF16)
                for h in range(B_HEADS):
                    o_ref[pl.ds(h, n_rows, stride=B_HEADS), :] = y[:, h * B_DH:(h + 1) * B_DH]
            return run

        main = rows_to(a_ref, outs[0], outs[1])
        if second is None:
            return [(i == 0, cast)], [main], []
        return [(i == 0, cast), (i == 0, rows_to(ins[2], outs[2], outs[3])), (i > 0, main)], [], []

    def out_pair(rows, index):
        return ([pl.BlockSpec((rows * B_HEADS, B_DH), index), pl.BlockSpec((rows, B_WIDTH), index)],
                lambda total: [jax.ShapeDtypeStruct((total * B_HEADS, B_DH), F32),
                               jax.ShapeDtypeStruct((total, B_WIDTH), BF16)])

    in_specs = [pl.BlockSpec((tm, k), lambda i, *_: (tile(i), 0)),
                pl.BlockSpec((pl.Element(B_WIDTH), pl.Element(k)), lambda i, *_: (row0, 0))]
    out_specs, shapes = out_pair(tm, lambda i, *_: (tile(i), 0))
    out_shape = shapes(m)
    args = [a, wt]
    if second is not None:
        m2 = second.shape[0]
        in_specs.append(pl.BlockSpec(second.shape, lambda i, *_: (0, 0)))
        specs2, shapes2 = out_pair(m2, lambda i, *_: (0, 0))
        out_specs, out_shape = out_specs + specs2, out_shape + shapes2(m2)
        args.append(second)
    outs, _ = _hosted_call("kv_proj", (n_steps,), lambda i: i, in_specs, out_specs, out_shape,
                           [pltpu.VMEM((B_WIDTH, k), BF16)], phases, args)
    return tuple(outs)


def out_proj_norm(lhs, ws, x, gain, tm, emit_x, norm_dtype, second=None, spatial=None):
    m = x.shape[0]
    n_lhs, n_w = len(lhs), len(ws)
    n_i = m // tm
    n_steps, tile = _two_group_steps(n_i, second is not None)
    n_out = 2 if emit_x else 1

    def phases(ins, outs, scratch):
        w_refs = ins[n_lhs:n_lhs + n_w]
        x_ref, g_ref = ins[n_lhs + n_w], ins[n_lhs + n_w + 1]
        rest = ins[n_lhs + n_w + 2:]
        i = pl.program_id(0)

        def group(make_lhs, a_refs, x_ref, out_refs, row_chunk):
            def run():
                if make_lhs is not None:
                    make_lhs()
                for r in range(0, x_ref.shape[0], row_chunk):
                    rows = slice(r, r + row_chunk)
                    y = x_ref[rows, :]
                    for a_ref, w_ref in zip(a_refs, w_refs):
                        y = y + _dot(a_ref[rows, :].astype(BF16), w_ref[...])
                    if emit_x:
                        out_refs[0][rows, :] = y
                    out_refs[-1][rows, :] = _rms(y, g_ref[...]).astype(out_refs[-1].dtype)
            return run

        if second is not None:
            a2_refs, x2_ref, rest = rest[:n_w], rest[n_w], rest[n_w + 1:]
        if spatial is None:
            main = group(None, ins[:n_lhs], x_ref, outs[:n_out], min(tm, 256))
        else:
            (y_sc,) = scratch
            fill = functools.partial(_spatial_kernel, *rest[:5], y_sc, chunk=C_CHUNK, n_chunks=tm // C_CHUNK)
            main = group(fill, [y_sc], x_ref, outs[:n_out], min(tm, 256))
        if second is None:
            return [], [main], []
        small = group(None, a2_refs, x2_ref, outs[n_out:], x2_ref.shape[0])
        return [(i == 0, small), (i > 0, main)], [], []

    row_spec = lambda width: pl.BlockSpec((tm, width), lambda i, *_: (tile(i), 0))
    whole = lambda arr: pl.BlockSpec(arr.shape, lambda i, *_: (0,) * arr.ndim)
    gain2d = gain.reshape(1, D_MODEL)
    in_specs = ([row_spec(a.shape[1]) for a in lhs] + [whole(w) for w in ws]
                + [row_spec(D_MODEL), whole(gain2d)])
    out_specs = [row_spec(D_MODEL)] * n_out
    out_shape = ([jax.ShapeDtypeStruct((m, D_MODEL), F32)] if emit_x else []) \
        + [jax.ShapeDtypeStruct((m, D_MODEL), norm_dtype)]
    args = [*lhs, *ws, x, gain2d]
    scratch = []
    if second is not None:
        lhs2, x2, norm_dtype2 = second
        m2 = x2.shape[0]
        in_specs += [whole(a) for a in lhs2] + [whole(x2)]
        out_specs += [pl.BlockSpec((m2, D_MODEL), lambda i, *_: (0, 0))] * n_out
        out_shape += ([jax.ShapeDtypeStruct((m2, D_MODEL), F32)] if emit_x else []) \
            + [jax.ShapeDtypeStruct((m2, D_MODEL), norm_dtype2)]
        args += [*lhs2, x2]
    if spatial is not None:
        u, v, gate, w_s, b_s_t = spatial
        in_specs += [row_spec(C_WIDTH)] * 3 + [whole(w_s), whole(b_s_t)]
        args += [u, v, gate, w_s, b_s_t]
        scratch = [pltpu.VMEM((tm, C_WIDTH), BF16)]
    outs, _ = _hosted_call("out_proj_norm", (n_steps,), lambda i: i, in_specs, out_specs, out_shape,
                           scratch, phases, args)
    return list(outs)


def _mlstm_chunk(q_ref, k_ref, v_ref, og_ref, gg_ref, gcol_ref, grow_ref, h_ref, c_sc, n_sc, m_sc, chunk, heads):
    L = chunk
    gcol = gcol_ref[0]
    grow = grow_ref[0]
    tt = lax.broadcasted_iota(jnp.int32, (L, L), 0)
    ss = lax.broadcasted_iota(jnp.int32, (L, L), 1)
    causal = ss <= tt

    for head in heads:
        cols = slice(head * A_DH, (head + 1) * A_DH)
        q = q_ref[0, :, cols]
        ks = k_ref[0, :, cols] * jnp.asarray(A_DH ** -0.5, BF16)
        v = v_ref[0, :, cols]
        ig_col = gcol[:, head:head + 1]
        lf_col = gcol[:, head + A_HEADS:head + A_HEADS + 1]
        ig_row = grow[head:head + 1, :]
        lf_row = grow[head + A_HEADS:head + A_HEADS + 1, :]
        b_col = jnp.sum(jnp.where(causal, lf_row, 0.0), axis=1, keepdims=True)
        b_row = jnp.sum(jnp.where(tt <= ss, lf_col, 0.0), axis=0, keepdims=True)
        b_last = jnp.sum(lf_row, axis=1, keepdims=True)

        m0 = m_sc[head][:, :1]
        n0 = n_sc[head]
        c0 = c_sc[head]

        d = jnp.where(causal, b_col - b_row + ig_row, NEG_BIG)
        m_carry = b_col + m0
        m = jnp.maximum(m_carry, jnp.max(d, axis=1, keepdims=True))
        w_intra = jnp.exp(d - m)
        w_carry = jnp.exp(m_carry - m)
        s = _dot_nt(q, ks) * w_intra
        qf = q.astype(F32)
        num = _dot(s.astype(BF16), v) + w_carry * _dot_nt(q, c0.astype(BF16))
        den = jnp.sum(s, axis=1, keepdims=True) + w_carry * jnp.sum(qf * n0, axis=1, keepdims=True)
        h = num / jnp.maximum(jnp.abs(den), jnp.exp(-m))
        gated = h * _sigmoid(og_ref[0, :, cols].astype(F32)) * _silu(gg_ref[0, :, cols].astype(F32))
        h_ref[0, :, cols] = gated.astype(h_ref.dtype)

        m_carry_last = b_last + m0
        d_last_row = b_last - b_row + ig_row
        m_new = jnp.maximum(m_carry_last, jnp.max(d_last_row, axis=1, keepdims=True))
        wc_last = jnp.exp(m_carry_last - m_new)
        w_last_col = jnp.exp(b_last - b_col + ig_col - m_new)
        vw = (v.astype(F32) * w_last_col).astype(BF16)
        c_new = wc_last * c0 + _dot_tn(vw, ks)
        n_new = wc_last * n0 + jnp.sum(ks.astype(F32) * w_last_col, axis=0, keepdims=True)
        c_sc[head] = c_new
        n_sc[head] = n_new
        m_sc[head] = jnp.broadcast_to(m_new, (1, GATE_LANES))


def mlstm(qkvog, gcol, grow, c0, n0, m0, chunk, rider=None):
    bsz, t_len, _ = qkvog.shape
    nc = t_len // chunk
    hd = A_HEADS

    def phases(ins, outs, state):
        c0_ref, n0_ref, m0_ref = ins[7:]
        h_ref, c_out_ref, n_out_ref, m_out_ref = outs
        c_sc, n_sc, m_sc = state
        ci = pl.program_id(1)

        def init():
            c_sc[...] = c0_ref[0]
            n_sc[...] = n0_ref[0]
            m_sc[...] = m0_ref[0]

        def head_part(head):
            return lambda: _mlstm_chunk(*ins[:7], h_ref, c_sc, n_sc, m_sc, chunk, (head,))

        parts = [head_part(head) for head in range(A_HEADS)]

        def final():
            c_out_ref[0] = c_sc[...]
            n_out_ref[0] = n_sc[...]
            m_out_ref[0] = m_sc[...]

        return [(ci == 0, init)], parts, [(ci == nc - 1, final)]

    blk = lambda seg: pl.BlockSpec((1, chunk, A_WIDTH), lambda b, c, *_, seg=seg: (b, c, seg))
    st4 = lambda r, w: pl.BlockSpec((1, hd, r, w), lambda b, c, *_: (b, 0, 0, 0))
    outs, rest = _hosted_call(
        "mlstm", (bsz, nc), lambda b, c: b * nc + c,
        [blk(0), blk(1), blk(2), blk(3), blk(4),
         pl.BlockSpec((1, chunk, GATE_LANES), lambda b, c, *_: (b, c, 0)),
         pl.BlockSpec((1, 8, chunk), lambda b, c, *_: (b, 0, c)),
         st4(A_DH, A_DH), st4(1, A_DH), st4(1, GATE_LANES)],
        [pl.BlockSpec((1, chunk, A_WIDTH), lambda b, c, *_: (b, c, 0)),
         st4(A_DH, A_DH), st4(1, A_DH), st4(1, GATE_LANES)],
        [jax.ShapeDtypeStruct((bsz, t_len, A_WIDTH), BF16),
         jax.ShapeDtypeStruct((bsz, hd, A_DH, A_DH), F32),
         jax.ShapeDtypeStruct((bsz, hd, 1, A_DH), F32),
         jax.ShapeDtypeStruct((bsz, hd, 1, GATE_LANES), F32)],
        [pltpu.VMEM((hd, A_DH, A_DH), F32), pltpu.VMEM((hd, 1, A_DH), F32),
         pltpu.VMEM((hd, 1, GATE_LANES), F32)],
        phases, [qkvog, qkvog, qkvog, qkvog, qkvog, gcol, grow, c0, n0, m0], rider)
    return tuple(outs) if rider is None else (tuple(outs), rest)


def _stick_block(q, kb, vb, bias, run, mask, upper):
    rows = q.shape[0]
    sub = upper.shape[0]
    n_sub = kb.shape[0] // sub
    z = _dot_nt(q, kb) if bias is None else _dot_nt(q, kb) * (B_DH ** -0.5) + bias
    sp = _softplus(z)
    spm = sp if mask is None else jnp.where(mask, sp, 0.0)
    hi, lo = _split_hi_lo(spm)
    laters = [None] * n_sub
    total = None
    for i in reversed(range(n_sub)):
        ln = slice(i * sub, (i + 1) * sub)
        both = _dot(jnp.concatenate([hi[:, ln], lo[:, ln]], axis=0), upper)
        carry = run if total is None else run + total
        laters[i] = both[:rows] + both[rows:] + carry
        part = jnp.sum(spm[:, ln], axis=1, keepdims=True)
        total = part if total is None else total + part
    later = laters[0] if n_sub == 1 else jnp.concatenate(laters, axis=1)
    a = jnp.exp(z - sp - later)
    if mask is not None:
        a = jnp.where(mask, a, 0.0)
    return _dot(a.astype(BF16), vb), total


def _strict_upper(n):
    j = lax.broadcasted_iota(jnp.int32, (n, n), 0)
    s = lax.broadcasted_iota(jnp.int32, (n, n), 1)
    return jnp.where(j > s, 1.0, 0.0).astype(BF16)


def _attn_prompt_kernel(bias_ref, q_ref, k_ref, v_ref, g_ref, o_ref, *, bq, bk, n_heads):
    head0 = pl.program_id(1) * n_heads
    qi = pl.program_id(2)
    kbf = k_ref.at[0]
    vbf = v_ref.at[0]
    upper = _strict_upper(min(bk, CUMSUM_BLOCK))
    row = lax.broadcasted_iota(jnp.int32, (bq, bk), 0)
    col = lax.broadcasted_iota(jnp.int32, (bq, bk), 1)
    lanes = [slice(h * B_DH, (h + 1) * B_DH) for h in range(n_heads)]
    lane_q = lax.broadcasted_iota(jnp.int32, (bq, B_DH), 1)
    lane_k = lax.broadcasted_iota(jnp.int32, (bk, B_DH), 1)
    ones_cols = jnp.where(lane_q < 2, 1.0, 0.0).astype(BF16)
    qs = [jnp.concatenate([q_ref[0, :, ln], ones_cols], axis=1) for ln in lanes]
    bias_cols = []
    for h in range(n_heads):
        b = jnp.full((bk, B_DH), bias_ref[head0 + h], F32)
        b_hi = b.astype(BF16).astype(F32)
        bias_cols.append(jnp.where(lane_k == 0, b_hi, jnp.where(lane_k == 1, b - b_hi, 0.0)).astype(BF16))

    def blocks(kj, runs, mask):
        start = pl.multiple_of(kj * bk, bk)
        return [_stick_block(qs[h], jnp.concatenate([kbf[pl.ds(start, bk), ln], bias_cols[h]], axis=1),
                             vbf[pl.ds(start, bk), ln], None, runs[h], mask, upper)
                for h, ln in enumerate(lanes)]

    q0 = qi * bq
    n_full = q0 // bk
    if bq == bk:
        half = bq // 2
        start = pl.multiple_of(qi * bq, bq)
        accs, runs = [], []
        for h, ln in enumerate(lanes):
            kb = jnp.concatenate([kbf[pl.ds(start, bk), ln], bias_cols[h]], axis=1)
            vb = vbf[pl.ds(start, bk), ln]
            zero = jnp.zeros((half, 1), F32)
            iota = lambda shape, dim: lax.broadcasted_iota(jnp.int32, shape, dim)
            top = _stick_block(qs[h][:half], kb[:half], vb[:half], None, zero,
                               iota((half, half), 1) < iota((half, half), 0), upper)
            bot = _stick_block(qs[h][half:], kb, vb, None, zero,
                               iota((half, bk), 1) < iota((half, bk), 0) + half, upper)
            accs.append(jnp.concatenate([top[0], bot[0]], axis=0))
            runs.append(jnp.concatenate([top[1], bot[1]], axis=0))
    else:
        accs = [jnp.zeros((bq, B_DH), F32)] * n_heads
        runs = [jnp.zeros((bq, 1), F32)] * n_heads
        for m in reversed(range(max(1, bq // bk))):
            kj = n_full + m
            res = blocks(kj, runs, col + (kj * bk - q0) < row)
            accs = [a + c for a, (c, _) in zip(accs, res)]
            runs = [r + t for r, (_, t) in zip(runs, res)]

    def body(it, carry):
        accs, runs = carry
        res = blocks(n_full - 1 - it, runs, None)
        return (tuple(a + c for a, (c, _) in zip(accs, res)),
                tuple(r + t for r, (_, t) in zip(runs, res)))

    accs, runs = lax.fori_loop(0, n_full, body, (tuple(accs), tuple(runs)))
    for h, ln in enumerate(lanes):
        o_ref[0, :, ln] = (accs[h] * _silu(g_ref[0, :, ln].astype(F32))).astype(o_ref.dtype)


def attn_prompt(q, g, k, v, b_sb, bq, bk, n_heads):
    bsz, t_len, _ = k.shape
    width = n_heads * B_DH
    q_spec = pl.BlockSpec((1, bq, width), lambda b, h, i: (b, i, h))
    kv_spec = pl.BlockSpec((1, t_len, width), lambda b, h, i: (b, 0, h))
    return pl.pallas_call(
        functools.partial(_attn_prompt_kernel, bq=bq, bk=bk, n_heads=n_heads),
        grid=(bsz, B_HEADS // n_heads, t_len // bq),
        in_specs=[pl.BlockSpec(memory_space=pltpu.SMEM), q_spec, kv_spec, kv_spec, q_spec],
        out_specs=q_spec,
        out_shape=jax.ShapeDtypeStruct((bsz, t_len, B_WIDTH), BF16),
        compiler_params=_params("arbitrary", "arbitrary", "arbitrary"),
        name="attn_prompt",
    )(b_sb, q, k, v, g)


def _attn_sample_phases(ins, outs, scratch, n_group):
    bias_ref, q_ref, knew_ref, vnew_ref, acc_in_ref, run_in_ref = ins[:6]
    k_refs = ins[6:6 + n_group]
    v_refs = ins[6 + n_group:]
    acc_out_ref, run_out_ref = outs
    qbd, acc, run, kcat, vcat = scratch
    rows = B_HEADS * SAMPLE_PAD
    upper = _strict_upper(PAGE_SIZE)
    bias = bias_ref[...][:, :1]

    def repack(page, dst, i):
        for h in range(B_HEADS):
            dst[i * PAGE_SIZE:(i + 1) * PAGE_SIZE, h * B_DH:(h + 1) * B_DH] = page(h).astype(BF16)

    def step(slot0, n_blk, mask):
        keys = slice(slot0 * PAGE_SIZE, (slot0 + n_blk) * PAGE_SIZE)
        z = _dot_nt(qbd[...], kcat[keys, :]) * (B_DH ** -0.5) + bias
        sp = _softplus(z)
        spm = sp if mask is None else jnp.where(mask, sp, 0.0)
        hi, lo = _split_hi_lo(spm)
        carry = run[...][:, :1]
        laters = []
        for i in range(n_blk):
            ln = slice(i * PAGE_SIZE, (i + 1) * PAGE_SIZE)
            both = _dot(jnp.concatenate([hi[:, ln], lo[:, ln]], axis=0), upper)
            laters.append(both[:rows] + both[rows:] + carry)
            carry = carry + jnp.sum(spm[:, ln], axis=1, keepdims=True)
        later = laters[0] if n_blk == 1 else jnp.concatenate(laters, axis=1)
        a = jnp.exp(z - sp - later)
        if mask is not None:
            a = jnp.where(mask, a, 0.0)
        acc[...] += _dot(a.astype(BF16), vcat[keys, :])
        run[...] = jnp.broadcast_to(carry, run.shape)

    def build_queries():
        r = lax.broadcasted_iota(jnp.int32, (rows, B_WIDTH), 0)
        c = lax.broadcasted_iota(jnp.int32, (rows, B_WIDTH), 1)
        q_rep = jnp.concatenate([q_ref[0].astype(F32)] * B_HEADS, axis=0)
        qbd[...] = jnp.where((r // SAMPLE_PAD) == (c // B_DH), q_rep, 0.0).astype(BF16)

    def start_sequence():
        build_queries()
        acc[...] = jnp.zeros_like(acc)
        run[...] = jnp.zeros_like(run)
        repack(lambda h: knew_ref[0, pl.ds(h, PAGE_SIZE, stride=B_HEADS), :], kcat, 0)
        repack(lambda h: vnew_ref[0, pl.ds(h, PAGE_SIZE, stride=B_HEADS), :], vcat, 0)
        t = lax.broadcasted_iota(jnp.int32, (rows, PAGE_SIZE), 0) % SAMPLE_PAD
        s = lax.broadcasted_iota(jnp.int32, (rows, PAGE_SIZE), 1)
        step(0, 1, s < t)

    def resume_sequence():
        build_queries()
        acc[...] = acc_in_ref[0]
        run[...] = run_in_ref[0]

    def pages():
        for i in range(n_group):
            repack(lambda h, r=k_refs[i]: r[0, 0, pl.ds(h, PAGE_SIZE, stride=B_HEADS), :], kcat, i)
            repack(lambda h, r=v_refs[i]: r[0, 0, pl.ds(h, PAGE_SIZE, stride=B_HEADS), :], vcat, i)
        step(0, n_group, None)
        acc_out_ref[0] = acc[...]
        run_out_ref[0] = run[...]

    return start_sequence, resume_sequence, pages


class _PagedAttnRider:
    n_out = 2
    n_scratch = 5

    def __init__(self, q, k_new, v_new, cache_k, cache_v, page_table, bias_rows, acc, run, first_step, n_group):
        self.arrays = (bias_rows, q, k_new, v_new, acc, run)
        self.caches = (cache_k, cache_v)
        self.page_table = page_table
        self.first_step = first_step
        self.n_group = n_group
        self.n_in = 6 + 2 * n_group
        self.steps_per_seq = page_table.shape[1] // n_group

    def specs(self, lin, rank):
        n_group, spq = self.n_group, self.steps_per_seq
        n_pages = self.page_table.shape[1]
        rows = B_HEADS * SAMPLE_PAD
        page_rows = PAGE_SIZE * B_HEADS
        gstep = lambda a: self.first_step + lin(*a[:rank])
        seq_map = lambda *a: (gstep(a) // spq, 0, 0)

        def page_spec(i):
            def index(*a):
                g, pt = gstep(a), a[rank]
                return (0, pt[g // spq, n_pages - 1 - ((g % spq) * n_group + i)], 0, 0)
            return pl.BlockSpec((1, 1, page_rows, B_DH), index)

        acc_spec = pl.BlockSpec((1, rows, B_WIDTH), seq_map)
        run_spec = pl.BlockSpec((1, rows, GATE_LANES), seq_map)
        new_spec = pl.BlockSpec((1, page_rows, B_DH), seq_map)
        acc, run = self.arrays[4:]
        return dict(
            in_specs=[pl.BlockSpec((rows, GATE_LANES), lambda *a: (0, 0)),
                      pl.BlockSpec((1, SAMPLE_PAD, B_WIDTH), seq_map), new_spec, new_spec,
                      acc_spec, run_spec] + [page_spec(i) for i in range(n_group)] * 2,
            out_specs=[acc_spec, run_spec],
            out_shape=[jax.ShapeDtypeStruct(acc.shape, F32), jax.ShapeDtypeStruct(run.shape, F32)],
            scratch=[pltpu.VMEM((rows, B_WIDTH), BF16), pltpu.VMEM((rows, B_WIDTH), F32),
                     pltpu.VMEM((rows, GATE_LANES), F32),
                     pltpu.VMEM((n_group * PAGE_SIZE, B_WIDTH), BF16),
                     pltpu.VMEM((n_group * PAGE_SIZE, B_WIDTH), BF16)],
            args=list(self.arrays) + [self.caches[0]] * n_group + [self.caches[1]] * n_group,
            aliases={4: 0, 5: 1},
        )

    def phases(self, ins, outs, scratch, local_step):
        start, resume, pages = _attn_sample_phases(ins, outs, scratch, self.n_group)
        p = (self.first_step + local_step) % self.steps_per_seq
        return [(p == 0, start), ((local_step == 0) & (p != 0), resume)], [pages], []


def attn_sample_finish(acc, g):
    n_seq = acc.shape[0]

    def kern(acc_ref, g_ref, o_ref):
        a = acc_ref[0]
        c = lax.broadcasted_iota(jnp.int32, (SAMPLE_PAD, B_WIDTH), 1) // B_DH
        out = jnp.zeros((SAMPLE_PAD, B_WIDTH), F32)
        for h in range(B_HEADS):
            out = out + jnp.where(c == h, a[h * SAMPLE_PAD:(h + 1) * SAMPLE_PAD, :], 0.0)
        o_ref[0] = out * _silu(g_ref[0])

    return pl.pallas_call(
        kern, grid=(n_seq,),
        in_specs=[pl.BlockSpec((1, B_HEADS * SAMPLE_PAD, B_WIDTH), lambda s: (s, 0, 0)),
                  pl.BlockSpec((1, SAMPLE_PAD, B_WIDTH), lambda s: (s, 0, 0))],
        out_specs=pl.BlockSpec((1, SAMPLE_PAD, B_WIDTH), lambda s: (s, 0, 0)),
        out_shape=jax.ShapeDtypeStruct((n_seq, SAMPLE_PAD, B_WIDTH), F32),
        compiler_params=_params("arbitrary"), name="attn_sample_finish",
    )(acc, g)


def proj_act(h, w, col0, v_gain, act, tm, tn, out_dtype, second=None):
    m = h.shape[0]
    j0 = col0 // tn
    n_col = C_WIDTH // tn
    n_i = m // tm
    n_steps, tile = _two_group_steps(n_i, second is not None)

    def phases(ins, outs, scratch):
        a_ref, w_ref, vg_ref = ins[:3]
        (wbf,) = scratch
        i = pl.program_id(1)

        def cast():
            wbf[...] = w_ref[...].astype(BF16)

        def group(rows_ref, o_ref, row_chunk):
            def run():
                for r in range(0, rows_ref.shape[0], row_chunk):
                    rows = slice(r, r + row_chunk)
                    y = _dot(rows_ref[rows, :], wbf[...])
                    if act == "gelu":
                        y = _gelu_tanh(y)
                    elif act == "silu":
                        y = _silu(y)
                    else:
                        y = _rms(_gelu_tanh(y), vg_ref[...])
                    o_ref[rows, :] = y.astype(o_ref.dtype)
            return run

        main = group(a_ref, outs[0], min(tm, 256))
        if second is None:
            return [(i == 0, cast)], [main], []
        small = group(ins[3], outs[1], ins[3].shape[0])
        return [(i == 0, cast), (i == 0, small), (i > 0, main)], [], []

    w_mode = dict(pipeline_mode=pl.Buffered(1)) if n_col == 1 else {}
    in_specs = [pl.BlockSpec((tm, D_MODEL), lambda j, i, *_: (tile(i), 0)),
                pl.BlockSpec((D_MODEL, tn), lambda j, i, *_: (0, j0 + j), **w_mode),
                pl.BlockSpec((1, tn), lambda j, i, *_: (0, j))]
    out_specs = [pl.BlockSpec((tm, tn), lambda j, i, *_: (tile(i), j))]
    out_shape = [jax.ShapeDtypeStruct((m, C_WIDTH), out_dtype)]
    args = [h, w, v_gain.reshape(1, C_WIDTH)]
    if second is not None:
        h2, dtype2 = second
        in_specs.append(pl.BlockSpec(h2.shape, lambda j, i, *_: (0, 0)))
        out_specs.append(pl.BlockSpec((h2.shape[0], tn), lambda j, i, *_: (0, j)))
        out_shape.append(jax.ShapeDtypeStruct((h2.shape[0], C_WIDTH), dtype2))
        args.append(h2)
    outs, _ = _hosted_call("proj_" + act, (n_col, n_steps), lambda j, i: j * n_steps + i,
                           in_specs, out_specs, out_shape, [pltpu.VMEM((D_MODEL, tn), BF16)], phases, args)
    return outs[0] if second is None else tuple(outs)


def odd_in(h, w, v_gain, tm, act_dtype, second=None):
    u = proj_act(h, w, 0, v_gain, "gelu", tm, 1024, act_dtype, second)
    v = proj_act(h, w, C_WIDTH, v_gain, "gelu_rms", min(tm, 512), C_WIDTH, act_dtype, second)
    g = proj_act(h, w, 2 * C_WIDTH, v_gain, "silu", tm, 1024, act_dtype, second)
    if second is None:
        return u, v, g
    return (u[0], v[0], g[0]), (u[1], v[1], g[1])


def _spatial_kernel(u_ref, v_ref, g_ref, ws_ref, bs_ref, y_ref, *, chunk, n_chunks):
    tt = lax.broadcasted_iota(jnp.int32, (chunk, chunk), 0)
    ss = lax.broadcasted_iota(jnp.int32, (chunk, chunk), 1)
    causal = ss <= tt
    for grp in range(C_GROUPS):
        wm = jnp.where(causal, ws_ref[grp], 0.0)
        bcol = bs_ref[:, grp:grp + 1]
        cols = slice(grp * C_GDIM, (grp + 1) * C_GDIM)
        for c in range(n_chunks):
            rows = slice(c * chunk, (c + 1) * chunk)
            vv = v_ref[rows, cols]
            if chunk >= 128:
                sv = _dot(wm.astype(BF16), vv)
            else:
                vf = vv.astype(F32)
                sv = jnp.zeros((chunk, C_GDIM), F32)
                for s in range(chunk):
                    sv = sv + wm[:, s:s + 1] * vf[s:s + 1, :]
            sv = sv + bcol
            y = u_ref[rows, cols].astype(F32) * sv * g_ref[rows, cols].astype(F32)
            y_ref[rows, cols] = y.astype(y_ref.dtype)


def spatial_gate(u, v, g, w_s, b_s_t, chunk, n_chunks):
    m = u.shape[0]
    tm = chunk * n_chunks
    row_spec = pl.BlockSpec((tm, C_WIDTH), lambda i: (i, 0))
    return pl.pallas_call(
        functools.partial(_spatial_kernel, chunk=chunk, n_chunks=n_chunks),
        grid=(m // tm,),
        in_specs=[row_spec, row_spec, row_spec,
                  pl.BlockSpec((C_GROUPS, chunk, chunk), lambda i: (0, 0, 0)),
                  pl.BlockSpec((chunk, C_GROUPS), lambda i: (0, 0))],
        out_specs=row_spec,
        out_shape=jax.ShapeDtypeStruct((m, C_WIDTH), u.dtype),
        compiler_params=_params("arbitrary"),
        name="spatial_gate",
    )(u, v, g, w_s, b_s_t)


def _even_weights(w_in, b_i, b_f):
    gate0 = 5 * A_WIDTH
    b0 = gate0 + 2 * A_HEADS
    wt = jnp.swapaxes(w_in, 0, 1)
    w_gate = jnp.pad(w_in[:, gate0:b0], ((0, 0), (0, GATE_LANES - 2 * A_HEADS)))
    bias = jnp.pad(jnp.concatenate([b_i, b_f]), (0, GATE_LANES - 2 * A_HEADS)).reshape(1, GATE_LANES)
    return wt, b0, w_gate, bias


def _mlstm_inputs(qkvog, gates, bsz, t_len, valid_len):
    qkvog = qkvog.reshape(bsz, t_len, 5 * A_WIDTH)
    gates = gates.reshape(bsz, t_len, GATE_LANES)
    t_pad = -(-t_len // A_CHUNK) * A_CHUNK
    pad = ((0, 0), (0, t_pad - t_len), (0, 0))
    if valid_len < t_pad:
        qkvog, gates = jnp.pad(qkvog, pad), jnp.pad(gates, pad)
        pos = jnp.arange(t_pad)[None, :, None]
        lane = jnp.arange(GATE_LANES)[None, None, :]
        gates = jnp.where((pos >= valid_len) & (lane < A_HEADS), NEG_BIG, gates)
        gates = jnp.where((pos >= valid_len) & (lane >= A_HEADS), 0.0, gates)
    return qkvog, gates, gates[:, :, :2 * A_HEADS].transpose(0, 2, 1)


def _even_front(xp, xs, ew, g_norm, tm):
    wt, b0, w_gate, bias = ew
    hp, gates_p = norm_gates(xp, g_norm, w_gate, bias, min(tm, 512))
    hs, gates_s = norm_gates(xs, g_norm, w_gate, bias, xs.shape[0])
    tn, tm_kv = 1024, tm
    qkvog = proj(hp, wt, 0, 5 * A_WIDTH, BF16, 2 * tm, tn, second=(hs, BF16))
    q_b = proj(hp, wt, b0, B_WIDTH, BF16, 2 * tm, tn, scale=B_DH ** -0.5, second=(hs, F32))
    g_b = proj(hp, wt, b0 + 3 * B_WIDTH, B_WIDTH, BF16, 2 * tm, tn, second=(hs, F32))
    k = kv_proj(hp, wt, b0 + B_WIDTH, tm_kv, second=hs)
    v = kv_proj(hp, wt, b0 + 2 * B_WIDTH, tm_kv, second=hs)
    groups = []
    for i, gates in enumerate((gates_p, gates_s)):
        groups.append(dict(qkvog=qkvog[i], gates=gates, q_b=q_b[i], g_b=g_b[i],
                           k_new=k[2 * i], k_bf=k[2 * i + 1], v_new=v[2 * i], v_bf=v[2 * i + 1]))
    return groups


def _even_back(xp, xs, mix_p, mix_s, w_out, next_gain, tm):
    w_out_b = w_out.astype(BF16)
    flat = lambda mix, m: [mix[0].reshape(m, A_WIDTH), mix[1].reshape(m, B_WIDTH)]
    return out_proj_norm(flat(mix_p, xp.shape[0]), [w_out_b[:A_WIDTH], w_out_b[A_WIDTH:]], xp, next_gain,
                         tm, True, BF16, second=(flat(mix_s, xs.shape[0]), xs, BF16))


def _odd_layer(xp, hp, xs, hs, w_in, v_gain, w_s, b_s, w_out_b, final_gain, tm, n_seq):
    (u, v, g), (u2, v2, g2) = odd_in(hp, w_in, v_gain, 4 * tm, BF16, second=(hs, F32))
    y2 = spatial_gate(u2, v2, g2, w_s[:, :SAMPLE_PAD, :SAMPLE_PAD], b_s[:, :SAMPLE_PAD].T, SAMPLE_PAD, n_seq)
    y_p, y_s = out_proj_norm([], [w_out_b], xp, final_gain, tm, False, F32, second=([y2], xs, F32),
                             spatial=(u, v, g, w_s, b_s.T))
    return y_p, y_s, v2


def kernel(x_prompt, x_sample, state_a_C, state_a_n, state_a_m, cache_b_k, cache_b_v, page_table,
           even_norm, even_w_in, even_b_i, even_b_f, even_b_sb, even_w_out,
           odd_norm, odd_w_in, odd_v_gain, odd_w_s, odd_b_s, odd_w_out, final_norm):
    bsz, seq, _ = x_prompt.shape
    n_seq, dec_seq, _ = x_sample.shape
    n_pool = cache_b_k.shape[1]

    ew = _even_weights(even_w_in[0], even_b_i[0], even_b_f[0])
    odd_w_in_b = odd_w_in[0]
    odd_w_out_b = odd_w_out[0].astype(BF16)

    xp = x_prompt.reshape(bsz * seq, D_MODEL)
    xs = jnp.pad(x_sample, ((0, 0), (0, SAMPLE_PAD - dec_seq), (0, 0))).reshape(n_seq * SAMPLE_PAD, D_MODEL)
    fp, fs = _even_front(xp, xs, ew, even_norm[0], 1024)
    qkvog_p, gates_p, grow_p = _mlstm_inputs(fp["qkvog"], fp["gates"], bsz, seq, seq)
    qkvog_s, gates_s, grow_s = _mlstm_inputs(fs["qkvog"], fs["gates"], n_seq, SAMPLE_PAD, dec_seq)
    q_s, g_s = (fs[name].reshape(n_seq, SAMPLE_PAD, B_WIDTH) for name in ("q_b", "g_b"))

    assert cache_b_k.shape[0] == 1 and cache_b_v.shape[0] == 1
    page_view = (1, n_pool, PAGE_SIZE * B_HEADS, B_DH)
    cache_k, cache_v = cache_b_k.reshape(page_view), cache_b_v.reshape(page_view)
    kv_pad = ((0, 0), (0, (PAGE_SIZE - SAMPLE_PAD) * B_HEADS), (0, 0))
    k_new_s = jnp.pad(fs["k_new"].reshape(n_seq, SAMPLE_PAD * B_HEADS, B_DH), kv_pad)
    v_new_s = jnp.pad(fs["v_new"].reshape(n_seq, SAMPLE_PAD * B_HEADS, B_DH), kv_pad)
    bias_rows = jnp.broadcast_to(jnp.repeat(even_b_sb[0], SAMPLE_PAD)[:, None],
                                 (B_HEADS * SAMPLE_PAD, GATE_LANES))
    att_rows = B_HEADS * SAMPLE_PAD
    rider = _PagedAttnRider(
        q_s, k_new_s, v_new_s, cache_k, cache_v, page_table, bias_rows,
        jnp.zeros((n_seq, att_rows, B_WIDTH), F32), jnp.zeros((n_seq, att_rows, GATE_LANES), F32),
        0, PAGES_PER_STEP)
    assert bsz * (seq // A_CHUNK) == n_seq * rider.steps_per_seq
    zero_state = (jnp.zeros((bsz, A_HEADS, A_DH, A_DH), F32),
                  jnp.zeros((bsz, A_HEADS, 1, A_DH), F32),
                  jnp.zeros((bsz, A_HEADS, 1, GATE_LANES), F32))
    (ha_p, c_p, n_p, m_p), (att_acc, _) = mlstm(
        qkvog_p, gates_p, grow_p, *zero_state, A_CHUNK, rider=rider)
    hb_s = attn_sample_finish(att_acc, g_s)

    as_seq = lambda a: a.reshape(bsz, seq, B_WIDTH)
    hb_p = attn_prompt(as_seq(fp["q_b"]), as_seq(fp["g_b"]), as_seq(fp["k_bf"]), as_seq(fp["v_bf"]),
                       even_b_sb[0], ATTN_BQ, ATTN_BK, ATTN_HEADS_PER_STEP)
    st_in = (state_a_C[0], state_a_n[0][:, :, None, :],
             jnp.broadcast_to(state_a_m[0][:, :, None, None], (n_seq, A_HEADS, 1, GATE_LANES)))
    ha_s, c_s, n_s, m_s_new = mlstm(qkvog_s, gates_s, grow_s, *st_in, A_CHUNK)
    ha_s = ha_s[:, :SAMPLE_PAD]

    xp1, hp1, xs1, hs1 = _even_back(xp, xs, (ha_p, hb_p), (ha_s, hb_s), even_w_out[0], odd_norm[0], 512)
    y_p, y_s, v_rows = _odd_layer(xp1, hp1, xs1, hs1, odd_w_in_b, odd_v_gain[0], odd_w_s[0], odd_b_s[0],
                                  odd_w_out_b, final_norm, 512, n_seq)

    def sample_rows(a, *dims):
        return a.reshape((n_seq, SAMPLE_PAD) + dims)[:, :dec_seq]

    return (y_p.reshape(bsz, seq, D_MODEL),
            sample_rows(y_s, D_MODEL),
            c_p[None], n_p[:, :, 0, :][None], m_p[:, :, 0, 0][None],
            c_s[None], n_s[:, :, 0, :][None], m_s_new[:, :, 0, 0][None],
            fp["k_new"].reshape(1, bsz, seq, B_HEADS, B_DH), fp["v_new"].reshape(1, bsz, seq, B_HEADS, B_DH),
            sample_rows(fs["k_new"], B_HEADS, B_DH)[None], sample_rows(fs["v_new"], B_HEADS, B_DH)[None],
            sample_rows(v_rows, C_WIDTH)[None])
```

```python
import functools

import jax
import jax.numpy as jnp
from jax import lax
from jax.experimental import pallas as pl
from jax.experimental.pallas import tpu as pltpu

F32 = jnp.float32
BF16 = jnp.bfloat16

D_MODEL = 2048
PAGE_SIZE = 128
A_HEADS = 4
A_DH = 256
A_WIDTH = A_HEADS * A_DH
A_CHUNK = 128
B_HEADS = 8
B_DH = 128
B_WIDTH = B_HEADS * B_DH
C_WIDTH = D_MODEL
C_GROUPS = 8
C_GDIM = C_WIDTH // C_GROUPS
C_CHUNK = 128
RMS_EPS = 1e-6
GATE_LANES = 128
NEG_BIG = -1e30
SAMPLE_PAD = 8
SAMPLE_CHUNK = 16
ATTN_BQ = 512
ATTN_BK = 512
CUMSUM_BLOCK = 256
ATTN_HEADS_PER_STEP = 2
PAGES_PER_STEP = 16

VMEM_LIMIT_BYTES = 56 * 1024 * 1024


def _params(*sem):
    return pltpu.CompilerParams(dimension_semantics=sem, vmem_limit_bytes=VMEM_LIMIT_BYTES)


def _dot(a, b):
    return jnp.dot(a, b, preferred_element_type=F32)


def _dot_nt(a, b):
    return lax.dot_general(a, b, (((1,), (1,)), ((), ())), preferred_element_type=F32)


def _dot_tn(a, b):
    return lax.dot_general(a, b, (((0,), (0,)), ((), ())), preferred_element_type=F32)


def _softplus(z):
    return jnp.maximum(z, 0.0) + jnp.log(1.0 + jnp.exp(-jnp.abs(z)))


def _sigmoid(z):
    return 1.0 / (1.0 + jnp.exp(-z))


def _silu(z):
    return z * _sigmoid(z)


def _gelu_tanh(x):
    c = 0.7978845608028654
    return x * (0.5 * (1.0 + jnp.tanh(c * (x + 0.044715 * (x * x * x)))))


def _rms(x, g):
    return x * lax.rsqrt(jnp.mean(x * x, axis=-1, keepdims=True) + RMS_EPS) * g


def _split_hi_lo(x):
    hi = x.astype(BF16)
    lo = (x - hi.astype(F32)).astype(BF16)
    return hi, lo


def _norm_gates_kernel(x_ref, g_ref, w_ref, bias_ref, h_ref, gate_ref):
    h = _rms(x_ref[...], g_ref[...])
    h_hi, h_lo = _split_hi_lo(h)
    h_ref[...] = h_hi
    a = _dot(h_hi, w_ref[...])
    b = _dot(h_lo, w_ref[...])
    pre = a[:, :GATE_LANES] + a[:, GATE_LANES:] + b[:, :GATE_LANES] + bias_ref[...]
    lane = lax.broadcasted_iota(jnp.int32, pre.shape, 1)
    is_forget = (lane >= A_HEADS) & (lane < 2 * A_HEADS)
    gate_ref[...] = jnp.where(is_forget, -_softplus(-pre), pre)


def norm_gates(x, gain, w_gate, bias, tm):
    m = x.shape[0]
    w_hi_lo = jnp.concatenate(_split_hi_lo(w_gate), axis=1)
    return pl.pallas_call(
        _norm_gates_kernel,
        grid=(m // tm,),
        in_specs=[pl.BlockSpec((tm, D_MODEL), lambda i: (i, 0)),
                  pl.BlockSpec((1, D_MODEL), lambda i: (0, 0)),
                  pl.BlockSpec((D_MODEL, 2 * GATE_LANES), lambda i: (0, 0)),
                  pl.BlockSpec((1, GATE_LANES), lambda i: (0, 0))],
        out_specs=[pl.BlockSpec((tm, D_MODEL), lambda i: (i, 0)),
                   pl.BlockSpec((tm, GATE_LANES), lambda i: (i, 0))],
        out_shape=[jax.ShapeDtypeStruct((m, D_MODEL), BF16),
                   jax.ShapeDtypeStruct((m, GATE_LANES), F32)],
        compiler_params=_params("arbitrary"),
        name="norm_gates",
    )(x, gain.reshape(1, D_MODEL), w_hi_lo, bias)


def _hosted_kernel(*refs, n_in, n_out, phases_fn, lin, rank, rider):
    if rider is not None:
        refs = refs[1:]
    r_in, r_out = (rider.n_in, rider.n_out) if rider is not None else (0, 0)
    ins, refs = refs[:n_in], refs[n_in:]
    r_ins, refs = refs[:r_in], refs[r_in:]
    outs, refs = refs[:n_out], refs[n_out:]
    r_outs, refs = refs[:r_out], refs[r_out:]
    n_sc = len(refs) - (rider.n_scratch if rider is not None else 0)
    scratch, r_scratch = refs[:n_sc], refs[n_sc:]
    sets = [phases_fn(ins, outs, scratch)]
    if rider is not None:
        step = lin(*[pl.program_id(d) for d in range(rank)])
        sets.append(rider.phases(r_ins, r_outs, r_scratch, step))
    for pre, _, _ in sets:
        for cond, fn in pre:
            pl.when(cond)(fn)
    for _, parts, _ in sets:
        for part in parts:
            part()
    for _, _, post in sets:
        for cond, fn in post:
            pl.when(cond)(fn)


def _hosted_call(name, grid, lin, in_specs, out_specs, out_shape, scratch, phases_fn, args, rider=None):
    n_in, n_out = len(in_specs), len(out_specs)
    aliases = {}
    if rider is not None:
        r = rider.specs(lin, len(grid))
        aliases = {1 + n_in + i: n_out + o for i, o in r["aliases"].items()}
        in_specs, out_specs = in_specs + r["in_specs"], out_specs + r["out_specs"]
        out_shape, scratch = out_shape + r["out_shape"], scratch + r["scratch"]
        args = [rider.page_table] + list(args) + r["args"]
    kern = functools.partial(_hosted_kernel, n_in=n_in, n_out=n_out, phases_fn=phases_fn,
                             lin=lin, rank=len(grid), rider=rider)
    grid_spec = pltpu.PrefetchScalarGridSpec(
        num_scalar_prefetch=0 if rider is None else 1, grid=grid,
        in_specs=in_specs, out_specs=out_specs, scratch_shapes=scratch)
    outs = pl.pallas_call(
        kern, grid_spec=grid_spec, out_shape=out_shape, input_output_aliases=aliases,
        compiler_params=_params(*(["arbitrary"] * len(grid))), name=name,
    )(*args)
    return outs[:n_out], outs[n_out:]


def _two_group_steps(n_i, has_second):
    if not has_second:
        return n_i, (lambda i: i)
    return n_i + 1, (lambda i: jnp.maximum(i - 1, 0))


def proj(a, wt, row0, n_out, out_dtype, tm, tn, scale=None, second=None):
    m, k = a.shape
    n_i = m // tm
    n_steps, tile = _two_group_steps(n_i, second is not None)

    def phases(ins, outs, scratch):
        a_ref, wt_ref = ins[:2]
        o_ref, (wbf,) = outs[0], scratch
        i = pl.program_id(1)

        def cast():
            wbf[...] = wt_ref[...].astype(BF16)

        def main():
            y = _dot_nt(a_ref[...], wbf[...])
            o_ref[...] = (y if scale is None else y * scale).astype(o_ref.dtype)

        if second is None:
            return [(i == 0, cast)], [main], []

        def small():
            outs[1][...] = _dot_nt(ins[2][...], wbf[...]).astype(outs[1].dtype)

        return [(i == 0, cast), (i == 0, small), (i > 0, main)], [], []

    in_specs = [pl.BlockSpec((tm, k), lambda j, i, *_: (tile(i), 0)),
                pl.BlockSpec((pl.Element(tn), pl.Element(k)),
                             lambda j, i, *_: (pl.multiple_of(row0 + j * tn, 8), 0))]
    out_specs = [pl.BlockSpec((tm, tn), lambda j, i, *_: (tile(i), j))]
    out_shape = [jax.ShapeDtypeStruct((m, n_out), out_dtype)]
    args = [a, wt]
    if second is not None:
        a2, dtype2 = second
        in_specs.append(pl.BlockSpec(a2.shape, lambda j, i, *_: (0, 0)))
        out_specs.append(pl.BlockSpec((a2.shape[0], tn), lambda j, i, *_: (0, j)))
        out_shape.append(jax.ShapeDtypeStruct((a2.shape[0], n_out), dtype2))
        args.append(a2)
    outs, _ = _hosted_call("proj", (n_out // tn, n_steps), lambda j, i: j * n_steps + i,
                           in_specs, out_specs, out_shape, [pltpu.VMEM((tn, k), BF16)], phases, args)
    return outs[0] if second is None else tuple(outs)


def kv_proj(a, wt, row0, tm, second=None):
    m, k = a.shape
    n_i = m // tm
    n_steps, tile = _two_group_steps(n_i, second is not None)

    def phases(ins, outs, scratch):
        a_ref, wt_ref = ins[:2]
        (wbf,) = scratch
        i = pl.program_id(0)

        def cast():
            wbf[...] = wt_ref[...].astype(BF16)

        def rows_to(a_rows_ref, o_ref, obf_ref):
            def run():
                n_rows = a_rows_ref.shape[0]
                y = _dot_nt(a_rows_ref[...], wbf[...])
                obf_ref[...] = y.astype(BF16)
                for h in range(B_HEADS):
                    o_ref[pl.ds(h, n_rows, stride=B_HEADS), :] = y[:, h * B_DH:(h + 1) * B_DH]
            return run

        main = rows_to(a_ref, outs[0], outs[1])
        if second is None:
            return [(i == 0, cast)], [main], []
        return [(i == 0, cast), (i == 0, rows_to(ins[2], outs[2], outs[3])), (i > 0, main)], [], []

    def out_pair(rows, index):
        return ([pl.BlockSpec((rows * B_HEADS, B_DH), index), pl.BlockSpec((rows, B_WIDTH), index)],
                lambda total: [jax.ShapeDtypeStruct((total * B_HEADS, B_DH), F32),
                               jax.ShapeDtypeStruct((total, B_WIDTH), BF16)])

    in_specs = [pl.BlockSpec((tm, k), lambda i, *_: (tile(i), 0)),
                pl.BlockSpec((pl.Element(B_WIDTH), pl.Element(k)), lambda i, *_: (row0, 0))]
    out_specs, shapes = out_pair(tm, lambda i, *_: (tile(i), 0))
    out_shape = shapes(m)
    args = [a, wt]
    if second is not None:
        m2 = second.shape[0]
        in_specs.append(pl.BlockSpec(second.shape, lambda i, *_: (0, 0)))
        specs2, shapes2 = out_pair(m2, lambda i, *_: (0, 0))
        out_specs, out_shape = out_specs + specs2, out_shape + shapes2(m2)
        args.append(second)
    outs, _ = _hosted_call("kv_proj", (n_steps,), lambda i: i, in_specs, out_specs, out_shape,
                           [pltpu.VMEM((B_WIDTH, k), BF16)], phases, args)
    return tuple(outs)


def out_proj_norm(lhs, ws, x, gain, tm, emit_x, norm_dtype, second=None, spatial=None):
    m = x.shape[0]
    n_lhs, n_w = len(lhs), len(ws)
    n_i = m // tm
    n_steps, tile = _two_group_steps(n_i, second is not None)
    n_out = 2 if emit_x else 1

    def phases(ins, outs, scratch):
        w_refs = ins[n_lhs:n_lhs + n_w]
        x_ref, g_ref = ins[n_lhs + n_w], ins[n_lhs + n_w + 1]
        rest = ins[n_lhs + n_w + 2:]
        i = pl.program_id(0)

        def group(make_lhs, a_refs, x_ref, out_refs, row_chunk):
            def run():
                if make_lhs is not None:
                    make_lhs()
                for r in range(0, x_ref.shape[0], row_chunk):
                    rows = slice(r, r + row_chunk)
                    y = x_ref[rows, :]
                    for a_ref, w_ref in zip(a_refs, w_refs):
                        y = y + _dot(a_ref[rows, :].astype(BF16), w_ref[...])
                    if emit_x:
                        out_refs[0][rows, :] = y
                    out_refs[-1][rows, :] = _rms(y, g_ref[...]).astype(out_refs[-1].dtype)
            return run

        if second is not None:
            a2_refs, x2_ref, rest = rest[:n_w], rest[n_w], rest[n_w + 1:]
        if spatial is None:
            main = group(None, ins[:n_lhs], x_ref, outs[:n_out], min(tm, 256))
        else:
            (y_sc,) = scratch
            fill = functools.partial(_spatial_kernel, *rest[:5], y_sc, chunk=C_CHUNK, n_chunks=tm // C_CHUNK)
            main = group(fill, [y_sc], x_ref, outs[:n_out], min(tm, 256))
        if second is None:
            return [], [main], []
        small = group(None, a2_refs, x2_ref, outs[n_out:], x2_ref.shape[0])
        return [(i == 0, small), (i > 0, main)], [], []

    row_spec = lambda width: pl.BlockSpec((tm, width), lambda i, *_: (tile(i), 0))
    whole = lambda arr: pl.BlockSpec(arr.shape, lambda i, *_: (0,) * arr.ndim)
    gain2d = gain.reshape(1, D_MODEL)
    in_specs = ([row_spec(a.shape[1]) for a in lhs] + [whole(w) for w in ws]
                + [row_spec(D_MODEL), whole(gain2d)])
    out_specs = [row_spec(D_MODEL)] * n_out
    out_shape = ([jax.ShapeDtypeStruct((m, D_MODEL), F32)] if emit_x else []) \
        + [jax.ShapeDtypeStruct((m, D_MODEL), norm_dtype)]
    args = [*lhs, *ws, x, gain2d]
    scratch = []
    if second is not None:
        lhs2, x2, norm_dtype2 = second
        m2 = x2.shape[0]
        in_specs += [whole(a) for a in lhs2] + [whole(x2)]
        out_specs += [pl.BlockSpec((m2, D_MODEL), lambda i, *_: (0, 0))] * n_out
        out_shape += ([jax.ShapeDtypeStruct((m2, D_MODEL), F32)] if emit_x else []) \
            + [jax.ShapeDtypeStruct((m2, D_MODEL), norm_dtype2)]
        args += [*lhs2, x2]
    if spatial is not None:
        u, v, gate, w_s, b_s_t = spatial
        in_specs += [row_spec(C_WIDTH)] * 3 + [whole(w_s), whole(b_s_t)]
        args += [u, v, gate, w_s, b_s_t]
        scratch = [pltpu.VMEM((tm, C_WIDTH), BF16)]
    outs, _ = _hosted_call("out_proj_norm", (n_steps,), lambda i: i, in_specs, out_specs, out_shape,
                           scratch, phases, args)
    return list(outs)


def _mlstm_chunk(q_ref, k_ref, v_ref, og_ref, gg_ref, gcol_ref, grow_ref, h_ref, c_sc, n_sc, m_sc, chunk, heads):
    L = chunk
    gcol = gcol_ref[0]
    grow = grow_ref[0]
    tt = lax.broadcasted_iota(jnp.int32, (L, L), 0)
    ss = lax.broadcasted_iota(jnp.int32, (L, L), 1)
    causal = ss <= tt

    for head in heads:
        cols = slice(head * A_DH, (head + 1) * A_DH)
        q = q_ref[0, :, cols]
        ks = k_ref[0, :, cols] * jnp.asarray(A_DH ** -0.5, BF16)
        v = v_ref[0, :, cols]
        ig_col = gcol[:, head:head + 1]
        lf_col = gcol[:, head + A_HEADS:head + A_HEADS + 1]
        ig_row = grow[head:head + 1, :]
        lf_row = grow[head + A_HEADS:head + A_HEADS + 1, :]
        b_col = jnp.sum(jnp.where(causal, lf_row, 0.0), axis=1, keepdims=True)
        b_row = jnp.sum(jnp.where(tt <= ss, lf_col, 0.0), axis=0, keepdims=True)
        b_last = jnp.sum(lf_row, axis=1, keepdims=True)

        m0 = m_sc[head][:, :1]
        n0 = n_sc[head]
        c0 = c_sc[head]

        d = jnp.where(causal, b_col - b_row + ig_row, NEG_BIG)
        m_carry = b_col + m0
        m = jnp.maximum(m_carry, jnp.max(d, axis=1, keepdims=True))
        w_intra = jnp.exp(d - m)
        w_carry = jnp.exp(m_carry - m)
        s = _dot_nt(q, ks) * w_intra
        qf = q.astype(F32)
        num = _dot(s.astype(BF16), v) + w_carry * _dot_nt(q, c0.astype(BF16))
        den = jnp.sum(s, axis=1, keepdims=True) + w_carry * jnp.sum(qf * n0, axis=1, keepdims=True)
        h = num / jnp.maximum(jnp.abs(den), jnp.exp(-m))
        gated = h * _sigmoid(og_ref[0, :, cols].astype(F32)) * _silu(gg_ref[0, :, cols].astype(F32))
        h_ref[0, :, cols] = gated.astype(h_ref.dtype)

        m_carry_last = b_last + m0
        d_last_row = b_last - b_row + ig_row
        m_new = jnp.maximum(m_carry_last, jnp.max(d_last_row, axis=1, keepdims=True))
        wc_last = jnp.exp(m_carry_last - m_new)
        w_last_col = jnp.exp(b_last - b_col + ig_col - m_new)
        vw = (v.astype(F32) * w_last_col).astype(BF16)
        c_new = wc_last * c0 + _dot_tn(vw, ks)
        n_new = wc_last * n0 + jnp.sum(ks.astype(F32) * w_last_col, axis=0, keepdims=True)
        c_sc[head] = c_new
        n_sc[head] = n_new
        m_sc[head] = jnp.broadcast_to(m_new, (1, GATE_LANES))


def mlstm(qkvog, gcol, grow, c0, n0, m0, chunk, rider=None):
    bsz, t_len, _ = qkvog.shape
    nc = t_len // chunk
    hd = A_HEADS

    def phases(ins, outs, state):
        c0_ref, n0_ref, m0_ref = ins[7:]
        h_ref, c_out_ref, n_out_ref, m_out_ref = outs
        c_sc, n_sc, m_sc = state
        ci = pl.program_id(1)

        def init():
            c_sc[...] = c0_ref[0]
            n_sc[...] = n0_ref[0]
            m_sc[...] = m0_ref[0]

        def head_part(head):
            return lambda: _mlstm_chunk(*ins[:7], h_ref, c_sc, n_sc, m_sc, chunk, (head,))

        parts = [head_part(head) for head in range(A_HEADS)]

        def final():
            c_out_ref[0] = c_sc[...]
            n_out_ref[0] = n_sc[...]
            m_out_ref[0] = m_sc[...]

        return [(ci == 0, init)], parts, [(ci == nc - 1, final)]

    blk = lambda seg: pl.BlockSpec((1, chunk, A_WIDTH), lambda b, c, *_, seg=seg: (b, c, seg))
    st4 = lambda r, w: pl.BlockSpec((1, hd, r, w), lambda b, c, *_: (b, 0, 0, 0))
    outs, rest = _hosted_call(
        "mlstm", (bsz, nc), lambda b, c: b * nc + c,
        [blk(0), blk(1), blk(2), blk(3), blk(4),
         pl.BlockSpec((1, chunk, GATE_LANES), lambda b, c, *_: (b, c, 0)),
         pl.BlockSpec((1, 8, chunk), lambda b, c, *_: (b, 0, c)),
         st4(A_DH, A_DH), st4(1, A_DH), st4(1, GATE_LANES)],
        [pl.BlockSpec((1, chunk, A_WIDTH), lambda b, c, *_: (b, c, 0)),
         st4(A_DH, A_DH), st4(1, A_DH), st4(1, GATE_LANES)],
        [jax.ShapeDtypeStruct((bsz, t_len, A_WIDTH), BF16),
         jax.ShapeDtypeStruct((bsz, hd, A_DH, A_DH), F32),
         jax.ShapeDtypeStruct((bsz, hd, 1, A_DH), F32),
         jax.ShapeDtypeStruct((bsz, hd, 1, GATE_LANES), F32)],
        [pltpu.VMEM((hd, A_DH, A_DH), F32), pltpu.VMEM((hd, 1, A_DH), F32),
         pltpu.VMEM((hd, 1, GATE_LANES), F32)],
        phases, [qkvog, qkvog, qkvog, qkvog, qkvog, gcol, grow, c0, n0, m0], rider)
    return tuple(outs) if rider is None else (tuple(outs), rest)


def _stick_block(q, kb, vb, bias, run, mask, upper):
    rows = q.shape[0]
    sub = upper.shape[0]
    n_sub = kb.shape[0] // sub
    z = _dot_nt(q, kb) if bias is None else _dot_nt(q, kb) * (B_DH ** -0.5) + bias
    sp = _softplus(z)
    spm = sp if mask is None else jnp.where(mask, sp, 0.0)
    hi, lo = _split_hi_lo(spm)
    laters = [None] * n_sub
    total = None
    for i in reversed(range(n_sub)):
        ln = slice(i * sub, (i + 1) * sub)
        both = _dot(jnp.concatenate([hi[:, ln], lo[:, ln]], axis=0), upper)
        carry = run if total is None else run + total
        laters[i] = both[:rows] + both[rows:] + carry
        part = jnp.sum(spm[:, ln], axis=1, keepdims=True)
        total = part if total is None else total + part
    later = laters[0] if n_sub == 1 else jnp.concatenate(laters, axis=1)
    a = jnp.exp(z - sp - later)
    if mask is not None:
        a = jnp.where(mask, a, 0.0)
    return _dot(a.astype(BF16), vb), total


def _strict_upper(n):
    j = lax.broadcasted_iota(jnp.int32, (n, n), 0)
    s = lax.broadcasted_iota(jnp.int32, (n, n), 1)
    return jnp.where(j > s, 1.0, 0.0).astype(BF16)


def _attn_prompt_kernel(bias_ref, q_ref, k_ref, v_ref, g_ref, o_ref, *, bq, bk, n_heads):
    head0 = pl.program_id(1) * n_heads
    qi = pl.program_id(2)
    kbf = k_ref.at[0]
    vbf = v_ref.at[0]
    upper = _strict_upper(min(bk, CUMSUM_BLOCK))
    row = lax.broadcasted_iota(jnp.int32, (bq, bk), 0)
    col = lax.broadcasted_iota(jnp.int32, (bq, bk), 1)
    lanes = [slice(h * B_DH, (h + 1) * B_DH) for h in range(n_heads)]
    lane_q = lax.broadcasted_iota(jnp.int32, (bq, B_DH), 1)
    lane_k = lax.broadcasted_iota(jnp.int32, (bk, B_DH), 1)
    ones_cols = jnp.where(lane_q < 2, 1.0, 0.0).astype(BF16)
    qs = [jnp.concatenate([q_ref[0, :, ln], ones_cols], axis=1) for ln in lanes]
    bias_cols = []
    for h in range(n_heads):
        b = jnp.full((bk, B_DH), bias_ref[head0 + h], F32)
        b_hi = b.astype(BF16).astype(F32)
        bias_cols.append(jnp.where(lane_k == 0, b_hi, jnp.where(lane_k == 1, b - b_hi, 0.0)).astype(BF16))

    def blocks(kj, runs, mask):
        start = pl.multiple_of(kj * bk, bk)
        return [_stick_block(qs[h], jnp.concatenate([kbf[pl.ds(start, bk), ln], bias_cols[h]], axis=1),
                             vbf[pl.ds(start, bk), ln], None, runs[h], mask, upper)
                for h, ln in enumerate(lanes)]

    q0 = qi * bq
    n_full = q0 // bk
    if bq == bk:
        half = bq // 2
        start = pl.multiple_of(qi * bq, bq)
        accs, runs = [], []
        for h, ln in enumerate(lanes):
            kb = jnp.concatenate([kbf[pl.ds(start, bk), ln], bias_cols[h]], axis=1)
            vb = vbf[pl.ds(start, bk), ln]
            zero = jnp.zeros((half, 1), F32)
            iota = lambda shape, dim: lax.broadcasted_iota(jnp.int32, shape, dim)
            top = _stick_block(qs[h][:half], kb[:half], vb[:half], None, zero,
                               iota((half, half), 1) < iota((half, half), 0), upper)
            bot = _stick_block(qs[h][half:], kb, vb, None, zero,
                               iota((half, bk), 1) < iota((half, bk), 0) + half, upper)
            accs.append(jnp.concatenate([top[0], bot[0]], axis=0))
            runs.append(jnp.concatenate([top[1], bot[1]], axis=0))
    else:
        accs = [jnp.zeros((bq, B_DH), F32)] * n_heads
        runs = [jnp.zeros((bq, 1), F32)] * n_heads
        for m in reversed(range(max(1, bq // bk))):
            kj = n_full + m
            res = blocks(kj, runs, col + (kj * bk - q0) < row)
            accs = [a + c for a, (c, _) in zip(accs, res)]
            runs = [r + t for r, (_, t) in zip(runs, res)]

    def body(it, carry):
        accs, runs = carry
        res = blocks(n_full - 1 - it, runs, None)
        return (tuple(a + c for a, (c, _) in zip(accs, res)),
                tuple(r + t for r, (_, t) in zip(runs, res)))

    accs, runs = lax.fori_loop(0, n_full, body, (tuple(accs), tuple(runs)))
    for h, ln in enumerate(lanes):
        o_ref[0, :, ln] = (accs[h] * _silu(g_ref[0, :, ln].astype(F32))).astype(o_ref.dtype)


def attn_prompt(q, g, k, v, b_sb, bq, bk, n_heads):
    bsz, t_len, _ = k.shape
    width = n_heads * B_DH
    q_spec = pl.BlockSpec((1, bq, width), lambda b, h, i: (b, i, h))
    kv_spec = pl.BlockSpec((1, t_len, width), lambda b, h, i: (b, 0, h))
    return pl.pallas_call(
        functools.partial(_attn_prompt_kernel, bq=bq, bk=bk, n_heads=n_heads),
        grid=(bsz, B_HEADS // n_heads, t_len // bq),
        in_specs=[pl.BlockSpec(memory_space=pltpu.SMEM), q_spec, kv_spec, kv_spec, q_spec],
        out_specs=q_spec,
        out_shape=jax.ShapeDtypeStruct((bsz, t_len, B_WIDTH), BF16),
        compiler_params=_params("arbitrary", "arbitrary", "arbitrary"),
        name="attn_prompt",
    )(b_sb, q, k, v, g)


def _attn_sample_phases(ins, outs, scratch, n_group):
    bias_ref, q_ref, knew_ref, vnew_ref, acc_in_ref, run_in_ref = ins[:6]
    k_refs = ins[6:6 + n_group]
    v_refs = ins[6 + n_group:]
    acc_out_ref, run_out_ref = outs
    qbd, acc, run, kcat, vcat = scratch
    rows = B_HEADS * SAMPLE_PAD
    upper = _strict_upper(PAGE_SIZE)
    bias = bias_ref[...][:, :1]

    def repack(page, dst, i):
        for h in range(B_HEADS):
            dst[i * PAGE_SIZE:(i + 1) * PAGE_SIZE, h * B_DH:(h + 1) * B_DH] = page(h).astype(BF16)

    def step(slot0, n_blk, mask):
        keys = slice(slot0 * PAGE_SIZE, (slot0 + n_blk) * PAGE_SIZE)
        z = _dot_nt(qbd[...], kcat[keys, :]) * (B_DH ** -0.5) + bias
        sp = _softplus(z)
        spm = sp if mask is None else jnp.where(mask, sp, 0.0)
        hi, lo = _split_hi_lo(spm)
        carry = run[...][:, :1]
        laters = []
        for i in range(n_blk):
            ln = slice(i * PAGE_SIZE, (i + 1) * PAGE_SIZE)
            both = _dot(jnp.concatenate([hi[:, ln], lo[:, ln]], axis=0), upper)
            laters.append(both[:rows] + both[rows:] + carry)
            carry = carry + jnp.sum(spm[:, ln], axis=1, keepdims=True)
        later = laters[0] if n_blk == 1 else jnp.concatenate(laters, axis=1)
        a = jnp.exp(z - sp - later)
        if mask is not None:
            a = jnp.where(mask, a, 0.0)
        acc[...] += _dot(a.astype(BF16), vcat[keys, :])
        run[...] = jnp.broadcast_to(carry, run.shape)

    def build_queries():
        r = lax.broadcasted_iota(jnp.int32, (rows, B_WIDTH), 0)
        c = lax.broadcasted_iota(jnp.int32, (rows, B_WIDTH), 1)
        q_rep = jnp.concatenate([q_ref[0].astype(F32)] * B_HEADS, axis=0)
        qbd[...] = jnp.where((r // SAMPLE_PAD) == (c // B_DH), q_rep, 0.0).astype(BF16)

    def start_sequence():
        build_queries()
        acc[...] = jnp.zeros_like(acc)
        run[...] = jnp.zeros_like(run)
        repack(lambda h: knew_ref[0, pl.ds(h, PAGE_SIZE, stride=B_HEADS), :], kcat, 0)
        repack(lambda h: vnew_ref[0, pl.ds(h, PAGE_SIZE, stride=B_HEADS), :], vcat, 0)
        t = lax.broadcasted_iota(jnp.int32, (rows, PAGE_SIZE), 0) % SAMPLE_PAD
        s = lax.broadcasted_iota(jnp.int32, (rows, PAGE_SIZE), 1)
        step(0, 1, s < t)

    def resume_sequence():
        build_queries()
        acc[...] = acc_in_ref[0]
        run[...] = run_in_ref[0]

    def pages():
        for i in range(n_group):
            repack(lambda h, r=k_refs[i]: r[0, 0, pl.ds(h, PAGE_SIZE, stride=B_HEADS), :], kcat, i)
            repack(lambda h, r=v_refs[i]: r[0, 0, pl.ds(h, PAGE_SIZE, stride=B_HEADS), :], vcat, i)
        step(0, n_group, None)
        acc_out_ref[0] = acc[...]
        run_out_ref[0] = run[...]

    return start_sequence, resume_sequence, pages


class _PagedAttnRider:
    n_out = 2
    n_scratch = 5

    def __init__(self, q, k_new, v_new, cache_k, cache_v, page_table, bias_rows, acc, run, first_step, n_group):
        self.arrays = (bias_rows, q, k_new, v_new, acc, run)
        self.caches = (cache_k, cache_v)
        self.page_table = page_table
        self.first_step = first_step
        self.n_group = n_group
        self.n_in = 6 + 2 * n_group
        self.steps_per_seq = page_table.shape[1] // n_group

    def specs(self, lin, rank):
        n_group, spq = self.n_group, self.steps_per_seq
        n_pages = self.page_table.shape[1]
        rows = B_HEADS * SAMPLE_PAD
        page_rows = PAGE_SIZE * B_HEADS
        gstep = lambda a: self.first_step + lin(*a[:rank])
        seq_map = lambda *a: (gstep(a) // spq, 0, 0)

        def page_spec(i):
            def index(*a):
                g, pt = gstep(a), a[rank]
                return (0, pt[g // spq, n_pages - 1 - ((g % spq) * n_group + i)], 0, 0)
            return pl.BlockSpec((1, 1, page_rows, B_DH), index)

        acc_spec = pl.BlockSpec((1, rows, B_WIDTH), seq_map)
        run_spec = pl.BlockSpec((1, rows, GATE_LANES), seq_map)
        new_spec = pl.BlockSpec((1, page_rows, B_DH), seq_map)
        acc, run = self.arrays[4:]
        return dict(
            in_specs=[pl.BlockSpec((rows, GATE_LANES), lambda *a: (0, 0)),
                      pl.BlockSpec((1, SAMPLE_PAD, B_WIDTH), seq_map), new_spec, new_spec,
                      acc_spec, run_spec] + [page_spec(i) for i in range(n_group)] * 2,
            out_specs=[acc_spec, run_spec],
            out_shape=[jax.ShapeDtypeStruct(acc.shape, F32), jax.ShapeDtypeStruct(run.shape, F32)],
            scratch=[pltpu.VMEM((rows, B_WIDTH), BF16), pltpu.VMEM((rows, B_WIDTH), F32),
                     pltpu.VMEM((rows, GATE_LANES), F32),
                     pltpu.VMEM((n_group * PAGE_SIZE, B_WIDTH), BF16),
                     pltpu.VMEM((n_group * PAGE_SIZE, B_WIDTH), BF16)],
            args=list(self.arrays) + [self.caches[0]] * n_group + [self.caches[1]] * n_group,
            aliases={4: 0, 5: 1},
        )

    def phases(self, ins, outs, scratch, local_step):
        start, resume, pages = _attn_sample_phases(ins, outs, scratch, self.n_group)
        p = (self.first_step + local_step) % self.steps_per_seq
        return [(p == 0, start), ((local_step == 0) & (p != 0), resume)], [pages], []


def attn_sample_finish(acc, g):
    n_seq = acc.shape[0]

    def kern(acc_ref, g_ref, o_ref):
        a = acc_ref[0]
        c = lax.broadcasted_iota(jnp.int32, (SAMPLE_PAD, B_WIDTH), 1) // B_DH
        out = jnp.zeros((SAMPLE_PAD, B_WIDTH), F32)
        for h in range(B_HEADS):
            out = out + jnp.where(c == h, a[h * SAMPLE_PAD:(h + 1) * SAMPLE_PAD, :], 0.0)
        o_ref[0] = out * _silu(g_ref[0])

    return pl.pallas_call(
        kern, grid=(n_seq,),
        in_specs=[pl.BlockSpec((1, B_HEADS * SAMPLE_PAD, B_WIDTH), lambda s: (s, 0, 0)),
                  pl.BlockSpec((1, SAMPLE_PAD, B_WIDTH), lambda s: (s, 0, 0))],
        out_specs=pl.BlockSpec((1, SAMPLE_PAD, B_WIDTH), lambda s: (s, 0, 0)),
        out_shape=jax.ShapeDtypeStruct((n_seq, SAMPLE_PAD, B_WIDTH), F32),
        compiler_params=_params("arbitrary"), name="attn_sample_finish",
    )(acc, g)


def proj_act(h, w, col0, v_gain, act, tm, tn, out_dtype, second=None):
    m = h.shape[0]
    j0 = col0 // tn
    n_col = C_WIDTH // tn
    n_i = m // tm
    n_steps, tile = _two_group_steps(n_i, second is not None)

    def phases(ins, outs, scratch):
        a_ref, w_ref, vg_ref = ins[:3]
        (wbf,) = scratch
        i = pl.program_id(1)

        def cast():
            wbf[...] = w_ref[...].astype(BF16)

        def group(rows_ref, o_ref, row_chunk):
            def run():
                for r in range(0, rows_ref.shape[0], row_chunk):
                    rows = slice(r, r + row_chunk)
                    y = _dot(rows_ref[rows, :], wbf[...])
                    if act == "gelu":
                        y = _gelu_tanh(y)
                    elif act == "silu":
                        y = _silu(y)
                    else:
                        y = _rms(_gelu_tanh(y), vg_ref[...])
                    o_ref[rows, :] = y.astype(o_ref.dtype)
            return run

        main = group(a_ref, outs[0], min(tm, 256))
        if second is None:
            return [(i == 0, cast)], [main], []
        small = group(ins[3], outs[1], ins[3].shape[0])
        return [(i == 0, cast), (i == 0, small), (i > 0, main)], [], []

    w_mode = dict(pipeline_mode=pl.Buffered(1)) if n_col == 1 else {}
    in_specs = [pl.BlockSpec((tm, D_MODEL), lambda j, i, *_: (tile(i), 0)),
                pl.BlockSpec((D_MODEL, tn), lambda j, i, *_: (0, j0 + j), **w_mode),
                pl.BlockSpec((1, tn), lambda j, i, *_: (0, j))]
    out_specs = [pl.BlockSpec((tm, tn), lambda j, i, *_: (tile(i), j))]
    out_shape = [jax.ShapeDtypeStruct((m, C_WIDTH), out_dtype)]
    args = [h, w, v_gain.reshape(1, C_WIDTH)]
    if second is not None:
        h2, dtype2 = second
        in_specs.append(pl.BlockSpec(h2.shape, lambda j, i, *_: (0, 0)))
        out_specs.append(pl.BlockSpec((h2.shape[0], tn), lambda j, i, *_: (0, j)))
        out_shape.append(jax.ShapeDtypeStruct((h2.shape[0], C_WIDTH), dtype2))
        args.append(h2)
    outs, _ = _hosted_call("proj_" + act, (n_col, n_steps), lambda j, i: j * n_steps + i,
                           in_specs, out_specs, out_shape, [pltpu.VMEM((D_MODEL, tn), BF16)], phases, args)
    return outs[0] if second is None else tuple(outs)


def odd_in(h, w, v_gain, tm, act_dtype, second=None):
    u = proj_act(h, w, 0, v_gain, "gelu", tm, 1024, act_dtype, second)
    v = proj_act(h, w, C_WIDTH, v_gain, "gelu_rms", min(tm, 512), C_WIDTH, act_dtype, second)
    g = proj_act(h, w, 2 * C_WIDTH, v_gain, "silu", tm, 1024, act_dtype, second)
    if second is None:
        return u, v, g
    return (u[0], v[0], g[0]), (u[1], v[1], g[1])


def _spatial_kernel(u_ref, v_ref, g_ref, ws_ref, bs_ref, y_ref, *, chunk, n_chunks):
    tt = lax.broadcasted_iota(jnp.int32, (chunk, chunk), 0)
    ss = lax.broadcasted_iota(jnp.int32, (chunk, chunk), 1)
    causal = ss <= tt
    for grp in range(C_GROUPS):
        wm = jnp.where(causal, ws_ref[grp], 0.0)
        bcol = bs_ref[:, grp:grp + 1]
        cols = slice(grp * C_GDIM, (grp + 1) * C_GDIM)
        for c in range(n_chunks):
            rows = slice(c * chunk, (c + 1) * chunk)
            vv = v_ref[rows, cols]
            if chunk >= 128:
                sv = _dot(wm.astype(BF16), vv)
            else:
                vf = vv.astype(F32)
                sv = jnp.zeros((chunk, C_GDIM), F32)
                for s in range(chunk):
                    sv = sv + wm[:, s:s + 1] * vf[s:s + 1, :]
            sv = sv + bcol
            y = u_ref[rows, cols].astype(F32) * sv * g_ref[rows, cols].astype(F32)
            y_ref[rows, cols] = y.astype(y_ref.dtype)


def spatial_gate(u, v, g, w_s, b_s_t, chunk, n_chunks):
    m = u.shape[0]
    tm = chunk * n_chunks
    row_spec = pl.BlockSpec((tm, C_WIDTH), lambda i: (i, 0))
    return pl.pallas_call(
        functools.partial(_spatial_kernel, chunk=chunk, n_chunks=n_chunks),
        grid=(m // tm,),
        in_specs=[row_spec, row_spec, row_spec,
                  pl.BlockSpec((C_GROUPS, chunk, chunk), lambda i: (0, 0, 0)),
                  pl.BlockSpec((chunk, C_GROUPS), lambda i: (0, 0))],
        out_specs=row_spec,
        out_shape=jax.ShapeDtypeStruct((m, C_WIDTH), u.dtype),
        compiler_params=_params("arbitrary"),
        name="spatial_gate",
    )(u, v, g, w_s, b_s_t)


def _even_weights(w_in, b_i, b_f):
    gate0 = 5 * A_WIDTH
    b0 = gate0 + 2 * A_HEADS
    wt = jnp.swapaxes(w_in, 0, 1)
    w_gate = jnp.pad(w_in[:, gate0:b0], ((0, 0), (0, GATE_LANES - 2 * A_HEADS)))
    bias = jnp.pad(jnp.concatenate([b_i, b_f]), (0, GATE_LANES - 2 * A_HEADS)).reshape(1, GATE_LANES)
    return wt, b0, w_gate, bias


def _mlstm_inputs(qkvog, gates, bsz, t_len, valid_len, chunk):
    qkvog = qkvog.reshape(bsz, t_len, 5 * A_WIDTH)
    gates = gates.reshape(bsz, t_len, GATE_LANES)
    t_pad = -(-t_len // chunk) * chunk
    pad = ((0, 0), (0, t_pad - t_len), (0, 0))
    if valid_len < t_pad:
        qkvog, gates = jnp.pad(qkvog, pad), jnp.pad(gates, pad)
        pos = jnp.arange(t_pad)[None, :, None]
        lane = jnp.arange(GATE_LANES)[None, None, :]
        gates = jnp.where((pos >= valid_len) & (lane < A_HEADS), NEG_BIG, gates)
        gates = jnp.where((pos >= valid_len) & (lane >= A_HEADS), 0.0, gates)
    return qkvog, gates, gates[:, :, :2 * A_HEADS].transpose(0, 2, 1)


def _even_front(xp, xs, ew, g_norm, tm):
    wt, b0, w_gate, bias = ew
    hp, gates_p = norm_gates(xp, g_norm, w_gate, bias, min(tm, 512))
    hs, gates_s = norm_gates(xs, g_norm, w_gate, bias, xs.shape[0])
    tn, tm_kv = 1024, tm
    qkvog = proj(hp, wt, 0, 5 * A_WIDTH, BF16, 2 * tm, tn, second=(hs, BF16))
    q_b = proj(hp, wt, b0, B_WIDTH, BF16, 2 * tm, tn, scale=B_DH ** -0.5, second=(hs, F32))
    g_b = proj(hp, wt, b0 + 3 * B_WIDTH, B_WIDTH, BF16, 2 * tm, tn, second=(hs, F32))
    k = kv_proj(hp, wt, b0 + B_WIDTH, tm_kv, second=hs)
    v = kv_proj(hp, wt, b0 + 2 * B_WIDTH, tm_kv, second=hs)
    groups = []
    for i, gates in enumerate((gates_p, gates_s)):
        groups.append(dict(qkvog=qkvog[i], gates=gates, q_b=q_b[i], g_b=g_b[i],
                           k_new=k[2 * i], k_bf=k[2 * i + 1], v_new=v[2 * i], v_bf=v[2 * i + 1]))
    return groups


def _even_back(xp, xs, mix_p, mix_s, w_out, next_gain, tm):
    w_out_b = w_out.astype(BF16)
    flat = lambda mix, m: [mix[0].reshape(m, A_WIDTH), mix[1].reshape(m, B_WIDTH)]
    return out_proj_norm(flat(mix_p, xp.shape[0]), [w_out_b[:A_WIDTH], w_out_b[A_WIDTH:]], xp, next_gain,
                         tm, True, BF16, second=(flat(mix_s, xs.shape[0]), xs, BF16))


def _odd_layer(xp, hp, xs, hs, w_in, v_gain, w_s, b_s, w_out_b, final_gain, tm, n_seq):
    (u, v, g), (u2, v2, g2) = odd_in(hp, w_in, v_gain, 4 * tm, BF16, second=(hs, F32))
    y2 = spatial_gate(u2, v2, g2, w_s[:, :SAMPLE_PAD, :SAMPLE_PAD], b_s[:, :SAMPLE_PAD].T, SAMPLE_PAD, n_seq)
    y_p, y_s = out_proj_norm([], [w_out_b], xp, final_gain, tm, False, F32, second=([y2], xs, F32),
                             spatial=(u, v, g, w_s, b_s.T))
    return y_p, y_s, v2


def kernel(x_prompt, x_sample, state_a_C, state_a_n, state_a_m, cache_b_k, cache_b_v, page_table,
           even_norm, even_w_in, even_b_i, even_b_f, even_b_sb, even_w_out,
           odd_norm, odd_w_in, odd_v_gain, odd_w_s, odd_b_s, odd_w_out, final_norm):
    bsz, seq, _ = x_prompt.shape
    n_seq, dec_seq, _ = x_sample.shape
    n_pool = cache_b_k.shape[1]

    ew = _even_weights(even_w_in[0], even_b_i[0], even_b_f[0])
    odd_w_in_b = odd_w_in[0]
    odd_w_out_b = odd_w_out[0].astype(BF16)

    xp = x_prompt.reshape(bsz * seq, D_MODEL)
    xs = jnp.pad(x_sample, ((0, 0), (0, SAMPLE_PAD - dec_seq), (0, 0))).reshape(n_seq * SAMPLE_PAD, D_MODEL)
    fp, fs = _even_front(xp, xs, ew, even_norm[0], 1024)
    qkvog_p, gates_p, grow_p = _mlstm_inputs(fp["qkvog"], fp["gates"], bsz, seq, seq, A_CHUNK)
    qkvog_s, gates_s, grow_s = _mlstm_inputs(fs["qkvog"], fs["gates"], n_seq, SAMPLE_PAD, dec_seq,
                                             SAMPLE_CHUNK)
    q_s, g_s = (fs[name].reshape(n_seq, SAMPLE_PAD, B_WIDTH) for name in ("q_b", "g_b"))

    assert cache_b_k.shape[0] == 1 and cache_b_v.shape[0] == 1
    page_view = (1, n_pool, PAGE_SIZE * B_HEADS, B_DH)
    cache_k, cache_v = cache_b_k.reshape(page_view), cache_b_v.reshape(page_view)
    kv_pad = ((0, 0), (0, (PAGE_SIZE - SAMPLE_PAD) * B_HEADS), (0, 0))
    k_new_s = jnp.pad(fs["k_new"].reshape(n_seq, SAMPLE_PAD * B_HEADS, B_DH), kv_pad)
    v_new_s = jnp.pad(fs["v_new"].reshape(n_seq, SAMPLE_PAD * B_HEADS, B_DH), kv_pad)
    bias_rows = jnp.broadcast_to(jnp.repeat(even_b_sb[0], SAMPLE_PAD)[:, None],
                                 (B_HEADS * SAMPLE_PAD, GATE_LANES))
    att_rows = B_HEADS * SAMPLE_PAD
    rider = _PagedAttnRider(
        q_s, k_new_s, v_new_s, cache_k, cache_v, page_table, bias_rows,
        jnp.zeros((n_seq, att_rows, B_WIDTH), F32), jnp.zeros((n_seq, att_rows, GATE_LANES), F32),
        0, PAGES_PER_STEP)
    assert bsz * (seq // A_CHUNK) == n_seq * rider.steps_per_seq
    zero_state = (jnp.zeros((bsz, A_HEADS, A_DH, A_DH), F32),
                  jnp.zeros((bsz, A_HEADS, 1, A_DH), F32),
                  jnp.zeros((bsz, A_HEADS, 1, GATE_LANES), F32))
    (ha_p, c_p, n_p, m_p), (att_acc, _) = mlstm(
        qkvog_p, gates_p, grow_p, *zero_state, A_CHUNK, rider=rider)
    hb_s = attn_sample_finish(att_acc, g_s)

    as_seq = lambda a: a.reshape(bsz, seq, B_WIDTH)
    hb_p = attn_prompt(as_seq(fp["q_b"]), as_seq(fp["g_b"]), as_seq(fp["k_bf"]), as_seq(fp["v_bf"]),
                       even_b_sb[0], ATTN_BQ, ATTN_BK, ATTN_HEADS_PER_STEP)
    st_in = (state_a_C[0], state_a_n[0][:, :, None, :],
             jnp.broadcast_to(state_a_m[0][:, :, None, None], (n_seq, A_HEADS, 1, GATE_LANES)))
    ha_s, c_s, n_s, m_s_new = mlstm(qkvog_s, gates_s, grow_s, *st_in, SAMPLE_CHUNK)
    ha_s = ha_s[:, :SAMPLE_PAD]

    xp1, hp1, xs1, hs1 = _even_back(xp, xs, (ha_p, hb_p), (ha_s, hb_s), even_w_out[0], odd_norm[0], 512)
    y_p, y_s, v_rows = _odd_layer(xp1, hp1, xs1, hs1, odd_w_in_b, odd_v_gain[0], odd_w_s[0], odd_b_s[0],
                                  odd_w_out_b, final_norm, 512, n_seq)

    def sample_rows(a, *dims):
        return a.reshape((n_seq, SAMPLE_PAD) + dims)[:, :dec_seq]

    return (y_p.reshape(bsz, seq, D_MODEL),
            sample_rows(y_s, D_MODEL),
            c_p[None], n_p[:, :, 0, :][None], m_p[:, :, 0, 0][None],
            c_s[None], n_s[:, :, 0, :][None], m_s_new[:, :, 0, 0][None],
            fp["k_new"].reshape(1, bsz, seq, B_HEADS, B_DH), fp["v_new"].reshape(1, bsz, seq, B_HEADS, B_DH),
            sample_rows(fs["k_new"], B_HEADS, B_DH)[None], sample_rows(fs["v_new"], B_HEADS, B_DH)[None],
            sample_rows(v_rows, C_WIDTH)[None])
```

```python
import functools

import jax
import jax.numpy as jnp
from jax import lax
from jax.experimental import pallas as pl
from jax.experimental.pallas import tpu as pltpu

F32 = jnp.float32
BF16 = jnp.bfloat16

D_MODEL = 2048
PAGE_SIZE = 128
A_HEADS = 4
A_DH = 256
A_WIDTH = A_HEADS * A_DH
A_CHUNK = 128
B_HEADS = 8
B_DH = 128
B_WIDTH = B_HEADS * B_DH
C_WIDTH = D_MODEL
C_GROUPS = 8
C_GDIM = C_WIDTH // C_GROUPS
C_CHUNK = 128
RMS_EPS = 1e-6
GATE_LANES = 128
NEG_BIG = -1e30
SAMPLE_PAD = 8
SAMPLE_CHUNK = 16
ATTN_BQ = 512
ATTN_BK = 512
CUMSUM_BLOCK = 256
ATTN_HEADS_PER_STEP = 2
PAGES_PER_STEP = 16

VMEM_LIMIT_BYTES = 56 * 1024 * 1024


def _params(*sem):
    return pltpu.CompilerParams(dimension_semantics=sem, vmem_limit_bytes=VMEM_LIMIT_BYTES)


def _dot(a, b):
    return jnp.dot(a, b, preferred_element_type=F32)


def _dot_nt(a, b):
    return lax.dot_general(a, b, (((1,), (1,)), ((), ())), preferred_element_type=F32)


def _dot_tn(a, b):
    return lax.dot_general(a, b, (((0,), (0,)), ((), ())), preferred_element_type=F32)


def _softplus(z):
    return jnp.maximum(z, 0.0) + jnp.log(1.0 + jnp.exp(-jnp.abs(z)))


def _sigmoid(z):
    return 1.0 / (1.0 + jnp.exp(-z))


def _silu(z):
    return z * _sigmoid(z)


def _gelu_tanh(x):
    c = 0.7978845608028654
    return x * (0.5 * (1.0 + jnp.tanh(c * (x + 0.044715 * (x * x * x)))))


def _rms(x, g):
    return x * lax.rsqrt(jnp.mean(x * x, axis=-1, keepdims=True) + RMS_EPS) * g


def _split_hi_lo(x):
    hi = x.astype(BF16)
    lo = (x - hi.astype(F32)).astype(BF16)
    return hi, lo


def _norm_gates_kernel(x_ref, g_ref, w_ref, bias_ref, h_ref, gate_ref):
    h = _rms(x_ref[...], g_ref[...])
    h_hi, h_lo = _split_hi_lo(h)
    h_ref[...] = h_hi
    a = _dot(h_hi, w_ref[...])
    b = _dot(h_lo, w_ref[...])
    pre = a[:, :GATE_LANES] + a[:, GATE_LANES:] + b[:, :GATE_LANES] + bias_ref[...]
    lane = lax.broadcasted_iota(jnp.int32, pre.shape, 1)
    is_forget = (lane >= A_HEADS) & (lane < 2 * A_HEADS)
    gate_ref[...] = jnp.where(is_forget, -_softplus(-pre), pre)


def norm_gates(x, gain, w_gate, bias, tm):
    m = x.shape[0]
    w_hi_lo = jnp.concatenate(_split_hi_lo(w_gate), axis=1)
    return pl.pallas_call(
        _norm_gates_kernel,
        grid=(m // tm,),
        in_specs=[pl.BlockSpec((tm, D_MODEL), lambda i: (i, 0)),
                  pl.BlockSpec((1, D_MODEL), lambda i: (0, 0)),
                  pl.BlockSpec((D_MODEL, 2 * GATE_LANES), lambda i: (0, 0)),
                  pl.BlockSpec((1, GATE_LANES), lambda i: (0, 0))],
        out_specs=[pl.BlockSpec((tm, D_MODEL), lambda i: (i, 0)),
                   pl.BlockSpec((tm, GATE_LANES), lambda i: (i, 0))],
        out_shape=[jax.ShapeDtypeStruct((m, D_MODEL), BF16),
                   jax.ShapeDtypeStruct((m, GATE_LANES), F32)],
        compiler_params=_params("arbitrary"),
        name="norm_gates",
    )(x, gain.reshape(1, D_MODEL), w_hi_lo, bias)


def _hosted_kernel(*refs, n_in, n_out, phases_fn, lin, rank, rider):
    if rider is not None:
        refs = refs[1:]
    r_in, r_out = (rider.n_in, rider.n_out) if rider is not None else (0, 0)
    ins, refs = refs[:n_in], refs[n_in:]
    r_ins, refs = refs[:r_in], refs[r_in:]
    outs, refs = refs[:n_out], refs[n_out:]
    r_outs, refs = refs[:r_out], refs[r_out:]
    n_sc = len(refs) - (rider.n_scratch if rider is not None else 0)
    scratch, r_scratch = refs[:n_sc], refs[n_sc:]
    sets = [phases_fn(ins, outs, scratch)]
    if rider is not None:
        step = lin(*[pl.program_id(d) for d in range(rank)])
        sets.append(rider.phases(r_ins, r_outs, r_scratch, step))
    for pre, _, _ in sets:
        for cond, fn in pre:
            pl.when(cond)(fn)
    for _, parts, _ in sets:
        for part in parts:
            part()
    for _, _, post in sets:
        for cond, fn in post:
            pl.when(cond)(fn)


def _hosted_call(name, grid, lin, in_specs, out_specs, out_shape, scratch, phases_fn, args, rider=None):
    n_in, n_out = len(in_specs), len(out_specs)
    aliases = {}
    if rider is not None:
        r = rider.specs(lin, len(grid))
        aliases = {1 + n_in + i: n_out + o for i, o in r["aliases"].items()}
        in_specs, out_specs = in_specs + r["in_specs"], out_specs + r["out_specs"]
        out_shape, scratch = out_shape + r["out_shape"], scratch + r["scratch"]
        args = [rider.page_table] + list(args) + r["args"]
    kern = functools.partial(_hosted_kernel, n_in=n_in, n_out=n_out, phases_fn=phases_fn,
                             lin=lin, rank=len(grid), rider=rider)
    grid_spec = pltpu.PrefetchScalarGridSpec(
        num_scalar_prefetch=0 if rider is None else 1, grid=grid,
        in_specs=in_specs, out_specs=out_specs, scratch_shapes=scratch)
    outs = pl.pallas_call(
        kern, grid_spec=grid_spec, out_shape=out_shape, input_output_aliases=aliases,
        compiler_params=_params(*(["arbitrary"] * len(grid))), name=name,
    )(*args)
    return outs[:n_out], outs[n_out:]


def _two_group_steps(n_i, has_second):
    if not has_second:
        return n_i, (lambda i: i)
    return n_i + 1, (lambda i: jnp.maximum(i - 1, 0))


def proj(a, wt, row0, n_out, out_dtype, tm, tn, scale=None, second=None, row_step=None):
    m, k = a.shape
    n_i = m // tm
    n_steps, tile = _two_group_steps(n_i, second is not None)
    row_step = tn if row_step is None else row_step

    def phases(ins, outs, scratch):
        a_ref, wt_ref = ins[:2]
        o_ref, (wbf,) = outs[0], scratch
        i = pl.program_id(1)

        def cast():
            wbf[...] = wt_ref[...].astype(BF16)

        def main():
            y = _dot_nt(a_ref[...], wbf[...])
            if scale is not None:
                y = y * jnp.where(pl.program_id(0) == 0, scale, 1.0)
            o_ref[...] = y.astype(o_ref.dtype)

        if second is None:
            return [(i == 0, cast)], [main], []

        def small():
            outs[1][...] = _dot_nt(ins[2][...], wbf[...]).astype(outs[1].dtype)

        return [(i == 0, cast), (i == 0, small), (i > 0, main)], [], []

    in_specs = [pl.BlockSpec((tm, k), lambda j, i, *_: (tile(i), 0)),
                pl.BlockSpec((pl.Element(tn), pl.Element(k)),
                             lambda j, i, *_: (pl.multiple_of(row0 + j * row_step, 8), 0))]
    out_specs = [pl.BlockSpec((tm, tn), lambda j, i, *_: (tile(i), j))]
    out_shape = [jax.ShapeDtypeStruct((m, n_out), out_dtype)]
    args = [a, wt]
    if second is not None:
        a2, dtype2 = second
        in_specs.append(pl.BlockSpec(a2.shape, lambda j, i, *_: (0, 0)))
        out_specs.append(pl.BlockSpec((a2.shape[0], tn), lambda j, i, *_: (0, j)))
        out_shape.append(jax.ShapeDtypeStruct((a2.shape[0], n_out), dtype2))
        args.append(a2)
    outs, _ = _hosted_call("proj", (n_out // tn, n_steps), lambda j, i: j * n_steps + i,
                           in_specs, out_specs, out_shape, [pltpu.VMEM((tn, k), BF16)], phases, args)
    return outs[0] if second is None else tuple(outs)


def kv_proj(a, wt, row0, tm, second=None):
    m, k = a.shape
    n_i = m // tm
    n_steps, tile = _two_group_steps(n_i, second is not None)

    def phases(ins, outs, scratch):
        a_ref, wt_ref = ins[:2]
        (wbf,) = scratch
        i = pl.program_id(0)

        def cast():
            wbf[...] = wt_ref[...].astype(BF16)

        def rows_to(a_rows_ref, o_ref, obf_ref):
            def run():
                n_rows = a_rows_ref.shape[0]
                y = _dot_nt(a_rows_ref[...], wbf[...])
                obf_ref[...] = y.astype(BF16)
                for h in range(B_HEADS):
                    o_ref[pl.ds(h, n_rows, stride=B_HEADS), :] = y[:, h * B_DH:(h + 1) * B_DH]
            return run

        main = rows_to(a_ref, outs[0], outs[1])
        if second is None:
            return [(i == 0, cast)], [main], []
        return [(i == 0, cast), (i == 0, rows_to(ins[2], outs[2], outs[3])), (i > 0, main)], [], []

    def out_pair(rows, index):
        return ([pl.BlockSpec((rows * B_HEADS, B_DH), index), pl.BlockSpec((rows, B_WIDTH), index)],
                lambda total: [jax.ShapeDtypeStruct((total * B_HEADS, B_DH), F32),
                               jax.ShapeDtypeStruct((total, B_WIDTH), BF16)])

    in_specs = [pl.BlockSpec((tm, k), lambda i, *_: (tile(i), 0)),
                pl.BlockSpec((pl.Element(B_WIDTH), pl.Element(k)), lambda i, *_: (row0, 0))]
    out_specs, shapes = out_pair(tm, lambda i, *_: (tile(i), 0))
    out_shape = shapes(m)
    args = [a, wt]
    if second is not None:
        m2 = second.shape[0]
        in_specs.append(pl.BlockSpec(second.shape, lambda i, *_: (0, 0)))
        specs2, shapes2 = out_pair(m2, lambda i, *_: (0, 0))
        out_specs, out_shape = out_specs + specs2, out_shape + shapes2(m2)
        args.append(second)
    outs, _ = _hosted_call("kv_proj", (n_steps,), lambda i: i, in_specs, out_specs, out_shape,
                           [pltpu.VMEM((B_WIDTH, k), BF16)], phases, args)
    return tuple(outs)


def out_proj_norm(lhs, ws, x, gain, tm, emit_x, norm_dtype, second=None, spatial=None):
    m = x.shape[0]
    n_lhs, n_w = len(lhs), len(ws)
    n_i = m // tm
    n_steps, tile = _two_group_steps(n_i, second is not None)
    n_out = 2 if emit_x else 1

    def phases(ins, outs, scratch):
        w_refs = ins[n_lhs:n_lhs + n_w]
        x_ref, g_ref = ins[n_lhs + n_w], ins[n_lhs + n_w + 1]
        rest = ins[n_lhs + n_w + 2:]
        i = pl.program_id(0)

        def group(make_lhs, a_refs, x_ref, out_refs, row_chunk):
            def run():
                if make_lhs is not None:
                    make_lhs()
                for r in range(0, x_ref.shape[0], row_chunk):
                    rows = slice(r, r + row_chunk)
                    y = x_ref[rows, :]
                    for a_ref, w_ref in zip(a_refs, w_refs):
                        y = y + _dot(a_ref[rows, :].astype(BF16), w_ref[...])
                    if emit_x:
                        out_refs[0][rows, :] = y
                    out_refs[-1][rows, :] = _rms(y, g_ref[...]).astype(out_refs[-1].dtype)
            return run

        if second is not None:
            a2_refs, x2_ref, rest = rest[:n_w], rest[n_w], rest[n_w + 1:]
        if spatial is None:
            main = group(None, ins[:n_lhs], x_ref, outs[:n_out], min(tm, 256))
        else:
            (y_sc,) = scratch
            fill = functools.partial(_spatial_kernel, *rest[:5], y_sc, chunk=C_CHUNK, n_chunks=tm // C_CHUNK)
            main = group(fill, [y_sc], x_ref, outs[:n_out], min(tm, 256))
        if second is None:
            return [], [main], []
        small = group(None, a2_refs, x2_ref, outs[n_out:], x2_ref.shape[0])
        return [(i == 0, small), (i > 0, main)], [], []

    row_spec = lambda width: pl.BlockSpec((tm, width), lambda i, *_: (tile(i), 0))
    whole = lambda arr: pl.BlockSpec(arr.shape, lambda i, *_: (0,) * arr.ndim)
    gain2d = gain.reshape(1, D_MODEL)
    in_specs = ([row_spec(a.shape[1]) for a in lhs] + [whole(w) for w in ws]
                + [row_spec(D_MODEL), whole(gain2d)])
    out_specs = [row_spec(D_MODEL)] * n_out
    out_shape = ([jax.ShapeDtypeStruct((m, D_MODEL), F32)] if emit_x else []) \
        + [jax.ShapeDtypeStruct((m, D_MODEL), norm_dtype)]
    args = [*lhs, *ws, x, gain2d]
    scratch = []
    if second is not None:
        lhs2, x2, norm_dtype2 = second
        m2 = x2.shape[0]
        in_specs += [whole(a) for a in lhs2] + [whole(x2)]
        out_specs += [pl.BlockSpec((m2, D_MODEL), lambda i, *_: (0, 0))] * n_out
        out_shape += ([jax.ShapeDtypeStruct((m2, D_MODEL), F32)] if emit_x else []) \
            + [jax.ShapeDtypeStruct((m2, D_MODEL), norm_dtype2)]
        args += [*lhs2, x2]
    if spatial is not None:
        u, v, gate, w_s, b_s_t = spatial
        in_specs += [row_spec(C_WIDTH)] * 3 + [whole(w_s), whole(b_s_t)]
        args += [u, v, gate, w_s, b_s_t]
        scratch = [pltpu.VMEM((tm, C_WIDTH), BF16)]
    outs, _ = _hosted_call("out_proj_norm", (n_steps,), lambda i: i, in_specs, out_specs, out_shape,
                           scratch, phases, args)
    return list(outs)


def _mlstm_chunk(q_ref, k_ref, v_ref, og_ref, gg_ref, gcol_ref, grow_ref, h_ref, c_sc, n_sc, m_sc, chunk, heads):
    L = chunk
    gcol = gcol_ref[0]
    grow = grow_ref[0]
    tt = lax.broadcasted_iota(jnp.int32, (L, L), 0)
    ss = lax.broadcasted_iota(jnp.int32, (L, L), 1)
    causal = ss <= tt

    for head in heads:
        cols = slice(head * A_DH, (head + 1) * A_DH)
        q = q_ref[0, :, cols]
        ks = k_ref[0, :, cols] * jnp.asarray(A_DH ** -0.5, BF16)
        v = v_ref[0, :, cols]
        ig_col = gcol[:, head:head + 1]
        lf_col = gcol[:, head + A_HEADS:head + A_HEADS + 1]
        ig_row = grow[head:head + 1, :]
        lf_row = grow[head + A_HEADS:head + A_HEADS + 1, :]
        b_col = jnp.sum(jnp.where(causal, lf_row, 0.0), axis=1, keepdims=True)
        b_row = jnp.sum(jnp.where(tt <= ss, lf_col, 0.0), axis=0, keepdims=True)
        b_last = jnp.sum(lf_row, axis=1, keepdims=True)

        m0 = m_sc[head][:, :1]
        n0 = n_sc[head]
        c0 = c_sc[head]

        d = jnp.where(causal, b_col - b_row + ig_row, NEG_BIG)
        m_carry = b_col + m0
        m = jnp.maximum(m_carry, jnp.max(d, axis=1, keepdims=True))
        w_intra = jnp.exp(d - m)
        w_carry = jnp.exp(m_carry - m)
        s = _dot_nt(q, ks) * w_intra
        qf = q.astype(F32)
        num = _dot(s.astype(BF16), v) + w_carry * _dot_nt(q, c0.astype(BF16))
        den = jnp.sum(s, axis=1, keepdims=True) + w_carry * jnp.sum(qf * n0, axis=1, keepdims=True)
        h = num / jnp.maximum(jnp.abs(den), jnp.exp(-m))
        gated = h * _sigmoid(og_ref[0, :, cols].astype(F32)) * _silu(gg_ref[0, :, cols].astype(F32))
        h_ref[0, :, cols] = gated.astype(h_ref.dtype)

        m_carry_last = b_last + m0
        d_last_row = b_last - b_row + ig_row
        m_new = jnp.maximum(m_carry_last, jnp.max(d_last_row, axis=1, keepdims=True))
        wc_last = jnp.exp(m_carry_last - m_new)
        w_last_col = jnp.exp(b_last - b_col + ig_col - m_new)
        vw = (v.astype(F32) * w_last_col).astype(BF16)
        c_new = wc_last * c0 + _dot_tn(vw, ks)
        n_new = wc_last * n0 + jnp.sum(ks.astype(F32) * w_last_col, axis=0, keepdims=True)
        c_sc[head] = c_new
        n_sc[head] = n_new
        m_sc[head] = jnp.broadcast_to(m_new, (1, GATE_LANES))


def mlstm(qkvog, gcol, grow, c0, n0, m0, chunk, rider=None):
    bsz, t_len, _ = qkvog.shape
    nc = t_len // chunk
    hd = A_HEADS

    def phases(ins, outs, state):
        c0_ref, n0_ref, m0_ref = ins[7:]
        h_ref, c_out_ref, n_out_ref, m_out_ref = outs
        c_sc, n_sc, m_sc = state
        ci = pl.program_id(1)

        def init():
            c_sc[...] = c0_ref[0]
            n_sc[...] = n0_ref[0]
            m_sc[...] = m0_ref[0]

        def head_part(head):
            return lambda: _mlstm_chunk(*ins[:7], h_ref, c_sc, n_sc, m_sc, chunk, (head,))

        parts = [head_part(head) for head in range(A_HEADS)]

        def final():
            c_out_ref[0] = c_sc[...]
            n_out_ref[0] = n_sc[...]
            m_out_ref[0] = m_sc[...]

        return [(ci == 0, init)], parts, [(ci == nc - 1, final)]

    blk = lambda seg: pl.BlockSpec((1, chunk, A_WIDTH), lambda b, c, *_, seg=seg: (b, c, seg))
    st4 = lambda r, w: pl.BlockSpec((1, hd, r, w), lambda b, c, *_: (b, 0, 0, 0))
    outs, rest = _hosted_call(
        "mlstm", (bsz, nc), lambda b, c: b * nc + c,
        [blk(0), blk(1), blk(2), blk(3), blk(4),
         pl.BlockSpec((1, chunk, GATE_LANES), lambda b, c, *_: (b, c, 0)),
         pl.BlockSpec((1, 8, chunk), lambda b, c, *_: (b, 0, c)),
         st4(A_DH, A_DH), st4(1, A_DH), st4(1, GATE_LANES)],
        [pl.BlockSpec((1, chunk, A_WIDTH), lambda b, c, *_: (b, c, 0)),
         st4(A_DH, A_DH), st4(1, A_DH), st4(1, GATE_LANES)],
        [jax.ShapeDtypeStruct((bsz, t_len, A_WIDTH), BF16),
         jax.ShapeDtypeStruct((bsz, hd, A_DH, A_DH), F32),
         jax.ShapeDtypeStruct((bsz, hd, 1, A_DH), F32),
         jax.ShapeDtypeStruct((bsz, hd, 1, GATE_LANES), F32)],
        [pltpu.VMEM((hd, A_DH, A_DH), F32), pltpu.VMEM((hd, 1, A_DH), F32),
         pltpu.VMEM((hd, 1, GATE_LANES), F32)],
        phases, [qkvog, qkvog, qkvog, qkvog, qkvog, gcol, grow, c0, n0, m0], rider)
    return tuple(outs) if rider is None else (tuple(outs), rest)


def _stick_block(q, kb, vb, bias, run, mask, upper):
    rows = q.shape[0]
    sub = upper.shape[0]
    n_sub = kb.shape[0] // sub
    z = _dot_nt(q, kb) if bias is None else _dot_nt(q, kb) * (B_DH ** -0.5) + bias
    sp = _softplus(z)
    spm = sp if mask is None else jnp.where(mask, sp, 0.0)
    hi, lo = _split_hi_lo(spm)
    laters = [None] * n_sub
    total = None
    for i in reversed(range(n_sub)):
        ln = slice(i * sub, (i + 1) * sub)
        both = _dot(jnp.concatenate([hi[:, ln], lo[:, ln]], axis=0), upper)
        carry = run if total is None else run + total
        laters[i] = both[:rows] + both[rows:] + carry
        part = jnp.sum(spm[:, ln], axis=1, keepdims=True)
        total = part if total is None else total + part
    later = laters[0] if n_sub == 1 else jnp.concatenate(laters, axis=1)
    a = jnp.exp(z - sp - later)
    if mask is not None:
        a = jnp.where(mask, a, 0.0)
    return _dot(a.astype(BF16), vb), total


def _strict_upper(n):
    j = lax.broadcasted_iota(jnp.int32, (n, n), 0)
    s = lax.broadcasted_iota(jnp.int32, (n, n), 1)
    return jnp.where(j > s, 1.0, 0.0).astype(BF16)


def _attn_prompt_kernel(bias_ref, q_ref, k_ref, v_ref, g_ref, o_ref, *, bq, bk, n_heads):
    head0 = pl.program_id(1) * n_heads
    qi = pl.program_id(2)
    kbf = k_ref.at[0]
    vbf = v_ref.at[0]
    upper = _strict_upper(min(bk, CUMSUM_BLOCK))
    row = lax.broadcasted_iota(jnp.int32, (bq, bk), 0)
    col = lax.broadcasted_iota(jnp.int32, (bq, bk), 1)
    lanes = [slice(h * B_DH, (h + 1) * B_DH) for h in range(n_heads)]
    lane_q = lax.broadcasted_iota(jnp.int32, (bq, B_DH), 1)
    lane_k = lax.broadcasted_iota(jnp.int32, (bk, B_DH), 1)
    ones_cols = jnp.where(lane_q < 2, 1.0, 0.0).astype(BF16)
    qs = [jnp.concatenate([q_ref[0, :, ln], ones_cols], axis=1) for ln in lanes]
    bias_cols = []
    for h in range(n_heads):
        b = jnp.full((bk, B_DH), bias_ref[head0 + h], F32)
        b_hi = b.astype(BF16).astype(F32)
        bias_cols.append(jnp.where(lane_k == 0, b_hi, jnp.where(lane_k == 1, b - b_hi, 0.0)).astype(BF16))

    def blocks(kj, runs, mask):
        start = pl.multiple_of(kj * bk, bk)
        return [_stick_block(qs[h], jnp.concatenate([kbf[pl.ds(start, bk), ln], bias_cols[h]], axis=1),
                             vbf[pl.ds(start, bk), ln], None, runs[h], mask, upper)
                for h, ln in enumerate(lanes)]

    q0 = qi * bq
    n_full = q0 // bk
    if bq == bk:
        half = bq // 2
        start = pl.multiple_of(qi * bq, bq)
        accs, runs = [], []
        for h, ln in enumerate(lanes):
            kb = jnp.concatenate([kbf[pl.ds(start, bk), ln], bias_cols[h]], axis=1)
            vb = vbf[pl.ds(start, bk), ln]
            zero = jnp.zeros((half, 1), F32)
            iota = lambda shape, dim: lax.broadcasted_iota(jnp.int32, shape, dim)
            top = _stick_block(qs[h][:half], kb[:half], vb[:half], None, zero,
                               iota((half, half), 1) < iota((half, half), 0), upper)
            bot = _stick_block(qs[h][half:], kb, vb, None, zero,
                               iota((half, bk), 1) < iota((half, bk), 0) + half, upper)
            accs.append(jnp.concatenate([top[0], bot[0]], axis=0))
            runs.append(jnp.concatenate([top[1], bot[1]], axis=0))
    else:
        accs = [jnp.zeros((bq, B_DH), F32)] * n_heads
        runs = [jnp.zeros((bq, 1), F32)] * n_heads
        for m in reversed(range(max(1, bq // bk))):
            kj = n_full + m
            res = blocks(kj, runs, col + (kj * bk - q0) < row)
            accs = [a + c for a, (c, _) in zip(accs, res)]
            runs = [r + t for r, (_, t) in zip(runs, res)]

    def body(it, carry):
        accs, runs = carry
        res = blocks(n_full - 1 - it, runs, None)
        return (tuple(a + c for a, (c, _) in zip(accs, res)),
                tuple(r + t for r, (_, t) in zip(runs, res)))

    accs, runs = lax.fori_loop(0, n_full, body, (tuple(accs), tuple(runs)))
    for h, ln in enumerate(lanes):
        o_ref[0, :, ln] = (accs[h] * _silu(g_ref[0, :, ln].astype(F32))).astype(o_ref.dtype)


def attn_prompt(qg, k, v, b_sb, bq, bk, n_heads):
    bsz, t_len, _ = k.shape
    width = n_heads * B_DH
    n_hgrp = B_HEADS // n_heads
    q_spec = pl.BlockSpec((1, bq, width), lambda b, h, i: (b, i, h))
    g_spec = pl.BlockSpec((1, bq, width), lambda b, h, i: (b, i, n_hgrp + h))
    kv_spec = pl.BlockSpec((1, t_len, width), lambda b, h, i: (b, 0, h))
    return pl.pallas_call(
        functools.partial(_attn_prompt_kernel, bq=bq, bk=bk, n_heads=n_heads),
        grid=(bsz, n_hgrp, t_len // bq),
        in_specs=[pl.BlockSpec(memory_space=pltpu.SMEM), q_spec, kv_spec, kv_spec, g_spec],
        out_specs=q_spec,
        out_shape=jax.ShapeDtypeStruct((bsz, t_len, B_WIDTH), BF16),
        compiler_params=_params("arbitrary", "arbitrary", "arbitrary"),
        name="attn_prompt",
    )(b_sb, qg, k, v, qg)


def _attn_sample_phases(ins, outs, scratch, n_group):
    bias_ref, q_ref, knew_ref, vnew_ref, acc_in_ref, run_in_ref = ins[:6]
    k_refs = ins[6:6 + n_group]
    v_refs = ins[6 + n_group:]
    acc_out_ref, run_out_ref = outs
    qbd, acc, run, kcat, vcat = scratch
    rows = B_HEADS * SAMPLE_PAD
    upper = _strict_upper(PAGE_SIZE)
    bias = bias_ref[...][:, :1]

    def repack(page, dst, i):
        for h in range(B_HEADS):
            dst[i * PAGE_SIZE:(i + 1) * PAGE_SIZE, h * B_DH:(h + 1) * B_DH] = page(h).astype(BF16)

    def step(slot0, n_blk, mask):
        keys = slice(slot0 * PAGE_SIZE, (slot0 + n_blk) * PAGE_SIZE)
        z = _dot_nt(qbd[...], kcat[keys, :]) * (B_DH ** -0.5) + bias
        sp = _softplus(z)
        spm = sp if mask is None else jnp.where(mask, sp, 0.0)
        hi, lo = _split_hi_lo(spm)
        carry = run[...][:, :1]
        laters = []
        for i in range(n_blk):
            ln = slice(i * PAGE_SIZE, (i + 1) * PAGE_SIZE)
            both = _dot(jnp.concatenate([hi[:, ln], lo[:, ln]], axis=0), upper)
            laters.append(both[:rows] + both[rows:] + carry)
            carry = carry + jnp.sum(spm[:, ln], axis=1, keepdims=True)
        later = laters[0] if n_blk == 1 else jnp.concatenate(laters, axis=1)
        a = jnp.exp(z - sp - later)
        if mask is not None:
            a = jnp.where(mask, a, 0.0)
        acc[...] += _dot(a.astype(BF16), vcat[keys, :])
        run[...] = jnp.broadcast_to(carry, run.shape)

    def build_queries():
        r = lax.broadcasted_iota(jnp.int32, (rows, B_WIDTH), 0)
        c = lax.broadcasted_iota(jnp.int32, (rows, B_WIDTH), 1)
        q_rep = jnp.concatenate([q_ref[0].astype(F32)] * B_HEADS, axis=0)
        qbd[...] = jnp.where((r // SAMPLE_PAD) == (c // B_DH), q_rep, 0.0).astype(BF16)

    def start_sequence():
        build_queries()
        acc[...] = jnp.zeros_like(acc)
        run[...] = jnp.zeros_like(run)
        repack(lambda h: knew_ref[0, pl.ds(h, PAGE_SIZE, stride=B_HEADS), :], kcat, 0)
        repack(lambda h: vnew_ref[0, pl.ds(h, PAGE_SIZE, stride=B_HEADS), :], vcat, 0)
        t = lax.broadcasted_iota(jnp.int32, (rows, PAGE_SIZE), 0) % SAMPLE_PAD
        s = lax.broadcasted_iota(jnp.int32, (rows, PAGE_SIZE), 1)
        step(0, 1, s < t)

    def resume_sequence():
        build_queries()
        acc[...] = acc_in_ref[0]
        run[...] = run_in_ref[0]

    def pages():
        for i in range(n_group):
            repack(lambda h, r=k_refs[i]: r[0, 0, pl.ds(h, PAGE_SIZE, stride=B_HEADS), :], kcat, i)
            repack(lambda h, r=v_refs[i]: r[0, 0, pl.ds(h, PAGE_SIZE, stride=B_HEADS), :], vcat, i)
        step(0, n_group, None)
        acc_out_ref[0] = acc[...]
        run_out_ref[0] = run[...]

    return start_sequence, resume_sequence, pages


class _PagedAttnRider:
    n_out = 2
    n_scratch = 5

    def __init__(self, q, k_new, v_new, cache_k, cache_v, page_table, bias_rows, acc, run, first_step, n_group):
        self.arrays = (bias_rows, q, k_new, v_new, acc, run)
        self.caches = (cache_k, cache_v)
        self.page_table = page_table
        self.first_step = first_step
        self.n_group = n_group
        self.n_in = 6 + 2 * n_group
        self.steps_per_seq = page_table.shape[1] // n_group

    def specs(self, lin, rank):
        n_group, spq = self.n_group, self.steps_per_seq
        n_pages = self.page_table.shape[1]
        rows = B_HEADS * SAMPLE_PAD
        page_rows = PAGE_SIZE * B_HEADS
        gstep = lambda a: self.first_step + lin(*a[:rank])
        seq_map = lambda *a: (gstep(a) // spq, 0, 0)

        def page_spec(i):
            def index(*a):
                g, pt = gstep(a), a[rank]
                return (0, pt[g // spq, n_pages - 1 - ((g % spq) * n_group + i)], 0, 0)
            return pl.BlockSpec((1, 1, page_rows, B_DH), index)

        acc_spec = pl.BlockSpec((1, rows, B_WIDTH), seq_map)
        run_spec = pl.BlockSpec((1, rows, GATE_LANES), seq_map)
        new_spec = pl.BlockSpec((1, page_rows, B_DH), seq_map)
        acc, run = self.arrays[4:]
        return dict(
            in_specs=[pl.BlockSpec((rows, GATE_LANES), lambda *a: (0, 0)),
                      pl.BlockSpec((1, SAMPLE_PAD, B_WIDTH), seq_map), new_spec, new_spec,
                      acc_spec, run_spec] + [page_spec(i) for i in range(n_group)] * 2,
            out_specs=[acc_spec, run_spec],
            out_shape=[jax.ShapeDtypeStruct(acc.shape, F32), jax.ShapeDtypeStruct(run.shape, F32)],
            scratch=[pltpu.VMEM((rows, B_WIDTH), BF16), pltpu.VMEM((rows, B_WIDTH), F32),
                     pltpu.VMEM((rows, GATE_LANES), F32),
                     pltpu.VMEM((n_group * PAGE_SIZE, B_WIDTH), BF16),
                     pltpu.VMEM((n_group * PAGE_SIZE, B_WIDTH), BF16)],
            args=list(self.arrays) + [self.caches[0]] * n_group + [self.caches[1]] * n_group,
            aliases={4: 0, 5: 1},
        )

    def phases(self, ins, outs, scratch, local_step):
        start, resume, pages = _attn_sample_phases(ins, outs, scratch, self.n_group)
        p = (self.first_step + local_step) % self.steps_per_seq
        return [(p == 0, start), ((local_step == 0) & (p != 0), resume)], [pages], []


def attn_sample_finish(acc, g):
    n_seq = acc.shape[0]

    def kern(acc_ref, g_ref, o_ref):
        a = acc_ref[0]
        c = lax.broadcasted_iota(jnp.int32, (SAMPLE_PAD, B_WIDTH), 1) // B_DH
        out = jnp.zeros((SAMPLE_PAD, B_WIDTH), F32)
        for h in range(B_HEADS):
            out = out + jnp.where(c == h, a[h * SAMPLE_PAD:(h + 1) * SAMPLE_PAD, :], 0.0)
        o_ref[0] = out * _silu(g_ref[0])

    return pl.pallas_call(
        kern, grid=(n_seq,),
        in_specs=[pl.BlockSpec((1, B_HEADS * SAMPLE_PAD, B_WIDTH), lambda s: (s, 0, 0)),
                  pl.BlockSpec((1, SAMPLE_PAD, B_WIDTH), lambda s: (s, 0, 0))],
        out_specs=pl.BlockSpec((1, SAMPLE_PAD, B_WIDTH), lambda s: (s, 0, 0)),
        out_shape=jax.ShapeDtypeStruct((n_seq, SAMPLE_PAD, B_WIDTH), F32),
        compiler_params=_params("arbitrary"), name="attn_sample_finish",
    )(acc, g)


def proj_act(h, w, col0, v_gain, act, tm, tn, out_dtype, second=None):
    m = h.shape[0]
    j0 = col0 // tn
    n_col = C_WIDTH // tn
    n_i = m // tm
    n_steps, tile = _two_group_steps(n_i, second is not None)

    def phases(ins, outs, scratch):
        a_ref, w_ref, vg_ref = ins[:3]
        (wbf,) = scratch
        i = pl.program_id(1)

        def cast():
            wbf[...] = w_ref[...].astype(BF16)

        def group(rows_ref, o_ref, row_chunk):
            def run():
                for r in range(0, rows_ref.shape[0], row_chunk):
                    rows = slice(r, r + row_chunk)
                    y = _dot(rows_ref[rows, :], wbf[...])
                    if act == "gelu":
                        y = _gelu_tanh(y)
                    elif act == "silu":
                        y = _silu(y)
                    else:
                        y = _rms(_gelu_tanh(y), vg_ref[...])
                    o_ref[rows, :] = y.astype(o_ref.dtype)
            return run

        main = group(a_ref, outs[0], min(tm, 256))
        if second is None:
            return [(i == 0, cast)], [main], []
        small = group(ins[3], outs[1], ins[3].shape[0])
        return [(i == 0, cast), (i == 0, small), (i > 0, main)], [], []

    w_mode = dict(pipeline_mode=pl.Buffered(1)) if n_col == 1 else {}
    in_specs = [pl.BlockSpec((tm, D_MODEL), lambda j, i, *_: (tile(i), 0)),
                pl.BlockSpec((D_MODEL, tn), lambda j, i, *_: (0, j0 + j), **w_mode),
                pl.BlockSpec((1, tn), lambda j, i, *_: (0, j))]
    out_specs = [pl.BlockSpec((tm, tn), lambda j, i, *_: (tile(i), j))]
    out_shape = [jax.ShapeDtypeStruct((m, C_WIDTH), out_dtype)]
    args = [h, w, v_gain.reshape(1, C_WIDTH)]
    if second is not None:
        h2, dtype2 = second
        in_specs.append(pl.BlockSpec(h2.shape, lambda j, i, *_: (0, 0)))
        out_specs.append(pl.BlockSpec((h2.shape[0], tn), lambda j, i, *_: (0, j)))
        out_shape.append(jax.ShapeDtypeStruct((h2.shape[0], C_WIDTH), dtype2))
        args.append(h2)
    outs, _ = _hosted_call("proj_" + act, (n_col, n_steps), lambda j, i: j * n_steps + i,
                           in_specs, out_specs, out_shape, [pltpu.VMEM((D_MODEL, tn), BF16)], phases, args)
    return outs[0] if second is None else tuple(outs)


def odd_in(h, w, v_gain, tm, act_dtype, second=None):
    u = proj_act(h, w, 0, v_gain, "gelu", tm, 1024, act_dtype, second)
    v = proj_act(h, w, C_WIDTH, v_gain, "gelu_rms", min(tm, 512), C_WIDTH, act_dtype, second)
    g = proj_act(h, w, 2 * C_WIDTH, v_gain, "silu", tm, 1024, act_dtype, second)
    if second is None:
        return u, v, g
    return (u[0], v[0], g[0]), (u[1], v[1], g[1])


def _spatial_kernel(u_ref, v_ref, g_ref, ws_ref, bs_ref, y_ref, *, chunk, n_chunks):
    tt = lax.broadcasted_iota(jnp.int32, (chunk, chunk), 0)
    ss = lax.broadcasted_iota(jnp.int32, (chunk, chunk), 1)
    causal = ss <= tt
    for grp in range(C_GROUPS):
        wm = jnp.where(causal, ws_ref[grp], 0.0)
        bcol = bs_ref[:, grp:grp + 1]
        cols = slice(grp * C_GDIM, (grp + 1) * C_GDIM)
        for c in range(n_chunks):
            rows = slice(c * chunk, (c + 1) * chunk)
            vv = v_ref[rows, cols]
            if chunk >= 128:
                sv = _dot(wm.astype(BF16), vv)
            else:
                vf = vv.astype(F32)
                sv = jnp.zeros((chunk, C_GDIM), F32)
                for s in range(chunk):
                    sv = sv + wm[:, s:s + 1] * vf[s:s + 1, :]
            sv = sv + bcol
            y = u_ref[rows, cols].astype(F32) * sv * g_ref[rows, cols].astype(F32)
            y_ref[rows, cols] = y.astype(y_ref.dtype)


def spatial_gate(u, v, g, w_s, b_s_t, chunk, n_chunks):
    m = u.shape[0]
    tm = chunk * n_chunks
    row_spec = pl.BlockSpec((tm, C_WIDTH), lambda i: (i, 0))
    return pl.pallas_call(
        functools.partial(_spatial_kernel, chunk=chunk, n_chunks=n_chunks),
        grid=(m // tm,),
        in_specs=[row_spec, row_spec, row_spec,
                  pl.BlockSpec((C_GROUPS, chunk, chunk), lambda i: (0, 0, 0)),
                  pl.BlockSpec((chunk, C_GROUPS), lambda i: (0, 0))],
        out_specs=row_spec,
        out_shape=jax.ShapeDtypeStruct((m, C_WIDTH), u.dtype),
        compiler_params=_params("arbitrary"),
        name="spatial_gate",
    )(u, v, g, w_s, b_s_t)


def _even_weights(w_in, b_i, b_f):
    gate0 = 5 * A_WIDTH
    b0 = gate0 + 2 * A_HEADS
    wt = jnp.swapaxes(w_in, 0, 1)
    w_gate = jnp.pad(w_in[:, gate0:b0], ((0, 0), (0, GATE_LANES - 2 * A_HEADS)))
    bias = jnp.pad(jnp.concatenate([b_i, b_f]), (0, GATE_LANES - 2 * A_HEADS)).reshape(1, GATE_LANES)
    return wt, b0, w_gate, bias


def _mlstm_inputs(qkvog, gates, bsz, t_len, valid_len, chunk):
    qkvog = qkvog.reshape(bsz, t_len, 5 * A_WIDTH)
    gates = gates.reshape(bsz, t_len, GATE_LANES)
    t_pad = -(-t_len // chunk) * chunk
    pad = ((0, 0), (0, t_pad - t_len), (0, 0))
    if valid_len < t_pad:
        qkvog, gates = jnp.pad(qkvog, pad), jnp.pad(gates, pad)
        pos = jnp.arange(t_pad)[None, :, None]
        lane = jnp.arange(GATE_LANES)[None, None, :]
        gates = jnp.where((pos >= valid_len) & (lane < A_HEADS), NEG_BIG, gates)
        gates = jnp.where((pos >= valid_len) & (lane >= A_HEADS), 0.0, gates)
    return qkvog, gates, gates[:, :, :2 * A_HEADS].transpose(0, 2, 1)


def _even_front(xp, xs, ew, g_norm, tm):
    wt, b0, w_gate, bias = ew
    hp, gates_p = norm_gates(xp, g_norm, w_gate, bias, min(tm, 512))
    hs, gates_s = norm_gates(xs, g_norm, w_gate, bias, xs.shape[0])
    tn, tm_kv = 1024, tm
    qkvog = proj(hp, wt, 0, 5 * A_WIDTH, BF16, 2 * tm, tn, second=(hs, BF16))
    qg_b = proj(hp, wt, b0, 2 * B_WIDTH, BF16, 2 * tm, tn, scale=B_DH ** -0.5, second=(hs, F32),
                row_step=3 * B_WIDTH)
    k = kv_proj(hp, wt, b0 + B_WIDTH, tm_kv, second=hs)
    v = kv_proj(hp, wt, b0 + 2 * B_WIDTH, tm_kv, second=hs)
    groups = []
    for i, gates in enumerate((gates_p, gates_s)):
        groups.append(dict(qkvog=qkvog[i], gates=gates, qg_b=qg_b[i],
                           k_new=k[2 * i], k_bf=k[2 * i + 1], v_new=v[2 * i], v_bf=v[2 * i + 1]))
    return groups


def _even_back(xp, xs, mix_p, mix_s, w_out, next_gain, tm):
    w_out_b = w_out.astype(BF16)
    flat = lambda mix, m: [mix[0].reshape(m, A_WIDTH), mix[1].reshape(m, B_WIDTH)]
    return out_proj_norm(flat(mix_p, xp.shape[0]), [w_out_b[:A_WIDTH], w_out_b[A_WIDTH:]], xp, next_gain,
                         tm, True, BF16, second=(flat(mix_s, xs.shape[0]), xs, BF16))


def _odd_layer(xp, hp, xs, hs, w_in, v_gain, w_s, b_s, w_out_b, final_gain, tm, n_seq):
    (u, v, g), (u2, v2, g2) = odd_in(hp, w_in, v_gain, 4 * tm, BF16, second=(hs, F32))
    y2 = spatial_gate(u2, v2, g2, w_s[:, :SAMPLE_PAD, :SAMPLE_PAD], b_s[:, :SAMPLE_PAD].T, SAMPLE_PAD, n_seq)
    y_p, y_s = out_proj_norm([], [w_out_b], xp, final_gain, tm, False, F32, second=([y2], xs, F32),
                             spatial=(u, v, g, w_s, b_s.T))
    return y_p, y_s, v2


def kernel(x_prompt, x_sample, state_a_C, state_a_n, state_a_m, cache_b_k, cache_b_v, page_table,
           even_norm, even_w_in, even_b_i, even_b_f, even_b_sb, even_w_out,
           odd_norm, odd_w_in, odd_v_gain, odd_w_s, odd_b_s, odd_w_out, final_norm):
    bsz, seq, _ = x_prompt.shape
    n_seq, dec_seq, _ = x_sample.shape
    n_pool = cache_b_k.shape[1]

    ew = _even_weights(even_w_in[0], even_b_i[0], even_b_f[0])
    odd_w_in_b = odd_w_in[0]
    odd_w_out_b = odd_w_out[0].astype(BF16)

    xp = x_prompt.reshape(bsz * seq, D_MODEL)
    xs = jnp.pad(x_sample, ((0, 0), (0, SAMPLE_PAD - dec_seq), (0, 0))).reshape(n_seq * SAMPLE_PAD, D_MODEL)
    fp, fs = _even_front(xp, xs, ew, even_norm[0], 1024)
    qkvog_p, gates_p, grow_p = _mlstm_inputs(fp["qkvog"], fp["gates"], bsz, seq, seq, A_CHUNK)
    qkvog_s, gates_s, grow_s = _mlstm_inputs(fs["qkvog"], fs["gates"], n_seq, SAMPLE_PAD, dec_seq,
                                             SAMPLE_CHUNK)
    qg_s = fs["qg_b"].reshape(n_seq, SAMPLE_PAD, 2 * B_WIDTH)
    q_s, g_s = qg_s[:, :, :B_WIDTH], qg_s[:, :, B_WIDTH:]

    assert cache_b_k.shape[0] == 1 and cache_b_v.shape[0] == 1
    page_view = (1, n_pool, PAGE_SIZE * B_HEADS, B_DH)
    cache_k, cache_v = cache_b_k.reshape(page_view), cache_b_v.reshape(page_view)
    kv_pad = ((0, 0), (0, (PAGE_SIZE - SAMPLE_PAD) * B_HEADS), (0, 0))
    k_new_s = jnp.pad(fs["k_new"].reshape(n_seq, SAMPLE_PAD * B_HEADS, B_DH), kv_pad)
    v_new_s = jnp.pad(fs["v_new"].reshape(n_seq, SAMPLE_PAD * B_HEADS, B_DH), kv_pad)
    bias_rows = jnp.broadcast_to(jnp.repeat(even_b_sb[0], SAMPLE_PAD)[:, None],
                                 (B_HEADS * SAMPLE_PAD, GATE_LANES))
    att_rows = B_HEADS * SAMPLE_PAD
    rider = _PagedAttnRider(
        q_s, k_new_s, v_new_s, cache_k, cache_v, page_table, bias_rows,
        jnp.zeros((n_seq, att_rows, B_WIDTH), F32), jnp.zeros((n_seq, att_rows, GATE_LANES), F32),
        0, PAGES_PER_STEP)
    assert bsz * (seq // A_CHUNK) == n_seq * rider.steps_per_seq
    zero_state = (jnp.zeros((bsz, A_HEADS, A_DH, A_DH), F32),
                  jnp.zeros((bsz, A_HEADS, 1, A_DH), F32),
                  jnp.zeros((bsz, A_HEADS, 1, GATE_LANES), F32))
    (ha_p, c_p, n_p, m_p), (att_acc, _) = mlstm(
        qkvog_p, gates_p, grow_p, *zero_state, A_CHUNK, rider=rider)
    hb_s = attn_sample_finish(att_acc, g_s)

    as_seq = lambda a: a.reshape(bsz, seq, B_WIDTH)
    hb_p = attn_prompt(fp["qg_b"].reshape(bsz, seq, 2 * B_WIDTH), as_seq(fp["k_bf"]), as_seq(fp["v_bf"]),
                       even_b_sb[0], ATTN_BQ, ATTN_BK, ATTN_HEADS_PER_STEP)
    st_in = (state_a_C[0], state_a_n[0][:, :, None, :],
             jnp.broadcast_to(state_a_m[0][:, :, None, None], (n_seq, A_HEADS, 1, GATE_LANES)))
    ha_s, c_s, n_s, m_s_new = mlstm(qkvog_s, gates_s, grow_s, *st_in, SAMPLE_CHUNK)
    ha_s = ha_s[:, :SAMPLE_PAD]

    xp1, hp1, xs1, hs1 = _even_back(xp, xs, (ha_p, hb_p), (ha_s, hb_s), even_w_out[0], odd_norm[0], 512)
    y_p, y_s, v_rows = _odd_layer(xp1, hp1, xs1, hs1, odd_w_in_b, odd_v_gain[0], odd_w_s[0], odd_b_s[0],
                                  odd_w_out_b, final_norm, 512, n_seq)

    def sample_rows(a, *dims):
        return a.reshape((n_seq, SAMPLE_PAD) + dims)[:, :dec_seq]

    return (y_p.reshape(bsz, seq, D_MODEL),
            sample_rows(y_s, D_MODEL),
            c_p[None], n_p[:, :, 0, :][None], m_p[:, :, 0, 0][None],
            c_s[None], n_s[:, :, 0, :][None], m_s_new[:, :, 0, 0][None],
            fp["k_new"].reshape(1, bsz, seq, B_HEADS, B_DH), fp["v_new"].reshape(1, bsz, seq, B_HEADS, B_DH),
            sample_rows(fs["k_new"], B_HEADS, B_DH)[None], sample_rows(fs["v_new"], B_HEADS, B_DH)[None],
            sample_rows(v_rows, C_WIDTH)[None])
```

```python
import functools

import jax
import jax.numpy as jnp
from jax import lax
from jax.experimental import pallas as pl
from jax.experimental.pallas import tpu as pltpu

F32 = jnp.float32
BF16 = jnp.bfloat16

D_MODEL = 2048
PAGE_SIZE = 128
A_HEADS = 4
A_DH = 256
A_WIDTH = A_HEADS * A_DH
A_CHUNK = 128
B_HEADS = 8
B_DH = 128
B_WIDTH = B_HEADS * B_DH
C_WIDTH = D_MODEL
C_GROUPS = 8
C_GDIM = C_WIDTH // C_GROUPS
C_CHUNK = 128
RMS_EPS = 1e-6
GATE_LANES = 128
NEG_BIG = -1e30
SAMPLE_PAD = 8
SAMPLE_CHUNK = 16
ATTN_BQ = 512
ATTN_BK = 512
CUMSUM_BLOCK = 256
ATTN_HEADS_PER_STEP = 4
PAGES_PER_STEP = 16

VMEM_LIMIT_BYTES = 56 * 1024 * 1024


def _params(*sem):
    return pltpu.CompilerParams(dimension_semantics=sem, vmem_limit_bytes=VMEM_LIMIT_BYTES)


def _dot(a, b):
    return jnp.dot(a, b, preferred_element_type=F32)


def _dot_nt(a, b):
    return lax.dot_general(a, b, (((1,), (1,)), ((), ())), preferred_element_type=F32)


def _dot_tn(a, b):
    return lax.dot_general(a, b, (((0,), (0,)), ((), ())), preferred_element_type=F32)


def _softplus(z):
    return jnp.maximum(z, 0.0) + jnp.log(1.0 + jnp.exp(-jnp.abs(z)))


def _sigmoid(z):
    return 1.0 / (1.0 + jnp.exp(-z))


def _silu(z):
    return z * _sigmoid(z)


def _gelu_tanh(x):
    c = 0.7978845608028654
    return x * (0.5 * (1.0 + jnp.tanh(c * (x + 0.044715 * (x * x * x)))))


def _rms(x, g):
    return x * lax.rsqrt(jnp.mean(x * x, axis=-1, keepdims=True) + RMS_EPS) * g


def _split_hi_lo(x):
    hi = x.astype(BF16)
    lo = (x - hi.astype(F32)).astype(BF16)
    return hi, lo


def _norm_gates_kernel(x_ref, g_ref, w_ref, bias_ref, h_ref, gate_ref):
    h = _rms(x_ref[...], g_ref[...])
    h_hi, h_lo = _split_hi_lo(h)
    h_ref[...] = h_hi
    a = _dot(h_hi, w_ref[...])
    b = _dot(h_lo, w_ref[...])
    pre = a[:, :GATE_LANES] + a[:, GATE_LANES:] + b[:, :GATE_LANES] + bias_ref[...]
    lane = lax.broadcasted_iota(jnp.int32, pre.shape, 1)
    is_forget = (lane >= A_HEADS) & (lane < 2 * A_HEADS)
    gate_ref[...] = jnp.where(is_forget, -_softplus(-pre), pre)


def norm_gates(x, gain, w_gate, bias, tm):
    m = x.shape[0]
    w_hi_lo = jnp.concatenate(_split_hi_lo(w_gate), axis=1)
    return pl.pallas_call(
        _norm_gates_kernel,
        grid=(m // tm,),
        in_specs=[pl.BlockSpec((tm, D_MODEL), lambda i: (i, 0)),
                  pl.BlockSpec((1, D_MODEL), lambda i: (0, 0)),
                  pl.BlockSpec((D_MODEL, 2 * GATE_LANES), lambda i: (0, 0)),
                  pl.BlockSpec((1, GATE_LANES), lambda i: (0, 0))],
        out_specs=[pl.BlockSpec((tm, D_MODEL), lambda i: (i, 0)),
                   pl.BlockSpec((tm, GATE_LANES), lambda i: (i, 0))],
        out_shape=[jax.ShapeDtypeStruct((m, D_MODEL), BF16),
                   jax.ShapeDtypeStruct((m, GATE_LANES), F32)],
        compiler_params=_params("arbitrary"),
        name="norm_gates",
    )(x, gain.reshape(1, D_MODEL), w_hi_lo, bias)


def _hosted_kernel(*refs, n_in, n_out, phases_fn, lin, rank, rider):
    if rider is not None:
        refs = refs[1:]
    r_in, r_out = (rider.n_in, rider.n_out) if rider is not None else (0, 0)
    ins, refs = refs[:n_in], refs[n_in:]
    r_ins, refs = refs[:r_in], refs[r_in:]
    outs, refs = refs[:n_out], refs[n_out:]
    r_outs, refs = refs[:r_out], refs[r_out:]
    n_sc = len(refs) - (rider.n_scratch if rider is not None else 0)
    scratch, r_scratch = refs[:n_sc], refs[n_sc:]
    sets = [phases_fn(ins, outs, scratch)]
    if rider is not None:
        step = lin(*[pl.program_id(d) for d in range(rank)])
        sets.append(rider.phases(r_ins, r_outs, r_scratch, step))
    for pre, _, _ in sets:
        for cond, fn in pre:
            pl.when(cond)(fn)
    for _, parts, _ in sets:
        for part in parts:
            part()
    for _, _, post in sets:
        for cond, fn in post:
            pl.when(cond)(fn)


def _hosted_call(name, grid, lin, in_specs, out_specs, out_shape, scratch, phases_fn, args, rider=None):
    n_in, n_out = len(in_specs), len(out_specs)
    aliases = {}
    if rider is not None:
        r = rider.specs(lin, len(grid))
        aliases = {1 + n_in + i: n_out + o for i, o in r["aliases"].items()}
        in_specs, out_specs = in_specs + r["in_specs"], out_specs + r["out_specs"]
        out_shape, scratch = out_shape + r["out_shape"], scratch + r["scratch"]
        args = [rider.page_table] + list(args) + r["args"]
    kern = functools.partial(_hosted_kernel, n_in=n_in, n_out=n_out, phases_fn=phases_fn,
                             lin=lin, rank=len(grid), rider=rider)
    grid_spec = pltpu.PrefetchScalarGridSpec(
        num_scalar_prefetch=0 if rider is None else 1, grid=grid,
        in_specs=in_specs, out_specs=out_specs, scratch_shapes=scratch)
    outs = pl.pallas_call(
        kern, grid_spec=grid_spec, out_shape=out_shape, input_output_aliases=aliases,
        compiler_params=_params(*(["arbitrary"] * len(grid))), name=name,
    )(*args)
    return outs[:n_out], outs[n_out:]


def _two_group_steps(n_i, has_second):
    if not has_second:
        return n_i, (lambda i: i)
    return n_i + 1, (lambda i: jnp.maximum(i - 1, 0))


def proj(a, wt, row0, n_out, out_dtype, tm, tn, scale=None, second=None, row_step=None):
    m, k = a.shape
    n_i = m // tm
    n_steps, tile = _two_group_steps(n_i, second is not None)
    row_step = tn if row_step is None else row_step

    def phases(ins, outs, scratch):
        a_ref, wt_ref = ins[:2]
        o_ref, (wbf,) = outs[0], scratch
        i = pl.program_id(1)

        def cast():
            wbf[...] = wt_ref[...].astype(BF16)

        def main():
            y = _dot_nt(a_ref[...], wbf[...])
            if scale is not None:
                y = y * jnp.where(pl.program_id(0) == 0, scale, 1.0)
            o_ref[...] = y.astype(o_ref.dtype)

        if second is None:
            return [(i == 0, cast)], [main], []

        def small():
            outs[1][...] = _dot_nt(ins[2][...], wbf[...]).astype(outs[1].dtype)

        return [(i == 0, cast), (i == 0, small), (i > 0, main)], [], []

    in_specs = [pl.BlockSpec((tm, k), lambda j, i, *_: (tile(i), 0)),
                pl.BlockSpec((pl.Element(tn), pl.Element(k)),
                             lambda j, i, *_: (pl.multiple_of(row0 + j * row_step, 8), 0))]
    out_specs = [pl.BlockSpec((tm, tn), lambda j, i, *_: (tile(i), j))]
    out_shape = [jax.ShapeDtypeStruct((m, n_out), out_dtype)]
    args = [a, wt]
    if second is not None:
        a2, dtype2 = second
        in_specs.append(pl.BlockSpec(a2.shape, lambda j, i, *_: (0, 0)))
        out_specs.append(pl.BlockSpec((a2.shape[0], tn), lambda j, i, *_: (0, j)))
        out_shape.append(jax.ShapeDtypeStruct((a2.shape[0], n_out), dtype2))
        args.append(a2)
    outs, _ = _hosted_call("proj", (n_out // tn, n_steps), lambda j, i: j * n_steps + i,
                           in_specs, out_specs, out_shape, [pltpu.VMEM((tn, k), BF16)], phases, args)
    return outs[0] if second is None else tuple(outs)


def kv_proj(a, wt, row0, tm, second=None):
    m, k = a.shape
    n_i = m // tm
    n_steps, tile = _two_group_steps(n_i, second is not None)

    def phases(ins, outs, scratch):
        a_ref, wt_ref = ins[:2]
        (wbf,) = scratch
        i = pl.program_id(0)

        def cast():
            wbf[...] = wt_ref[...].astype(BF16)

        def rows_to(a_rows_ref, o_ref, obf_ref):
            def run():
                n_rows = a_rows_ref.shape[0]
                y = _dot_nt(a_rows_ref[...], wbf[...])
                obf_ref[...] = y.astype(BF16)
                for h in range(B_HEADS):
                    o_ref[pl.ds(h, n_rows, stride=B_HEADS), :] = y[:, h * B_DH:(h + 1) * B_DH]
            return run

        main = rows_to(a_ref, outs[0], outs[1])
        if second is None:
            return [(i == 0, cast)], [main], []
        return [(i == 0, cast), (i == 0, rows_to(ins[2], outs[2], outs[3])), (i > 0, main)], [], []

    def out_pair(rows, index):
        return ([pl.BlockSpec((rows * B_HEADS, B_DH), index), pl.BlockSpec((rows, B_WIDTH), index)],
                lambda total: [jax.ShapeDtypeStruct((total * B_HEADS, B_DH), F32),
                               jax.ShapeDtypeStruct((total, B_WIDTH), BF16)])

    in_specs = [pl.BlockSpec((tm, k), lambda i, *_: (tile(i), 0)),
                pl.BlockSpec((pl.Element(B_WIDTH), pl.Element(k)), lambda i, *_: (row0, 0))]
    out_specs, shapes = out_pair(tm, lambda i, *_: (tile(i), 0))
    out_shape = shapes(m)
    args = [a, wt]
    if second is not None:
        m2 = second.shape[0]
        in_specs.append(pl.BlockSpec(second.shape, lambda i, *_: (0, 0)))
        specs2, shapes2 = out_pair(m2, lambda i, *_: (0, 0))
        out_specs, out_shape = out_specs + specs2, out_shape + shapes2(m2)
        args.append(second)
    outs, _ = _hosted_call("kv_proj", (n_steps,), lambda i: i, in_specs, out_specs, out_shape,
                           [pltpu.VMEM((B_WIDTH, k), BF16)], phases, args)
    return tuple(outs)


def out_proj_norm(lhs, ws, x, gain, tm, emit_x, norm_dtype, second=None, spatial=None):
    m = x.shape[0]
    n_lhs, n_w = len(lhs), len(ws)
    n_i = m // tm
    n_steps, tile = _two_group_steps(n_i, second is not None)
    n_out = 2 if emit_x else 1

    def phases(ins, outs, scratch):
        w_refs = ins[n_lhs:n_lhs + n_w]
        x_ref, g_ref = ins[n_lhs + n_w], ins[n_lhs + n_w + 1]
        rest = ins[n_lhs + n_w + 2:]
        i = pl.program_id(0)

        def group(make_lhs, a_refs, x_ref, out_refs, row_chunk):
            def run():
                if make_lhs is not None:
                    make_lhs()
                for r in range(0, x_ref.shape[0], row_chunk):
                    rows = slice(r, r + row_chunk)
                    y = x_ref[rows, :]
                    for a_ref, w_ref in zip(a_refs, w_refs):
                        y = y + _dot(a_ref[rows, :].astype(BF16), w_ref[...])
                    if emit_x:
                        out_refs[0][rows, :] = y
                    out_refs[-1][rows, :] = _rms(y, g_ref[...]).astype(out_refs[-1].dtype)
            return run

        if second is not None:
            a2_refs, x2_ref, rest = rest[:n_w], rest[n_w], rest[n_w + 1:]
        if spatial is None:
            main = group(None, ins[:n_lhs], x_ref, outs[:n_out], min(tm, 256))
        else:
            (y_sc,) = scratch
            fill = functools.partial(_spatial_kernel, *rest[:5], y_sc, chunk=C_CHUNK, n_chunks=tm // C_CHUNK)
            main = group(fill, [y_sc], x_ref, outs[:n_out], min(tm, 256))
        if second is None:
            return [], [main], []
        small = group(None, a2_refs, x2_ref, outs[n_out:], x2_ref.shape[0])
        return [(i == 0, small), (i > 0, main)], [], []

    row_spec = lambda width: pl.BlockSpec((tm, width), lambda i, *_: (tile(i), 0))
    whole = lambda arr: pl.BlockSpec(arr.shape, lambda i, *_: (0,) * arr.ndim)
    gain2d = gain.reshape(1, D_MODEL)
    in_specs = ([row_spec(a.shape[1]) for a in lhs] + [whole(w) for w in ws]
                + [row_spec(D_MODEL), whole(gain2d)])
    out_specs = [row_spec(D_MODEL)] * n_out
    out_shape = ([jax.ShapeDtypeStruct((m, D_MODEL), F32)] if emit_x else []) \
        + [jax.ShapeDtypeStruct((m, D_MODEL), norm_dtype)]
    args = [*lhs, *ws, x, gain2d]
    scratch = []
    if second is not None:
        lhs2, x2, norm_dtype2 = second
        m2 = x2.shape[0]
        in_specs += [whole(a) for a in lhs2] + [whole(x2)]
        out_specs += [pl.BlockSpec((m2, D_MODEL), lambda i, *_: (0, 0))] * n_out
        out_shape += ([jax.ShapeDtypeStruct((m2, D_MODEL), F32)] if emit_x else []) \
            + [jax.ShapeDtypeStruct((m2, D_MODEL), norm_dtype2)]
        args += [*lhs2, x2]
    if spatial is not None:
        u, v, gate, w_s, b_s_t = spatial
        in_specs += [row_spec(C_WIDTH)] * 3 + [whole(w_s), whole(b_s_t)]
        args += [u, v, gate, w_s, b_s_t]
        scratch = [pltpu.VMEM((tm, C_WIDTH), BF16)]
    outs, _ = _hosted_call("out_proj_norm", (n_steps,), lambda i: i, in_specs, out_specs, out_shape,
                           scratch, phases, args)
    return list(outs)


def _mlstm_chunk(q_ref, k_ref, v_ref, og_ref, gg_ref, gcol_ref, grow_ref, h_ref, c_sc, n_sc, m_sc, chunk, heads):
    L = chunk
    gcol = gcol_ref[0]
    grow = grow_ref[0]
    tt = lax.broadcasted_iota(jnp.int32, (L, L), 0)
    ss = lax.broadcasted_iota(jnp.int32, (L, L), 1)
    causal = ss <= tt

    for head in heads:
        cols = slice(head * A_DH, (head + 1) * A_DH)
        q = q_ref[0, :, cols]
        ks = k_ref[0, :, cols] * jnp.asarray(A_DH ** -0.5, BF16)
        v = v_ref[0, :, cols]
        ig_col = gcol[:, head:head + 1]
        lf_col = gcol[:, head + A_HEADS:head + A_HEADS + 1]
        ig_row = grow[head:head + 1, :]
        lf_row = grow[head + A_HEADS:head + A_HEADS + 1, :]
        b_col = jnp.sum(jnp.where(causal, lf_row, 0.0), axis=1, keepdims=True)
        b_row = jnp.sum(jnp.where(tt <= ss, lf_col, 0.0), axis=0, keepdims=True)
        b_last = jnp.sum(lf_row, axis=1, keepdims=True)

        m0 = m_sc[head][:, :1]
        n0 = n_sc[head]
        c0 = c_sc[head]

        d = jnp.where(causal, b_col - b_row + ig_row, NEG_BIG)
        m_carry = b_col + m0
        m = jnp.maximum(m_carry, jnp.max(d, axis=1, keepdims=True))
        w_intra = jnp.exp(d - m)
        w_carry = jnp.exp(m_carry - m)
        s = _dot_nt(q, ks) * w_intra
        qf = q.astype(F32)
        num = _dot(s.astype(BF16), v) + w_carry * _dot_nt(q, c0.astype(BF16))
        den = jnp.sum(s, axis=1, keepdims=True) + w_carry * jnp.sum(qf * n0, axis=1, keepdims=True)
        h = num / jnp.maximum(jnp.abs(den), jnp.exp(-m))
        gated = h * _sigmoid(og_ref[0, :, cols].astype(F32)) * _silu(gg_ref[0, :, cols].astype(F32))
        h_ref[0, :, cols] = gated.astype(h_ref.dtype)

        m_carry_last = b_last + m0
        d_last_row = b_last - b_row + ig_row
        m_new = jnp.maximum(m_carry_last, jnp.max(d_last_row, axis=1, keepdims=True))
        wc_last = jnp.exp(m_carry_last - m_new)
        w_last_col = jnp.exp(b_last - b_col + ig_col - m_new)
        vw = (v.astype(F32) * w_last_col).astype(BF16)
        c_new = wc_last * c0 + _dot_tn(vw, ks)
        n_new = wc_last * n0 + jnp.sum(ks.astype(F32) * w_last_col, axis=0, keepdims=True)
        c_sc[head] = c_new
        n_sc[head] = n_new
        m_sc[head] = jnp.broadcast_to(m_new, (1, GATE_LANES))


def mlstm(qkvog, gcol, grow, c0, n0, m0, chunk, rider=None):
    bsz, t_len, _ = qkvog.shape
    nc = t_len // chunk
    hd = A_HEADS

    def phases(ins, outs, state):
        c0_ref, n0_ref, m0_ref = ins[7:]
        h_ref, c_out_ref, n_out_ref, m_out_ref = outs
        c_sc, n_sc, m_sc = state
        ci = pl.program_id(1)

        def init():
            c_sc[...] = c0_ref[0]
            n_sc[...] = n0_ref[0]
            m_sc[...] = m0_ref[0]

        def head_part(head):
            return lambda: _mlstm_chunk(*ins[:7], h_ref, c_sc, n_sc, m_sc, chunk, (head,))

        parts = [head_part(head) for head in range(A_HEADS)]

        def final():
            c_out_ref[0] = c_sc[...]
            n_out_ref[0] = n_sc[...]
            m_out_ref[0] = m_sc[...]

        return [(ci == 0, init)], parts, [(ci == nc - 1, final)]

    blk = lambda seg: pl.BlockSpec((1, chunk, A_WIDTH), lambda b, c, *_, seg=seg: (b, c, seg))
    st4 = lambda r, w: pl.BlockSpec((1, hd, r, w), lambda b, c, *_: (b, 0, 0, 0))
    outs, rest = _hosted_call(
        "mlstm", (bsz, nc), lambda b, c: b * nc + c,
        [blk(0), blk(1), blk(2), blk(3), blk(4),
         pl.BlockSpec((1, chunk, GATE_LANES), lambda b, c, *_: (b, c, 0)),
         pl.BlockSpec((1, 8, chunk), lambda b, c, *_: (b, 0, c)),
         st4(A_DH, A_DH), st4(1, A_DH), st4(1, GATE_LANES)],
        [pl.BlockSpec((1, chunk, A_WIDTH), lambda b, c, *_: (b, c, 0)),
         st4(A_DH, A_DH), st4(1, A_DH), st4(1, GATE_LANES)],
        [jax.ShapeDtypeStruct((bsz, t_len, A_WIDTH), BF16),
         jax.ShapeDtypeStruct((bsz, hd, A_DH, A_DH), F32),
         jax.ShapeDtypeStruct((bsz, hd, 1, A_DH), F32),
         jax.ShapeDtypeStruct((bsz, hd, 1, GATE_LANES), F32)],
        [pltpu.VMEM((hd, A_DH, A_DH), F32), pltpu.VMEM((hd, 1, A_DH), F32),
         pltpu.VMEM((hd, 1, GATE_LANES), F32)],
        phases, [qkvog, qkvog, qkvog, qkvog, qkvog, gcol, grow, c0, n0, m0], rider)
    return tuple(outs) if rider is None else (tuple(outs), rest)


def _stick_block(q, kb, vb, bias, run, mask, upper):
    rows = q.shape[0]
    sub = upper.shape[0]
    n_sub = kb.shape[0] // sub
    z = _dot_nt(q, kb) if bias is None else _dot_nt(q, kb) * (B_DH ** -0.5) + bias
    sp = _softplus(z)
    spm = sp if mask is None else jnp.where(mask, sp, 0.0)
    hi, lo = _split_hi_lo(spm)
    laters = [None] * n_sub
    total = None
    for i in reversed(range(n_sub)):
        ln = slice(i * sub, (i + 1) * sub)
        both = _dot(jnp.concatenate([hi[:, ln], lo[:, ln]], axis=0), upper)
        carry = run if total is None else run + total
        laters[i] = both[:rows] + both[rows:] + carry
        part = jnp.sum(spm[:, ln], axis=1, keepdims=True)
        total = part if total is None else total + part
    later = laters[0] if n_sub == 1 else jnp.concatenate(laters, axis=1)
    a = jnp.exp(z - sp - later)
    if mask is not None:
        a = jnp.where(mask, a, 0.0)
    return _dot(a.astype(BF16), vb), total


def _strict_upper(n):
    j = lax.broadcasted_iota(jnp.int32, (n, n), 0)
    s = lax.broadcasted_iota(jnp.int32, (n, n), 1)
    return jnp.where(j > s, 1.0, 0.0).astype(BF16)


def _attn_prompt_kernel(bias_ref, q_ref, k_ref, v_ref, g_ref, o_ref, *, bq, bk, n_heads):
    head0 = pl.program_id(1) * n_heads
    qi = pl.program_id(2)
    kbf = k_ref.at[0]
    vbf = v_ref.at[0]
    upper = _strict_upper(min(bk, CUMSUM_BLOCK))
    row = lax.broadcasted_iota(jnp.int32, (bq, bk), 0)
    col = lax.broadcasted_iota(jnp.int32, (bq, bk), 1)
    lanes = [slice(h * B_DH, (h + 1) * B_DH) for h in range(n_heads)]
    lane_q = lax.broadcasted_iota(jnp.int32, (bq, B_DH), 1)
    lane_k = lax.broadcasted_iota(jnp.int32, (bk, B_DH), 1)
    ones_cols = jnp.where(lane_q < 2, 1.0, 0.0).astype(BF16)
    qs = [jnp.concatenate([q_ref[0, :, ln], ones_cols], axis=1) for ln in lanes]
    bias_cols = []
    for h in range(n_heads):
        b = jnp.full((bk, B_DH), bias_ref[head0 + h], F32)
        b_hi = b.astype(BF16).astype(F32)
        bias_cols.append(jnp.where(lane_k == 0, b_hi, jnp.where(lane_k == 1, b - b_hi, 0.0)).astype(BF16))

    def blocks(kj, runs, mask):
        start = pl.multiple_of(kj * bk, bk)
        return [_stick_block(qs[h], jnp.concatenate([kbf[pl.ds(start, bk), ln], bias_cols[h]], axis=1),
                             vbf[pl.ds(start, bk), ln], None, runs[h], mask, upper)
                for h, ln in enumerate(lanes)]

    q0 = qi * bq
    n_full = q0 // bk
    if bq == bk:
        half = bq // 2
        start = pl.multiple_of(qi * bq, bq)
        accs, runs = [], []
        for h, ln in enumerate(lanes):
            kb = jnp.concatenate([kbf[pl.ds(start, bk), ln], bias_cols[h]], axis=1)
            vb = vbf[pl.ds(start, bk), ln]
            zero = jnp.zeros((half, 1), F32)
            iota = lambda shape, dim: lax.broadcasted_iota(jnp.int32, shape, dim)
            top = _stick_block(qs[h][:half], kb[:half], vb[:half], None, zero,
                               iota((half, half), 1) < iota((half, half), 0), upper)
            bot = _stick_block(qs[h][half:], kb, vb, None, zero,
                               iota((half, bk), 1) < iota((half, bk), 0) + half, upper)
            accs.append(jnp.concatenate([top[0], bot[0]], axis=0))
            runs.append(jnp.concatenate([top[1], bot[1]], axis=0))
    else:
        accs = [jnp.zeros((bq, B_DH), F32)] * n_heads
        runs = [jnp.zeros((bq, 1), F32)] * n_heads
        for m in reversed(range(max(1, bq // bk))):
            kj = n_full + m
            res = blocks(kj, runs, col + (kj * bk - q0) < row)
            accs = [a + c for a, (c, _) in zip(accs, res)]
            runs = [r + t for r, (_, t) in zip(runs, res)]

    def body(it, carry):
        accs, runs = carry
        res = blocks(n_full - 1 - it, runs, None)
        return (tuple(a + c for a, (c, _) in zip(accs, res)),
                tuple(r + t for r, (_, t) in zip(runs, res)))

    accs, runs = lax.fori_loop(0, n_full, body, (tuple(accs), tuple(runs)))
    for h, ln in enumerate(lanes):
        o_ref[0, :, ln] = (accs[h] * _silu(g_ref[0, :, ln].astype(F32))).astype(o_ref.dtype)


def attn_prompt(qg, k, v, b_sb, bq, bk, n_heads):
    bsz, t_len, _ = k.shape
    width = n_heads * B_DH
    n_hgrp = B_HEADS // n_heads
    q_spec = pl.BlockSpec((1, bq, width), lambda b, h, i: (b, i, h))
    g_spec = pl.BlockSpec((1, bq, width), lambda b, h, i: (b, i, n_hgrp + h))
    kv_spec = pl.BlockSpec((1, t_len, width), lambda b, h, i: (b, 0, h))
    return pl.pallas_call(
        functools.partial(_attn_prompt_kernel, bq=bq, bk=bk, n_heads=n_heads),
        grid=(bsz, n_hgrp, t_len // bq),
        in_specs=[pl.BlockSpec(memory_space=pltpu.SMEM), q_spec, kv_spec, kv_spec, g_spec],
        out_specs=q_spec,
        out_shape=jax.ShapeDtypeStruct((bsz, t_len, B_WIDTH), BF16),
        compiler_params=_params("arbitrary", "arbitrary", "arbitrary"),
        name="attn_prompt",
    )(b_sb, qg, k, v, qg)


def _attn_sample_phases(ins, outs, scratch, n_group):
    bias_ref, q_ref, knew_ref, vnew_ref, acc_in_ref, run_in_ref = ins[:6]
    k_refs = ins[6:6 + n_group]
    v_refs = ins[6 + n_group:]
    acc_out_ref, run_out_ref = outs
    qbd, acc, run, kcat, vcat = scratch
    rows = B_HEADS * SAMPLE_PAD
    upper = _strict_upper(PAGE_SIZE)
    bias = bias_ref[...][:, :1]

    def repack(page, dst, i):
        for h in range(B_HEADS):
            dst[i * PAGE_SIZE:(i + 1) * PAGE_SIZE, h * B_DH:(h + 1) * B_DH] = page(h).astype(BF16)

    def step(slot0, n_blk, mask):
        keys = slice(slot0 * PAGE_SIZE, (slot0 + n_blk) * PAGE_SIZE)
        z = _dot_nt(qbd[...], kcat[keys, :]) * (B_DH ** -0.5) + bias
        sp = _softplus(z)
        spm = sp if mask is None else jnp.where(mask, sp, 0.0)
        hi, lo = _split_hi_lo(spm)
        carry = run[...][:, :1]
        laters = []
        for i in range(n_blk):
            ln = slice(i * PAGE_SIZE, (i + 1) * PAGE_SIZE)
            both = _dot(jnp.concatenate([hi[:, ln], lo[:, ln]], axis=0), upper)
            laters.append(both[:rows] + both[rows:] + carry)
            carry = carry + jnp.sum(spm[:, ln], axis=1, keepdims=True)
        later = laters[0] if n_blk == 1 else jnp.concatenate(laters, axis=1)
        a = jnp.exp(z - sp - later)
        if mask is not None:
            a = jnp.where(mask, a, 0.0)
        acc[...] += _dot(a.astype(BF16), vcat[keys, :])
        run[...] = jnp.broadcast_to(carry, run.shape)

    def build_queries():
        r = lax.broadcasted_iota(jnp.int32, (rows, B_WIDTH), 0)
        c = lax.broadcasted_iota(jnp.int32, (rows, B_WIDTH), 1)
        q_rep = jnp.concatenate([q_ref[0].astype(F32)] * B_HEADS, axis=0)
        qbd[...] = jnp.where((r // SAMPLE_PAD) == (c // B_DH), q_rep, 0.0).astype(BF16)

    def start_sequence():
        build_queries()
        acc[...] = jnp.zeros_like(acc)
        run[...] = jnp.zeros_like(run)
        repack(lambda h: knew_ref[0, pl.ds(h, PAGE_SIZE, stride=B_HEADS), :], kcat, 0)
        repack(lambda h: vnew_ref[0, pl.ds(h, PAGE_SIZE, stride=B_HEADS), :], vcat, 0)
        t = lax.broadcasted_iota(jnp.int32, (rows, PAGE_SIZE), 0) % SAMPLE_PAD
        s = lax.broadcasted_iota(jnp.int32, (rows, PAGE_SIZE), 1)
        step(0, 1, s < t)

    def resume_sequence():
        build_queries()
        acc[...] = acc_in_ref[0]
        run[...] = run_in_ref[0]

    def pages():
        for i in range(n_group):
            repack(lambda h, r=k_refs[i]: r[0, 0, pl.ds(h, PAGE_SIZE, stride=B_HEADS), :], kcat, i)
            repack(lambda h, r=v_refs[i]: r[0, 0, pl.ds(h, PAGE_SIZE, stride=B_HEADS), :], vcat, i)
        step(0, n_group, None)
        acc_out_ref[0] = acc[...]
        run_out_ref[0] = run[...]

    return start_sequence, resume_sequence, pages


class _PagedAttnRider:
    n_out = 2
    n_scratch = 5

    def __init__(self, q, k_new, v_new, cache_k, cache_v, page_table, bias_rows, acc, run, first_step, n_group):
        self.arrays = (bias_rows, q, k_new, v_new, acc, run)
        self.caches = (cache_k, cache_v)
        self.page_table = page_table
        self.first_step = first_step
        self.n_group = n_group
        self.n_in = 6 + 2 * n_group
        self.steps_per_seq = page_table.shape[1] // n_group

    def specs(self, lin, rank):
        n_group, spq = self.n_group, self.steps_per_seq
        n_pages = self.page_table.shape[1]
        rows = B_HEADS * SAMPLE_PAD
        page_rows = PAGE_SIZE * B_HEADS
        gstep = lambda a: self.first_step + lin(*a[:rank])
        seq_map = lambda *a: (gstep(a) // spq, 0, 0)

        def page_spec(i):
            def index(*a):
                g, pt = gstep(a), a[rank]
                return (0, pt[g // spq, n_pages - 1 - ((g % spq) * n_group + i)], 0, 0)
            return pl.BlockSpec((1, 1, page_rows, B_DH), index)

        acc_spec = pl.BlockSpec((1, rows, B_WIDTH), seq_map)
        run_spec = pl.BlockSpec((1, rows, GATE_LANES), seq_map)
        new_spec = pl.BlockSpec((1, page_rows, B_DH), seq_map)
        acc, run = self.arrays[4:]
        return dict(
            in_specs=[pl.BlockSpec((rows, GATE_LANES), lambda *a: (0, 0)),
                      pl.BlockSpec((1, SAMPLE_PAD, B_WIDTH), seq_map), new_spec, new_spec,
                      acc_spec, run_spec] + [page_spec(i) for i in range(n_group)] * 2,
            out_specs=[acc_spec, run_spec],
            out_shape=[jax.ShapeDtypeStruct(acc.shape, F32), jax.ShapeDtypeStruct(run.shape, F32)],
            scratch=[pltpu.VMEM((rows, B_WIDTH), BF16), pltpu.VMEM((rows, B_WIDTH), F32),
                     pltpu.VMEM((rows, GATE_LANES), F32),
                     pltpu.VMEM((n_group * PAGE_SIZE, B_WIDTH), BF16),
                     pltpu.VMEM((n_group * PAGE_SIZE, B_WIDTH), BF16)],
            args=list(self.arrays) + [self.caches[0]] * n_group + [self.caches[1]] * n_group,
            aliases={4: 0, 5: 1},
        )

    def phases(self, ins, outs, scratch, local_step):
        start, resume, pages = _attn_sample_phases(ins, outs, scratch, self.n_group)
        p = (self.first_step + local_step) % self.steps_per_seq
        return [(p == 0, start), ((local_step == 0) & (p != 0), resume)], [pages], []


def attn_sample_finish(acc, g):
    n_seq = acc.shape[0]

    def kern(acc_ref, g_ref, o_ref):
        a = acc_ref[0]
        c = lax.broadcasted_iota(jnp.int32, (SAMPLE_PAD, B_WIDTH), 1) // B_DH
        out = jnp.zeros((SAMPLE_PAD, B_WIDTH), F32)
        for h in range(B_HEADS):
            out = out + jnp.where(c == h, a[h * SAMPLE_PAD:(h + 1) * SAMPLE_PAD, :], 0.0)
        o_ref[0] = out * _silu(g_ref[0])

    return pl.pallas_call(
        kern, grid=(n_seq,),
        in_specs=[pl.BlockSpec((1, B_HEADS * SAMPLE_PAD, B_WIDTH), lambda s: (s, 0, 0)),
                  pl.BlockSpec((1, SAMPLE_PAD, B_WIDTH), lambda s: (s, 0, 0))],
        out_specs=pl.BlockSpec((1, SAMPLE_PAD, B_WIDTH), lambda s: (s, 0, 0)),
        out_shape=jax.ShapeDtypeStruct((n_seq, SAMPLE_PAD, B_WIDTH), F32),
        compiler_params=_params("arbitrary"), name="attn_sample_finish",
    )(acc, g)


def proj_act(h, w, col0, v_gain, act, tm, tn, out_dtype, second=None):
    m = h.shape[0]
    j0 = col0 // tn
    n_col = C_WIDTH // tn
    n_i = m // tm
    n_steps, tile = _two_group_steps(n_i, second is not None)

    def phases(ins, outs, scratch):
        a_ref, w_ref, vg_ref = ins[:3]
        (wbf,) = scratch
        i = pl.program_id(1)

        def cast():
            wbf[...] = w_ref[...].astype(BF16)

        def group(rows_ref, o_ref, row_chunk):
            def run():
                for r in range(0, rows_ref.shape[0], row_chunk):
                    rows = slice(r, r + row_chunk)
                    y = _dot(rows_ref[rows, :], wbf[...])
                    if act == "gelu":
                        y = _gelu_tanh(y)
                    elif act == "silu":
                        y = _silu(y)
                    else:
                        y = _rms(_gelu_tanh(y), vg_ref[...])
                    o_ref[rows, :] = y.astype(o_ref.dtype)
            return run

        main = group(a_ref, outs[0], min(tm, 256))
        if second is None:
            return [(i == 0, cast)], [main], []
        small = group(ins[3], outs[1], ins[3].shape[0])
        return [(i == 0, cast), (i == 0, small), (i > 0, main)], [], []

    w_mode = dict(pipeline_mode=pl.Buffered(1)) if n_col == 1 else {}
    in_specs = [pl.BlockSpec((tm, D_MODEL), lambda j, i, *_: (tile(i), 0)),
                pl.BlockSpec((D_MODEL, tn), lambda j, i, *_: (0, j0 + j), **w_mode),
                pl.BlockSpec((1, tn), lambda j, i, *_: (0, j))]
    out_specs = [pl.BlockSpec((tm, tn), lambda j, i, *_: (tile(i), j))]
    out_shape = [jax.ShapeDtypeStruct((m, C_WIDTH), out_dtype)]
    args = [h, w, v_gain.reshape(1, C_WIDTH)]
    if second is not None:
        h2, dtype2 = second
        in_specs.append(pl.BlockSpec(h2.shape, lambda j, i, *_: (0, 0)))
        out_specs.append(pl.BlockSpec((h2.shape[0], tn), lambda j, i, *_: (0, j)))
        out_shape.append(jax.ShapeDtypeStruct((h2.shape[0], C_WIDTH), dtype2))
        args.append(h2)
    outs, _ = _hosted_call("proj_" + act, (n_col, n_steps), lambda j, i: j * n_steps + i,
                           in_specs, out_specs, out_shape, [pltpu.VMEM((D_MODEL, tn), BF16)], phases, args)
    return outs[0] if second is None else tuple(outs)


def odd_in(h, w, v_gain, tm, act_dtype, second=None):
    u = proj_act(h, w, 0, v_gain, "gelu", tm, 1024, act_dtype, second)
    v = proj_act(h, w, C_WIDTH, v_gain, "gelu_rms", min(tm, 512), C_WIDTH, act_dtype, second)
    g = proj_act(h, w, 2 * C_WIDTH, v_gain, "silu", tm, 1024, act_dtype, second)
    if second is None:
        return u, v, g
    return (u[0], v[0], g[0]), (u[1], v[1], g[1])


def _spatial_kernel(u_ref, v_ref, g_ref, ws_ref, bs_ref, y_ref, *, chunk, n_chunks):
    tt = lax.broadcasted_iota(jnp.int32, (chunk, chunk), 0)
    ss = lax.broadcasted_iota(jnp.int32, (chunk, chunk), 1)
    causal = ss <= tt
    for grp in range(C_GROUPS):
        wm = jnp.where(causal, ws_ref[grp], 0.0)
        bcol = bs_ref[:, grp:grp + 1]
        cols = slice(grp * C_GDIM, (grp + 1) * C_GDIM)
        for c in range(n_chunks):
            rows = slice(c * chunk, (c + 1) * chunk)
            vv = v_ref[rows, cols]
            if chunk >= 128:
                sv = _dot(wm.astype(BF16), vv)
            else:
                vf = vv.astype(F32)
                sv = jnp.zeros((chunk, C_GDIM), F32)
                for s in range(chunk):
                    sv = sv + wm[:, s:s + 1] * vf[s:s + 1, :]
            sv = sv + bcol
            y = u_ref[rows, cols].astype(F32) * sv * g_ref[rows, cols].astype(F32)
            y_ref[rows, cols] = y.astype(y_ref.dtype)


def spatial_gate(u, v, g, w_s, b_s_t, chunk, n_chunks):
    m = u.shape[0]
    tm = chunk * n_chunks
    row_spec = pl.BlockSpec((tm, C_WIDTH), lambda i: (i, 0))
    return pl.pallas_call(
        functools.partial(_spatial_kernel, chunk=chunk, n_chunks=n_chunks),
        grid=(m // tm,),
        in_specs=[row_spec, row_spec, row_spec,
                  pl.BlockSpec((C_GROUPS, chunk, chunk), lambda i: (0, 0, 0)),
                  pl.BlockSpec((chunk, C_GROUPS), lambda i: (0, 0))],
        out_specs=row_spec,
        out_shape=jax.ShapeDtypeStruct((m, C_WIDTH), u.dtype),
        compiler_params=_params("arbitrary"),
        name="spatial_gate",
    )(u, v, g, w_s, b_s_t)


def _even_weights(w_in, b_i, b_f):
    gate0 = 5 * A_WIDTH
    b0 = gate0 + 2 * A_HEADS
    wt = jnp.swapaxes(w_in, 0, 1)
    w_gate = jnp.pad(w_in[:, gate0:b0], ((0, 0), (0, GATE_LANES - 2 * A_HEADS)))
    bias = jnp.pad(jnp.concatenate([b_i, b_f]), (0, GATE_LANES - 2 * A_HEADS)).reshape(1, GATE_LANES)
    return wt, b0, w_gate, bias


def _mlstm_inputs(qkvog, gates, bsz, t_len, valid_len, chunk):
    qkvog = qkvog.reshape(bsz, t_len, 5 * A_WIDTH)
    gates = gates.reshape(bsz, t_len, GATE_LANES)
    t_pad = -(-t_len // chunk) * chunk
    pad = ((0, 0), (0, t_pad - t_len), (0, 0))
    if valid_len < t_pad:
        qkvog, gates = jnp.pad(qkvog, pad), jnp.pad(gates, pad)
        pos = jnp.arange(t_pad)[None, :, None]
        lane = jnp.arange(GATE_LANES)[None, None, :]
        gates = jnp.where((pos >= valid_len) & (lane < A_HEADS), NEG_BIG, gates)
        gates = jnp.where((pos >= valid_len) & (lane >= A_HEADS), 0.0, gates)
    return qkvog, gates, gates[:, :, :2 * A_HEADS].transpose(0, 2, 1)


def _even_front(xp, xs, ew, g_norm, tm):
    wt, b0, w_gate, bias = ew
    hp, gates_p = norm_gates(xp, g_norm, w_gate, bias, min(tm, 512))
    hs, gates_s = norm_gates(xs, g_norm, w_gate, bias, xs.shape[0])
    tn, tm_kv = 1024, tm
    qkvog = proj(hp, wt, 0, 5 * A_WIDTH, BF16, 2 * tm, tn, second=(hs, BF16))
    qg_b = proj(hp, wt, b0, 2 * B_WIDTH, BF16, 2 * tm, tn, scale=B_DH ** -0.5, second=(hs, F32),
                row_step=3 * B_WIDTH)
    k = kv_proj(hp, wt, b0 + B_WIDTH, tm_kv, second=hs)
    v = kv_proj(hp, wt, b0 + 2 * B_WIDTH, tm_kv, second=hs)
    groups = []
    for i, gates in enumerate((gates_p, gates_s)):
        groups.append(dict(qkvog=qkvog[i], gates=gates, qg_b=qg_b[i],
                           k_new=k[2 * i], k_bf=k[2 * i + 1], v_new=v[2 * i], v_bf=v[2 * i + 1]))
    return groups


def _even_back(xp, xs, mix_p, mix_s, w_out, next_gain, tm):
    w_out_b = w_out.astype(BF16)
    flat = lambda mix, m: [mix[0].reshape(m, A_WIDTH), mix[1].reshape(m, B_WIDTH)]
    return out_proj_norm(flat(mix_p, xp.shape[0]), [w_out_b[:A_WIDTH], w_out_b[A_WIDTH:]], xp, next_gain,
                         tm, True, BF16, second=(flat(mix_s, xs.shape[0]), xs, BF16))


def _odd_layer(xp, hp, xs, hs, w_in, v_gain, w_s, b_s, w_out_b, final_gain, tm, n_seq):
    (u, v, g), (u2, v2, g2) = odd_in(hp, w_in, v_gain, 4 * tm, BF16, second=(hs, F32))
    y2 = spatial_gate(u2, v2, g2, w_s[:, :SAMPLE_PAD, :SAMPLE_PAD], b_s[:, :SAMPLE_PAD].T, SAMPLE_PAD, n_seq)
    y_p, y_s = out_proj_norm([], [w_out_b], xp, final_gain, tm, False, F32, second=([y2], xs, F32),
                             spatial=(u, v, g, w_s, b_s.T))
    return y_p, y_s, v2


def kernel(x_prompt, x_sample, state_a_C, state_a_n, state_a_m, cache_b_k, cache_b_v, page_table,
           even_norm, even_w_in, even_b_i, even_b_f, even_b_sb, even_w_out,
           odd_norm, odd_w_in, odd_v_gain, odd_w_s, odd_b_s, odd_w_out, final_norm):
    bsz, seq, _ = x_prompt.shape
    n_seq, dec_seq, _ = x_sample.shape
    n_pool = cache_b_k.shape[1]

    ew = _even_weights(even_w_in[0], even_b_i[0], even_b_f[0])
    odd_w_in_b = odd_w_in[0]
    odd_w_out_b = odd_w_out[0].astype(BF16)

    xp = x_prompt.reshape(bsz * seq, D_MODEL)
    xs = jnp.pad(x_sample, ((0, 0), (0, SAMPLE_PAD - dec_seq), (0, 0))).reshape(n_seq * SAMPLE_PAD, D_MODEL)
    fp, fs = _even_front(xp, xs, ew, even_norm[0], 1024)
    qkvog_p, gates_p, grow_p = _mlstm_inputs(fp["qkvog"], fp["gates"], bsz, seq, seq, A_CHUNK)
    qkvog_s, gates_s, grow_s = _mlstm_inputs(fs["qkvog"], fs["gates"], n_seq, SAMPLE_PAD, dec_seq,
                                             SAMPLE_CHUNK)
    qg_s = fs["qg_b"].reshape(n_seq, SAMPLE_PAD, 2 * B_WIDTH)
    q_s, g_s = qg_s[:, :, :B_WIDTH], qg_s[:, :, B_WIDTH:]

    assert cache_b_k.shape[0] == 1 and cache_b_v.shape[0] == 1
    page_view = (1, n_pool, PAGE_SIZE * B_HEADS, B_DH)
    cache_k, cache_v = cache_b_k.reshape(page_view), cache_b_v.reshape(page_view)
    kv_pad = ((0, 0), (0, (PAGE_SIZE - SAMPLE_PAD) * B_HEADS), (0, 0))
    k_new_s = jnp.pad(fs["k_new"].reshape(n_seq, SAMPLE_PAD * B_HEADS, B_DH), kv_pad)
    v_new_s = jnp.pad(fs["v_new"].reshape(n_seq, SAMPLE_PAD * B_HEADS, B_DH), kv_pad)
    bias_rows = jnp.broadcast_to(jnp.repeat(even_b_sb[0], SAMPLE_PAD)[:, None],
                                 (B_HEADS * SAMPLE_PAD, GATE_LANES))
    att_rows = B_HEADS * SAMPLE_PAD
    rider = _PagedAttnRider(
        q_s, k_new_s, v_new_s, cache_k, cache_v, page_table, bias_rows,
        jnp.zeros((n_seq, att_rows, B_WIDTH), F32), jnp.zeros((n_seq, att_rows, GATE_LANES), F32),
        0, PAGES_PER_STEP)
    assert bsz * (seq // A_CHUNK) == n_seq * rider.steps_per_seq
    zero_state = (jnp.zeros((bsz, A_HEADS, A_DH, A_DH), F32),
                  jnp.zeros((bsz, A_HEADS, 1, A_DH), F32),
                  jnp.zeros((bsz, A_HEADS, 1, GATE_LANES), F32))
    (ha_p, c_p, n_p, m_p), (att_acc, _) = mlstm(
        qkvog_p, gates_p, grow_p, *zero_state, A_CHUNK, rider=rider)
    hb_s = attn_sample_finish(att_acc, g_s)

    as_seq = lambda a: a.reshape(bsz, seq, B_WIDTH)
    hb_p = attn_prompt(fp["qg_b"].reshape(bsz, seq, 2 * B_WIDTH), as_seq(fp["k_bf"]), as_seq(fp["v_bf"]),
                       even_b_sb[0], ATTN_BQ, ATTN_BK, ATTN_HEADS_PER_STEP)
    st_in = (state_a_C[0], state_a_n[0][:, :, None, :],
             jnp.broadcast_to(state_a_m[0][:, :, None, None], (n_seq, A_HEADS, 1, GATE_LANES)))
    ha_s, c_s, n_s, m_s_new = mlstm(qkvog_s, gates_s, grow_s, *st_in, SAMPLE_CHUNK)
    ha_s = ha_s[:, :SAMPLE_PAD]

    xp1, hp1, xs1, hs1 = _even_back(xp, xs, (ha_p, hb_p), (ha_s, hb_s), even_w_out[0], odd_norm[0], 512)
    y_p, y_s, v_rows = _odd_layer(xp1, hp1, xs1, hs1, odd_w_in_b, odd_v_gain[0], odd_w_s[0], odd_b_s[0],
                                  odd_w_out_b, final_norm, 512, n_seq)

    def sample_rows(a, *dims):
        return a.reshape((n_seq, SAMPLE_PAD) + dims)[:, :dec_seq]

    return (y_p.reshape(bsz, seq, D_MODEL),
            sample_rows(y_s, D_MODEL),
            c_p[None], n_p[:, :, 0, :][None], m_p[:, :, 0, 0][None],
            c_s[None], n_s[:, :, 0, :][None], m_s_new[:, :, 0, 0][None],
            fp["k_new"].reshape(1, bsz, seq, B_HEADS, B_DH), fp["v_new"].reshape(1, bsz, seq, B_HEADS, B_DH),
            sample_rows(fs["k_new"], B_HEADS, B_DH)[None], sample_rows(fs["v_new"], B_HEADS, B_DH)[None],
            sample_rows(v_rows, C_WIDTH)[None])
```

```python
import functools

import jax
import jax.numpy as jnp
from jax import lax
from jax.experimental import pallas as pl
from jax.experimental.pallas import tpu as pltpu

F32 = jnp.float32
BF16 = jnp.bfloat16

D_MODEL = 2048
PAGE_SIZE = 128
A_HEADS = 4
A_DH = 256
A_WIDTH = A_HEADS * A_DH
A_CHUNK = 128
B_HEADS = 8
B_DH = 128
B_WIDTH = B_HEADS * B_DH
C_WIDTH = D_MODEL
C_GROUPS = 8
C_GDIM = C_WIDTH // C_GROUPS
C_CHUNK = 128
RMS_EPS = 1e-6
GATE_LANES = 128
NEG_BIG = -1e30
SAMPLE_PAD = 8
SAMPLE_CHUNK = 16
ATTN_BQ = 512
ATTN_BK = 512
CUMSUM_BLOCK = 256
ATTN_HEADS_PER_STEP = 8
PAGES_PER_STEP = 16

VMEM_LIMIT_BYTES = 56 * 1024 * 1024


def _params(*sem):
    return pltpu.CompilerParams(dimension_semantics=sem, vmem_limit_bytes=VMEM_LIMIT_BYTES)


def _dot(a, b):
    return jnp.dot(a, b, preferred_element_type=F32)


def _dot_nt(a, b):
    return lax.dot_general(a, b, (((1,), (1,)), ((), ())), preferred_element_type=F32)


def _dot_tn(a, b):
    return lax.dot_general(a, b, (((0,), (0,)), ((), ())), preferred_element_type=F32)


def _softplus(z):
    return jnp.maximum(z, 0.0) + jnp.log(1.0 + jnp.exp(-jnp.abs(z)))


def _sigmoid(z):
    return 1.0 / (1.0 + jnp.exp(-z))


def _silu(z):
    return z * _sigmoid(z)


def _gelu_tanh(x):
    c = 0.7978845608028654
    return x * (0.5 * (1.0 + jnp.tanh(c * (x + 0.044715 * (x * x * x)))))


def _rms(x, g):
    return x * lax.rsqrt(jnp.mean(x * x, axis=-1, keepdims=True) + RMS_EPS) * g


def _split_hi_lo(x):
    hi = x.astype(BF16)
    lo = (x - hi.astype(F32)).astype(BF16)
    return hi, lo


def _norm_gates_kernel(x_ref, g_ref, w_ref, bias_ref, h_ref, gate_ref):
    h = _rms(x_ref[...], g_ref[...])
    h_hi, h_lo = _split_hi_lo(h)
    h_ref[...] = h_hi
    a = _dot(h_hi, w_ref[...])
    b = _dot(h_lo, w_ref[...])
    pre = a[:, :GATE_LANES] + a[:, GATE_LANES:] + b[:, :GATE_LANES] + bias_ref[...]
    lane = lax.broadcasted_iota(jnp.int32, pre.shape, 1)
    is_forget = (lane >= A_HEADS) & (lane < 2 * A_HEADS)
    gate_ref[...] = jnp.where(is_forget, -_softplus(-pre), pre)


def norm_gates(x, gain, w_gate, bias, tm):
    m = x.shape[0]
    w_hi_lo = jnp.concatenate(_split_hi_lo(w_gate), axis=1)
    return pl.pallas_call(
        _norm_gates_kernel,
        grid=(m // tm,),
        in_specs=[pl.BlockSpec((tm, D_MODEL), lambda i: (i, 0)),
                  pl.BlockSpec((1, D_MODEL), lambda i: (0, 0)),
                  pl.BlockSpec((D_MODEL, 2 * GATE_LANES), lambda i: (0, 0)),
                  pl.BlockSpec((1, GATE_LANES), lambda i: (0, 0))],
        out_specs=[pl.BlockSpec((tm, D_MODEL), lambda i: (i, 0)),
                   pl.BlockSpec((tm, GATE_LANES), lambda i: (i, 0))],
        out_shape=[jax.ShapeDtypeStruct((m, D_MODEL), BF16),
                   jax.ShapeDtypeStruct((m, GATE_LANES), F32)],
        compiler_params=_params("arbitrary"),
        name="norm_gates",
    )(x, gain.reshape(1, D_MODEL), w_hi_lo, bias)


def _hosted_kernel(*refs, n_in, n_out, phases_fn, lin, rank, rider):
    if rider is not None:
        refs = refs[1:]
    r_in, r_out = (rider.n_in, rider.n_out) if rider is not None else (0, 0)
    ins, refs = refs[:n_in], refs[n_in:]
    r_ins, refs = refs[:r_in], refs[r_in:]
    outs, refs = refs[:n_out], refs[n_out:]
    r_outs, refs = refs[:r_out], refs[r_out:]
    n_sc = len(refs) - (rider.n_scratch if rider is not None else 0)
    scratch, r_scratch = refs[:n_sc], refs[n_sc:]
    sets = [phases_fn(ins, outs, scratch)]
    if rider is not None:
        step = lin(*[pl.program_id(d) for d in range(rank)])
        sets.append(rider.phases(r_ins, r_outs, r_scratch, step))
    for pre, _, _ in sets:
        for cond, fn in pre:
            pl.when(cond)(fn)
    for _, parts, _ in sets:
        for part in parts:
            part()
    for _, _, post in sets:
        for cond, fn in post:
            pl.when(cond)(fn)


def _hosted_call(name, grid, lin, in_specs, out_specs, out_shape, scratch, phases_fn, args, rider=None):
    n_in, n_out = len(in_specs), len(out_specs)
    aliases = {}
    if rider is not None:
        r = rider.specs(lin, len(grid))
        aliases = {1 + n_in + i: n_out + o for i, o in r["aliases"].items()}
        in_specs, out_specs = in_specs + r["in_specs"], out_specs + r["out_specs"]
        out_shape, scratch = out_shape + r["out_shape"], scratch + r["scratch"]
        args = [rider.page_table] + list(args) + r["args"]
    kern = functools.partial(_hosted_kernel, n_in=n_in, n_out=n_out, phases_fn=phases_fn,
                             lin=lin, rank=len(grid), rider=rider)
    grid_spec = pltpu.PrefetchScalarGridSpec(
        num_scalar_prefetch=0 if rider is None else 1, grid=grid,
        in_specs=in_specs, out_specs=out_specs, scratch_shapes=scratch)
    outs = pl.pallas_call(
        kern, grid_spec=grid_spec, out_shape=out_shape, input_output_aliases=aliases,
        compiler_params=_params(*(["arbitrary"] * len(grid))), name=name,
    )(*args)
    return outs[:n_out], outs[n_out:]


def _two_group_steps(n_i, has_second):
    if not has_second:
        return n_i, (lambda i: i)
    return n_i + 1, (lambda i: jnp.maximum(i - 1, 0))


def proj(a, wt, row0, n_out, out_dtype, tm, tn, scale=None, second=None, row_step=None):
    m, k = a.shape
    n_i = m // tm
    n_steps, tile = _two_group_steps(n_i, second is not None)
    row_step = tn if row_step is None else row_step

    def phases(ins, outs, scratch):
        a_ref, wt_ref = ins[:2]
        o_ref, (wbf,) = outs[0], scratch
        i = pl.program_id(1)

        def cast():
            wbf[...] = wt_ref[...].astype(BF16)

        def main():
            y = _dot_nt(a_ref[...], wbf[...])
            if scale is not None:
                y = y * jnp.where(pl.program_id(0) == 0, scale, 1.0)
            o_ref[...] = y.astype(o_ref.dtype)

        if second is None:
            return [(i == 0, cast)], [main], []

        def small():
            outs[1][...] = _dot_nt(ins[2][...], wbf[...]).astype(outs[1].dtype)

        return [(i == 0, cast), (i == 0, small), (i > 0, main)], [], []

    in_specs = [pl.BlockSpec((tm, k), lambda j, i, *_: (tile(i), 0)),
                pl.BlockSpec((pl.Element(tn), pl.Element(k)),
                             lambda j, i, *_: (pl.multiple_of(row0 + j * row_step, 8), 0))]
    out_specs = [pl.BlockSpec((tm, tn), lambda j, i, *_: (tile(i), j))]
    out_shape = [jax.ShapeDtypeStruct((m, n_out), out_dtype)]
    args = [a, wt]
    if second is not None:
        a2, dtype2 = second
        in_specs.append(pl.BlockSpec(a2.shape, lambda j, i, *_: (0, 0)))
        out_specs.append(pl.BlockSpec((a2.shape[0], tn), lambda j, i, *_: (0, j)))
        out_shape.append(jax.ShapeDtypeStruct((a2.shape[0], n_out), dtype2))
        args.append(a2)
    outs, _ = _hosted_call("proj", (n_out // tn, n_steps), lambda j, i: j * n_steps + i,
                           in_specs, out_specs, out_shape, [pltpu.VMEM((tn, k), BF16)], phases, args)
    return outs[0] if second is None else tuple(outs)


def kv_proj(a, wt, row0, tm, second=None):
    m, k = a.shape
    n_i = m // tm
    n_steps, tile = _two_group_steps(n_i, second is not None)

    def phases(ins, outs, scratch):
        a_ref, wt_ref = ins[:2]
        (wbf,) = scratch
        i = pl.program_id(0)

        def cast():
            wbf[...] = wt_ref[...].astype(BF16)

        def rows_to(a_rows_ref, o_ref, obf_ref):
            def run():
                n_rows = a_rows_ref.shape[0]
                y = _dot_nt(a_rows_ref[...], wbf[...])
                obf_ref[...] = y.astype(BF16)
                for h in range(B_HEADS):
                    o_ref[pl.ds(h, n_rows, stride=B_HEADS), :] = y[:, h * B_DH:(h + 1) * B_DH]
            return run

        main = rows_to(a_ref, outs[0], outs[1])
        if second is None:
            return [(i == 0, cast)], [main], []
        return [(i == 0, cast), (i == 0, rows_to(ins[2], outs[2], outs[3])), (i > 0, main)], [], []

    def out_pair(rows, index):
        return ([pl.BlockSpec((rows * B_HEADS, B_DH), index), pl.BlockSpec((rows, B_WIDTH), index)],
                lambda total: [jax.ShapeDtypeStruct((total * B_HEADS, B_DH), F32),
                               jax.ShapeDtypeStruct((total, B_WIDTH), BF16)])

    in_specs = [pl.BlockSpec((tm, k), lambda i, *_: (tile(i), 0)),
                pl.BlockSpec((pl.Element(B_WIDTH), pl.Element(k)), lambda i, *_: (row0, 0))]
    out_specs, shapes = out_pair(tm, lambda i, *_: (tile(i), 0))
    out_shape = shapes(m)
    args = [a, wt]
    if second is not None:
        m2 = second.shape[0]
        in_specs.append(pl.BlockSpec(second.shape, lambda i, *_: (0, 0)))
        specs2, shapes2 = out_pair(m2, lambda i, *_: (0, 0))
        out_specs, out_shape = out_specs + specs2, out_shape + shapes2(m2)
        args.append(second)
    outs, _ = _hosted_call("kv_proj", (n_steps,), lambda i: i, in_specs, out_specs, out_shape,
                           [pltpu.VMEM((B_WIDTH, k), BF16)], phases, args)
    return tuple(outs)


def out_proj_norm(lhs, ws, x, gain, tm, emit_x, norm_dtype, second=None, spatial=None):
    m = x.shape[0]
    n_lhs, n_w = len(lhs), len(ws)
    n_i = m // tm
    n_steps, tile = _two_group_steps(n_i, second is not None)
    n_out = 2 if emit_x else 1

    def phases(ins, outs, scratch):
        w_refs = ins[n_lhs:n_lhs + n_w]
        x_ref, g_ref = ins[n_lhs + n_w], ins[n_lhs + n_w + 1]
        rest = ins[n_lhs + n_w + 2:]
        i = pl.program_id(0)

        def group(make_lhs, a_refs, x_ref, out_refs, row_chunk):
            def run():
                if make_lhs is not None:
                    make_lhs()
                for r in range(0, x_ref.shape[0], row_chunk):
                    rows = slice(r, r + row_chunk)
                    y = x_ref[rows, :]
                    for a_ref, w_ref in zip(a_refs, w_refs):
                        y = y + _dot(a_ref[rows, :].astype(BF16), w_ref[...])
                    if emit_x:
                        out_refs[0][rows, :] = y
                    out_refs[-1][rows, :] = _rms(y, g_ref[...]).astype(out_refs[-1].dtype)
            return run

        if second is not None:
            a2_refs, x2_ref, rest = rest[:n_w], rest[n_w], rest[n_w + 1:]
        if spatial is None:
            main = group(None, ins[:n_lhs], x_ref, outs[:n_out], min(tm, 256))
        else:
            (y_sc,) = scratch
            fill = functools.partial(_spatial_kernel, *rest[:5], y_sc, chunk=C_CHUNK, n_chunks=tm // C_CHUNK)
            main = group(fill, [y_sc], x_ref, outs[:n_out], min(tm, 256))
        if second is None:
            return [], [main], []
        small = group(None, a2_refs, x2_ref, outs[n_out:], x2_ref.shape[0])
        return [(i == 0, small), (i > 0, main)], [], []

    row_spec = lambda width: pl.BlockSpec((tm, width), lambda i, *_: (tile(i), 0))
    whole = lambda arr: pl.BlockSpec(arr.shape, lambda i, *_: (0,) * arr.ndim)
    gain2d = gain.reshape(1, D_MODEL)
    in_specs = ([row_spec(a.shape[1]) for a in lhs] + [whole(w) for w in ws]
                + [row_spec(D_MODEL), whole(gain2d)])
    out_specs = [row_spec(D_MODEL)] * n_out
    out_shape = ([jax.ShapeDtypeStruct((m, D_MODEL), F32)] if emit_x else []) \
        + [jax.ShapeDtypeStruct((m, D_MODEL), norm_dtype)]
    args = [*lhs, *ws, x, gain2d]
    scratch = []
    if second is not None:
        lhs2, x2, norm_dtype2 = second
        m2 = x2.shape[0]
        in_specs += [whole(a) for a in lhs2] + [whole(x2)]
        out_specs += [pl.BlockSpec((m2, D_MODEL), lambda i, *_: (0, 0))] * n_out
        out_shape += ([jax.ShapeDtypeStruct((m2, D_MODEL), F32)] if emit_x else []) \
            + [jax.ShapeDtypeStruct((m2, D_MODEL), norm_dtype2)]
        args += [*lhs2, x2]
    if spatial is not None:
        u, v, gate, w_s, b_s_t = spatial
        in_specs += [row_spec(C_WIDTH)] * 3 + [whole(w_s), whole(b_s_t)]
        args += [u, v, gate, w_s, b_s_t]
        scratch = [pltpu.VMEM((tm, C_WIDTH), BF16)]
    outs, _ = _hosted_call("out_proj_norm", (n_steps,), lambda i: i, in_specs, out_specs, out_shape,
                           scratch, phases, args)
    return list(outs)


def _mlstm_chunk(q_ref, k_ref, v_ref, og_ref, gg_ref, gcol_ref, grow_ref, h_ref, c_sc, n_sc, m_sc, chunk, heads):
    L = chunk
    gcol = gcol_ref[0]
    grow = grow_ref[0]
    tt = lax.broadcasted_iota(jnp.int32, (L, L), 0)
    ss = lax.broadcasted_iota(jnp.int32, (L, L), 1)
    causal = ss <= tt

    for head in heads:
        cols = slice(head * A_DH, (head + 1) * A_DH)
        q = q_ref[0, :, cols]
        ks = k_ref[0, :, cols] * jnp.asarray(A_DH ** -0.5, BF16)
        v = v_ref[0, :, cols]
        ig_col = gcol[:, head:head + 1]
        lf_col = gcol[:, head + A_HEADS:head + A_HEADS + 1]
        ig_row = grow[head:head + 1, :]
        lf_row = grow[head + A_HEADS:head + A_HEADS + 1, :]
        b_col = jnp.sum(jnp.where(causal, lf_row, 0.0), axis=1, keepdims=True)
        b_row = jnp.sum(jnp.where(tt <= ss, lf_col, 0.0), axis=0, keepdims=True)
        b_last = jnp.sum(lf_row, axis=1, keepdims=True)

        m0 = m_sc[head][:, :1]
        n0 = n_sc[head]
        c0 = c_sc[head]

        d = jnp.where(causal, b_col - b_row + ig_row, NEG_BIG)
        m_carry = b_col + m0
        m = jnp.maximum(m_carry, jnp.max(d, axis=1, keepdims=True))
        w_intra = jnp.exp(d - m)
        w_carry = jnp.exp(m_carry - m)
        s = _dot_nt(q, ks) * w_intra
        qf = q.astype(F32)
        num = _dot(s.astype(BF16), v) + w_carry * _dot_nt(q, c0.astype(BF16))
        den = jnp.sum(s, axis=1, keepdims=True) + w_carry * jnp.sum(qf * n0, axis=1, keepdims=True)
        h = num / jnp.maximum(jnp.abs(den), jnp.exp(-m))
        gated = h * _sigmoid(og_ref[0, :, cols].astype(F32)) * _silu(gg_ref[0, :, cols].astype(F32))
        h_ref[0, :, cols] = gated.astype(h_ref.dtype)

        m_carry_last = b_last + m0
        d_last_row = b_last - b_row + ig_row
        m_new = jnp.maximum(m_carry_last, jnp.max(d_last_row, axis=1, keepdims=True))
        wc_last = jnp.exp(m_carry_last - m_new)
        w_last_col = jnp.exp(b_last - b_col + ig_col - m_new)
        vw = (v.astype(F32) * w_last_col).astype(BF16)
        c_new = wc_last * c0 + _dot_tn(vw, ks)
        n_new = wc_last * n0 + jnp.sum(ks.astype(F32) * w_last_col, axis=0, keepdims=True)
        c_sc[head] = c_new
        n_sc[head] = n_new
        m_sc[head] = jnp.broadcast_to(m_new, (1, GATE_LANES))


def mlstm(qkvog, gcol, grow, c0, n0, m0, chunk, rider=None):
    bsz, t_len, _ = qkvog.shape
    nc = t_len // chunk
    hd = A_HEADS

    def phases(ins, outs, state):
        c0_ref, n0_ref, m0_ref = ins[7:]
        h_ref, c_out_ref, n_out_ref, m_out_ref = outs
        c_sc, n_sc, m_sc = state
        ci = pl.program_id(1)

        def init():
            c_sc[...] = c0_ref[0]
            n_sc[...] = n0_ref[0]
            m_sc[...] = m0_ref[0]

        def head_part(head):
            return lambda: _mlstm_chunk(*ins[:7], h_ref, c_sc, n_sc, m_sc, chunk, (head,))

        parts = [head_part(head) for head in range(A_HEADS)]

        def final():
            c_out_ref[0] = c_sc[...]
            n_out_ref[0] = n_sc[...]
            m_out_ref[0] = m_sc[...]

        return [(ci == 0, init)], parts, [(ci == nc - 1, final)]

    blk = lambda seg: pl.BlockSpec((1, chunk, A_WIDTH), lambda b, c, *_, seg=seg: (b, c, seg))
    st4 = lambda r, w: pl.BlockSpec((1, hd, r, w), lambda b, c, *_: (b, 0, 0, 0))
    outs, rest = _hosted_call(
        "mlstm", (bsz, nc), lambda b, c: b * nc + c,
        [blk(0), blk(1), blk(2), blk(3), blk(4),
         pl.BlockSpec((1, chunk, GATE_LANES), lambda b, c, *_: (b, c, 0)),
         pl.BlockSpec((1, 8, chunk), lambda b, c, *_: (b, 0, c)),
         st4(A_DH, A_DH), st4(1, A_DH), st4(1, GATE_LANES)],
        [pl.BlockSpec((1, chunk, A_WIDTH), lambda b, c, *_: (b, c, 0)),
         st4(A_DH, A_DH), st4(1, A_DH), st4(1, GATE_LANES)],
        [jax.ShapeDtypeStruct((bsz, t_len, A_WIDTH), BF16),
         jax.ShapeDtypeStruct((bsz, hd, A_DH, A_DH), F32),
         jax.ShapeDtypeStruct((bsz, hd, 1, A_DH), F32),
         jax.ShapeDtypeStruct((bsz, hd, 1, GATE_LANES), F32)],
        [pltpu.VMEM((hd, A_DH, A_DH), F32), pltpu.VMEM((hd, 1, A_DH), F32),
         pltpu.VMEM((hd, 1, GATE_LANES), F32)],
        phases, [qkvog, qkvog, qkvog, qkvog, qkvog, gcol, grow, c0, n0, m0], rider)
    return tuple(outs) if rider is None else (tuple(outs), rest)


def _stick_block(q, kb, vb, bias, run, mask, upper):
    rows = q.shape[0]
    sub = upper.shape[0]
    n_sub = kb.shape[0] // sub
    z = _dot_nt(q, kb) if bias is None else _dot_nt(q, kb) * (B_DH ** -0.5) + bias
    sp = _softplus(z)
    spm = sp if mask is None else jnp.where(mask, sp, 0.0)
    hi, lo = _split_hi_lo(spm)
    laters = [None] * n_sub
    total = None
    for i in reversed(range(n_sub)):
        ln = slice(i * sub, (i + 1) * sub)
        both = _dot(jnp.concatenate([hi[:, ln], lo[:, ln]], axis=0), upper)
        carry = run if total is None else run + total
        laters[i] = both[:rows] + both[rows:] + carry
        part = jnp.sum(spm[:, ln], axis=1, keepdims=True)
        total = part if total is None else total + part
    later = laters[0] if n_sub == 1 else jnp.concatenate(laters, axis=1)
    a = jnp.exp(z - sp - later)
    if mask is not None:
        a = jnp.where(mask, a, 0.0)
    return _dot(a.astype(BF16), vb), total


def _strict_upper(n):
    j = lax.broadcasted_iota(jnp.int32, (n, n), 0)
    s = lax.broadcasted_iota(jnp.int32, (n, n), 1)
    return jnp.where(j > s, 1.0, 0.0).astype(BF16)


def _attn_prompt_kernel(bias_ref, q_ref, k_ref, v_ref, g_ref, o_ref, *, bq, bk, n_heads):
    head0 = pl.program_id(1) * n_heads
    qi = pl.program_id(2)
    kbf = k_ref.at[0]
    vbf = v_ref.at[0]
    upper = _strict_upper(min(bk, CUMSUM_BLOCK))
    row = lax.broadcasted_iota(jnp.int32, (bq, bk), 0)
    col = lax.broadcasted_iota(jnp.int32, (bq, bk), 1)
    lanes = [slice(h * B_DH, (h + 1) * B_DH) for h in range(n_heads)]
    lane_q = lax.broadcasted_iota(jnp.int32, (bq, B_DH), 1)
    lane_k = lax.broadcasted_iota(jnp.int32, (bk, B_DH), 1)
    ones_cols = jnp.where(lane_q < 2, 1.0, 0.0).astype(BF16)
    qs = [jnp.concatenate([q_ref[0, :, ln], ones_cols], axis=1) for ln in lanes]
    bias_cols = []
    for h in range(n_heads):
        b = jnp.full((bk, B_DH), bias_ref[head0 + h], F32)
        b_hi = b.astype(BF16).astype(F32)
        bias_cols.append(jnp.where(lane_k == 0, b_hi, jnp.where(lane_k == 1, b - b_hi, 0.0)).astype(BF16))

    def blocks(kj, runs, mask):
        start = pl.multiple_of(kj * bk, bk)
        return [_stick_block(qs[h], jnp.concatenate([kbf[pl.ds(start, bk), ln], bias_cols[h]], axis=1),
                             vbf[pl.ds(start, bk), ln], None, runs[h], mask, upper)
                for h, ln in enumerate(lanes)]

    q0 = qi * bq
    n_full = q0 // bk
    if bq == bk:
        half = bq // 2
        start = pl.multiple_of(qi * bq, bq)
        accs, runs = [], []
        for h, ln in enumerate(lanes):
            kb = jnp.concatenate([kbf[pl.ds(start, bk), ln], bias_cols[h]], axis=1)
            vb = vbf[pl.ds(start, bk), ln]
            zero = jnp.zeros((half, 1), F32)
            iota = lambda shape, dim: lax.broadcasted_iota(jnp.int32, shape, dim)
            top = _stick_block(qs[h][:half], kb[:half], vb[:half], None, zero,
                               iota((half, half), 1) < iota((half, half), 0), upper)
            bot = _stick_block(qs[h][half:], kb, vb, None, zero,
                               iota((half, bk), 1) < iota((half, bk), 0) + half, upper)
            accs.append(jnp.concatenate([top[0], bot[0]], axis=0))
            runs.append(jnp.concatenate([top[1], bot[1]], axis=0))
    else:
        accs = [jnp.zeros((bq, B_DH), F32)] * n_heads
        runs = [jnp.zeros((bq, 1), F32)] * n_heads
        for m in reversed(range(max(1, bq // bk))):
            kj = n_full + m
            res = blocks(kj, runs, col + (kj * bk - q0) < row)
            accs = [a + c for a, (c, _) in zip(accs, res)]
            runs = [r + t for r, (_, t) in zip(runs, res)]

    def body(it, carry):
        accs, runs = carry
        res = blocks(n_full - 1 - it, runs, None)
        return (tuple(a + c for a, (c, _) in zip(accs, res)),
                tuple(r + t for r, (_, t) in zip(runs, res)))

    accs, runs = lax.fori_loop(0, n_full, body, (tuple(accs), tuple(runs)))
    for h, ln in enumerate(lanes):
        o_ref[0, :, ln] = (accs[h] * _silu(g_ref[0, :, ln].astype(F32))).astype(o_ref.dtype)


def attn_prompt(qg, k, v, b_sb, bq, bk, n_heads):
    bsz, t_len, _ = k.shape
    width = n_heads * B_DH
    n_hgrp = B_HEADS // n_heads
    q_spec = pl.BlockSpec((1, bq, width), lambda b, h, i: (b, i, h))
    g_spec = pl.BlockSpec((1, bq, width), lambda b, h, i: (b, i, n_hgrp + h))
    kv_spec = pl.BlockSpec((1, t_len, width), lambda b, h, i: (b, 0, h))
    return pl.pallas_call(
        functools.partial(_attn_prompt_kernel, bq=bq, bk=bk, n_heads=n_heads),
        grid=(bsz, n_hgrp, t_len // bq),
        in_specs=[pl.BlockSpec(memory_space=pltpu.SMEM), q_spec, kv_spec, kv_spec, g_spec],
        out_specs=q_spec,
        out_shape=jax.ShapeDtypeStruct((bsz, t_len, B_WIDTH), BF16),
        compiler_params=_params("arbitrary", "arbitrary", "arbitrary"),
        name="attn_prompt",
    )(b_sb, qg, k, v, qg)


def _attn_sample_phases(ins, outs, scratch, n_group):
    bias_ref, q_ref, knew_ref, vnew_ref, acc_in_ref, run_in_ref = ins[:6]
    k_refs = ins[6:6 + n_group]
    v_refs = ins[6 + n_group:]
    acc_out_ref, run_out_ref = outs
    qbd, acc, run, kcat, vcat = scratch
    rows = B_HEADS * SAMPLE_PAD
    upper = _strict_upper(PAGE_SIZE)
    bias = bias_ref[...][:, :1]

    def repack(page, dst, i):
        for h in range(B_HEADS):
            dst[i * PAGE_SIZE:(i + 1) * PAGE_SIZE, h * B_DH:(h + 1) * B_DH] = page(h).astype(BF16)

    def step(slot0, n_blk, mask):
        keys = slice(slot0 * PAGE_SIZE, (slot0 + n_blk) * PAGE_SIZE)
        z = _dot_nt(qbd[...], kcat[keys, :]) * (B_DH ** -0.5) + bias
        sp = _softplus(z)
        spm = sp if mask is None else jnp.where(mask, sp, 0.0)
        hi, lo = _split_hi_lo(spm)
        carry = run[...][:, :1]
        laters = []
        for i in range(n_blk):
            ln = slice(i * PAGE_SIZE, (i + 1) * PAGE_SIZE)
            both = _dot(jnp.concatenate([hi[:, ln], lo[:, ln]], axis=0), upper)
            laters.append(both[:rows] + both[rows:] + carry)
            carry = carry + jnp.sum(spm[:, ln], axis=1, keepdims=True)
        later = laters[0] if n_blk == 1 else jnp.concatenate(laters, axis=1)
        a = jnp.exp(z - sp - later)
        if mask is not None:
            a = jnp.where(mask, a, 0.0)
        acc[...] += _dot(a.astype(BF16), vcat[keys, :])
        run[...] = jnp.broadcast_to(carry, run.shape)

    def build_queries():
        r = lax.broadcasted_iota(jnp.int32, (rows, B_WIDTH), 0)
        c = lax.broadcasted_iota(jnp.int32, (rows, B_WIDTH), 1)
        q_rep = jnp.concatenate([q_ref[0].astype(F32)] * B_HEADS, axis=0)
        qbd[...] = jnp.where((r // SAMPLE_PAD) == (c // B_DH), q_rep, 0.0).astype(BF16)

    def start_sequence():
        build_queries()
        acc[...] = jnp.zeros_like(acc)
        run[...] = jnp.zeros_like(run)
        repack(lambda h: knew_ref[0, pl.ds(h, PAGE_SIZE, stride=B_HEADS), :], kcat, 0)
        repack(lambda h: vnew_ref[0, pl.ds(h, PAGE_SIZE, stride=B_HEADS), :], vcat, 0)
        t = lax.broadcasted_iota(jnp.int32, (rows, PAGE_SIZE), 0) % SAMPLE_PAD
        s = lax.broadcasted_iota(jnp.int32, (rows, PAGE_SIZE), 1)
        step(0, 1, s < t)

    def resume_sequence():
        build_queries()
        acc[...] = acc_in_ref[0]
        run[...] = run_in_ref[0]

    def pages():
        for i in range(n_group):
            repack(lambda h, r=k_refs[i]: r[0, 0, pl.ds(h, PAGE_SIZE, stride=B_HEADS), :], kcat, i)
            repack(lambda h, r=v_refs[i]: r[0, 0, pl.ds(h, PAGE_SIZE, stride=B_HEADS), :], vcat, i)
        step(0, n_group, None)
        acc_out_ref[0] = acc[...]
        run_out_ref[0] = run[...]

    return start_sequence, resume_sequence, pages


class _PagedAttnRider:
    n_out = 2
    n_scratch = 5

    def __init__(self, q, k_new, v_new, cache_k, cache_v, page_table, bias_rows, acc, run, first_step, n_group):
        self.arrays = (bias_rows, q, k_new, v_new, acc, run)
        self.caches = (cache_k, cache_v)
        self.page_table = page_table
        self.first_step = first_step
        self.n_group = n_group
        self.n_in = 6 + 2 * n_group
        self.steps_per_seq = page_table.shape[1] // n_group

    def specs(self, lin, rank):
        n_group, spq = self.n_group, self.steps_per_seq
        n_pages = self.page_table.shape[1]
        rows = B_HEADS * SAMPLE_PAD
        page_rows = PAGE_SIZE * B_HEADS
        gstep = lambda a: self.first_step + lin(*a[:rank])
        seq_map = lambda *a: (gstep(a) // spq, 0, 0)

        def page_spec(i):
            def index(*a):
                g, pt = gstep(a), a[rank]
                return (0, pt[g // spq, n_pages - 1 - ((g % spq) * n_group + i)], 0, 0)
            return pl.BlockSpec((1, 1, page_rows, B_DH), index)

        acc_spec = pl.BlockSpec((1, rows, B_WIDTH), seq_map)
        run_spec = pl.BlockSpec((1, rows, GATE_LANES), seq_map)
        new_spec = pl.BlockSpec((1, page_rows, B_DH), seq_map)
        acc, run = self.arrays[4:]
        return dict(
            in_specs=[pl.BlockSpec((rows, GATE_LANES), lambda *a: (0, 0)),
                      pl.BlockSpec((1, SAMPLE_PAD, B_WIDTH), seq_map), new_spec, new_spec,
                      acc_spec, run_spec] + [page_spec(i) for i in range(n_group)] * 2,
            out_specs=[acc_spec, run_spec],
            out_shape=[jax.ShapeDtypeStruct(acc.shape, F32), jax.ShapeDtypeStruct(run.shape, F32)],
            scratch=[pltpu.VMEM((rows, B_WIDTH), BF16), pltpu.VMEM((rows, B_WIDTH), F32),
                     pltpu.VMEM((rows, GATE_LANES), F32),
                     pltpu.VMEM((n_group * PAGE_SIZE, B_WIDTH), BF16),
                     pltpu.VMEM((n_group * PAGE_SIZE, B_WIDTH), BF16)],
            args=list(self.arrays) + [self.caches[0]] * n_group + [self.caches[1]] * n_group,
            aliases={4: 0, 5: 1},
        )

    def phases(self, ins, outs, scratch, local_step):
        start, resume, pages = _attn_sample_phases(ins, outs, scratch, self.n_group)
        p = (self.first_step + local_step) % self.steps_per_seq
        return [(p == 0, start), ((local_step == 0) & (p != 0), resume)], [pages], []


def attn_sample_finish(acc, g):
    n_seq = acc.shape[0]

    def kern(acc_ref, g_ref, o_ref):
        a = acc_ref[0]
        c = lax.broadcasted_iota(jnp.int32, (SAMPLE_PAD, B_WIDTH), 1) // B_DH
        out = jnp.zeros((SAMPLE_PAD, B_WIDTH), F32)
        for h in range(B_HEADS):
            out = out + jnp.where(c == h, a[h * SAMPLE_PAD:(h + 1) * SAMPLE_PAD, :], 0.0)
        o_ref[0] = out * _silu(g_ref[0])

    return pl.pallas_call(
        kern, grid=(n_seq,),
        in_specs=[pl.BlockSpec((1, B_HEADS * SAMPLE_PAD, B_WIDTH), lambda s: (s, 0, 0)),
                  pl.BlockSpec((1, SAMPLE_PAD, B_WIDTH), lambda s: (s, 0, 0))],
        out_specs=pl.BlockSpec((1, SAMPLE_PAD, B_WIDTH), lambda s: (s, 0, 0)),
        out_shape=jax.ShapeDtypeStruct((n_seq, SAMPLE_PAD, B_WIDTH), F32),
        compiler_params=_params("arbitrary"), name="attn_sample_finish",
    )(acc, g)


def proj_act(h, w, col0, v_gain, act, tm, tn, out_dtype, second=None):
    m = h.shape[0]
    j0 = col0 // tn
    n_col = C_WIDTH // tn
    n_i = m // tm
    n_steps, tile = _two_group_steps(n_i, second is not None)

    def phases(ins, outs, scratch):
        a_ref, w_ref, vg_ref = ins[:3]
        (wbf,) = scratch
        i = pl.program_id(1)

        def cast():
            wbf[...] = w_ref[...].astype(BF16)

        def group(rows_ref, o_ref, row_chunk):
            def run():
                for r in range(0, rows_ref.shape[0], row_chunk):
                    rows = slice(r, r + row_chunk)
                    y = _dot(rows_ref[rows, :], wbf[...])
                    if act == "gelu":
                        y = _gelu_tanh(y)
                    elif act == "silu":
                        y = _silu(y)
                    else:
                        y = _rms(_gelu_tanh(y), vg_ref[...])
                    o_ref[rows, :] = y.astype(o_ref.dtype)
            return run

        main = group(a_ref, outs[0], min(tm, 256))
        if second is None:
            return [(i == 0, cast)], [main], []
        small = group(ins[3], outs[1], ins[3].shape[0])
        return [(i == 0, cast), (i == 0, small), (i > 0, main)], [], []

    w_mode = dict(pipeline_mode=pl.Buffered(1)) if n_col == 1 else {}
    in_specs = [pl.BlockSpec((tm, D_MODEL), lambda j, i, *_: (tile(i), 0)),
                pl.BlockSpec((D_MODEL, tn), lambda j, i, *_: (0, j0 + j), **w_mode),
                pl.BlockSpec((1, tn), lambda j, i, *_: (0, j))]
    out_specs = [pl.BlockSpec((tm, tn), lambda j, i, *_: (tile(i), j))]
    out_shape = [jax.ShapeDtypeStruct((m, C_WIDTH), out_dtype)]
    args = [h, w, v_gain.reshape(1, C_WIDTH)]
    if second is not None:
        h2, dtype2 = second
        in_specs.append(pl.BlockSpec(h2.shape, lambda j, i, *_: (0, 0)))
        out_specs.append(pl.BlockSpec((h2.shape[0], tn), lambda j, i, *_: (0, j)))
        out_shape.append(jax.ShapeDtypeStruct((h2.shape[0], C_WIDTH), dtype2))
        args.append(h2)
    outs, _ = _hosted_call("proj_" + act, (n_col, n_steps), lambda j, i: j * n_steps + i,
                           in_specs, out_specs, out_shape, [pltpu.VMEM((D_MODEL, tn), BF16)], phases, args)
    return outs[0] if second is None else tuple(outs)


def odd_in(h, w, v_gain, tm, act_dtype, second=None):
    u = proj_act(h, w, 0, v_gain, "gelu", tm, 1024, act_dtype, second)
    v = proj_act(h, w, C_WIDTH, v_gain, "gelu_rms", min(tm, 512), C_WIDTH, act_dtype, second)
    g = proj_act(h, w, 2 * C_WIDTH, v_gain, "silu", tm, 1024, act_dtype, second)
    if second is None:
        return u, v, g
    return (u[0], v[0], g[0]), (u[1], v[1], g[1])


def _spatial_kernel(u_ref, v_ref, g_ref, ws_ref, bs_ref, y_ref, *, chunk, n_chunks):
    tt = lax.broadcasted_iota(jnp.int32, (chunk, chunk), 0)
    ss = lax.broadcasted_iota(jnp.int32, (chunk, chunk), 1)
    causal = ss <= tt
    for grp in range(C_GROUPS):
        wm = jnp.where(causal, ws_ref[grp], 0.0)
        bcol = bs_ref[:, grp:grp + 1]
        cols = slice(grp * C_GDIM, (grp + 1) * C_GDIM)
        for c in range(n_chunks):
            rows = slice(c * chunk, (c + 1) * chunk)
            vv = v_ref[rows, cols]
            if chunk >= 128:
                sv = _dot(wm.astype(BF16), vv)
            else:
                vf = vv.astype(F32)
                sv = jnp.zeros((chunk, C_GDIM), F32)
                for s in range(chunk):
                    sv = sv + wm[:, s:s + 1] * vf[s:s + 1, :]
            sv = sv + bcol
            y = u_ref[rows, cols].astype(F32) * sv * g_ref[rows, cols].astype(F32)
            y_ref[rows, cols] = y.astype(y_ref.dtype)


def spatial_gate(u, v, g, w_s, b_s_t, chunk, n_chunks):
    m = u.shape[0]
    tm = chunk * n_chunks
    row_spec = pl.BlockSpec((tm, C_WIDTH), lambda i: (i, 0))
    return pl.pallas_call(
        functools.partial(_spatial_kernel, chunk=chunk, n_chunks=n_chunks),
        grid=(m // tm,),
        in_specs=[row_spec, row_spec, row_spec,
                  pl.BlockSpec((C_GROUPS, chunk, chunk), lambda i: (0, 0, 0)),
                  pl.BlockSpec((chunk, C_GROUPS), lambda i: (0, 0))],
        out_specs=row_spec,
        out_shape=jax.ShapeDtypeStruct((m, C_WIDTH), u.dtype),
        compiler_params=_params("arbitrary"),
        name="spatial_gate",
    )(u, v, g, w_s, b_s_t)


def _even_weights(w_in, b_i, b_f):
    gate0 = 5 * A_WIDTH
    b0 = gate0 + 2 * A_HEADS
    wt = jnp.swapaxes(w_in, 0, 1)
    w_gate = jnp.pad(w_in[:, gate0:b0], ((0, 0), (0, GATE_LANES - 2 * A_HEADS)))
    bias = jnp.pad(jnp.concatenate([b_i, b_f]), (0, GATE_LANES - 2 * A_HEADS)).reshape(1, GATE_LANES)
    return wt, b0, w_gate, bias


def _mlstm_inputs(qkvog, gates, bsz, t_len, valid_len, chunk):
    qkvog = qkvog.reshape(bsz, t_len, 5 * A_WIDTH)
    gates = gates.reshape(bsz, t_len, GATE_LANES)
    t_pad = -(-t_len // chunk) * chunk
    pad = ((0, 0), (0, t_pad - t_len), (0, 0))
    if valid_len < t_pad:
        qkvog, gates = jnp.pad(qkvog, pad), jnp.pad(gates, pad)
        pos = jnp.arange(t_pad)[None, :, None]
        lane = jnp.arange(GATE_LANES)[None, None, :]
        gates = jnp.where((pos >= valid_len) & (lane < A_HEADS), NEG_BIG, gates)
        gates = jnp.where((pos >= valid_len) & (lane >= A_HEADS), 0.0, gates)
    return qkvog, gates, gates[:, :, :2 * A_HEADS].transpose(0, 2, 1)


def _even_front(xp, xs, ew, g_norm, tm):
    wt, b0, w_gate, bias = ew
    hp, gates_p = norm_gates(xp, g_norm, w_gate, bias, min(tm, 512))
    hs, gates_s = norm_gates(xs, g_norm, w_gate, bias, xs.shape[0])
    tn, tm_kv = 1024, tm
    qkvog = proj(hp, wt, 0, 5 * A_WIDTH, BF16, 2 * tm, tn, second=(hs, BF16))
    qg_b = proj(hp, wt, b0, 2 * B_WIDTH, BF16, 2 * tm, tn, scale=B_DH ** -0.5, second=(hs, F32),
                row_step=3 * B_WIDTH)
    k = kv_proj(hp, wt, b0 + B_WIDTH, tm_kv, second=hs)
    v = kv_proj(hp, wt, b0 + 2 * B_WIDTH, tm_kv, second=hs)
    groups = []
    for i, gates in enumerate((gates_p, gates_s)):
        groups.append(dict(qkvog=qkvog[i], gates=gates, qg_b=qg_b[i],
                           k_new=k[2 * i], k_bf=k[2 * i + 1], v_new=v[2 * i], v_bf=v[2 * i + 1]))
    return groups


def _even_back(xp, xs, mix_p, mix_s, w_out, next_gain, tm):
    w_out_b = w_out.astype(BF16)
    flat = lambda mix, m: [mix[0].reshape(m, A_WIDTH), mix[1].reshape(m, B_WIDTH)]
    return out_proj_norm(flat(mix_p, xp.shape[0]), [w_out_b[:A_WIDTH], w_out_b[A_WIDTH:]], xp, next_gain,
                         tm, True, BF16, second=(flat(mix_s, xs.shape[0]), xs, BF16))


def _odd_layer(xp, hp, xs, hs, w_in, v_gain, w_s, b_s, w_out_b, final_gain, tm, n_seq):
    (u, v, g), (u2, v2, g2) = odd_in(hp, w_in, v_gain, 4 * tm, BF16, second=(hs, F32))
    y2 = spatial_gate(u2, v2, g2, w_s[:, :SAMPLE_PAD, :SAMPLE_PAD], b_s[:, :SAMPLE_PAD].T, SAMPLE_PAD, n_seq)
    y_p, y_s = out_proj_norm([], [w_out_b], xp, final_gain, tm, False, F32, second=([y2], xs, F32),
                             spatial=(u, v, g, w_s, b_s.T))
    return y_p, y_s, v2


def kernel(x_prompt, x_sample, state_a_C, state_a_n, state_a_m, cache_b_k, cache_b_v, page_table,
           even_norm, even_w_in, even_b_i, even_b_f, even_b_sb, even_w_out,
           odd_norm, odd_w_in, odd_v_gain, odd_w_s, odd_b_s, odd_w_out, final_norm):
    bsz, seq, _ = x_prompt.shape
    n_seq, dec_seq, _ = x_sample.shape
    n_pool = cache_b_k.shape[1]

    ew = _even_weights(even_w_in[0], even_b_i[0], even_b_f[0])
    odd_w_in_b = odd_w_in[0]
    odd_w_out_b = odd_w_out[0].astype(BF16)

    xp = x_prompt.reshape(bsz * seq, D_MODEL)
    xs = jnp.pad(x_sample, ((0, 0), (0, SAMPLE_PAD - dec_seq), (0, 0))).reshape(n_seq * SAMPLE_PAD, D_MODEL)
    fp, fs = _even_front(xp, xs, ew, even_norm[0], 1024)
    qkvog_p, gates_p, grow_p = _mlstm_inputs(fp["qkvog"], fp["gates"], bsz, seq, seq, A_CHUNK)
    qkvog_s, gates_s, grow_s = _mlstm_inputs(fs["qkvog"], fs["gates"], n_seq, SAMPLE_PAD, dec_seq,
                                             SAMPLE_CHUNK)
    qg_s = fs["qg_b"].reshape(n_seq, SAMPLE_PAD, 2 * B_WIDTH)
    q_s, g_s = qg_s[:, :, :B_WIDTH], qg_s[:, :, B_WIDTH:]

    assert cache_b_k.shape[0] == 1 and cache_b_v.shape[0] == 1
    page_view = (1, n_pool, PAGE_SIZE * B_HEADS, B_DH)
    cache_k, cache_v = cache_b_k.reshape(page_view), cache_b_v.reshape(page_view)
    kv_pad = ((0, 0), (0, (PAGE_SIZE - SAMPLE_PAD) * B_HEADS), (0, 0))
    k_new_s = jnp.pad(fs["k_new"].reshape(n_seq, SAMPLE_PAD * B_HEADS, B_DH), kv_pad)
    v_new_s = jnp.pad(fs["v_new"].reshape(n_seq, SAMPLE_PAD * B_HEADS, B_DH), kv_pad)
    bias_rows = jnp.broadcast_to(jnp.repeat(even_b_sb[0], SAMPLE_PAD)[:, None],
                                 (B_HEADS * SAMPLE_PAD, GATE_LANES))
    att_rows = B_HEADS * SAMPLE_PAD
    rider = _PagedAttnRider(
        q_s, k_new_s, v_new_s, cache_k, cache_v, page_table, bias_rows,
        jnp.zeros((n_seq, att_rows, B_WIDTH), F32), jnp.zeros((n_seq, att_rows, GATE_LANES), F32),
        0, PAGES_PER_STEP)
    assert bsz * (seq // A_CHUNK) == n_seq * rider.steps_per_seq
    zero_state = (jnp.zeros((bsz, A_HEADS, A_DH, A_DH), F32),
                  jnp.zeros((bsz, A_HEADS, 1, A_DH), F32),
                  jnp.zeros((bsz, A_HEADS, 1, GATE_LANES), F32))
    (ha_p, c_p, n_p, m_p), (att_acc, _) = mlstm(
        qkvog_p, gates_p, grow_p, *zero_state, A_CHUNK, rider=rider)
    hb_s = attn_sample_finish(att_acc, g_s)

    as_seq = lambda a: a.reshape(bsz, seq, B_WIDTH)
    hb_p = attn_prompt(fp["qg_b"].reshape(bsz, seq, 2 * B_WIDTH), as_seq(fp["k_bf"]), as_seq(fp["v_bf"]),
                       even_b_sb[0], ATTN_BQ, ATTN_BK, ATTN_HEADS_PER_STEP)
    st_in = (state_a_C[0], state_a_n[0][:, :, None, :],
             jnp.broadcast_to(state_a_m[0][:, :, None, None], (n_seq, A_HEADS, 1, GATE_LANES)))
    ha_s, c_s, n_s, m_s_new = mlstm(qkvog_s, gates_s, grow_s, *st_in, SAMPLE_CHUNK)
    ha_s = ha_s[:, :SAMPLE_PAD]

    xp1, hp1, xs1, hs1 = _even_back(xp, xs, (ha_p, hb_p), (ha_s, hb_s), even_w_out[0], odd_norm[0], 512)
    y_p, y_s, v_rows = _odd_layer(xp1, hp1, xs1, hs1, odd_w_in_b, odd_v_gain[0], odd_w_s[0], odd_b_s[0],
                                  odd_w_out_b, final_norm, 512, n_seq)

    def sample_rows(a, *dims):
        return a.reshape((n_seq, SAMPLE_PAD) + dims)[:, :dec_seq]

    return (y_p.reshape(bsz, seq, D_MODEL),
            sample_rows(y_s, D_MODEL),
            c_p[None], n_p[:, :, 0, :][None], m_p[:, :, 0, 0][None],
            c_s[None], n_s[:, :, 0, :][None], m_s_new[:, :, 0, 0][None],
            fp["k_new"].reshape(1, bsz, seq, B_HEADS, B_DH), fp["v_new"].reshape(1, bsz, seq, B_HEADS, B_DH),
            sample_rows(fs["k_new"], B_HEADS, B_DH)[None], sample_rows(fs["v_new"], B_HEADS, B_DH)[None],
            sample_rows(v_rows, C_WIDTH)[None])
```
